```python
import jax
import jax.numpy as jnp
from jax import lax
import numpy as np

D_MODEL = 1024
BATCH = 8
SEQ = 4096
DEPTH = 4

GRID_W = 64
CTX_LEN = 256
N_MIXERS = 3
N_LAYERS_NA = (DEPTH + 2) // 3
N_LAYERS_GLA = (DEPTH + 1) // 3
N_LAYERS_RWKV = DEPTH // 3
D_FF = 4 * D_MODEL
NORM_EPS = 1e-6

NA_HEAD_DIM = 64
NA_HEADS = D_MODEL // NA_HEAD_DIM
NA_WIN_R = 8
NA_WIN_C = 16

GLA_HEADS = 4
GLA_DK = D_MODEL // 2
GLA_DV = D_MODEL
GLA_HK = GLA_DK // GLA_HEADS
GLA_HV = GLA_DV // GLA_HEADS
GLA_LOW_RANK = 16
GLA_GATE_NORM = 16.0
GLA_CHUNK = 64
GLA_IN = 2 * GLA_DK + 2 * GLA_DV + 2 * GLA_LOW_RANK

RW_HEAD = 64
RW_HEADS = D_MODEL // RW_HEAD
RW_DECAY_LORA = 64
RW_AAA_LORA = 64
RW_GATE_LORA = 160
RW_GN_EPS = 64e-5

kernel_name = 'hybrid_natten_gla_rwkv7_dit_trunk'


def rms_norm(x, g):
    xf = x.astype(jnp.float32)
    y = xf * lax.rsqrt(jnp.mean(xf * xf, axis=-1, keepdims=True) + NORM_EPS)
    return (y * g.astype(jnp.float32)).astype(x.dtype)


def modulate(h, shift, scale):
    return h * (1.0 + scale) + shift


def sq_relu_mlp(h, w1, w2):
    return jnp.square(jax.nn.relu(h @ w1)) @ w2


def maybe_flip(t, axis, flip):
    return jnp.flip(t, axis=axis) if flip else t


def natten_latent(q, k, v, k_ctx, v_ctx, rpb):
    B, H, R, W, dh = q.shape
    kr = min(NA_WIN_R, R)
    kc = min(NA_WIN_C, W)
    n_lat = kr * W
    cols = jnp.arange(W)
    c_start = jnp.clip(cols - kc // 2, 0, W - kc)
    col_ok = (cols[None, :] >= c_start[:, None]) & (cols[None, :] < c_start[:, None] + kc)
    col_idx = jnp.clip(cols[None, :] - cols[:, None], -(NA_WIN_C - 1), NA_WIN_C - 1) + NA_WIN_C - 1
    scale = dh ** -0.5
    neg = jnp.finfo(jnp.float32).min
    rpb = rpb.astype(jnp.float32)

    def one_row(args):
        r, q_r = args
        r0 = jnp.clip(r - kr // 2, 0, R - kr)
        k_s = lax.dynamic_slice_in_dim(k, r0, kr, axis=2).reshape(B, H, n_lat, dh)
        v_s = lax.dynamic_slice_in_dim(v, r0, kr, axis=2).reshape(B, H, n_lat, dh)
        row_idx = r0 + jnp.arange(kr) - r + NA_WIN_R - 1
        bias = rpb[:, row_idx[None, :, None], col_idx[:, None, :]]
        bias = jnp.where(col_ok[None, :, None, :], bias, neg).reshape(H, W, n_lat)
        s_lat = jnp.einsum('bhqd,bhkd->bhqk', q_r, k_s).astype(jnp.float32) * scale + bias
        s_ctx = jnp.einsum('bhqd,bhcd->bhqc', q_r, k_ctx).astype(jnp.float32) * scale
        p = jax.nn.softmax(jnp.concatenate([s_lat, s_ctx], axis=-1), axis=-1).astype(v.dtype)
        return (jnp.einsum('bhqk,bhkd->bhqd', p[..., :n_lat], v_s)
                + jnp.einsum('bhqc,bhcd->bhqd', p[..., n_lat:], v_ctx))

    out = lax.map(one_row, (jnp.arange(R), jnp.moveaxis(q, 2, 0)))
    return jnp.moveaxis(out, 0, 2)


def dense_attention(q, k, v):
    s = jnp.einsum('bhqd,bhkd->bhqk', q, k).astype(jnp.float32) * q.shape[-1] ** -0.5
    p = jax.nn.softmax(s, axis=-1).astype(v.dtype)
    return jnp.einsum('bhqk,bhkd->bhqd', p, v)


def natten_mixer(h_x, h_z, w_qkv, w_o, rpb, need_ctx):
    B, S, D = h_x.shape
    R = S // GRID_W

    def heads(h):
        T = h.shape[1]
        return (h @ w_qkv).reshape(B, T, 3, NA_HEADS, NA_HEAD_DIM).transpose(2, 0, 3, 1, 4)

    q, k, v = heads(h_x)
    q_c, k_c, v_c = heads(h_z)
    grid = lambda t: t.reshape(B, NA_HEADS, R, GRID_W, NA_HEAD_DIM)
    o = natten_latent(grid(q), grid(k), grid(v), k_c, v_c, rpb)
    o_x = o.reshape(B, NA_HEADS, S, NA_HEAD_DIM).transpose(0, 2, 1, 3).reshape(B, S, D) @ w_o
    o_z = None
    if need_ctx:
        oc = dense_attention(q_c, k_c, v_c)
        o_z = oc.transpose(0, 2, 1, 3).reshape(B, h_z.shape[1], D) @ w_o
    return o_x, o_z


def gla_scan(q, k, v, g, s0):
    B, H, T, hk = q.shape
    hv = v.shape[-1]
    n = T // GLA_CHUNK
    ch = lambda t: t.reshape(B, H, n, GLA_CHUNK, t.shape[-1])
    q, k, v, g = ch(q), ch(k), ch(v), ch(g)
    b = jnp.cumsum(g, axis=3)
    b_last = b[:, :, :, -1:, :]
    q_e = q * jnp.exp(b)
    k_e = k * jnp.exp(-b)
    order_mask = jnp.tril(jnp.ones((GLA_CHUNK, GLA_CHUNK), dtype=bool))
    a = jnp.where(order_mask, jnp.einsum('bhnld,bhnmd->bhnlm', q_e, k_e), 0.0)
    o_intra = jnp.einsum('bhnlm,bhnmv->bhnlv', a, v)
    u = jnp.einsum('bhnld,bhnlv->bhndv', k * jnp.exp(b_last - b), v)
    dec = jnp.exp(b_last[:, :, :, 0])

    def step(s, inp):
        d, du = inp
        return d[..., None] * s + du, s

    s_fin, s_prev = lax.scan(step, s0, (jnp.moveaxis(dec, 2, 0), jnp.moveaxis(u, 2, 0)))
    o_inter = jnp.einsum('bhnld,nbhdv->bhnlv', q_e, s_prev)
    return (o_intra + o_inter).reshape(B, H, T, hv), s_fin


def gla_mixer(h_x, h_z, w_in, w_dec2, b_dec, norm_g, w_o, need_ctx):
    B = h_x.shape[0]

    def project(h):
        T = h.shape[1]
        p = h.astype(jnp.float32) @ w_in
        q, k, v, gt, lr = jnp.split(p, [GLA_DK, 2 * GLA_DK, 2 * GLA_DK + GLA_DV, 2 * GLA_DK + 2 * GLA_DV], axis=-1)
        hd = lambda t, d: t.reshape(B, T, GLA_HEADS, d).transpose(0, 2, 1, 3)
        q = hd(q, GLA_HK) * GLA_HK ** -0.5
        k = hd(k, GLA_HK)
        v = hd(v, GLA_HV)
        lr = lr.reshape(B, T, 2, GLA_LOW_RANK)
        z = jnp.einsum('btdr,drk->btdk', lr, w_dec2) + b_dec
        g = jax.nn.log_sigmoid(z) / GLA_GATE_NORM
        return q, k, v, gt, [hd(g[:, :, d], GLA_HK) for d in range(2)]

    sides = [project(h_x), project(h_z)]
    s0 = jnp.zeros((B, GLA_HEADS, GLA_HK, GLA_HV), jnp.float32)
    o = [0.0, 0.0]
    for d in range(2):
        flip = d == 1
        s_start = s0
        for side in (1, 0):
            q, k, v, _, gs = sides[side]
            os_, s_start = gla_scan(*(maybe_flip(t, 2, flip) for t in (q, k, v, gs[d])), s_start)
            o[side] = o[side] + maybe_flip(os_, 2, flip)

    def finish(side):
        ov = o[side]
        gt = sides[side][3]
        T = ov.shape[2]
        ov = ov * lax.rsqrt(jnp.mean(ov * ov, axis=-1, keepdims=True) + NORM_EPS) * norm_g
        ov = ov.transpose(0, 2, 1, 3).reshape(B, T, GLA_DV) * jax.nn.silu(gt)
        return (ov @ w_o).astype(h_x.dtype)

    return finish(0), (finish(1) if need_ctx else None)


def centred_shift(h):
    p = jnp.pad(h, ((0, 0), (1, 1), (0, 0)))
    return 0.5 * (p[:, :-2] + p[:, 2:])


def rwkv_project(h, mix, w_rkv, w0, w1, w2, a0, a1, a2, g1, g2, k_k, k_a):
    B, T, D = h.shape
    hd = lambda t: t.reshape(B, T, RW_HEADS, RW_HEAD)
    xx = centred_shift(h) - h
    xr, xw, xk, xv, xa, xg = (h + xx * mix[m] for m in range(6))
    r = hd(xr @ w_rkv[0])
    k = hd(xk @ w_rkv[1])
    v = hd(xv @ w_rkv[2])
    g = jax.nn.sigmoid(xg @ g1) @ g2
    kk = k * k_k.reshape(RW_HEADS, RW_HEAD)
    kk = kk * lax.rsqrt(jnp.maximum(jnp.sum(kk * kk, axis=-1, keepdims=True), 1e-24))
    k_a = k_a.reshape(RW_HEADS, RW_HEAD)
    dirs = []
    for d in range(2):
        w_log = -jax.nn.softplus(-(w0[d] + jnp.tanh(xw @ w1[d]) @ w2[d])) - 0.5
        decay = hd(jnp.exp(-jnp.exp(w_log)))
        a = hd(jax.nn.sigmoid(a0[d] + (xa @ a1[d]) @ a2[d]))
        dirs.append((decay, k * (1.0 + (a - 1.0) * k_a), a))
    return r, v, g, kk, dirs


def rwkv_scan(s0, r, decay, k, v, kk, a):
    tm = lambda t: jnp.moveaxis(t, 1, 0)

    def step(s, inp):
        r_t, w_t, k_t, v_t, kk_t, b_t = inp
        sa = jnp.einsum('bhij,bhj->bhi', s, -kk_t)
        s = s * w_t[:, :, None, :] + sa[..., None] * b_t[:, :, None, :] + v_t[..., None] * k_t[:, :, None, :]
        return s, jnp.einsum('bhij,bhj->bhi', s, r_t)

    s_fin, y = lax.scan(step, s0, (tm(r), tm(decay), tm(k), tm(v), tm(kk), tm(kk * a)))
    return jnp.moveaxis(y, 0, 1), s_fin


def rwkv_mixer(h_x, h_z, mix, w_rkv, w0, w1, w2, a0, a1, a2, g1, g2, k_k, k_a, r_k,
               ln_g, ln_b, w_o, need_ctx):
    prm = (mix, w_rkv, w0, w1, w2, a0, a1, a2, g1, g2, k_k, k_a)
    sides = [rwkv_project(h.astype(jnp.float32), *prm) for h in (h_x, h_z)]
    B, _, D = h_x.shape
    r_k = r_k.astype(jnp.float32)
    s0 = jnp.zeros((B, RW_HEADS, RW_HEAD, RW_HEAD), jnp.float32)
    y = [0.0, 0.0]
    bonus = [0.0, 0.0]
    for d in range(2):
        flip = d == 1
        s_start = s0
        for side in (1, 0):
            r, v, _, kk, dirs = sides[side]
            decay, k_d, a = dirs[d]
            ys, s_start = rwkv_scan(s_start, *(maybe_flip(t, 1, flip) for t in (r, decay, k_d, v, kk, a)))
            y[side] = y[side] + maybe_flip(ys, 1, flip)
            bonus[side] = bonus[side] + jnp.sum(r * k_d * r_k, axis=-1, keepdims=True) * v

    def finish(side):
        yv = y[side]
        g = sides[side][2]
        T = yv.shape[1]
        mu = jnp.mean(yv, axis=-1, keepdims=True)
        var = jnp.mean(jnp.square(yv - mu), axis=-1, keepdims=True)
        yn = ((yv - mu) * lax.rsqrt(var + RW_GN_EPS)).reshape(B, T, D) * ln_g + ln_b
        out = (yn + bonus[side].reshape(B, T, D)) * g
        return (out @ w_o).astype(h_x.dtype)

    return finish(0), (finish(1) if need_ctx else None)


def setup_inputs(seed: int = 0) -> dict:
    key = jax.random.key(seed)
    keys = list(jax.random.split(key, 40))
    nrm = lambda shape, s: jax.random.normal(keys.pop(), shape, jnp.float32) * s
    uni = lambda shape, lo, hi: jax.random.uniform(keys.pop(), shape, jnp.float32, lo, hi)
    D = D_MODEL
    nA, nB, nC = N_LAYERS_NA, N_LAYERS_GLA, N_LAYERS_RWKV
    return {
        'x': nrm((BATCH, SEQ, D), 1.0),
        'c': nrm((BATCH, D), 1.0),
        'ctx': nrm((BATCH, CTX_LEN, D), 1.0),
        'c_ctx': nrm((D,), 1.0),
        'ada_w': nrm((DEPTH, D, 6 * D), 0.5 * D ** -0.5),
        'ada_b': nrm((DEPTH, 6 * D), 0.01),
        'norm1_g': 1.0 + nrm((DEPTH, D), 0.02),
        'norm2_g': 1.0 + nrm((DEPTH, D), 0.02),
        'mlp_w1': nrm((DEPTH, D, D_FF), D ** -0.5),
        'mlp_w2': nrm((DEPTH, D_FF, D), D_FF ** -0.5),
        'final_g': 1.0 + nrm((D,), 0.02),
        'na_w_qkv': nrm((nA, D, 3 * D), D ** -0.5),
        'na_w_o': nrm((nA, D, D), D ** -0.5),
        'na_rpb': nrm((nA, NA_HEADS, 2 * NA_WIN_R - 1, 2 * NA_WIN_C - 1), 0.1),
        'gla_w_in': nrm((nB, D, GLA_IN), D ** -0.5),
        'gla_w_dec2': nrm((nB, 2, GLA_LOW_RANK, GLA_DK), GLA_LOW_RANK ** -0.5),
        'gla_b_dec': 1.0 + nrm((nB, 2, GLA_DK), 0.5),
        'gla_norm_g': 1.0 + nrm((nB, GLA_HV), 0.02),
        'gla_w_o': nrm((nB, GLA_DV, D), GLA_DV ** -0.5),
        'rw_mix': uni((nC, 6, D), 0.0, 1.0),
        'rw_w_rkv': nrm((nC, 3, D, D), D ** -0.5),
        'rw_w0': uni((nC, 2, D), -5.0, 1.0),
        'rw_w1': nrm((nC, 2, D, RW_DECAY_LORA), D ** -0.5),
        'rw_w2': nrm((nC, 2, RW_DECAY_LORA, D), 0.1),
        'rw_a0': nrm((nC, 2, D), 0.1),
        'rw_a1': nrm((nC, 2, D, RW_AAA_LORA), D ** -0.5),
        'rw_a2': nrm((nC, 2, RW_AAA_LORA, D), 0.1),
        'rw_g1': nrm((nC, D, RW_GATE_LORA), D ** -0.5),
        'rw_g2': nrm((nC, RW_GATE_LORA, D), RW_GATE_LORA ** -0.5),
        'rw_k_k': 0.85 + nrm((nC, D), 0.02),
        'rw_k_a': 1.0 + nrm((nC, D), 0.02),
        'rw_r_k': nrm((nC, RW_HEADS, RW_HEAD), 0.1),
        'rw_ln_g': 1.0 + nrm((nC, D), 0.02),
        'rw_ln_b': nrm((nC, D), 0.01),
        'rw_w_o': nrm((nC, D, D), D ** -0.5),
    }


def reference(x, c, ctx, c_ctx, ada_w, ada_b, norm1_g, norm2_g, mlp_w1, mlp_w2, final_g,
              na_w_qkv, na_w_o, na_rpb, gla_w_in, gla_w_dec2, gla_b_dec, gla_norm_g, gla_w_o,
              rw_mix, rw_w_rkv, rw_w0, rw_w1, rw_w2, rw_a0, rw_a1, rw_a2, rw_g1, rw_g2,
              rw_k_k, rw_k_a, rw_r_k, rw_ln_g, rw_ln_b, rw_w_o):
    z = ctx.astype(x.dtype)
    sc = jax.nn.silu(c)
    scc = jax.nn.silu(c_ctx)
    for i in range(DEPTH):
        kind, j = i % N_MIXERS, i // N_MIXERS
        need_ctx = i < DEPTH - 1
        mx = jnp.split((sc @ ada_w[i] + ada_b[i])[:, None, :], 6, axis=-1)
        mz = jnp.split((scc @ ada_w[i] + ada_b[i])[None, None, :], 6, axis=-1)
        h_x = modulate(rms_norm(x, norm1_g[i]), mx[0], mx[1])
        h_z = modulate(rms_norm(z, norm1_g[i]), mz[0], mz[1])
        if kind == 0:
            o_x, o_z = natten_mixer(h_x, h_z, na_w_qkv[j], na_w_o[j], na_rpb[j], need_ctx)
        elif kind == 1:
            o_x, o_z = gla_mixer(h_x, h_z, gla_w_in[j], gla_w_dec2[j], gla_b_dec[j], gla_norm_g[j],
                                 gla_w_o[j], need_ctx)
        else:
            o_x, o_z = rwkv_mixer(h_x, h_z, rw_mix[j], rw_w_rkv[j], rw_w0[j], rw_w1[j], rw_w2[j],
                                  rw_a0[j], rw_a1[j], rw_a2[j], rw_g1[j], rw_g2[j], rw_k_k[j],
                                  rw_k_a[j], rw_r_k[j], rw_ln_g[j], rw_ln_b[j], rw_w_o[j], need_ctx)
        x = x + mx[2] * o_x
        x = x + mx[5] * sq_relu_mlp(modulate(rms_norm(x, norm2_g[i]), mx[3], mx[4]), mlp_w1[i], mlp_w2[i])
        if need_ctx:
            z = z + mz[2] * o_z
            z = z + mz[5] * sq_relu_mlp(modulate(rms_norm(z, norm2_g[i]), mz[3], mz[4]), mlp_w1[i], mlp_w2[i])
    return rms_norm(x, final_g)
```

```python
import functools

import jax
import jax.numpy as jnp
from jax import lax
from jax.experimental import pallas as pl
from jax.experimental.pallas import tpu as pltpu

F32 = jnp.float32
BF16 = jnp.bfloat16
HIGHEST = lax.Precision.HIGHEST

D_MODEL = 1024
D_FF = 4 * D_MODEL
NORM_EPS = 1e-6
GRID_W = 64

NA_HEAD_DIM = 64
NA_HEADS = D_MODEL // NA_HEAD_DIM
NA_WIN_R = 8
NA_WIN_C = 16
NA_MASK = -1e30

GLA_HEADS = 4
GLA_DK = D_MODEL // 2
GLA_DV = D_MODEL
GLA_HK = GLA_DK // GLA_HEADS
GLA_HV = GLA_DV // GLA_HEADS
GLA_LOW_RANK = 16
GLA_GATE_NORM = 16.0
CHUNK = 64

RW_HEAD = 64
RW_GATE_LORA = 160
RW_LORA = 64
RW_GN_EPS = 64e-5

LANES = 128
VMEM_LIMIT = 56 * 1024 * 1024


def _cparams(*sem):
    return pltpu.CompilerParams(dimension_semantics=sem, vmem_limit_bytes=VMEM_LIMIT)


def _pick_tile(n, target, mult=16):
    best = None
    for t in range(mult, min(n, target) + 1, mult):
        if n % t == 0:
            best = t
    assert best is not None, (n, target)
    return best


def _const_spec(shape):
    nd = len(shape)
    return pl.BlockSpec(shape, lambda *_: (0,) * nd, pipeline_mode=pl.Buffered(1))


def _dot(a, b):
    return jnp.dot(a, b, preferred_element_type=F32)


def _dot_nt(a, b):
    return lax.dot_general(a, b, (((1,), (1,)), ((), ())), preferred_element_type=F32)


def _sigmoid(x):
    return 1.0 / (1.0 + jnp.exp(-x))


def _softplus(x):
    return jnp.maximum(x, 0.0) + jnp.log(1.0 + jnp.exp(-jnp.abs(x)))


def _rms(u, g):
    return u * lax.rsqrt(jnp.mean(u * u, axis=-1, keepdims=True) + NORM_EPS) * g


def _mod_rows(mx, mz, idx, row0, rows, n_ctx):
    vx = mx[idx:idx + 1]
    if n_ctx == 0:
        return vx
    r = row0 + lax.broadcasted_iota(jnp.int32, (rows, 1), 0)
    return jnp.where(r < n_ctx, mz[idx:idx + 1], vx)


def _norm_mod(u, g, mx, mz, row0, n_ctx, i_shift, i_scale):
    rows = u.shape[0]
    shift = _mod_rows(mx, mz, i_shift, row0, rows, n_ctx)
    scale = _mod_rows(mx, mz, i_scale, row0, rows, n_ctx)
    return _rms(u, g) * (1.0 + scale) + shift


def _pair_sum(x, lane_lo):
    s0 = jnp.sum(jnp.where(lane_lo, x, 0.0), axis=-1, keepdims=True)
    s1 = jnp.sum(jnp.where(lane_lo, 0.0, x), axis=-1, keepdims=True)
    return jnp.where(lane_lo, s0, s1)


def _ada_kernel(c_ref, w_ref, b_ref, o_ref):
    c = c_ref[...]
    sc = (c * _sigmoid(c)).astype(BF16)
    o_ref[0] = _dot(sc, w_ref[0]) + b_ref[0]


def _ada_tables(c_all, ada_w, ada_b):
    depth, d, n = ada_w.shape
    rows = c_all.shape[0]
    tn = 1536
    return pl.pallas_call(
        _ada_kernel,
        grid=(depth, n // tn),
        in_specs=[
            pl.BlockSpec((rows, d), lambda l, j: (0, 0)),
            pl.BlockSpec((1, d, tn), lambda l, j: (l, 0, j)),
            pl.BlockSpec((1, 1, tn), lambda l, j: (l, 0, j)),
        ],
        out_specs=pl.BlockSpec((1, rows, tn), lambda l, j: (l, 0, j)),
        out_shape=jax.ShapeDtypeStruct((depth, rows, n), F32),
        compiler_params=_cparams("parallel", "parallel"),
        name="ada_tables",
    )(c_all, ada_w.astype(BF16), ada_b.reshape(depth, 1, n))


def _proj_kernel(u_ref, mx_ref, mz_ref, g_ref, w_ref, o_ref, *, tm, n_ctx, nsplit):
    t = pl.program_id(1)
    h = _norm_mod(u_ref[0], g_ref[...], mx_ref[0], mz_ref[...], t * tm, n_ctx, 0, 1).astype(BF16)
    n = w_ref.shape[1] // nsplit
    for s in range(nsplit):
        o_ref[0, :, s * n:(s + 1) * n] = _dot(h, w_ref[:, s * n:(s + 1) * n]).astype(o_ref.dtype)


def _norm_proj(u, mx, mz, g, w, n_ctx, out_dtype):
    b, t, d = u.shape
    n = w.shape[1]
    tm = _pick_tile(t, 544)
    return pl.pallas_call(
        functools.partial(_proj_kernel, tm=tm, n_ctx=n_ctx, nsplit=n // D_MODEL),
        grid=(b, t // tm),
        in_specs=[
            pl.BlockSpec((1, tm, d), lambda i, j: (i, j, 0)),
            pl.BlockSpec((1, 6, d), lambda i, j: (i, 0, 0)),
            _const_spec((6, d)),
            _const_spec((1, d)),
            _const_spec((d, n)),
        ],
        out_specs=pl.BlockSpec((1, tm, n), lambda i, j: (i, j, 0)),
        out_shape=jax.ShapeDtypeStruct((b, t, n), out_dtype),
        compiler_params=_cparams("parallel", "parallel"),
        name="norm_proj",
    )(u, mx, mz, g.reshape(1, d), w)


def _na_bias_table(rpb):
    w = GRID_W
    cols = jnp.arange(w)
    c_start = jnp.clip(cols - NA_WIN_C // 2, 0, w - NA_WIN_C)
    col_ok = (cols[None, :] >= c_start[:, None]) & (cols[None, :] < c_start[:, None] + NA_WIN_C)
    col_idx = jnp.clip(cols[None, :] - cols[:, None], -(NA_WIN_C - 1), NA_WIN_C - 1) + NA_WIN_C - 1
    delta = jnp.arange(NA_WIN_R)[:, None]
    kr = jnp.arange(NA_WIN_R)[None, :]
    row_idx = kr - delta + NA_WIN_R - 1
    bias = rpb.astype(F32)[:, row_idx[:, None, :, None], col_idx[None, :, None, :]]
    bias = jnp.where(col_ok[None, None, :, None, :], bias, NA_MASK)
    return bias.reshape(rpb.shape[0], NA_WIN_R, w, NA_WIN_R * w)


def _na_kernel(q_ref, k_ref, v_ref, bias_ref, o_ref, *, n_ctx, n_rows, rb_off, with_ctx):
    rb = pl.program_id(2) + rb_off
    nctxb = n_ctx // GRID_W
    scale = NA_HEAD_DIM ** -0.5
    lane = lax.broadcasted_iota(jnp.int32, (1, LANES), 1)
    lane_lo = lane < NA_HEAD_DIM
    head_masks = (jnp.where(lane_lo, 1.0, 0.0).astype(BF16), jnp.where(lane_lo, 0.0, 1.0).astype(BF16))
    n_pairs = q_ref.shape[2] // LANES
    strip = NA_WIN_R * GRID_W

    def softmax_pv(scores, values):
        m = scores[0].max(axis=-1, keepdims=True)
        for s in scores[1:]:
            m = jnp.maximum(m, s.max(axis=-1, keepdims=True))
        es = [jnp.exp(s - m) for s in scores]
        den = es[0].sum(axis=-1, keepdims=True)
        for e in es[1:]:
            den = den + e.sum(axis=-1, keepdims=True)
        o = _dot(es[0].astype(BF16), values[0])
        for e, v in zip(es[1:], values[1:]):
            o = o + _dot(e.astype(BF16), v)
        return o / den

    def latent_rows():
        r = rb - nctxb
        r0 = jnp.clip(r - NA_WIN_R // 2, 0, n_rows - NA_WIN_R)
        start = pl.multiple_of(n_ctx + r0 * GRID_W, GRID_W)
        for p in range(n_pairs):
            ls = slice(p * LANES, (p + 1) * LANES)
            q2 = q_ref[0, :, ls]
            ks = k_ref[0, pl.ds(start, strip), ls]
            vs = v_ref[0, pl.ds(start, strip), ls]
            kc = k_ref[0, 0:n_ctx, ls]
            vc = v_ref[0, 0:n_ctx, ls]
            outs = []
            for hh in range(2):
                qm = q2 * head_masks[hh]
                s_lat = _dot_nt(qm, ks) * scale + bias_ref[2 * p + hh, 0]
                s_ctx = _dot_nt(qm, kc) * scale
                outs.append(softmax_pv([s_lat, s_ctx], [vs, vc]))
            o_ref[0, :, ls] = jnp.where(lane_lo, outs[0], outs[1]).astype(o_ref.dtype)

    def context_rows():
        for p in range(n_pairs):
            ls = slice(p * LANES, (p + 1) * LANES)
            q2 = q_ref[0, :, ls]
            kc = k_ref[0, 0:n_ctx, ls]
            vc = v_ref[0, 0:n_ctx, ls]
            outs = []
            for hh in range(2):
                qm = q2 * head_masks[hh]
                outs.append(softmax_pv([_dot_nt(qm, kc) * scale], [vc]))
            o_ref[0, :, ls] = jnp.where(lane_lo, outs[0], outs[1]).astype(o_ref.dtype)

    if with_ctx:
        pl.when(rb < nctxb)(context_rows)
        pl.when(rb >= nctxb)(latent_rows)
    else:
        latent_rows()


def _na_attention(qkv, bias, n_ctx, with_ctx):
    b, t, _ = qkv.shape
    seq = t - n_ctx
    n_rows = seq // GRID_W
    assert seq % GRID_W == 0 and n_ctx % GRID_W == 0 and n_rows >= NA_WIN_R
    hw = 2 * LANES
    ng = D_MODEL // hw
    nctxb = n_ctx // GRID_W
    rb_off = 0 if with_ctx else nctxb
    nrb = t // GRID_W - rb_off
    t_out = t if with_ctx else seq

    def delta(j):
        r = j + rb_off - nctxb
        r0 = jnp.clip(r - NA_WIN_R // 2, 0, n_rows - NA_WIN_R)
        return jnp.clip(r - r0, 0, NA_WIN_R - 1)

    return pl.pallas_call(
        functools.partial(_na_kernel, n_ctx=n_ctx, n_rows=n_rows, rb_off=rb_off, with_ctx=with_ctx),
        grid=(b, ng, nrb),
        in_specs=[
            pl.BlockSpec((1, GRID_W, hw), lambda i, g, j: (i, j + rb_off, g)),
            pl.BlockSpec((1, t, hw), lambda i, g, j: (i, 0, ng + g)),
            pl.BlockSpec((1, t, hw), lambda i, g, j: (i, 0, 2 * ng + g)),
            pl.BlockSpec((hw // NA_HEAD_DIM, 1, GRID_W, NA_WIN_R * GRID_W),
                         lambda i, g, j: (g, delta(j), 0, 0)),
        ],
        out_specs=pl.BlockSpec((1, GRID_W, hw), lambda i, g, j: (i, j, g)),
        out_shape=jax.ShapeDtypeStruct((b, t_out, D_MODEL), BF16),
        compiler_params=_cparams("parallel", "parallel", "arbitrary"),
        name="na_attention",
    )(qkv, qkv, qkv, bias)


def _gated_residual(u, o_bf16_parts, w_ref, mx_ref, mz_ref, row0, n_ctx):
    acc = None
    k0 = 0
    for part in o_bf16_parts:
        kw = part.shape[1]
        term = _dot(part, w_ref[k0:k0 + kw, :])
        acc = term if acc is None else acc + term
        k0 += kw
    gate = _mod_rows(mx_ref[0], mz_ref[...], 2, row0, u.shape[0], n_ctx)
    return u + gate * acc


def _na_out_kernel(u_ref, o_ref, mx_ref, mz_ref, w_ref, out_ref, *, tm, n_ctx, blk_off):
    row0 = (pl.program_id(1) + blk_off) * tm
    out_ref[0] = _gated_residual(u_ref[0], [o_ref[0]], w_ref, mx_ref, mz_ref, row0, n_ctx)


def _na_out(u, o, mx, mz, w_o, n_ctx, latent_only):
    b, t, d = u.shape
    if latent_only:
        tm = _pick_tile(n_ctx, 256)
        assert (t - n_ctx) % tm == 0
        blk_off = n_ctx // tm
        nt = (t - n_ctx) // tm
    else:
        tm = _pick_tile(t, 544)
        blk_off = 0
        nt = t // tm
    return pl.pallas_call(
        functools.partial(_na_out_kernel, tm=tm, n_ctx=n_ctx, blk_off=blk_off),
        grid=(b, nt),
        in_specs=[
            pl.BlockSpec((1, tm, d), lambda i, j: (i, j + blk_off, 0)),
            pl.BlockSpec((1, tm, d), lambda i, j: (i, j, 0)),
            pl.BlockSpec((1, 6, d), lambda i, j: (i, 0, 0)),
            _const_spec((6, d)),
            _const_spec((d, d)),
        ],
        out_specs=pl.BlockSpec((1, tm, d), lambda i, j: (i, j, 0)),
        out_shape=jax.ShapeDtypeStruct((b, nt * tm, d), F32),
        compiler_params=_cparams("parallel", "parallel"),
        name="na_out",
    )(u, o, mx, mz, w_o)


def _mlp_kernel(u_ref, mx_ref, mz_ref, g_ref, w1_ref, w2_ref, fg_ref, out_ref, *, tm, n_ctx, fc, final):
    row0 = pl.program_id(1) * tm
    u = u_ref[0]
    mx = mx_ref[0]
    mz = mz_ref[...]
    h = _norm_mod(u, g_ref[...], mx, mz, row0, n_ctx, 3, 4).astype(BF16)
    acc = None
    for c in range(D_FF // fc):
        a = _dot(h, w1_ref[:, c * fc:(c + 1) * fc])
        a = jnp.square(jnp.maximum(a, 0.0)).astype(BF16)
        term = _dot(a, w2_ref[c * fc:(c + 1) * fc, :])
        acc = term if acc is None else acc + term
    y = u + _mod_rows(mx, mz, 5, row0, tm, n_ctx) * acc
    if final:
        y = _rms(y, fg_ref[...])
    out_ref[0] = y


def _mlp(u, mx, mz, g, w1, w2, final_g, n_ctx, final):
    b, t, d = u.shape
    tm = _pick_tile(t, 544)
    return pl.pallas_call(
        functools.partial(_mlp_kernel, tm=tm, n_ctx=n_ctx, fc=1024, final=final),
        grid=(b, t // tm),
        in_specs=[
            pl.BlockSpec((1, tm, d), lambda i, j: (i, j, 0)),
            pl.BlockSpec((1, 6, d), lambda i, j: (i, 0, 0)),
            _const_spec((6, d)),
            _const_spec((1, d)),
            _const_spec((d, D_FF)),
            _const_spec((D_FF, d)),
            _const_spec((1, d)),
        ],
        out_specs=pl.BlockSpec((1, tm, d), lambda i, j: (i, j, 0)),
        out_shape=jax.ShapeDtypeStruct((b, t, d), F32),
        compiler_params=_cparams("parallel", "parallel"),
        name="mlp",
    )(u, mx, mz, g.reshape(1, d), w1, w2, final_g.reshape(1, d))


def _gla_proj_kernel(u_ref, mx_ref, mz_ref, g_ref, w_ref, wlr_ref, wdec_ref, bdec_ref,
                     p_ref, gate_ref, *, tm, n_ctx):
    t = pl.program_id(1)
    h = _norm_mod(u_ref[0], g_ref[...], mx_ref[0], mz_ref[...], t * tm, n_ctx, 0, 1).astype(BF16)
    n = D_MODEL
    for s in range(w_ref.shape[1] // n):
        p_ref[0, :, s * n:(s + 1) * n] = _dot(h, w_ref[:, s * n:(s + 1) * n])
    lr = _dot(h, wlr_ref[...]).astype(BF16)
    z = _dot(lr, wdec_ref[...]) + bdec_ref[...]
    gate_ref[0] = -_softplus(-z) * (1.0 / GLA_GATE_NORM)


def _gla_proj(u, mx, mz, g, w_main, w_lr, w_dec, b_dec, n_ctx):
    b, t, d = u.shape
    n = w_main.shape[1]
    tm = _pick_tile(t, 544)
    return pl.pallas_call(
        functools.partial(_gla_proj_kernel, tm=tm, n_ctx=n_ctx),
        grid=(b, t // tm),
        in_specs=[
            pl.BlockSpec((1, tm, d), lambda i, j: (i, j, 0)),
            pl.BlockSpec((1, 6, d), lambda i, j: (i, 0, 0)),
            _const_spec((6, d)),
            _const_spec((1, d)),
            _const_spec((d, n)),
            _const_spec((d, LANES)),
            _const_spec((LANES, 2 * GLA_DK)),
            _const_spec((1, 2 * GLA_DK)),
        ],
        out_specs=[
            pl.BlockSpec((1, tm, n), lambda i, j: (i, j, 0)),
            pl.BlockSpec((1, tm, 2 * GLA_DK), lambda i, j: (i, j, 0)),
        ],
        out_shape=[
            jax.ShapeDtypeStruct((b, t, n), F32),
            jax.ShapeDtypeStruct((b, t, 2 * GLA_DK), F32),
        ],
        compiler_params=_cparams("parallel", "parallel"),
        name="gla_proj",
    )(u, mx, mz, g.reshape(1, d), w_main, w_lr, w_dec, b_dec)


def _tri(n, upper, strict):
    r = lax.broadcasted_iota(jnp.int32, (n, n), 0)
    c = lax.broadcasted_iota(jnp.int32, (n, n), 1)
    if upper:
        return (r < c) if strict else (r <= c)
    return (r > c) if strict else (r >= c)


def _gla_scan_kernel(q_ref, k_ref, v_ref, g0_ref, g1_ref, o_ref, *, n_ctx_chunks, n_chunks):
    qscale = GLA_HK ** -0.5

    def run(direction, g_ref, c_lo, c_hi, state):
        upper = direction == 1
        incl = _tri(CHUNK, upper, False)
        tri = incl.astype(F32)

        def body(i, s):
            c = (c_hi - 1 - i) if upper else (c_lo + i)
            rows = pl.ds(pl.multiple_of(c * CHUNK, CHUNK), CHUNK)
            q = q_ref[0, rows, :]
            k = k_ref[0, rows, :]
            v = v_ref[0, rows, :].astype(BF16)
            g = g_ref[0, rows, :]
            bcum = jnp.dot(tri, g, precision=HIGHEST, preferred_element_type=F32)
            g_t = g.T
            b_last_col = jnp.sum(g_t, axis=1, keepdims=True)
            b_last_row = jnp.sum(g, axis=0, keepdims=True)
            q_e = (q * jnp.exp(bcum) * qscale).astype(BF16)
            k_e = (k * jnp.exp(-bcum)).astype(BF16)
            a = jnp.where(incl, _dot_nt(q_e, k_e), 0.0).astype(BF16)
            o = _dot(a, v) + _dot(q_e, s.astype(BF16))
            k_dec_t = (k * jnp.exp(b_last_row - bcum)).T.astype(BF16)
            s_new = jnp.exp(b_last_col) * s + _dot(k_dec_t, v)
            if direction == 0:
                o_ref[0, rows, :] = o
            else:
                o_ref[0, rows, :] += o
            return s_new

        return lax.fori_loop(0, c_hi - c_lo, body, state)

    s0 = jnp.zeros((GLA_HK, GLA_HV), F32)
    for direction, g_ref in ((0, g0_ref), (1, g1_ref)):
        s = run(direction, g_ref, 0, n_ctx_chunks, s0)
        run(direction, g_ref, n_ctx_chunks, n_chunks, s)


def _gla_scan(p, gates, n_ctx):
    b, t, _ = p.shape
    nkb = GLA_DK // GLA_HK
    return pl.pallas_call(
        functools.partial(_gla_scan_kernel, n_ctx_chunks=n_ctx // CHUNK, n_chunks=t // CHUNK),
        grid=(b, GLA_HEADS),
        in_specs=[
            pl.BlockSpec((1, t, GLA_HK), lambda i, h: (i, 0, h)),
            pl.BlockSpec((1, t, GLA_HK), lambda i, h: (i, 0, nkb + h)),
            pl.BlockSpec((1, t, GLA_HV), lambda i, h: (i, 0, (2 * GLA_DK) // GLA_HV + h)),
            pl.BlockSpec((1, t, GLA_HK), lambda i, h: (i, 0, h)),
            pl.BlockSpec((1, t, GLA_HK), lambda i, h: (i, 0, nkb + h)),
        ],
        out_specs=pl.BlockSpec((1, t, GLA_HV), lambda i, h: (i, 0, h)),
        out_shape=jax.ShapeDtypeStruct((b, t, GLA_DV), F32),
        compiler_params=_cparams("parallel", "parallel"),
        name="gla_scan",
    )(p, p, p, gates, gates)


def _gla_out_kernel(u_ref, o_ref, gt_ref, ng_ref, mx_ref, mz_ref, w_ref, out_ref, *, tm, n_ctx):
    row0 = pl.program_id(1) * tm
    parts = []
    for h in range(GLA_HEADS):
        ls = slice(h * GLA_HV, (h + 1) * GLA_HV)
        ov = o_ref[0, :, ls]
        ov = ov * lax.rsqrt(jnp.mean(ov * ov, axis=-1, keepdims=True) + NORM_EPS) * ng_ref[...]
        gt = gt_ref[0, :, ls]
        parts.append((ov * (gt * _sigmoid(gt))).astype(BF16))
    out_ref[0] = _gated_residual(u_ref[0], parts, w_ref, mx_ref, mz_ref, row0, n_ctx)


def _gla_out(u, o, p, norm_g, mx, mz, w_o, n_ctx):
    b, t, d = u.shape
    tm = _pick_tile(t, 544)
    gt_blk = (2 * GLA_DK + GLA_DV) // d
    return pl.pallas_call(
        functools.partial(_gla_out_kernel, tm=tm, n_ctx=n_ctx),
        grid=(b, t // tm),
        in_specs=[
            pl.BlockSpec((1, tm, d), lambda i, j: (i, j, 0)),
            pl.BlockSpec((1, tm, d), lambda i, j: (i, j, 0)),
            pl.BlockSpec((1, tm, d), lambda i, j: (i, j, gt_blk)),
            _const_spec((1, GLA_HV)),
            pl.BlockSpec((1, 6, d), lambda i, j: (i, 0, 0)),
            _const_spec((6, d)),
            _const_spec((d, d)),
        ],
        out_specs=pl.BlockSpec((1, tm, d), lambda i, j: (i, j, 0)),
        out_shape=jax.ShapeDtypeStruct((b, t, d), F32),
        compiler_params=_cparams("parallel", "parallel"),
        name="gla_out",
    )(u, o, p, norm_g.reshape(1, GLA_HV), mx, mz, w_o)


def _rw_proj_kernel(u_ref, up_ref, un_ref, mx_ref, mz_ref, g_ref, mix_ref, wrkv_ref, w1_ref, w2_ref,
                    w0_ref, a1_ref, a2_ref, a0_ref, g1_ref, g2_ref,
                    r_ref, k_ref, v_ref, gate_ref, lw_ref, a_ref, *, tm, n_ctx, t_total):
    row0 = pl.program_id(1) * tm
    g = g_ref[...]
    mx = mx_ref[0]
    mz = mz_ref[...]
    h = _norm_mod(u_ref[0], g, mx, mz, row0, n_ctx, 0, 1)
    h_prev = _norm_mod(up_ref[0], g, mx, mz, row0 - 8, n_ctx, 0, 1)[7:8]
    h_next = _norm_mod(un_ref[0], g, mx, mz, row0 + tm, n_ctx, 0, 1)[0:1]
    idx = lax.broadcasted_iota(jnp.int32, (tm, 1), 0)
    rows = row0 + idx
    h_dn = jnp.where(idx == 0, h_prev, pltpu.roll(h, 1, 0))
    h_dn = jnp.where((rows == 0) | (rows == n_ctx), 0.0, h_dn)
    h_up = jnp.where(idx == tm - 1, h_next, pltpu.roll(h, tm - 1, 0))
    h_up = jnp.where((rows == n_ctx - 1) | (rows == t_total - 1), 0.0, h_up)
    xx = 0.5 * (h_dn + h_up) - h
    xr, xw, xk, xv, xa, xg = ((h + xx * mix_ref[m:m + 1]).astype(BF16) for m in range(6))
    r_ref[0] = _dot(xr, wrkv_ref[0])
    k_ref[0] = _dot(xk, wrkv_ref[1])
    v_ref[0] = _dot(xv, wrkv_ref[2])
    gate_ref[0] = _dot(_sigmoid(_dot(xg, g1_ref[...])).astype(BF16), g2_ref[...])
    tw = jnp.tanh(_dot(xw, w1_ref[...])).astype(BF16)
    w_log = -_softplus(-(w0_ref[...] + _dot(tw, w2_ref[...]))) - 0.5
    lw_ref[0] = -jnp.exp(w_log)
    ta = _dot(xa, a1_ref[...]).astype(BF16)
    a_ref[0] = _sigmoid(a0_ref[...] + _dot(ta, a2_ref[...]))


def _rw_proj(u, mx, mz, g, wts, n_ctx):
    b, t, d = u.shape
    tm = _pick_tile(t, 272)
    nb8 = t // 8
    tb = tm // 8
    full = lambda n: jax.ShapeDtypeStruct((b, t, n), F32)
    row_spec = lambda n: pl.BlockSpec((1, tm, n), lambda i, j: (i, j, 0))
    return pl.pallas_call(
        functools.partial(_rw_proj_kernel, tm=tm, n_ctx=n_ctx, t_total=t),
        grid=(b, t // tm),
        in_specs=[
            row_spec(d),
            pl.BlockSpec((1, 8, d), lambda i, j: (i, jnp.maximum(j * tb - 1, 0), 0)),
            pl.BlockSpec((1, 8, d), lambda i, j: (i, jnp.minimum((j + 1) * tb, nb8 - 1), 0)),
            pl.BlockSpec((1, 6, d), lambda i, j: (i, 0, 0)),
            _const_spec((6, d)),
            _const_spec((1, d)),
            _const_spec((6, d)),
            _const_spec((3, d, d)),
            _const_spec((d, 2 * RW_LORA)),
            _const_spec((2 * RW_LORA, 2 * d)),
            _const_spec((1, 2 * d)),
            _const_spec((d, 2 * RW_LORA)),
            _const_spec((2 * RW_LORA, 2 * d)),
            _const_spec((1, 2 * d)),
            _const_spec((d, 2 * LANES)),
            _const_spec((2 * LANES, d)),
        ],
        out_specs=[row_spec(d), row_spec(d), row_spec(d), row_spec(d), row_spec(2 * d), row_spec(2 * d)],
        out_shape=[full(d), full(d), full(d), full(d), full(2 * d), full(2 * d)],
        compiler_params=_cparams("parallel", "parallel"),
        name="rw_proj",
    )(u, u, u, mx, mz, g.reshape(1, d), wts["mix"], wts["w_rkv"], wts["w1"], wts["w2"], wts["w0"],
      wts["a1"], wts["a2"], wts["a0"], wts["g1"], wts["g2"])


def _rw_chunk_terms(r2, k2, v2, lw2, a2, kkw, kaw, upper):
    lane_lo = lax.broadcasted_iota(jnp.int32, (1, LANES), 1) < RW_HEAD
    kk = k2 * kkw
    kk = kk * lax.rsqrt(jnp.maximum(_pair_sum(kk * kk, lane_lo), 1e-24))
    kd = k2 * (1.0 + (a2 - 1.0) * kaw)
    bvec = kk * a2
    incl = _tri(CHUNK, upper, False)
    strict = _tri(CHUNK, upper, True)
    cum = jnp.dot(incl.astype(F32), lw2, precision=HIGHEST, preferred_element_type=F32)
    tot = jnp.sum(lw2, axis=0, keepdims=True)
    e_neg = jnp.exp(-cum)
    a_t = -kk * jnp.exp(cum - lw2)
    r_t = r2 * jnp.exp(cum)
    b_h = (bvec * e_neg).astype(BF16)
    k_h = (kd * e_neg).astype(BF16)
    e_rem = jnp.exp(tot - cum)
    b_rem_t = (bvec * e_rem).T.astype(BF16)
    k_rem_t = (kd * e_rem).T.astype(BF16)
    v_b = v2.astype(BF16)
    eye = _tri(CHUNK, False, False) & _tri(CHUNK, True, False)

    ap = None
    w_acc = None
    qp = r_t
    y0 = None
    for hh in range(2):
        m = lane_lo if hh == 0 else ~lane_lo
        a_m = jnp.where(m, a_t, 0.0)
        a_mb = a_m.astype(BF16)
        r_mb = jnp.where(m, r_t, 0.0).astype(BF16)
        l_ab = jnp.where(strict, _dot_nt(a_mb, b_h), 0.0)
        l_ak = jnp.where(strict, _dot_nt(a_mb, k_h), 0.0).astype(BF16)
        m_rb = jnp.where(incl, _dot_nt(r_mb, b_h), 0.0).astype(BF16)
        m_rk = jnp.where(incl, _dot_nt(r_mb, k_h), 0.0).astype(BF16)
        x = jnp.where(eye, 1.0, 0.0) + l_ab
        pw = l_ab
        for _ in range(5):
            pw = jnp.dot(pw, pw, precision=HIGHEST, preferred_element_type=F32)
            x = x + jnp.dot(x, pw, precision=HIGHEST, preferred_element_type=F32)
        t_inv = x.astype(BF16)
        ap_h = _dot(t_inv, a_mb)
        w_h = _dot(t_inv, _dot(l_ak, v_b).astype(BF16))
        qp = qp + _dot(m_rb, ap_h.astype(BF16))
        y0_h = _dot(m_rb, w_h.astype(BF16)) + _dot(m_rk, v_b)
        ap = ap_h if ap is None else ap + ap_h
        w_acc = w_h if w_acc is None else jnp.where(m, w_h, w_acc)
        y0 = y0_h if y0 is None else jnp.where(m, y0_h, y0)

    r = lax.broadcasted_iota(jnp.int32, (LANES, LANES), 0)
    c = lax.broadcasted_iota(jnp.int32, (LANES, LANES), 1)
    same_head = (r < RW_HEAD) == (c < RW_HEAD)
    g_mat = jnp.where(same_head, _dot(b_rem_t, ap.astype(BF16)), 0.0)
    g_mat = g_mat + jnp.where(r == c, jnp.exp(tot), 0.0)
    h_mat = jnp.where(same_head, _dot(b_rem_t, w_acc.astype(BF16)) + _dot(k_rem_t, v_b), 0.0)
    return qp, y0, g_mat, h_mat


def _rw_terms_kernel(r_ref, k_ref, v_ref, lw0_ref, lw1_ref, a0_ref, a1_ref, kk_ref, ka_ref,
                     qp_ref, y0_ref, g_ref, h_ref, *, nb):
    lw_refs = (lw0_ref, lw1_ref)
    a_refs = (a0_ref, a1_ref)
    for c in range(nb):
        rows = slice(c * CHUNK, (c + 1) * CHUNK)
        srows = slice(c * LANES, (c + 1) * LANES)
        r2 = r_ref[0, rows, :]
        k2 = k_ref[0, rows, :]
        v2 = v_ref[0, rows, :]
        for d in range(2):
            qp, y0, g_mat, h_mat = _rw_chunk_terms(
                r2, k2, v2, lw_refs[d][0, rows, :], a_refs[d][0, rows, :],
                kk_ref[...], ka_ref[...], upper=(d == 1))
            qp_ref[d, 0, rows, :] = qp
            y0_ref[d, 0, rows, :] = y0
            g_ref[d, 0, srows, :] = g_mat
            h_ref[d, 0, srows, :] = h_mat


def _rw_terms(r, k, v, lw, a, k_k, k_a, nb):
    b, t, d = r.shape
    npairs = d // LANES
    nc = t // CHUNK
    rows = nb * CHUNK
    srows = nb * LANES
    tok = lambda off: pl.BlockSpec((1, rows, LANES), lambda i, p, j: (i, j, off + p))
    par = pl.BlockSpec((1, LANES), lambda i, p, j: (0, p))
    return pl.pallas_call(
        functools.partial(_rw_terms_kernel, nb=nb),
        grid=(b, npairs, nc // nb),
        in_specs=[tok(0), tok(0), tok(0), tok(0), tok(npairs), tok(0), tok(npairs), par, par],
        out_specs=[
            pl.BlockSpec((2, 1, rows, LANES), lambda i, p, j: (0, i, j, p)),
            pl.BlockSpec((2, 1, rows, LANES), lambda i, p, j: (0, i, j, p)),
            pl.BlockSpec((2, 1, srows, LANES), lambda i, p, j: (0, i, j, p)),
            pl.BlockSpec((2, 1, srows, LANES), lambda i, p, j: (0, i, j, p)),
        ],
        out_shape=[
            jax.ShapeDtypeStruct((2, b, t, d), F32),
            jax.ShapeDtypeStruct((2, b, t, d), F32),
            jax.ShapeDtypeStruct((2, b, nc * LANES, d), F32),
            jax.ShapeDtypeStruct((2, b, nc * LANES, d), F32),
        ],
        compiler_params=_cparams("parallel", "parallel", "parallel"),
        name="rw_terms",
    )(r, k, v, lw, lw, a, a, k_k.reshape(1, d), k_a.reshape(1, d))


def _rw_scan_kernel(qp0_ref, qp1_ref, y00_ref, y01_ref, g0_ref, g1_ref, h0_ref, h1_ref,
                    y0_ref, y1_ref, st_ref, *, nb, npairs):
    @pl.when(pl.program_id(2) == 0)
    def _():
        st_ref[...] = jnp.zeros_like(st_ref)

    dirs = ((qp0_ref, y00_ref, g0_ref, h0_ref, y0_ref), (qp1_ref, y01_ref, g1_ref, h1_ref, y1_ref))
    for d, (qp_ref, yc_ref, g_ref, h_ref, y_ref) in enumerate(dirs):
        for ci in range(nb):
            c = ci if d == 0 else nb - 1 - ci
            rows = slice(c * CHUNK, (c + 1) * CHUNK)
            srows = slice(c * LANES, (c + 1) * LANES)
            for p in range(npairs):
                ls = slice(p * LANES, (p + 1) * LANES)
                st = st_ref[d, p]
                st_b = st.astype(BF16)
                y_ref[0, rows, ls] = _dot(qp_ref[0, 0, rows, ls].astype(BF16), st_b) + yc_ref[0, 0, rows, ls]
                st_ref[d, p] = _dot(g_ref[0, 0, srows, ls].astype(BF16), st_b) + h_ref[0, 0, srows, ls]


def _rw_scan(qp, y0, g, h, n_ctx, nb):
    _, b, t, d = qp.shape
    ppg = 4
    pw = ppg * LANES
    nblk = t // (nb * CHUNK)
    ncb = n_ctx // (nb * CHUNK)
    rows = nb * CHUNK
    srows = nb * LANES

    def rev(j):
        return jnp.where(j < ncb, ncb - 1 - j, nblk - 1 - (j - ncb))

    tok = lambda dd, nr: pl.BlockSpec(
        (1, 1, nr, pw), (lambda i, p, j: (0, i, j, p)) if dd == 0 else (lambda i, p, j: (1, i, rev(j), p)))
    return pl.pallas_call(
        functools.partial(_rw_scan_kernel, nb=nb, npairs=ppg),
        grid=(b, d // pw, nblk),
        in_specs=[tok(0, rows), tok(1, rows), tok(0, rows), tok(1, rows),
                  tok(0, srows), tok(1, srows), tok(0, srows), tok(1, srows)],
        out_specs=[
            pl.BlockSpec((1, rows, pw), lambda i, p, j: (i, j, p)),
            pl.BlockSpec((1, rows, pw), lambda i, p, j: (i, rev(j), p)),
        ],
        out_shape=[jax.ShapeDtypeStruct((b, t, d), F32), jax.ShapeDtypeStruct((b, t, d), F32)],
        scratch_shapes=[pltpu.VMEM((2, ppg, LANES, LANES), F32)],
        compiler_params=_cparams("parallel", "parallel", "arbitrary"),
        name="rw_scan",
    )(qp, qp, y0, y0, g, g, h, h)


def _rw_out_kernel(u_ref, y0_ref, y1_ref, r_ref, k_ref, v_ref, gate_ref, a0_ref, a1_ref,
                   ka_ref, rk_ref, lng_ref, lnb_ref, mx_ref, mz_ref, w_ref, out_ref, *, tm, n_ctx):
    row0 = pl.program_id(1) * tm
    lane_lo = lax.broadcasted_iota(jnp.int32, (1, LANES), 1) < RW_HEAD
    parts = []
    for p in range(D_MODEL // LANES):
        ls = slice(p * LANES, (p + 1) * LANES)
        y = y0_ref[0, :, ls] + y1_ref[0, :, ls]
        mu = _pair_sum(y, lane_lo) * (1.0 / RW_HEAD)
        yc = y - mu
        var = _pair_sum(yc * yc, lane_lo) * (1.0 / RW_HEAD)
        yn = yc * lax.rsqrt(var + RW_GN_EPS) * lng_ref[:, ls] + lnb_ref[:, ls]
        k = k_ref[0, :, ls]
        ka = ka_ref[:, ls]
        kd_sum = k * (1.0 + (a0_ref[0, :, ls] - 1.0) * ka) + k * (1.0 + (a1_ref[0, :, ls] - 1.0) * ka)
        bonus = _pair_sum(r_ref[0, :, ls] * kd_sum * rk_ref[:, ls], lane_lo) * v_ref[0, :, ls]
        parts.append(((yn + bonus) * gate_ref[0, :, ls]).astype(BF16))
    out_ref[0] = _gated_residual(u_ref[0], parts, w_ref, mx_ref, mz_ref, row0, n_ctx)


def _rw_out(u, y0, y1, r, k, v, gate, a, k_a, r_k, ln_g, ln_b, mx, mz, w_o, n_ctx):
    b, t, d = u.shape
    tm = _pick_tile(t, 272)
    tok = lambda blk: pl.BlockSpec((1, tm, d), lambda i, j: (i, j, blk))
    vec = _const_spec((1, d))
    return pl.pallas_call(
        functools.partial(_rw_out_kernel, tm=tm, n_ctx=n_ctx),
        grid=(b, t // tm),
        in_specs=[tok(0)] * 7 + [tok(0), tok(1), vec, vec, vec, vec,
                                 pl.BlockSpec((1, 6, d), lambda i, j: (i, 0, 0)),
                                 _const_spec((6, d)), _const_spec((d, d))],
        out_specs=tok(0),
        out_shape=jax.ShapeDtypeStruct((b, t, d), F32),
        compiler_params=_cparams("parallel", "parallel"),
        name="rw_out",
    )(u, y0, y1, r, k, v, gate, a, a, k_a.reshape(1, d), r_k.reshape(1, d), ln_g.reshape(1, d),
      ln_b.reshape(1, d), mx, mz, w_o)


def _block_rows(w_pair):
    _, k, n = w_pair.shape
    z = jnp.zeros((k, n), w_pair.dtype)
    return jnp.concatenate([jnp.concatenate([w_pair[0], z], axis=1),
                            jnp.concatenate([z, w_pair[1]], axis=1)], axis=0)


def kernel(x, c, ctx, c_ctx, ada_w, ada_b, norm1_g, norm2_g, mlp_w1, mlp_w2, final_g, na_w_qkv, na_w_o, na_rpb, gla_w_in, gla_w_dec2, gla_b_dec, gla_norm_g, gla_w_o, rw_mix, rw_w_rkv, rw_w0, rw_w1, rw_w2, rw_a0, rw_a1, rw_a2, rw_g1, rw_g2, rw_k_k, rw_k_a, rw_r_k, rw_ln_g, rw_ln_b, rw_w_o):
    b, seq, d = x.shape
    n_ctx = ctx.shape[1]
    depth = ada_w.shape[0]
    assert d == D_MODEL and n_ctx % CHUNK == 0 and seq % CHUNK == 0

    u = jnp.concatenate([ctx.astype(x.dtype), x], axis=1)

    rows = -(-(b + 1) // 8) * 8
    c_all = jnp.zeros((rows, d), F32).at[:b].set(c).at[b].set(c_ctx)
    tables = _ada_tables(c_all, ada_w, ada_b)
    mods_x = tables[:, :b].reshape(depth, b, 6, d)
    mods_z = tables[:, b].reshape(depth, 6, d)

    nb = max(n for n in (4, 2, 1) if (n_ctx // CHUNK) % n == 0 and (seq // CHUNK) % n == 0)

    for i in range(depth):
        kind, j = i % 3, i // 3
        last = i == depth - 1
        mx, mz = mods_x[i], mods_z[i]
        if kind == 0:
            qkv = _norm_proj(u, mx, mz, norm1_g[i], na_w_qkv[j].astype(BF16), n_ctx, BF16)
            o = _na_attention(qkv, _na_bias_table(na_rpb[j]), n_ctx, with_ctx=not last)
            u = _na_out(u, o, mx, mz, na_w_o[j].astype(BF16), n_ctx, latent_only=last)
        elif kind == 1:
            n_main = 2 * GLA_DK + 2 * GLA_DV
            w_in = gla_w_in[j]
            w_lr = jnp.zeros((d, LANES), F32).at[:, :2 * GLA_LOW_RANK].set(w_in[:, n_main:])
            w_dec = jnp.zeros((LANES, 2 * GLA_DK), F32).at[:2 * GLA_LOW_RANK].set(_block_rows(gla_w_dec2[j]))
            p, gates = _gla_proj(u, mx, mz, norm1_g[i], w_in[:, :n_main].astype(BF16), w_lr.astype(BF16),
                                 w_dec.astype(BF16), gla_b_dec[j].reshape(1, 2 * GLA_DK), n_ctx)
            o = _gla_scan(p, gates, n_ctx)
            u = _gla_out(u, o, p, gla_norm_g[j], mx, mz, gla_w_o[j].astype(BF16), n_ctx)
        else:
            g1 = jnp.zeros((d, 2 * LANES), F32).at[:, :RW_GATE_LORA].set(rw_g1[j])
            g2 = jnp.zeros((2 * LANES, d), F32).at[:RW_GATE_LORA].set(rw_g2[j])
            wts = dict(
                mix=rw_mix[j], w_rkv=rw_w_rkv[j].astype(BF16),
                w1=jnp.concatenate([rw_w1[j, 0], rw_w1[j, 1]], axis=1).astype(BF16),
                w2=_block_rows(rw_w2[j]).astype(BF16), w0=rw_w0[j].reshape(1, 2 * d),
                a1=jnp.concatenate([rw_a1[j, 0], rw_a1[j, 1]], axis=1).astype(BF16),
                a2=_block_rows(rw_a2[j]).astype(BF16), a0=rw_a0[j].reshape(1, 2 * d),
                g1=g1.astype(BF16), g2=g2.astype(BF16))
            r, k, v, gate, lw, a = _rw_proj(u, mx, mz, norm1_g[i], wts, n_ctx)
            qp, yc, gm, hm = _rw_terms(r, k, v, lw, a, rw_k_k[j], rw_k_a[j], nb)
            y0, y1 = _rw_scan(qp, yc, gm, hm, n_ctx, nb)
            u = _rw_out(u, y0, y1, r, k, v, gate, a, rw_k_a[j], rw_r_k[j], rw_ln_g[j], rw_ln_b[j],
                        mx, mz, rw_w_o[j].astype(BF16), n_ctx)
        if last and u.shape[1] != seq:
            u = u[:, n_ctx:]
        u = _mlp(u, mx, mz, norm2_g[i], mlp_w1[i].astype(BF16), mlp_w2[i].astype(BF16), final_g,
                 0 if last else n_ctx, final=last)
    return u
```

```python
import functools

import jax
import jax.numpy as jnp
from jax import lax
from jax.experimental import pallas as pl
from jax.experimental.pallas import tpu as pltpu

F32 = jnp.float32
BF16 = jnp.bfloat16
HIGHEST = lax.Precision.HIGHEST

D_MODEL = 1024
D_FF = 4 * D_MODEL
NORM_EPS = 1e-6
GRID_W = 64

NA_HEAD_DIM = 64
NA_HEADS = D_MODEL // NA_HEAD_DIM
NA_WIN_R = 8
NA_WIN_C = 16
NA_MASK = -1e30

GLA_HEADS = 4
GLA_DK = D_MODEL // 2
GLA_DV = D_MODEL
GLA_HK = GLA_DK // GLA_HEADS
GLA_HV = GLA_DV // GLA_HEADS
GLA_LOW_RANK = 16
GLA_GATE_NORM = 16.0
CHUNK = 64

RW_HEAD = 64
RW_GATE_LORA = 160
RW_LORA = 64
RW_GN_EPS = 64e-5

LANES = 128
VMEM_LIMIT = 56 * 1024 * 1024


def _cparams(*sem):
    return pltpu.CompilerParams(dimension_semantics=sem, vmem_limit_bytes=VMEM_LIMIT)


def _pick_tile(n, target, mult=16):
    best = None
    for t in range(mult, min(n, target) + 1, mult):
        if n % t == 0:
            best = t
    assert best is not None, (n, target)
    return best


def _const_spec(shape):
    nd = len(shape)
    return pl.BlockSpec(shape, lambda *_: (0,) * nd, pipeline_mode=pl.Buffered(1))


def _dot(a, b):
    return jnp.dot(a, b, preferred_element_type=F32)


def _dot_nt(a, b):
    return lax.dot_general(a, b, (((1,), (1,)), ((), ())), preferred_element_type=F32)


def _sigmoid(x):
    return 1.0 / (1.0 + jnp.exp(-x))


def _softplus(x):
    return jnp.maximum(x, 0.0) + jnp.log(1.0 + jnp.exp(-jnp.abs(x)))


def _rms(u, g):
    return u * lax.rsqrt(jnp.mean(u * u, axis=-1, keepdims=True) + NORM_EPS) * g


def _mod_rows(mx, mz, idx, row0, rows, n_ctx):
    vx = mx[idx:idx + 1]
    if n_ctx == 0:
        return vx
    r = row0 + lax.broadcasted_iota(jnp.int32, (rows, 1), 0)
    return jnp.where(r < n_ctx, mz[idx:idx + 1], vx)


def _norm_mod(u, g, mx, mz, row0, n_ctx, i_shift, i_scale):
    rows = u.shape[0]
    shift = _mod_rows(mx, mz, i_shift, row0, rows, n_ctx)
    scale = _mod_rows(mx, mz, i_scale, row0, rows, n_ctx)
    return _rms(u, g) * (1.0 + scale) + shift


def _pair_sum(x, lane_lo):
    s0 = jnp.sum(jnp.where(lane_lo, x, 0.0), axis=-1, keepdims=True)
    s1 = jnp.sum(jnp.where(lane_lo, 0.0, x), axis=-1, keepdims=True)
    return jnp.where(lane_lo, s0, s1)


def _ada_kernel(c_ref, w_ref, b_ref, o_ref):
    c = c_ref[...]
    sc = (c * _sigmoid(c)).astype(BF16)
    o_ref[0] = _dot(sc, w_ref[0]) + b_ref[0]


def _ada_tables(c_all, ada_w, ada_b):
    depth, d, n = ada_w.shape
    rows = c_all.shape[0]
    tn = 1536
    return pl.pallas_call(
        _ada_kernel,
        grid=(depth, n // tn),
        in_specs=[
            pl.BlockSpec((rows, d), lambda l, j: (0, 0)),
            pl.BlockSpec((1, d, tn), lambda l, j: (l, 0, j)),
            pl.BlockSpec((1, 1, tn), lambda l, j: (l, 0, j)),
        ],
        out_specs=pl.BlockSpec((1, rows, tn), lambda l, j: (l, 0, j)),
        out_shape=jax.ShapeDtypeStruct((depth, rows, n), F32),
        compiler_params=_cparams("parallel", "parallel"),
        name="ada_tables",
    )(c_all, ada_w.astype(BF16), ada_b.reshape(depth, 1, n))


def _proj_kernel(u_ref, mx_ref, mz_ref, g_ref, w_ref, o_ref, *, tm, n_ctx, nsplit):
    t = pl.program_id(1)
    h = _norm_mod(u_ref[0], g_ref[...], mx_ref[0], mz_ref[...], t * tm, n_ctx, 0, 1).astype(BF16)
    n = w_ref.shape[1] // nsplit
    for s in range(nsplit):
        o_ref[0, :, s * n:(s + 1) * n] = _dot(h, w_ref[:, s * n:(s + 1) * n]).astype(o_ref.dtype)


def _norm_proj(u, mx, mz, g, w, n_ctx, out_dtype):
    b, t, d = u.shape
    n = w.shape[1]
    tm = _pick_tile(t, 544)
    return pl.pallas_call(
        functools.partial(_proj_kernel, tm=tm, n_ctx=n_ctx, nsplit=n // D_MODEL),
        grid=(b, t // tm),
        in_specs=[
            pl.BlockSpec((1, tm, d), lambda i, j: (i, j, 0)),
            pl.BlockSpec((1, 6, d), lambda i, j: (i, 0, 0)),
            _const_spec((6, d)),
            _const_spec((1, d)),
            _const_spec((d, n)),
        ],
        out_specs=pl.BlockSpec((1, tm, n), lambda i, j: (i, j, 0)),
        out_shape=jax.ShapeDtypeStruct((b, t, n), out_dtype),
        compiler_params=_cparams("parallel", "parallel"),
        name="norm_proj",
    )(u, mx, mz, g.reshape(1, d), w)


def _na_bias_table(rpb):
    w = GRID_W
    cols = jnp.arange(w)
    c_start = jnp.clip(cols - NA_WIN_C // 2, 0, w - NA_WIN_C)
    col_ok = (cols[None, :] >= c_start[:, None]) & (cols[None, :] < c_start[:, None] + NA_WIN_C)
    col_idx = jnp.clip(cols[None, :] - cols[:, None], -(NA_WIN_C - 1), NA_WIN_C - 1) + NA_WIN_C - 1
    onehot = col_idx[:, :, None] == jnp.arange(2 * NA_WIN_C - 1)
    m = jnp.sum(jnp.where(onehot[None, None], rpb.astype(F32)[:, :, None, None, :], 0.0), axis=-1)
    m = jnp.where(col_ok[None, None], m, NA_MASK)
    return jnp.concatenate([m[:, :-1], m[:, 1:]], axis=-1)


def _na_kernel(q_ref, k_ref, v_ref, bias_ref, o_ref, *, n_ctx, n_rows, rb_off, with_ctx, rq):
    rb0 = pl.program_id(2) * rq + rb_off
    nctxb = n_ctx // GRID_W
    scale = NA_HEAD_DIM ** -0.5
    lane_lo = lax.broadcasted_iota(jnp.int32, (1, LANES), 1) < NA_HEAD_DIM
    head_masks = (jnp.where(lane_lo, 1.0, 0.0).astype(BF16), jnp.where(lane_lo, 0.0, 1.0).astype(BF16))
    n_pairs = q_ref.shape[2] // LANES
    strip = NA_WIN_R * GRID_W
    jobs_idx = [(qi, p) for qi in range(rq) for p in range(n_pairs)]

    def stacked_q(qi, p):
        q2 = q_ref[0, qi * GRID_W:(qi + 1) * GRID_W, p * LANES:(p + 1) * LANES]
        return jnp.concatenate([q2 * head_masks[0], q2 * head_masks[1]], axis=0)

    def attend(jobs):
        scores = [[_dot_nt(q, k) * scale if b is None else _dot_nt(q, k) * scale + b
                   for k, b in zip(ks, bs)] for q, ks, _, bs in jobs]
        tops = []
        for sc in scores:
            m = sc[0].max(axis=-1, keepdims=True)
            for s in sc[1:]:
                m = jnp.maximum(m, s.max(axis=-1, keepdims=True))
            tops.append(m)
        exps = [[jnp.exp(s - m) for s in sc] for sc, m in zip(scores, tops)]
        dens = []
        for es in exps:
            den = es[0].sum(axis=-1, keepdims=True)
            for e in es[1:]:
                den = den + e.sum(axis=-1, keepdims=True)
            dens.append(den)
        for (qi, p), (_, _, vs, _), es, den in zip(jobs_idx, jobs, exps, dens):
            o = _dot(es[0].astype(BF16), vs[0])
            for e, v in zip(es[1:], vs[1:]):
                o = o + _dot(e.astype(BF16), v)
            o = o / den
            o2 = jnp.where(lane_lo, o[:GRID_W], o[GRID_W:])
            o_ref[0, qi * GRID_W:(qi + 1) * GRID_W, p * LANES:(p + 1) * LANES] = o2.astype(o_ref.dtype)

    def latent_rows():
        jobs = []
        for qi, p in jobs_idx:
            ls = slice(p * LANES, (p + 1) * LANES)
            r = rb0 + qi - nctxb
            r0 = jnp.clip(r - NA_WIN_R // 2, 0, n_rows - NA_WIN_R)
            start = pl.multiple_of(n_ctx + r0 * GRID_W, GRID_W)
            ri0 = NA_WIN_R - 1 - (r - r0)
            bias = jnp.concatenate(
                [jnp.concatenate([bias_ref[2 * p + hh, ri0 + 2 * m] for m in range(NA_WIN_R // 2)], axis=1)
                 for hh in range(2)], axis=0)
            jobs.append((stacked_q(qi, p),
                         [k_ref[0, pl.ds(start, strip), ls], k_ref[0, 0:n_ctx, ls]],
                         [v_ref[0, pl.ds(start, strip), ls], v_ref[0, 0:n_ctx, ls]],
                         [bias, None]))
        attend(jobs)

    def context_rows():
        jobs = []
        for qi, p in jobs_idx:
            ls = slice(p * LANES, (p + 1) * LANES)
            jobs.append((stacked_q(qi, p), [k_ref[0, 0:n_ctx, ls]], [v_ref[0, 0:n_ctx, ls]], [None]))
        attend(jobs)

    if with_ctx:
        pl.when(rb0 < nctxb)(context_rows)
        pl.when(rb0 >= nctxb)(latent_rows)
    else:
        latent_rows()


def _na_attention(qkv, bias, n_ctx, with_ctx):
    b, t, _ = qkv.shape
    seq = t - n_ctx
    n_rows = seq // GRID_W
    assert seq % GRID_W == 0 and n_ctx % GRID_W == 0 and n_rows >= NA_WIN_R
    hw = 2 * LANES
    ng = D_MODEL // hw
    nctxb = n_ctx // GRID_W
    rq = 2
    assert nctxb % rq == 0 and n_rows % rq == 0
    rb_off = 0 if with_ctx else nctxb
    nrb = (t // GRID_W - rb_off) // rq
    t_out = t if with_ctx else seq
    qrows = rq * GRID_W

    return pl.pallas_call(
        functools.partial(_na_kernel, n_ctx=n_ctx, n_rows=n_rows, rb_off=rb_off, with_ctx=with_ctx, rq=rq),
        grid=(b, ng, nrb),
        in_specs=[
            pl.BlockSpec((1, qrows, hw), lambda i, g, j: (i, j + rb_off // rq, g)),
            pl.BlockSpec((1, t, hw), lambda i, g, j: (i, 0, ng + g)),
            pl.BlockSpec((1, t, hw), lambda i, g, j: (i, 0, 2 * ng + g)),
            pl.BlockSpec((hw // NA_HEAD_DIM, 2 * NA_WIN_R - 2, GRID_W, LANES), lambda i, g, j: (g, 0, 0, 0)),
        ],
        out_specs=pl.BlockSpec((1, qrows, hw), lambda i, g, j: (i, j, g)),
        out_shape=jax.ShapeDtypeStruct((b, t_out, D_MODEL), BF16),
        compiler_params=_cparams("parallel", "parallel", "arbitrary"),
        name="na_attention",
    )(qkv, qkv, qkv, bias)


def _gated_residual(u, o_bf16_parts, w_ref, mx_ref, mz_ref, row0, n_ctx):
    acc = None
    k0 = 0
    for part in o_bf16_parts:
        kw = part.shape[1]
        term = _dot(part, w_ref[k0:k0 + kw, :])
        acc = term if acc is None else acc + term
        k0 += kw
    gate = _mod_rows(mx_ref[0], mz_ref[...], 2, row0, u.shape[0], n_ctx)
    return u + gate * acc


def _na_out_kernel(u_ref, o_ref, mx_ref, mz_ref, w_ref, out_ref, *, tm, n_ctx, blk_off):
    row0 = (pl.program_id(1) + blk_off) * tm
    out_ref[0] = _gated_residual(u_ref[0], [o_ref[0]], w_ref, mx_ref, mz_ref, row0, n_ctx)


def _na_out(u, o, mx, mz, w_o, n_ctx, latent_only):
    b, t, d = u.shape
    if latent_only:
        tm = _pick_tile(n_ctx, 256)
        assert (t - n_ctx) % tm == 0
        blk_off = n_ctx // tm
        nt = (t - n_ctx) // tm
    else:
        tm = _pick_tile(t, 544)
        blk_off = 0
        nt = t // tm
    return pl.pallas_call(
        functools.partial(_na_out_kernel, tm=tm, n_ctx=n_ctx, blk_off=blk_off),
        grid=(b, nt),
        in_specs=[
            pl.BlockSpec((1, tm, d), lambda i, j: (i, j + blk_off, 0)),
            pl.BlockSpec((1, tm, d), lambda i, j: (i, j, 0)),
            pl.BlockSpec((1, 6, d), lambda i, j: (i, 0, 0)),
            _const_spec((6, d)),
            _const_spec((d, d)),
        ],
        out_specs=pl.BlockSpec((1, tm, d), lambda i, j: (i, j, 0)),
        out_shape=jax.ShapeDtypeStruct((b, nt * tm, d), F32),
        compiler_params=_cparams("parallel", "parallel"),
        name="na_out",
    )(u, o, mx, mz, w_o)


def _mlp_kernel(u_ref, mx_ref, mz_ref, g_ref, w1_ref, w2_ref, fg_ref, out_ref, *, tm, n_ctx, fc, final):
    row0 = pl.program_id(1) * tm
    u = u_ref[0]
    mx = mx_ref[0]
    mz = mz_ref[...]
    h = _norm_mod(u, g_ref[...], mx, mz, row0, n_ctx, 3, 4).astype(BF16)
    acc = None
    for c in range(D_FF // fc):
        a = _dot(h, w1_ref[:, c * fc:(c + 1) * fc])
        a = jnp.square(jnp.maximum(a, 0.0)).astype(BF16)
        term = _dot(a, w2_ref[c * fc:(c + 1) * fc, :])
        acc = term if acc is None else acc + term
    y = u + _mod_rows(mx, mz, 5, row0, tm, n_ctx) * acc
    if final:
        y = _rms(y, fg_ref[...])
    out_ref[0] = y


def _mlp(u, mx, mz, g, w1, w2, final_g, n_ctx, final):
    b, t, d = u.shape
    tm = _pick_tile(t, 544)
    return pl.pallas_call(
        functools.partial(_mlp_kernel, tm=tm, n_ctx=n_ctx, fc=1024, final=final),
        grid=(b, t // tm),
        in_specs=[
            pl.BlockSpec((1, tm, d), lambda i, j: (i, j, 0)),
            pl.BlockSpec((1, 6, d), lambda i, j: (i, 0, 0)),
            _const_spec((6, d)),
            _const_spec((1, d)),
            _const_spec((d, D_FF)),
            _const_spec((D_FF, d)),
            _const_spec((1, d)),
        ],
        out_specs=pl.BlockSpec((1, tm, d), lambda i, j: (i, j, 0)),
        out_shape=jax.ShapeDtypeStruct((b, t, d), F32),
        compiler_params=_cparams("parallel", "parallel"),
        name="mlp",
    )(u, mx, mz, g.reshape(1, d), w1, w2, final_g.reshape(1, d))


def _gla_proj_kernel(u_ref, mx_ref, mz_ref, g_ref, w_ref, wlr_ref, wdec_ref, bdec_ref,
                     p_ref, gate_ref, *, tm, n_ctx):
    t = pl.program_id(1)
    h = _norm_mod(u_ref[0], g_ref[...], mx_ref[0], mz_ref[...], t * tm, n_ctx, 0, 1).astype(BF16)
    n = D_MODEL
    for s in range(w_ref.shape[1] // n):
        p_ref[0, :, s * n:(s + 1) * n] = _dot(h, w_ref[:, s * n:(s + 1) * n])
    lr = _dot(h, wlr_ref[...]).astype(BF16)
    z = _dot(lr, wdec_ref[...]) + bdec_ref[...]
    gate_ref[0] = -_softplus(-z) * (1.0 / GLA_GATE_NORM)


def _gla_proj(u, mx, mz, g, w_main, w_lr, w_dec, b_dec, n_ctx):
    b, t, d = u.shape
    n = w_main.shape[1]
    tm = _pick_tile(t, 544)
    return pl.pallas_call(
        functools.partial(_gla_proj_kernel, tm=tm, n_ctx=n_ctx),
        grid=(b, t // tm),
        in_specs=[
            pl.BlockSpec((1, tm, d), lambda i, j: (i, j, 0)),
            pl.BlockSpec((1, 6, d), lambda i, j: (i, 0, 0)),
            _const_spec((6, d)),
            _const_spec((1, d)),
            _const_spec((d, n)),
            _const_spec((d, LANES)),
            _const_spec((LANES, 2 * GLA_DK)),
            _const_spec((1, 2 * GLA_DK)),
        ],
        out_specs=[
            pl.BlockSpec((1, tm, n), lambda i, j: (i, j, 0)),
            pl.BlockSpec((1, tm, 2 * GLA_DK), lambda i, j: (i, j, 0)),
        ],
        out_shape=[
            jax.ShapeDtypeStruct((b, t, n), F32),
            jax.ShapeDtypeStruct((b, t, 2 * GLA_DK), F32),
        ],
        compiler_params=_cparams("parallel", "parallel"),
        name="gla_proj",
    )(u, mx, mz, g.reshape(1, d), w_main, w_lr, w_dec, b_dec)


def _tri(n, upper, strict):
    r = lax.broadcasted_iota(jnp.int32, (n, n), 0)
    c = lax.broadcasted_iota(jnp.int32, (n, n), 1)
    if upper:
        return (r < c) if strict else (r <= c)
    return (r > c) if strict else (r >= c)


def _split3(x):
    hi = x.astype(BF16)
    r1 = x - hi.astype(F32)
    mid = r1.astype(BF16)
    lo = (r1 - mid.astype(F32)).astype(BF16)
    return hi, mid, lo


def _dot_hilo(a, b):
    a_hi = a.astype(BF16)
    a_lo = (a - a_hi.astype(F32)).astype(BF16)
    b_hi = b.astype(BF16)
    b_lo = (b - b_hi.astype(F32)).astype(BF16)
    return _dot(a_hi, b_hi) + _dot(a_hi, b_lo) + _dot(a_lo, b_hi)


def _tri_cumsum(tri_b, x):
    hi, mid, lo = _split3(x)
    return _dot(tri_b, hi) + _dot(tri_b, mid) + _dot(tri_b, lo)


def _gla_scan_kernel(q_ref, k_ref, v_ref, g0_ref, g1_ref, o_ref, s_ref, *, n_ctx_chunks, n_chunks):
    qscale = GLA_HK ** -0.5
    g_refs = (g0_ref, g1_ref)
    incl = (_tri(CHUNK, False, False), _tri(CHUNK, True, False))
    tri_b = tuple(jnp.where(m, 1.0, 0.0).astype(BF16) for m in incl)
    o_ref[...] = jnp.zeros_like(o_ref)

    def step(chunks):
        rows = [pl.ds(pl.multiple_of(c * CHUNK, CHUNK), CHUNK) for c in chunks]
        q = [q_ref[0, r, :] for r in rows]
        k = [k_ref[0, r, :] for r in rows]
        v = [v_ref[0, r, :].astype(BF16) for r in rows]
        g = [g_refs[d][0, rows[d], :] for d in range(2)]
        bcum = [_tri_cumsum(tri_b[d], g[d]) for d in range(2)]
        b_last_row = [jnp.sum(g[d], axis=0, keepdims=True) for d in range(2)]
        b_last_col = [jnp.sum(g[d].T, axis=1, keepdims=True) for d in range(2)]
        q_e = [(q[d] * jnp.exp(bcum[d]) * qscale).astype(BF16) for d in range(2)]
        k_e = [(k[d] * jnp.exp(-bcum[d])).astype(BF16) for d in range(2)]
        k_dec_t = [(k[d] * jnp.exp(b_last_row[d] - bcum[d])).T.astype(BF16) for d in range(2)]
        a = [jnp.where(incl[d], _dot_nt(q_e[d], k_e[d]), 0.0).astype(BF16) for d in range(2)]
        s = [s_ref[d] for d in range(2)]
        o = [_dot(a[d], v[d]) + _dot(q_e[d], s[d].astype(BF16)) for d in range(2)]
        for d in range(2):
            s_ref[d] = jnp.exp(b_last_col[d]) * s[d] + _dot(k_dec_t[d], v[d])
            o_ref[0, rows[d], :] += o[d]

    s_ref[...] = jnp.zeros_like(s_ref)
    n_lat = n_chunks - n_ctx_chunks

    def ctx_body(i, carry):
        step((i, n_ctx_chunks - 1 - i))
        return carry

    def lat_body(i, carry):
        step((n_ctx_chunks + i, n_chunks - 1 - i))
        return carry

    lax.fori_loop(0, n_ctx_chunks, ctx_body, 0)
    lax.fori_loop(0, n_lat, lat_body, 0)


def _gla_scan(p, gates, n_ctx):
    b, t, _ = p.shape
    nkb = GLA_DK // GLA_HK
    return pl.pallas_call(
        functools.partial(_gla_scan_kernel, n_ctx_chunks=n_ctx // CHUNK, n_chunks=t // CHUNK),
        grid=(b, GLA_HEADS),
        in_specs=[
            pl.BlockSpec((1, t, GLA_HK), lambda i, h: (i, 0, h)),
            pl.BlockSpec((1, t, GLA_HK), lambda i, h: (i, 0, nkb + h)),
            pl.BlockSpec((1, t, GLA_HV), lambda i, h: (i, 0, (2 * GLA_DK) // GLA_HV + h)),
            pl.BlockSpec((1, t, GLA_HK), lambda i, h: (i, 0, h)),
            pl.BlockSpec((1, t, GLA_HK), lambda i, h: (i, 0, nkb + h)),
        ],
        out_specs=pl.BlockSpec((1, t, GLA_HV), lambda i, h: (i, 0, h)),
        out_shape=jax.ShapeDtypeStruct((b, t, GLA_DV), F32),
        scratch_shapes=[pltpu.VMEM((2, GLA_HK, GLA_HV), F32)],
        compiler_params=_cparams("parallel", "parallel"),
        name="gla_scan",
    )(p, p, p, gates, gates)


def _gla_out_kernel(u_ref, o_ref, gt_ref, ng_ref, mx_ref, mz_ref, w_ref, out_ref, *, tm, n_ctx):
    row0 = pl.program_id(1) * tm
    parts = []
    for h in range(GLA_HEADS):
        ls = slice(h * GLA_HV, (h + 1) * GLA_HV)
        ov = o_ref[0, :, ls]
        ov = ov * lax.rsqrt(jnp.mean(ov * ov, axis=-1, keepdims=True) + NORM_EPS) * ng_ref[...]
        gt = gt_ref[0, :, ls]
        parts.append((ov * (gt * _sigmoid(gt))).astype(BF16))
    out_ref[0] = _gated_residual(u_ref[0], parts, w_ref, mx_ref, mz_ref, row0, n_ctx)


def _gla_out(u, o, p, norm_g, mx, mz, w_o, n_ctx):
    b, t, d = u.shape
    tm = _pick_tile(t, 544)
    gt_blk = (2 * GLA_DK + GLA_DV) // d
    return pl.pallas_call(
        functools.partial(_gla_out_kernel, tm=tm, n_ctx=n_ctx),
        grid=(b, t // tm),
        in_specs=[
            pl.BlockSpec((1, tm, d), lambda i, j: (i, j, 0)),
            pl.BlockSpec((1, tm, d), lambda i, j: (i, j, 0)),
            pl.BlockSpec((1, tm, d), lambda i, j: (i, j, gt_blk)),
            _const_spec((1, GLA_HV)),
            pl.BlockSpec((1, 6, d), lambda i, j: (i, 0, 0)),
            _const_spec((6, d)),
            _const_spec((d, d)),
        ],
        out_specs=pl.BlockSpec((1, tm, d), lambda i, j: (i, j, 0)),
        out_shape=jax.ShapeDtypeStruct((b, t, d), F32),
        compiler_params=_cparams("parallel", "parallel"),
        name="gla_out",
    )(u, o, p, norm_g.reshape(1, GLA_HV), mx, mz, w_o)


def _rw_proj_kernel(u_ref, up_ref, un_ref, mx_ref, mz_ref, g_ref, mix_ref, wrkv_ref, w1_ref, w2_ref,
                    w0_ref, a1_ref, a2_ref, a0_ref, g1_ref, g2_ref,
                    r_ref, k_ref, v_ref, gate_ref, lw_ref, a_ref, *, tm, n_ctx, t_total):
    row0 = pl.program_id(1) * tm
    g = g_ref[...]
    mx = mx_ref[0]
    mz = mz_ref[...]
    h = _norm_mod(u_ref[0], g, mx, mz, row0, n_ctx, 0, 1)
    h_prev = _norm_mod(up_ref[0], g, mx, mz, row0 - 8, n_ctx, 0, 1)[7:8]
    h_next = _norm_mod(un_ref[0], g, mx, mz, row0 + tm, n_ctx, 0, 1)[0:1]
    idx = lax.broadcasted_iota(jnp.int32, (tm, 1), 0)
    rows = row0 + idx
    h_dn = jnp.where(idx == 0, h_prev, pltpu.roll(h, 1, 0))
    h_dn = jnp.where((rows == 0) | (rows == n_ctx), 0.0, h_dn)
    h_up = jnp.where(idx == tm - 1, h_next, pltpu.roll(h, tm - 1, 0))
    h_up = jnp.where((rows == n_ctx - 1) | (rows == t_total - 1), 0.0, h_up)
    xx = 0.5 * (h_dn + h_up) - h
    xr, xw, xk, xv, xa, xg = ((h + xx * mix_ref[m:m + 1]).astype(BF16) for m in range(6))
    r_ref[0] = _dot(xr, wrkv_ref[0])
    k_ref[0] = _dot(xk, wrkv_ref[1])
    v_ref[0] = _dot(xv, wrkv_ref[2])
    gate_ref[0] = _dot(_sigmoid(_dot(xg, g1_ref[...])).astype(BF16), g2_ref[...])
    tw = jnp.tanh(_dot(xw, w1_ref[...])).astype(BF16)
    w_log = -_softplus(-(w0_ref[...] + _dot(tw, w2_ref[...]))) - 0.5
    lw_ref[0] = -jnp.exp(w_log)
    ta = _dot(xa, a1_ref[...]).astype(BF16)
    a_ref[0] = _sigmoid(a0_ref[...] + _dot(ta, a2_ref[...]))


def _rw_proj(u, mx, mz, g, wts, n_ctx):
    b, t, d = u.shape
    tm = _pick_tile(t, 272)
    nb8 = t // 8
    tb = tm // 8
    full = lambda n: jax.ShapeDtypeStruct((b, t, n), F32)
    row_spec = lambda n: pl.BlockSpec((1, tm, n), lambda i, j: (i, j, 0))
    return pl.pallas_call(
        functools.partial(_rw_proj_kernel, tm=tm, n_ctx=n_ctx, t_total=t),
        grid=(b, t // tm),
        in_specs=[
            row_spec(d),
            pl.BlockSpec((1, 8, d), lambda i, j: (i, jnp.maximum(j * tb - 1, 0), 0)),
            pl.BlockSpec((1, 8, d), lambda i, j: (i, jnp.minimum((j + 1) * tb, nb8 - 1), 0)),
            pl.BlockSpec((1, 6, d), lambda i, j: (i, 0, 0)),
            _const_spec((6, d)),
            _const_spec((1, d)),
            _const_spec((6, d)),
            _const_spec((3, d, d)),
            _const_spec((d, 2 * RW_LORA)),
            _const_spec((2 * RW_LORA, 2 * d)),
            _const_spec((1, 2 * d)),
            _const_spec((d, 2 * RW_LORA)),
            _const_spec((2 * RW_LORA, 2 * d)),
            _const_spec((1, 2 * d)),
            _const_spec((d, 2 * LANES)),
            _const_spec((2 * LANES, d)),
        ],
        out_specs=[row_spec(d), row_spec(d), row_spec(d), row_spec(d), row_spec(2 * d), row_spec(2 * d)],
        out_shape=[full(d), full(d), full(d), full(d), full(2 * d), full(2 * d)],
        compiler_params=_cparams("parallel", "parallel"),
        name="rw_proj",
    )(u, u, u, mx, mz, g.reshape(1, d), wts["mix"], wts["w_rkv"], wts["w1"], wts["w2"], wts["w0"],
      wts["a1"], wts["a2"], wts["a0"], wts["g1"], wts["g2"])


def _rw_chunk_terms(units, kkw, kaw):
    lane_lo = lax.broadcasted_iota(jnp.int32, (1, LANES), 1) < RW_HEAD
    head_lanes = (lane_lo, ~lane_lo)
    incl = (_tri(CHUNK, False, False), _tri(CHUNK, True, False))
    strict = (_tri(CHUNK, False, True), _tri(CHUNK, True, True))
    tri_b = tuple(jnp.where(m, 1.0, 0.0).astype(BF16) for m in incl)
    eye = jnp.where(incl[0] & incl[1], 1.0, 0.0)
    nu = len(units)
    heads = [(u, hh) for u in range(nu) for hh in range(2)]

    pre = []
    for r2, k2, v2, lw2, a2, desc in units:
        d = int(desc)
        kk = k2 * kkw
        kk = kk * lax.rsqrt(jnp.maximum(_pair_sum(kk * kk, lane_lo), 1e-24))
        kd = k2 * (1.0 + (a2 - 1.0) * kaw)
        bvec = kk * a2
        cum = _tri_cumsum(tri_b[d], lw2)
        tot = jnp.sum(lw2, axis=0, keepdims=True)
        e_neg = jnp.exp(-cum)
        e_rem = jnp.exp(tot - cum)
        pre.append(dict(
            d=d, tot=tot, v_b=v2.astype(BF16),
            a_t=-kk * jnp.exp(cum - lw2), r_t=r2 * jnp.exp(cum),
            b_h=(bvec * e_neg).astype(BF16), k_h=(kd * e_neg).astype(BF16),
            b_rem_t=(bvec * e_rem).T.astype(BF16), k_rem_t=(kd * e_rem).T.astype(BF16)))

    stacked = [jnp.concatenate([jnp.where(head_lanes[hh], pre[u]["a_t"], 0.0),
                                jnp.where(head_lanes[hh], pre[u]["r_t"], 0.0)], axis=0).astype(BF16)
               for u, hh in heads]
    p_b = [_dot_nt(st, pre[u]["b_h"]) for st, (u, _) in zip(stacked, heads)]
    p_k = [_dot_nt(st, pre[u]["k_h"]) for st, (u, _) in zip(stacked, heads)]
    l_ab = [jnp.where(strict[pre[u]["d"]], p[:CHUNK], 0.0) for p, (u, _) in zip(p_b, heads)]
    m_rb = [jnp.where(incl[pre[u]["d"]], p[CHUNK:], 0.0).astype(BF16) for p, (u, _) in zip(p_b, heads)]
    l_ak = [jnp.where(strict[pre[u]["d"]], p[:CHUNK], 0.0).astype(BF16) for p, (u, _) in zip(p_k, heads)]
    m_rk = [jnp.where(incl[pre[u]["d"]], p[CHUNK:], 0.0).astype(BF16) for p, (u, _) in zip(p_k, heads)]

    pw = [_dot_hilo(l, l) for l in l_ab]
    xo = [l + p + _dot_hilo(l, p) for l, p in zip(l_ab, pw)]
    pw = [_dot_hilo(p, p) for p in pw]
    xo = [xi + p + _dot_hilo(xi, p) for xi, p in zip(xo, pw)]
    for _ in range(3):
        pw = [_dot(p.astype(BF16), p.astype(BF16)) for p in pw]
        xo = [xi + p + _dot(xi.astype(BF16), p.astype(BF16)) for xi, p in zip(xo, pw)]
    t_inv = [(eye + xi).astype(BF16) for xi in xo]

    lakv = [_dot(l, pre[u]["v_b"]).astype(BF16) for l, (u, _) in zip(l_ak, heads)]
    tw = [_dot(ti, jnp.concatenate([st[:CHUNK], lv], axis=1)) for ti, st, lv in zip(t_inv, stacked, lakv)]
    qy = [_dot(mb, t.astype(BF16)) for mb, t in zip(m_rb, tw)]
    yk = [_dot(mk, pre[u]["v_b"]) for mk, (u, _) in zip(m_rk, heads)]

    r = lax.broadcasted_iota(jnp.int32, (LANES, LANES), 0)
    c = lax.broadcasted_iota(jnp.int32, (LANES, LANES), 1)
    same_head = (r < RW_HEAD) == (c < RW_HEAD)
    out = []
    for u in range(nu):
        h0, h1 = 2 * u, 2 * u + 1
        qp = pre[u]["r_t"] + qy[h0][:, :LANES] + qy[h1][:, :LANES]
        y0 = jnp.where(lane_lo, qy[h0][:, LANES:] + yk[h0], qy[h1][:, LANES:] + yk[h1])
        aw = jnp.concatenate([tw[h0][:, :LANES] + tw[h1][:, :LANES],
                              jnp.where(lane_lo, tw[h0][:, LANES:], tw[h1][:, LANES:])], axis=1).astype(BF16)
        gh = _dot(pre[u]["b_rem_t"], aw)
        g_mat = jnp.where(same_head, gh[:, :LANES], 0.0) + jnp.where(r == c, jnp.exp(pre[u]["tot"]), 0.0)
        h_mat = jnp.where(same_head, gh[:, LANES:] + _dot(pre[u]["k_rem_t"], pre[u]["v_b"]), 0.0)
        out.append((qp, y0, g_mat, h_mat))
    return out


def _rw_terms_kernel(r_ref, k_ref, v_ref, lw0_ref, lw1_ref, a0_ref, a1_ref, kk_ref, ka_ref,
                     qp_ref, y0_ref, g_ref, h_ref, *, nb):
    lw_refs = (lw0_ref, lw1_ref)
    a_refs = (a0_ref, a1_ref)
    units = []
    for c in range(nb):
        rows = slice(c * CHUNK, (c + 1) * CHUNK)
        for d in range(2):
            units.append((r_ref[0, rows, :], k_ref[0, rows, :], v_ref[0, rows, :],
                          lw_refs[d][0, rows, :], a_refs[d][0, rows, :], d == 1))
    results = _rw_chunk_terms(units, kk_ref[...], ka_ref[...])
    for i, (qp, y0, g_mat, h_mat) in enumerate(results):
        c, d = divmod(i, 2)
        rows = slice(c * CHUNK, (c + 1) * CHUNK)
        srows = slice(c * LANES, (c + 1) * LANES)
        qp_ref[d, 0, rows, :] = qp.astype(qp_ref.dtype)
        y0_ref[d, 0, rows, :] = y0
        g_ref[d, 0, srows, :] = g_mat.astype(g_ref.dtype)
        h_ref[d, 0, srows, :] = h_mat


def _rw_terms(r, k, v, lw, a, k_k, k_a, nb):
    b, t, d = r.shape
    npairs = d // LANES
    nc = t // CHUNK
    rows = nb * CHUNK
    srows = nb * LANES
    tok = lambda off: pl.BlockSpec((1, rows, LANES), lambda i, p, j: (i, j, off + p))
    par = pl.BlockSpec((1, LANES), lambda i, p, j: (0, p))
    return pl.pallas_call(
        functools.partial(_rw_terms_kernel, nb=nb),
        grid=(b, npairs, nc // nb),
        in_specs=[tok(0), tok(0), tok(0), tok(0), tok(npairs), tok(0), tok(npairs), par, par],
        out_specs=[
            pl.BlockSpec((2, 1, rows, LANES), lambda i, p, j: (0, i, j, p)),
            pl.BlockSpec((2, 1, rows, LANES), lambda i, p, j: (0, i, j, p)),
            pl.BlockSpec((2, 1, srows, LANES), lambda i, p, j: (0, i, j, p)),
            pl.BlockSpec((2, 1, srows, LANES), lambda i, p, j: (0, i, j, p)),
        ],
        out_shape=[
            jax.ShapeDtypeStruct((2, b, t, d), BF16),
            jax.ShapeDtypeStruct((2, b, t, d), F32),
            jax.ShapeDtypeStruct((2, b, nc * LANES, d), BF16),
            jax.ShapeDtypeStruct((2, b, nc * LANES, d), F32),
        ],
        compiler_params=_cparams("parallel", "parallel", "parallel"),
        name="rw_terms",
    )(r, k, v, lw, lw, a, a, k_k.reshape(1, d), k_a.reshape(1, d))


def _rw_scan_kernel(qp0_ref, qp1_ref, y00_ref, y01_ref, g0_ref, g1_ref, h0_ref, h1_ref,
                    y0_ref, y1_ref, st_ref, *, nb, npairs):
    @pl.when(pl.program_id(2) == 0)
    def _():
        st_ref[...] = jnp.zeros_like(st_ref)

    dirs = ((qp0_ref, y00_ref, g0_ref, h0_ref, y0_ref), (qp1_ref, y01_ref, g1_ref, h1_ref, y1_ref))
    for d, (qp_ref, yc_ref, g_ref, h_ref, y_ref) in enumerate(dirs):
        for ci in range(nb):
            c = ci if d == 0 else nb - 1 - ci
            rows = slice(c * CHUNK, (c + 1) * CHUNK)
            srows = slice(c * LANES, (c + 1) * LANES)
            for p in range(npairs):
                ls = slice(p * LANES, (p + 1) * LANES)
                st = st_ref[d, p]
                st_b = st.astype(BF16)
                y_ref[0, rows, ls] = _dot(qp_ref[0, 0, rows, ls].astype(BF16), st_b) + yc_ref[0, 0, rows, ls]
                st_ref[d, p] = _dot(g_ref[0, 0, srows, ls].astype(BF16), st_b) + h_ref[0, 0, srows, ls]


def _rw_scan(qp, y0, g, h, n_ctx, nb):
    _, b, t, d = qp.shape
    ppg = 4
    pw = ppg * LANES
    nblk = t // (nb * CHUNK)
    ncb = n_ctx // (nb * CHUNK)
    rows = nb * CHUNK
    srows = nb * LANES

    def rev(j):
        return jnp.where(j < ncb, ncb - 1 - j, nblk - 1 - (j - ncb))

    tok = lambda dd, nr: pl.BlockSpec(
        (1, 1, nr, pw), (lambda i, p, j: (0, i, j, p)) if dd == 0 else (lambda i, p, j: (1, i, rev(j), p)))
    return pl.pallas_call(
        functools.partial(_rw_scan_kernel, nb=nb, npairs=ppg),
        grid=(b, d // pw, nblk),
        in_specs=[tok(0, rows), tok(1, rows), tok(0, rows), tok(1, rows),
                  tok(0, srows), tok(1, srows), tok(0, srows), tok(1, srows)],
        out_specs=[
            pl.BlockSpec((1, rows, pw), lambda i, p, j: (i, j, p)),
            pl.BlockSpec((1, rows, pw), lambda i, p, j: (i, rev(j), p)),
        ],
        out_shape=[jax.ShapeDtypeStruct((b, t, d), F32), jax.ShapeDtypeStruct((b, t, d), F32)],
        scratch_shapes=[pltpu.VMEM((2, ppg, LANES, LANES), F32)],
        compiler_params=_cparams("parallel", "parallel", "arbitrary"),
        name="rw_scan",
    )(qp, qp, y0, y0, g, g, h, h)


def _rw_out_kernel(u_ref, y0_ref, y1_ref, r_ref, k_ref, v_ref, gate_ref, a0_ref, a1_ref,
                   ka_ref, rk_ref, lng_ref, lnb_ref, mx_ref, mz_ref, w_ref, out_ref, *, tm, n_ctx):
    row0 = pl.program_id(1) * tm
    lane_lo = lax.broadcasted_iota(jnp.int32, (1, LANES), 1) < RW_HEAD
    parts = []
    for p in range(D_MODEL // LANES):
        ls = slice(p * LANES, (p + 1) * LANES)
        y = y0_ref[0, :, ls] + y1_ref[0, :, ls]
        mu = _pair_sum(y, lane_lo) * (1.0 / RW_HEAD)
        yc = y - mu
        var = _pair_sum(yc * yc, lane_lo) * (1.0 / RW_HEAD)
        yn = yc * lax.rsqrt(var + RW_GN_EPS) * lng_ref[:, ls] + lnb_ref[:, ls]
        k = k_ref[0, :, ls]
        ka = ka_ref[:, ls]
        kd_sum = k * (1.0 + (a0_ref[0, :, ls] - 1.0) * ka) + k * (1.0 + (a1_ref[0, :, ls] - 1.0) * ka)
        bonus = _pair_sum(r_ref[0, :, ls] * kd_sum * rk_ref[:, ls], lane_lo) * v_ref[0, :, ls]
        parts.append(((yn + bonus) * gate_ref[0, :, ls]).astype(BF16))
    out_ref[0] = _gated_residual(u_ref[0], parts, w_ref, mx_ref, mz_ref, row0, n_ctx)


def _rw_out(u, y0, y1, r, k, v, gate, a, k_a, r_k, ln_g, ln_b, mx, mz, w_o, n_ctx):
    b, t, d = u.shape
    tm = _pick_tile(t, 272)
    tok = lambda blk: pl.BlockSpec((1, tm, d), lambda i, j: (i, j, blk))
    vec = _const_spec((1, d))
    return pl.pallas_call(
        functools.partial(_rw_out_kernel, tm=tm, n_ctx=n_ctx),
        grid=(b, t // tm),
        in_specs=[tok(0)] * 7 + [tok(0), tok(1), vec, vec, vec, vec,
                                 pl.BlockSpec((1, 6, d), lambda i, j: (i, 0, 0)),
                                 _const_spec((6, d)), _const_spec((d, d))],
        out_specs=tok(0),
        out_shape=jax.ShapeDtypeStruct((b, t, d), F32),
        compiler_params=_cparams("parallel", "parallel"),
        name="rw_out",
    )(u, y0, y1, r, k, v, gate, a, a, k_a.reshape(1, d), r_k.reshape(1, d), ln_g.reshape(1, d),
      ln_b.reshape(1, d), mx, mz, w_o)


def _block_rows(w_pair):
    _, k, n = w_pair.shape
    z = jnp.zeros((k, n), w_pair.dtype)
    return jnp.concatenate([jnp.concatenate([w_pair[0], z], axis=1),
                            jnp.concatenate([z, w_pair[1]], axis=1)], axis=0)


def kernel(x, c, ctx, c_ctx, ada_w, ada_b, norm1_g, norm2_g, mlp_w1, mlp_w2, final_g, na_w_qkv, na_w_o, na_rpb, gla_w_in, gla_w_dec2, gla_b_dec, gla_norm_g, gla_w_o, rw_mix, rw_w_rkv, rw_w0, rw_w1, rw_w2, rw_a0, rw_a1, rw_a2, rw_g1, rw_g2, rw_k_k, rw_k_a, rw_r_k, rw_ln_g, rw_ln_b, rw_w_o):
    b, seq, d = x.shape
    n_ctx = ctx.shape[1]
    depth = ada_w.shape[0]
    assert d == D_MODEL and n_ctx % CHUNK == 0 and seq % CHUNK == 0

    u = jnp.concatenate([ctx.astype(x.dtype), x], axis=1)

    rows = -(-(b + 1) // 8) * 8
    c_all = jnp.zeros((rows, d), F32).at[:b].set(c).at[b].set(c_ctx)
    tables = _ada_tables(c_all, ada_w, ada_b)
    mods_x = tables[:, :b].reshape(depth, b, 6, d)
    mods_z = tables[:, b].reshape(depth, 6, d)

    nb = max(n for n in (4, 2, 1) if (n_ctx // CHUNK) % n == 0 and (seq // CHUNK) % n == 0)

    for i in range(depth):
        kind, j = i % 3, i // 3
        last = i == depth - 1
        mx, mz = mods_x[i], mods_z[i]
        if kind == 0:
            qkv = _norm_proj(u, mx, mz, norm1_g[i], na_w_qkv[j].astype(BF16), n_ctx, BF16)
            o = _na_attention(qkv, _na_bias_table(na_rpb[j]), n_ctx, with_ctx=not last)
            u = _na_out(u, o, mx, mz, na_w_o[j].astype(BF16), n_ctx, latent_only=last)
        elif kind == 1:
            n_main = 2 * GLA_DK + 2 * GLA_DV
            w_in = gla_w_in[j]
            w_lr = jnp.zeros((d, LANES), F32).at[:, :2 * GLA_LOW_RANK].set(w_in[:, n_main:])
            w_dec = jnp.zeros((LANES, 2 * GLA_DK), F32).at[:2 * GLA_LOW_RANK].set(_block_rows(gla_w_dec2[j]))
            p, gates = _gla_proj(u, mx, mz, norm1_g[i], w_in[:, :n_main].astype(BF16), w_lr.astype(BF16),
                                 w_dec.astype(BF16), gla_b_dec[j].reshape(1, 2 * GLA_DK), n_ctx)
            o = _gla_scan(p, gates, n_ctx)
            u = _gla_out(u, o, p, gla_norm_g[j], mx, mz, gla_w_o[j].astype(BF16), n_ctx)
        else:
            g1 = jnp.zeros((d, 2 * LANES), F32).at[:, :RW_GATE_LORA].set(rw_g1[j])
            g2 = jnp.zeros((2 * LANES, d), F32).at[:RW_GATE_LORA].set(rw_g2[j])
            wts = dict(
                mix=rw_mix[j], w_rkv=rw_w_rkv[j].astype(BF16),
                w1=jnp.concatenate([rw_w1[j, 0], rw_w1[j, 1]], axis=1).astype(BF16),
                w2=_block_rows(rw_w2[j]).astype(BF16), w0=rw_w0[j].reshape(1, 2 * d),
                a1=jnp.concatenate([rw_a1[j, 0], rw_a1[j, 1]], axis=1).astype(BF16),
                a2=_block_rows(rw_a2[j]).astype(BF16), a0=rw_a0[j].reshape(1, 2 * d),
                g1=g1.astype(BF16), g2=g2.astype(BF16))
            r, k, v, gate, lw, a = _rw_proj(u, mx, mz, norm1_g[i], wts, n_ctx)
            qp, yc, gm, hm = _rw_terms(r, k, v, lw, a, rw_k_k[j], rw_k_a[j], nb)
            y0, y1 = _rw_scan(qp, yc, gm, hm, n_ctx, nb)
            u = _rw_out(u, y0, y1, r, k, v, gate, a, rw_k_a[j], rw_r_k[j], rw_ln_g[j], rw_ln_b[j],
                        mx, mz, rw_w_o[j].astype(BF16), n_ctx)
        if last and u.shape[1] != seq:
            u = u[:, n_ctx:]
        u = _mlp(u, mx, mz, norm2_g[i], mlp_w1[i].astype(BF16), mlp_w2[i].astype(BF16), final_g,
                 0 if last else n_ctx, final=last)
    return u
```

```python
import functools

import jax
import jax.numpy as jnp
from jax import lax
from jax.experimental import pallas as pl
from jax.experimental.pallas import tpu as pltpu

F32 = jnp.float32
BF16 = jnp.bfloat16

D_MODEL = 1024
D_FF = 4 * D_MODEL
NORM_EPS = 1e-6
GRID_W = 64

NA_HEAD_DIM = 64
NA_HEADS = D_MODEL // NA_HEAD_DIM
NA_WIN_R = 8
NA_WIN_C = 16
NA_MASK = -1e30

GLA_HEADS = 4
GLA_DK = D_MODEL // 2
GLA_DV = D_MODEL
GLA_HK = GLA_DK // GLA_HEADS
GLA_HV = GLA_DV // GLA_HEADS
GLA_LOW_RANK = 16
GLA_GATE_NORM = 16.0
CHUNK = 64

RW_HEAD = 64
RW_GATE_LORA = 160
RW_LORA = 64
RW_GN_EPS = 64e-5

LANES = 128
VMEM_LIMIT = 56 * 1024 * 1024


def _cparams(*sem):
    return pltpu.CompilerParams(dimension_semantics=sem, vmem_limit_bytes=VMEM_LIMIT)


def _pick_tile(n, target, mult=16):
    best = None
    for t in range(mult, min(n, target) + 1, mult):
        if n % t == 0:
            best = t
    assert best is not None, (n, target)
    return best


def _const_spec(shape):
    nd = len(shape)
    return pl.BlockSpec(shape, lambda *_: (0,) * nd, pipeline_mode=pl.Buffered(1))


def _dot(a, b):
    return jnp.dot(a, b, preferred_element_type=F32)


def _dot_nt(a, b):
    return lax.dot_general(a, b, (((1,), (1,)), ((), ())), preferred_element_type=F32)


def _sigmoid(x):
    return 1.0 / (1.0 + jnp.exp(-x))


def _softplus(x):
    return jnp.maximum(x, 0.0) + jnp.log(1.0 + jnp.exp(-jnp.abs(x)))


def _rms(u, g):
    return u * lax.rsqrt(jnp.mean(u * u, axis=-1, keepdims=True) + NORM_EPS) * g


def _mod_rows(mx, mz, idx, row0, rows, n_ctx):
    vx = mx[idx:idx + 1]
    if n_ctx == 0:
        return vx
    r = row0 + lax.broadcasted_iota(jnp.int32, (rows, 1), 0)
    return jnp.where(r < n_ctx, mz[idx:idx + 1], vx)


def _norm_mod(u, g, mx, mz, row0, n_ctx, i_shift, i_scale):
    rows = u.shape[0]
    shift = _mod_rows(mx, mz, i_shift, row0, rows, n_ctx)
    scale = _mod_rows(mx, mz, i_scale, row0, rows, n_ctx)
    return _rms(u, g) * (1.0 + scale) + shift


def _pair_sum(x, lane_lo):
    s0 = jnp.sum(jnp.where(lane_lo, x, 0.0), axis=-1, keepdims=True)
    s1 = jnp.sum(jnp.where(lane_lo, 0.0, x), axis=-1, keepdims=True)
    return jnp.where(lane_lo, s0, s1)


def _ada_kernel(c_ref, w_ref, b_ref, o_ref):
    c = c_ref[...]
    sc = (c * _sigmoid(c)).astype(BF16)
    o_ref[0] = _dot(sc, w_ref[0]) + b_ref[0]


def _ada_tables(c_all, ada_w, ada_b):
    depth, d, n = ada_w.shape
    rows = c_all.shape[0]
    tn = 1536
    return pl.pallas_call(
        _ada_kernel,
        grid=(depth, n // tn),
        in_specs=[
            pl.BlockSpec((rows, d), lambda l, j: (0, 0)),
            pl.BlockSpec((1, d, tn), lambda l, j: (l, 0, j)),
            pl.BlockSpec((1, 1, tn), lambda l, j: (l, 0, j)),
        ],
        out_specs=pl.BlockSpec((1, rows, tn), lambda l, j: (l, 0, j)),
        out_shape=jax.ShapeDtypeStruct((depth, rows, n), F32),
        compiler_params=_cparams("parallel", "parallel"),
        name="ada_tables",
    )(c_all, ada_w.astype(BF16), ada_b.reshape(depth, 1, n))


def _proj_kernel(u_ref, mx_ref, mz_ref, g_ref, w_ref, o_ref, *, tm, n_ctx, nsplit):
    t = pl.program_id(1)
    h = _norm_mod(u_ref[0], g_ref[...], mx_ref[0], mz_ref[...], t * tm, n_ctx, 0, 1).astype(BF16)
    n = w_ref.shape[1] // nsplit
    for s in range(nsplit):
        o_ref[0, :, s * n:(s + 1) * n] = _dot(h, w_ref[:, s * n:(s + 1) * n]).astype(o_ref.dtype)


def _norm_proj(u, mx, mz, g, w, n_ctx, out_dtype):
    b, t, d = u.shape
    n = w.shape[1]
    tm = _pick_tile(t, 544)
    return pl.pallas_call(
        functools.partial(_proj_kernel, tm=tm, n_ctx=n_ctx, nsplit=n // D_MODEL),
        grid=(b, t // tm),
        in_specs=[
            pl.BlockSpec((1, tm, d), lambda i, j: (i, j, 0)),
            pl.BlockSpec((1, 6, d), lambda i, j: (i, 0, 0)),
            _const_spec((6, d)),
            _const_spec((1, d)),
            _const_spec((d, n)),
        ],
        out_specs=pl.BlockSpec((1, tm, n), lambda i, j: (i, j, 0)),
        out_shape=jax.ShapeDtypeStruct((b, t, n), out_dtype),
        compiler_params=_cparams("parallel", "parallel"),
        name="norm_proj",
    )(u, mx, mz, g.reshape(1, d), w)


def _na_bias_table(rpb):
    w = GRID_W
    cols = jnp.arange(w)
    c_start = jnp.clip(cols - NA_WIN_C // 2, 0, w - NA_WIN_C)
    col_ok = (cols[None, :] >= c_start[:, None]) & (cols[None, :] < c_start[:, None] + NA_WIN_C)
    col_idx = jnp.clip(cols[None, :] - cols[:, None], -(NA_WIN_C - 1), NA_WIN_C - 1) + NA_WIN_C - 1
    onehot = col_idx[:, :, None] == jnp.arange(2 * NA_WIN_C - 1)
    m = jnp.sum(jnp.where(onehot[None, None], rpb.astype(F32)[:, :, None, None, :], 0.0), axis=-1)
    m = jnp.where(col_ok[None, None], m, NA_MASK)
    return jnp.concatenate([m[:, :-1], m[:, 1:]], axis=-1)


def _na_kernel(q_ref, k_ref, v_ref, bias_ref, o_ref, *, n_ctx, n_rows, rb_off, with_ctx, rq):
    rb0 = pl.program_id(2) * rq + rb_off
    nctxb = n_ctx // GRID_W
    scale = NA_HEAD_DIM ** -0.5
    lane_lo = lax.broadcasted_iota(jnp.int32, (1, LANES), 1) < NA_HEAD_DIM
    head_masks = (jnp.where(lane_lo, 1.0, 0.0).astype(BF16), jnp.where(lane_lo, 0.0, 1.0).astype(BF16))
    n_pairs = q_ref.shape[2] // LANES
    strip = NA_WIN_R * GRID_W
    jobs_idx = [(qi, p) for qi in range(rq) for p in range(n_pairs)]

    def stacked_q(qi, p):
        q2 = q_ref[0, qi * GRID_W:(qi + 1) * GRID_W, p * LANES:(p + 1) * LANES]
        return jnp.concatenate([q2 * head_masks[0], q2 * head_masks[1]], axis=0)

    def attend(jobs):
        scores = [[_dot_nt(q, k) * scale if b is None else _dot_nt(q, k) * scale + b
                   for k, b in zip(ks, bs)] for q, ks, _, bs in jobs]
        tops = []
        for sc in scores:
            m = sc[0].max(axis=-1, keepdims=True)
            for s in sc[1:]:
                m = jnp.maximum(m, s.max(axis=-1, keepdims=True))
            tops.append(m)
        exps = [[jnp.exp(s - m) for s in sc] for sc, m in zip(scores, tops)]
        dens = []
        for es in exps:
            den = es[0].sum(axis=-1, keepdims=True)
            for e in es[1:]:
                den = den + e.sum(axis=-1, keepdims=True)
            dens.append(den)
        for (qi, p), (_, _, vs, _), es, den in zip(jobs_idx, jobs, exps, dens):
            o = _dot(es[0].astype(BF16), vs[0])
            for e, v in zip(es[1:], vs[1:]):
                o = o + _dot(e.astype(BF16), v)
            o = o / den
            o2 = jnp.where(lane_lo, o[:GRID_W], o[GRID_W:])
            o_ref[0, qi * GRID_W:(qi + 1) * GRID_W, p * LANES:(p + 1) * LANES] = o2.astype(o_ref.dtype)

    def latent_rows():
        jobs = []
        for qi, p in jobs_idx:
            ls = slice(p * LANES, (p + 1) * LANES)
            r = rb0 + qi - nctxb
            r0 = jnp.clip(r - NA_WIN_R // 2, 0, n_rows - NA_WIN_R)
            start = pl.multiple_of(n_ctx + r0 * GRID_W, GRID_W)
            ri0 = NA_WIN_R - 1 - (r - r0)
            bias = jnp.concatenate(
                [jnp.concatenate([bias_ref[2 * p + hh, ri0 + 2 * m] for m in range(NA_WIN_R // 2)], axis=1)
                 for hh in range(2)], axis=0)
            jobs.append((stacked_q(qi, p),
                         [k_ref[0, pl.ds(start, strip), ls], k_ref[0, 0:n_ctx, ls]],
                         [v_ref[0, pl.ds(start, strip), ls], v_ref[0, 0:n_ctx, ls]],
                         [bias, None]))
        attend(jobs)

    def context_rows():
        jobs = []
        for qi, p in jobs_idx:
            ls = slice(p * LANES, (p + 1) * LANES)
            jobs.append((stacked_q(qi, p), [k_ref[0, 0:n_ctx, ls]], [v_ref[0, 0:n_ctx, ls]], [None]))
        attend(jobs)

    if with_ctx:
        pl.when(rb0 < nctxb)(context_rows)
        pl.when(rb0 >= nctxb)(latent_rows)
    else:
        latent_rows()


def _na_attention(qkv, bias, n_ctx, with_ctx):
    b, t, _ = qkv.shape
    seq = t - n_ctx
    n_rows = seq // GRID_W
    assert seq % GRID_W == 0 and n_ctx % GRID_W == 0 and n_rows >= NA_WIN_R
    hw = 2 * LANES
    ng = D_MODEL // hw
    nctxb = n_ctx // GRID_W
    rq = 2
    assert nctxb % rq == 0 and n_rows % rq == 0
    rb_off = 0 if with_ctx else nctxb
    nrb = (t // GRID_W - rb_off) // rq
    t_out = t if with_ctx else seq
    qrows = rq * GRID_W

    return pl.pallas_call(
        functools.partial(_na_kernel, n_ctx=n_ctx, n_rows=n_rows, rb_off=rb_off, with_ctx=with_ctx, rq=rq),
        grid=(b, ng, nrb),
        in_specs=[
            pl.BlockSpec((1, qrows, hw), lambda i, g, j: (i, j + rb_off // rq, g)),
            pl.BlockSpec((1, t, hw), lambda i, g, j: (i, 0, ng + g)),
            pl.BlockSpec((1, t, hw), lambda i, g, j: (i, 0, 2 * ng + g)),
            pl.BlockSpec((hw // NA_HEAD_DIM, 2 * NA_WIN_R - 2, GRID_W, LANES), lambda i, g, j: (g, 0, 0, 0)),
        ],
        out_specs=pl.BlockSpec((1, qrows, hw), lambda i, g, j: (i, j, g)),
        out_shape=jax.ShapeDtypeStruct((b, t_out, D_MODEL), BF16),
        compiler_params=_cparams("parallel", "parallel", "arbitrary"),
        name="na_attention",
    )(qkv, qkv, qkv, bias)


def _gated_residual(u, o_bf16_parts, w_ref, mx_ref, mz_ref, row0, n_ctx):
    acc = None
    k0 = 0
    for part in o_bf16_parts:
        kw = part.shape[1]
        term = _dot(part, w_ref[k0:k0 + kw, :])
        acc = term if acc is None else acc + term
        k0 += kw
    gate = _mod_rows(mx_ref[0], mz_ref[...], 2, row0, u.shape[0], n_ctx)
    return u + gate * acc


def _na_out_kernel(u_ref, o_ref, mx_ref, mz_ref, w_ref, out_ref, *, tm, n_ctx, blk_off):
    row0 = (pl.program_id(1) + blk_off) * tm
    out_ref[0] = _gated_residual(u_ref[0], [o_ref[0]], w_ref, mx_ref, mz_ref, row0, n_ctx)


def _na_out(u, o, mx, mz, w_o, n_ctx, latent_only):
    b, t, d = u.shape
    if latent_only:
        tm = _pick_tile(n_ctx, 256)
        assert (t - n_ctx) % tm == 0
        blk_off = n_ctx // tm
        nt = (t - n_ctx) // tm
    else:
        tm = _pick_tile(t, 544)
        blk_off = 0
        nt = t // tm
    return pl.pallas_call(
        functools.partial(_na_out_kernel, tm=tm, n_ctx=n_ctx, blk_off=blk_off),
        grid=(b, nt),
        in_specs=[
            pl.BlockSpec((1, tm, d), lambda i, j: (i, j + blk_off, 0)),
            pl.BlockSpec((1, tm, d), lambda i, j: (i, j, 0)),
            pl.BlockSpec((1, 6, d), lambda i, j: (i, 0, 0)),
            _const_spec((6, d)),
            _const_spec((d, d)),
        ],
        out_specs=pl.BlockSpec((1, tm, d), lambda i, j: (i, j, 0)),
        out_shape=jax.ShapeDtypeStruct((b, nt * tm, d), F32),
        compiler_params=_cparams("parallel", "parallel"),
        name="na_out",
    )(u, o, mx, mz, w_o)


def _mlp_kernel(u_ref, mx_ref, mz_ref, g_ref, w1_ref, w2_ref, fg_ref, out_ref, *, tm, n_ctx, fc, final):
    row0 = pl.program_id(1) * tm
    u = u_ref[0]
    mx = mx_ref[0]
    mz = mz_ref[...]
    h = _norm_mod(u, g_ref[...], mx, mz, row0, n_ctx, 3, 4).astype(BF16)
    acc = None
    for c in range(D_FF // fc):
        a = _dot(h, w1_ref[:, c * fc:(c + 1) * fc])
        a = jnp.square(jnp.maximum(a, 0.0)).astype(BF16)
        term = _dot(a, w2_ref[c * fc:(c + 1) * fc, :])
        acc = term if acc is None else acc + term
    y = u + _mod_rows(mx, mz, 5, row0, tm, n_ctx) * acc
    if final:
        y = _rms(y, fg_ref[...])
    out_ref[0] = y


def _mlp(u, mx, mz, g, w1, w2, final_g, n_ctx, final):
    b, t, d = u.shape
    tm = _pick_tile(t, 544)
    return pl.pallas_call(
        functools.partial(_mlp_kernel, tm=tm, n_ctx=n_ctx, fc=1024, final=final),
        grid=(b, t // tm),
        in_specs=[
            pl.BlockSpec((1, tm, d), lambda i, j: (i, j, 0)),
            pl.BlockSpec((1, 6, d), lambda i, j: (i, 0, 0)),
            _const_spec((6, d)),
            _const_spec((1, d)),
            _const_spec((d, D_FF)),
            _const_spec((D_FF, d)),
            _const_spec((1, d)),
        ],
        out_specs=pl.BlockSpec((1, tm, d), lambda i, j: (i, j, 0)),
        out_shape=jax.ShapeDtypeStruct((b, t, d), F32),
        compiler_params=_cparams("parallel", "parallel"),
        name="mlp",
    )(u, mx, mz, g.reshape(1, d), w1, w2, final_g.reshape(1, d))


def _gla_proj_kernel(u_ref, mx_ref, mz_ref, g_ref, w_ref, wlr_ref, wdec_ref, bdec_ref,
                     p_ref, gate_ref, *, tm, n_ctx):
    t = pl.program_id(1)
    h = _norm_mod(u_ref[0], g_ref[...], mx_ref[0], mz_ref[...], t * tm, n_ctx, 0, 1).astype(BF16)
    n = D_MODEL
    for s in range(w_ref.shape[1] // n):
        p_ref[0, :, s * n:(s + 1) * n] = _dot(h, w_ref[:, s * n:(s + 1) * n])
    lr = _dot(h, wlr_ref[...]).astype(BF16)
    z = _dot(lr, wdec_ref[...]) + bdec_ref[...]
    gate_ref[0] = -_softplus(-z) * (1.0 / GLA_GATE_NORM)


def _gla_proj(u, mx, mz, g, w_main, w_lr, w_dec, b_dec, n_ctx):
    b, t, d = u.shape
    n = w_main.shape[1]
    tm = _pick_tile(t, 544)
    return pl.pallas_call(
        functools.partial(_gla_proj_kernel, tm=tm, n_ctx=n_ctx),
        grid=(b, t // tm),
        in_specs=[
            pl.BlockSpec((1, tm, d), lambda i, j: (i, j, 0)),
            pl.BlockSpec((1, 6, d), lambda i, j: (i, 0, 0)),
            _const_spec((6, d)),
            _const_spec((1, d)),
            _const_spec((d, n)),
            _const_spec((d, LANES)),
            _const_spec((LANES, 2 * GLA_DK)),
            _const_spec((1, 2 * GLA_DK)),
        ],
        out_specs=[
            pl.BlockSpec((1, tm, n), lambda i, j: (i, j, 0)),
            pl.BlockSpec((1, tm, 2 * GLA_DK), lambda i, j: (i, j, 0)),
        ],
        out_shape=[
            jax.ShapeDtypeStruct((b, t, n), F32),
            jax.ShapeDtypeStruct((b, t, 2 * GLA_DK), F32),
        ],
        compiler_params=_cparams("parallel", "parallel"),
        name="gla_proj",
    )(u, mx, mz, g.reshape(1, d), w_main, w_lr, w_dec, b_dec)


def _tri(n, upper, strict):
    r = lax.broadcasted_iota(jnp.int32, (n, n), 0)
    c = lax.broadcasted_iota(jnp.int32, (n, n), 1)
    if upper:
        return (r < c) if strict else (r <= c)
    return (r > c) if strict else (r >= c)


def _split3(x):
    hi = x.astype(BF16)
    r1 = x - hi.astype(F32)
    mid = r1.astype(BF16)
    lo = (r1 - mid.astype(F32)).astype(BF16)
    return hi, mid, lo


def _tri_cumsum(tri_b, x):
    hi, mid, lo = _split3(x)
    return _dot(tri_b, hi) + _dot(tri_b, mid) + _dot(tri_b, lo)


def _gla_scan_kernel(q_ref, k_ref, v_ref, g0_ref, g1_ref, o_ref, s_ref, *, n_ctx_chunks, n_chunks, nb):
    qscale = GLA_HK ** -0.5
    g_refs = (g0_ref, g1_ref)
    incl = (_tri(CHUNK, False, False), _tri(CHUNK, True, False))
    tri_b = tuple(jnp.where(m, 1.0, 0.0).astype(BF16) for m in incl)
    o_ref[...] = jnp.zeros_like(o_ref)

    def step(first):
        units = [(d, first[d] + (i if d == 0 else -i)) for i in range(nb) for d in range(2)]
        rows = [pl.ds(pl.multiple_of(c * CHUNK, CHUNK), CHUNK) for _, c in units]
        q = [q_ref[0, r, :] for r in rows]
        k = [k_ref[0, r, :] for r in rows]
        v = [v_ref[0, r, :].astype(BF16) for r in rows]
        g = [g_refs[d][0, r, :] for (d, _), r in zip(units, rows)]
        bcum = [_tri_cumsum(tri_b[d], gi) for (d, _), gi in zip(units, g)]
        b_last_row = [jnp.sum(gi, axis=0, keepdims=True) for gi in g]
        dec_col = [jnp.exp(jnp.sum(gi.T, axis=1, keepdims=True)) for gi in g]
        q_e = [(qi * jnp.exp(bi) * qscale).astype(BF16) for qi, bi in zip(q, bcum)]
        k_e = [(ki * jnp.exp(-bi)).astype(BF16) for ki, bi in zip(k, bcum)]
        k_dec_t = [(ki * jnp.exp(bl - bi)).T.astype(BF16) for ki, bl, bi in zip(k, b_last_row, bcum)]
        a = [jnp.where(incl[d], _dot_nt(qe, ke), 0.0).astype(BF16) for (d, _), qe, ke in zip(units, q_e, k_e)]
        o_intra = [_dot(ai, vi) for ai, vi in zip(a, v)]
        s_inc = [_dot(kt, vi) for kt, vi in zip(k_dec_t, v)]
        s = [s_ref[0], s_ref[1]]
        for i, (d, _) in enumerate(units):
            o_ref[0, rows[i], :] += o_intra[i] + _dot(q_e[i], s[d].astype(BF16))
            s[d] = dec_col[i] * s[d] + s_inc[i]
        s_ref[0] = s[0]
        s_ref[1] = s[1]

    s_ref[...] = jnp.zeros_like(s_ref)
    n_lat = n_chunks - n_ctx_chunks

    def ctx_body(i, carry):
        step((i * nb, n_ctx_chunks - 1 - i * nb))
        return carry

    def lat_body(i, carry):
        step((n_ctx_chunks + i * nb, n_chunks - 1 - i * nb))
        return carry

    lax.fori_loop(0, n_ctx_chunks // nb, ctx_body, 0)
    lax.fori_loop(0, n_lat // nb, lat_body, 0)


def _gla_scan(p, gates, n_ctx, nb):
    b, t, _ = p.shape
    nkb = GLA_DK // GLA_HK
    return pl.pallas_call(
        functools.partial(_gla_scan_kernel, n_ctx_chunks=n_ctx // CHUNK, n_chunks=t // CHUNK, nb=nb),
        grid=(b, GLA_HEADS),
        in_specs=[
            pl.BlockSpec((1, t, GLA_HK), lambda i, h: (i, 0, h)),
            pl.BlockSpec((1, t, GLA_HK), lambda i, h: (i, 0, nkb + h)),
            pl.BlockSpec((1, t, GLA_HV), lambda i, h: (i, 0, (2 * GLA_DK) // GLA_HV + h)),
            pl.BlockSpec((1, t, GLA_HK), lambda i, h: (i, 0, h)),
            pl.BlockSpec((1, t, GLA_HK), lambda i, h: (i, 0, nkb + h)),
        ],
        out_specs=pl.BlockSpec((1, t, GLA_HV), lambda i, h: (i, 0, h)),
        out_shape=jax.ShapeDtypeStruct((b, t, GLA_DV), F32),
        scratch_shapes=[pltpu.VMEM((2, GLA_HK, GLA_HV), F32)],
        compiler_params=_cparams("parallel", "parallel"),
        name="gla_scan",
    )(p, p, p, gates, gates)


def _gla_out_kernel(u_ref, o_ref, gt_ref, ng_ref, mx_ref, mz_ref, w_ref, out_ref, *, tm, n_ctx):
    row0 = pl.program_id(1) * tm
    parts = []
    for h in range(GLA_HEADS):
        ls = slice(h * GLA_HV, (h + 1) * GLA_HV)
        ov = o_ref[0, :, ls]
        ov = ov * lax.rsqrt(jnp.mean(ov * ov, axis=-1, keepdims=True) + NORM_EPS) * ng_ref[...]
        gt = gt_ref[0, :, ls]
        parts.append((ov * (gt * _sigmoid(gt))).astype(BF16))
    out_ref[0] = _gated_residual(u_ref[0], parts, w_ref, mx_ref, mz_ref, row0, n_ctx)


def _gla_out(u, o, p, norm_g, mx, mz, w_o, n_ctx):
    b, t, d = u.shape
    tm = _pick_tile(t, 544)
    gt_blk = (2 * GLA_DK + GLA_DV) // d
    return pl.pallas_call(
        functools.partial(_gla_out_kernel, tm=tm, n_ctx=n_ctx),
        grid=(b, t // tm),
        in_specs=[
            pl.BlockSpec((1, tm, d), lambda i, j: (i, j, 0)),
            pl.BlockSpec((1, tm, d), lambda i, j: (i, j, 0)),
            pl.BlockSpec((1, tm, d), lambda i, j: (i, j, gt_blk)),
            _const_spec((1, GLA_HV)),
            pl.BlockSpec((1, 6, d), lambda i, j: (i, 0, 0)),
            _const_spec((6, d)),
            _const_spec((d, d)),
        ],
        out_specs=pl.BlockSpec((1, tm, d), lambda i, j: (i, j, 0)),
        out_shape=jax.ShapeDtypeStruct((b, t, d), F32),
        compiler_params=_cparams("parallel", "parallel"),
        name="gla_out",
    )(u, o, p, norm_g.reshape(1, GLA_HV), mx, mz, w_o)


def _rw_proj_kernel(u_ref, up_ref, un_ref, mx_ref, mz_ref, g_ref, mix_ref, wrkv_ref, w1_ref, w2_ref,
                    w0_ref, a1_ref, a2_ref, a0_ref, g1_ref, g2_ref,
                    r_ref, k_ref, v_ref, gate_ref, lw_ref, a_ref, *, tm, n_ctx, t_total):
    row0 = pl.program_id(1) * tm
    g = g_ref[...]
    mx = mx_ref[0]
    mz = mz_ref[...]
    h = _norm_mod(u_ref[0], g, mx, mz, row0, n_ctx, 0, 1)
    h_prev = _norm_mod(up_ref[0], g, mx, mz, row0 - 8, n_ctx, 0, 1)[7:8]
    h_next = _norm_mod(un_ref[0], g, mx, mz, row0 + tm, n_ctx, 0, 1)[0:1]
    idx = lax.broadcasted_iota(jnp.int32, (tm, 1), 0)
    rows = row0 + idx
    h_dn = jnp.where(idx == 0, h_prev, pltpu.roll(h, 1, 0))
    h_dn = jnp.where((rows == 0) | (rows == n_ctx), 0.0, h_dn)
    h_up = jnp.where(idx == tm - 1, h_next, pltpu.roll(h, tm - 1, 0))
    h_up = jnp.where((rows == n_ctx - 1) | (rows == t_total - 1), 0.0, h_up)
    xx = 0.5 * (h_dn + h_up) - h
    xr, xw, xk, xv, xa, xg = ((h + xx * mix_ref[m:m + 1]).astype(BF16) for m in range(6))
    r_ref[0] = _dot(xr, wrkv_ref[0])
    k_ref[0] = _dot(xk, wrkv_ref[1])
    v_ref[0] = _dot(xv, wrkv_ref[2])
    gate_ref[0] = _dot(_sigmoid(_dot(xg, g1_ref[...])).astype(BF16), g2_ref[...])
    tw = jnp.tanh(_dot(xw, w1_ref[...])).astype(BF16)
    w_log = -_softplus(-(w0_ref[...] + _dot(tw, w2_ref[...]))) - 0.5
    lw_ref[0] = -jnp.exp(w_log)
    ta = _dot(xa, a1_ref[...]).astype(BF16)
    a_ref[0] = _sigmoid(a0_ref[...] + _dot(ta, a2_ref[...]))


def _rw_proj(u, mx, mz, g, wts, n_ctx):
    b, t, d = u.shape
    tm = _pick_tile(t, 272)
    nb8 = t // 8
    tb = tm // 8
    full = lambda n: jax.ShapeDtypeStruct((b, t, n), F32)
    row_spec = lambda n: pl.BlockSpec((1, tm, n), lambda i, j: (i, j, 0))
    return pl.pallas_call(
        functools.partial(_rw_proj_kernel, tm=tm, n_ctx=n_ctx, t_total=t),
        grid=(b, t // tm),
        in_specs=[
            row_spec(d),
            pl.BlockSpec((1, 8, d), lambda i, j: (i, jnp.maximum(j * tb - 1, 0), 0)),
            pl.BlockSpec((1, 8, d), lambda i, j: (i, jnp.minimum((j + 1) * tb, nb8 - 1), 0)),
            pl.BlockSpec((1, 6, d), lambda i, j: (i, 0, 0)),
            _const_spec((6, d)),
            _const_spec((1, d)),
            _const_spec((6, d)),
            _const_spec((3, d, d)),
            _const_spec((d, 2 * RW_LORA)),
            _const_spec((2 * RW_LORA, 2 * d)),
            _const_spec((1, 2 * d)),
            _const_spec((d, 2 * RW_LORA)),
            _const_spec((2 * RW_LORA, 2 * d)),
            _const_spec((1, 2 * d)),
            _const_spec((d, 2 * LANES)),
            _const_spec((2 * LANES, d)),
        ],
        out_specs=[row_spec(d), row_spec(d), row_spec(d), row_spec(d), row_spec(2 * d), row_spec(2 * d)],
        out_shape=[full(d), full(d), full(d), full(d), full(2 * d), full(2 * d)],
        compiler_params=_cparams("parallel", "parallel"),
        name="rw_proj",
    )(u, u, u, mx, mz, g.reshape(1, d), wts["mix"], wts["w_rkv"], wts["w1"], wts["w2"], wts["w0"],
      wts["a1"], wts["a2"], wts["a0"], wts["g1"], wts["g2"])


def _rw_pair_terms(units, kkw, kaw, side_work=()):
    C = CHUNK
    lane_lo = lax.broadcasted_iota(jnp.int32, (1, LANES), 1) < RW_HEAD
    row2 = lax.broadcasted_iota(jnp.int32, (C, LANES), 0)
    col2 = lax.broadcasted_iota(jnp.int32, (C, LANES), 1) & (C - 1)
    strict2 = (row2 > col2, row2 < col2)
    incl2 = (row2 >= col2, row2 <= col2)
    tri_b = tuple(jnp.where(_tri(C, upper, False), 1.0, 0.0).astype(BF16) for upper in (False, True))
    eye2 = jnp.where(row2 == col2, 1.0, 0.0)
    nu = len(units)

    def bd(x):
        zero = jnp.zeros_like(x)
        return jnp.concatenate([jnp.where(lane_lo, x, zero), jnp.where(lane_lo, zero, x)], axis=0)

    def hilo(x):
        hi = x.astype(BF16)
        return hi, (x - hi.astype(F32)).astype(BF16)

    side = iter(side_work)

    def stage_done():
        thunk = next(side, None)
        if thunk is not None:
            thunk()

    pre = []
    for r2, k2, v2, lw2, a2, desc in units:
        d = int(desc)
        kk = k2 * kkw
        kk = kk * lax.rsqrt(jnp.maximum(_pair_sum(kk * kk, lane_lo), 1e-24))
        kd = k2 * (1.0 + (a2 - 1.0) * kaw)
        bvec = kk * a2
        cum = _tri_cumsum(tri_b[d], lw2)
        tot = jnp.sum(lw2, axis=0, keepdims=True)
        e_neg = jnp.exp(-cum)
        e_rem = jnp.exp(tot - cum)
        pre.append(dict(
            d=d, tot=tot, v_b=v2.astype(BF16),
            a_t=-kk * jnp.exp(cum - lw2), r_t=r2 * jnp.exp(cum),
            b_h=(bvec * e_neg).astype(BF16), k_h=(kd * e_neg).astype(BF16),
            b_rem_t=(bvec * e_rem).T.astype(BF16), k_rem_t=(kd * e_rem).T.astype(BF16)))

    ar = [jnp.concatenate([p["a_t"], p["r_t"]], axis=0).astype(BF16) for p in pre]
    p_b = [_dot_nt(ar[u], bd(pre[u]["b_h"])) for u in range(nu)]
    p_k = [_dot_nt(ar[u], bd(pre[u]["k_h"])) for u in range(nu)]
    l_ab = [jnp.where(strict2[pre[u]["d"]], p_b[u][:C], 0.0) for u in range(nu)]
    m_rb = [jnp.where(incl2[pre[u]["d"]], p_b[u][C:], 0.0).astype(BF16) for u in range(nu)]
    l_ak = [jnp.where(strict2[pre[u]["d"]], p_k[u][:C], 0.0).astype(BF16) for u in range(nu)]
    m_rk = [jnp.where(incl2[pre[u]["d"]], p_k[u][C:], 0.0).astype(BF16) for u in range(nu)]
    stage_done()

    l_hl = [hilo(l) for l in l_ab]
    r0 = [_dot(jnp.concatenate([hi, lo], axis=0), bd(hi)) for hi, lo in l_hl]
    r0b = [_dot(hi, bd(lo)) for hi, lo in l_hl]
    pw = [a[:C] + a[C:] + b for a, b in zip(r0, r0b)]
    xo = l_ab
    lhs_hl = l_hl
    for _ in range(2):
        p_hl = [hilo(p) for p in pw]
        ra = [_dot(jnp.concatenate([ph, plo, xh, xlo], axis=0), bd(ph))
              for (ph, plo), (xh, xlo) in zip(p_hl, lhs_hl)]
        rb = [_dot(jnp.concatenate([ph, xh], axis=0), bd(plo)) for (ph, plo), (xh, _) in zip(p_hl, lhs_hl)]
        xo = [x + p + (a[2 * C:3 * C] + a[3 * C:] + b[C:]) for x, p, a, b in zip(xo, pw, ra, rb)]
        pw = [a[:C] + a[C:2 * C] + b[:C] for a, b in zip(ra, rb)]
        lhs_hl = [hilo(x) for x in xo]
        stage_done()
    for _ in range(2):
        p_b16 = [p.astype(BF16) for p in pw]
        ra = [_dot(jnp.concatenate([p, x.astype(BF16)], axis=0), bd(p)) for p, x in zip(p_b16, xo)]
        xo = [x + p + a[C:] for x, p, a in zip(xo, pw, ra)]
        pw = [a[:C] for a in ra]
        stage_done()
    xo = [x + p + _dot(x.astype(BF16), bd(p.astype(BF16))) for x, p in zip(xo, pw)]
    t_inv = [(eye2 + x).astype(BF16) for x in xo]

    lvyk = [_dot(jnp.concatenate([l_ak[u], m_rk[u]], axis=0), bd(pre[u]["v_b"])) for u in range(nu)]
    kv = [_dot(pre[u]["k_rem_t"], pre[u]["v_b"]) for u in range(nu)]
    tw = [_dot(t_inv[u], jnp.concatenate([bd(ar[u][:C]), bd(lvyk[u][:C].astype(BF16))], axis=1))
          for u in range(nu)]
    qy = [_dot(m_rb[u], jnp.concatenate([bd(tw[u][:, :LANES].astype(BF16)),
                                         bd(tw[u][:, LANES:].astype(BF16))], axis=1)) for u in range(nu)]
    gh = [_dot(pre[u]["b_rem_t"], tw[u].astype(BF16)) for u in range(nu)]

    r = lax.broadcasted_iota(jnp.int32, (LANES, LANES), 0)
    c = lax.broadcasted_iota(jnp.int32, (LANES, LANES), 1)
    same_head = (r < RW_HEAD) == (c < RW_HEAD)
    out = []
    for u in range(nu):
        qp = pre[u]["r_t"] + qy[u][:, :LANES]
        y0 = qy[u][:, LANES:] + lvyk[u][C:]
        g_mat = jnp.where(same_head, gh[u][:, :LANES], 0.0) + jnp.where(r == c, jnp.exp(pre[u]["tot"]), 0.0)
        h_mat = jnp.where(same_head, gh[u][:, LANES:] + kv[u], 0.0)
        out.append((qp, y0, g_mat, h_mat))
    return out


def _rw_mix_kernel(r0_ref, k0_ref, v0_ref, lw0_ref, a0_ref, r1_ref, k1_ref, v1_ref, lw1_ref, a1_ref,
                   kk_ref, ka_ref, y0_ref, y1_ref, st_ref, qp_s, yc_s, g_s, h_s, *, nb):
    def zero_scratch(_, carry):
        for ref in (st_ref, qp_s, yc_s, g_s, h_s):
            ref[...] = jnp.zeros_like(ref)
        return carry

    lax.fori_loop(0, jnp.where(pl.program_id(2) == 0, 1, 0), zero_scratch, 0)

    y_refs = (y0_ref, y1_ref)
    states = [st_ref[0], st_ref[1]]
    prev_terms = [[(qp_s[d, c], yc_s[d, c], g_s[d, c], h_s[d, c]) for c in range(nb)] for d in range(2)]

    def link(ci):
        def emit():
            for d in range(2):
                c = ci if d == 0 else nb - 1 - ci
                qp, yc, g_mat, h_mat = prev_terms[d][c]
                st_b = states[d].astype(BF16)
                y_refs[d][0, c * CHUNK:(c + 1) * CHUNK, :] = _dot(qp, st_b) + yc
                states[d] = _dot(g_mat, st_b) + h_mat
        return emit

    ins = ((r0_ref, k0_ref, v0_ref, lw0_ref, a0_ref), (r1_ref, k1_ref, v1_ref, lw1_ref, a1_ref))
    units = []
    for c in range(nb):
        rows = slice(c * CHUNK, (c + 1) * CHUNK)
        for d in range(2):
            units.append(tuple(ref[0, rows, :] for ref in ins[d]) + (d == 1,))
    links = [link(ci) for ci in range(nb)]
    assert nb <= 5, "one recurrence link per stage boundary of _rw_pair_terms"
    results = _rw_pair_terms(units, kk_ref[...], ka_ref[...], side_work=links)
    st_ref[0] = states[0]
    st_ref[1] = states[1]
    for i, (qp, y0, g_mat, h_mat) in enumerate(results):
        c, d = divmod(i, 2)
        qp_s[d, c] = qp.astype(BF16)
        yc_s[d, c] = y0
        g_s[d, c] = g_mat.astype(BF16)
        h_s[d, c] = h_mat


def _rw_mix(r, k, v, lw, a, k_k, k_a, n_ctx, nb):
    b, t, d = r.shape
    npairs = d // LANES
    rows = nb * CHUNK
    nblk = t // rows
    ncb = n_ctx // rows

    def rev(j):
        return jnp.where(j < ncb, ncb - 1 - j, nblk - 1 - (j - ncb))

    cur = lambda j: jnp.minimum(j, nblk - 1)
    prev = lambda j: jnp.maximum(j - 1, 0)
    fwd = lambda off, blk: pl.BlockSpec((1, rows, LANES), lambda i, p, j: (i, blk(j), off + p))
    bwd = lambda off, blk: pl.BlockSpec((1, rows, LANES), lambda i, p, j: (i, rev(blk(j)), off + p))
    par = pl.BlockSpec((1, LANES), lambda i, p, j: (0, p))
    return pl.pallas_call(
        functools.partial(_rw_mix_kernel, nb=nb),
        grid=(b, npairs, nblk + 1),
        in_specs=[fwd(0, cur), fwd(0, cur), fwd(0, cur), fwd(0, cur), fwd(0, cur),
                  bwd(0, cur), bwd(0, cur), bwd(0, cur), bwd(npairs, cur), bwd(npairs, cur), par, par],
        out_specs=[fwd(0, prev), bwd(0, prev)],
        out_shape=[jax.ShapeDtypeStruct((b, t, d), F32), jax.ShapeDtypeStruct((b, t, d), F32)],
        scratch_shapes=[pltpu.VMEM((2, LANES, LANES), F32),
                        pltpu.VMEM((2, nb, CHUNK, LANES), BF16), pltpu.VMEM((2, nb, CHUNK, LANES), F32),
                        pltpu.VMEM((2, nb, LANES, LANES), BF16), pltpu.VMEM((2, nb, LANES, LANES), F32)],
        compiler_params=_cparams("parallel", "parallel", "arbitrary"),
        name="rw_mix",
    )(r, k, v, lw, a, r, k, v, lw, a, k_k.reshape(1, d), k_a.reshape(1, d))


def _rw_out_kernel(u_ref, y0_ref, y1_ref, r_ref, k_ref, v_ref, gate_ref, a0_ref, a1_ref,
                   ka_ref, rk_ref, lng_ref, lnb_ref, mx_ref, mz_ref, w_ref, out_ref, *, tm, n_ctx):
    row0 = pl.program_id(1) * tm
    lane_lo = lax.broadcasted_iota(jnp.int32, (1, LANES), 1) < RW_HEAD
    parts = []
    for p in range(D_MODEL // LANES):
        ls = slice(p * LANES, (p + 1) * LANES)
        y = y0_ref[0, :, ls] + y1_ref[0, :, ls]
        mu = _pair_sum(y, lane_lo) * (1.0 / RW_HEAD)
        yc = y - mu
        var = _pair_sum(yc * yc, lane_lo) * (1.0 / RW_HEAD)
        yn = yc * lax.rsqrt(var + RW_GN_EPS) * lng_ref[:, ls] + lnb_ref[:, ls]
        k = k_ref[0, :, ls]
        ka = ka_ref[:, ls]
        kd_sum = k * (1.0 + (a0_ref[0, :, ls] - 1.0) * ka) + k * (1.0 + (a1_ref[0, :, ls] - 1.0) * ka)
        bonus = _pair_sum(r_ref[0, :, ls] * kd_sum * rk_ref[:, ls], lane_lo) * v_ref[0, :, ls]
        parts.append(((yn + bonus) * gate_ref[0, :, ls]).astype(BF16))
    out_ref[0] = _gated_residual(u_ref[0], parts, w_ref, mx_ref, mz_ref, row0, n_ctx)


def _rw_out(u, y0, y1, r, k, v, gate, a, k_a, r_k, ln_g, ln_b, mx, mz, w_o, n_ctx):
    b, t, d = u.shape
    tm = _pick_tile(t, 272)
    tok = lambda blk: pl.BlockSpec((1, tm, d), lambda i, j: (i, j, blk))
    vec = _const_spec((1, d))
    return pl.pallas_call(
        functools.partial(_rw_out_kernel, tm=tm, n_ctx=n_ctx),
        grid=(b, t // tm),
        in_specs=[tok(0)] * 7 + [tok(0), tok(1), vec, vec, vec, vec,
                                 pl.BlockSpec((1, 6, d), lambda i, j: (i, 0, 0)),
                                 _const_spec((6, d)), _const_spec((d, d))],
        out_specs=tok(0),
        out_shape=jax.ShapeDtypeStruct((b, t, d), F32),
        compiler_params=_cparams("parallel", "parallel"),
        name="rw_out",
    )(u, y0, y1, r, k, v, gate, a, a, k_a.reshape(1, d), r_k.reshape(1, d), ln_g.reshape(1, d),
      ln_b.reshape(1, d), mx, mz, w_o)


def _block_rows(w_pair):
    _, k, n = w_pair.shape
    z = jnp.zeros((k, n), w_pair.dtype)
    return jnp.concatenate([jnp.concatenate([w_pair[0], z], axis=1),
                            jnp.concatenate([z, w_pair[1]], axis=1)], axis=0)


def kernel(x, c, ctx, c_ctx, ada_w, ada_b, norm1_g, norm2_g, mlp_w1, mlp_w2, final_g, na_w_qkv, na_w_o, na_rpb, gla_w_in, gla_w_dec2, gla_b_dec, gla_norm_g, gla_w_o, rw_mix, rw_w_rkv, rw_w0, rw_w1, rw_w2, rw_a0, rw_a1, rw_a2, rw_g1, rw_g2, rw_k_k, rw_k_a, rw_r_k, rw_ln_g, rw_ln_b, rw_w_o):
    b, seq, d = x.shape
    n_ctx = ctx.shape[1]
    depth = ada_w.shape[0]
    assert d == D_MODEL and n_ctx % CHUNK == 0 and seq % CHUNK == 0

    u = jnp.concatenate([ctx.astype(x.dtype), x], axis=1)

    rows = -(-(b + 1) // 8) * 8
    c_all = jnp.zeros((rows, d), F32).at[:b].set(c).at[b].set(c_ctx)
    tables = _ada_tables(c_all, ada_w, ada_b)
    mods_x = tables[:, :b].reshape(depth, b, 6, d)
    mods_z = tables[:, b].reshape(depth, 6, d)

    nb = max(n for n in (4, 2, 1) if (n_ctx // CHUNK) % n == 0 and (seq // CHUNK) % n == 0)

    for i in range(depth):
        kind, j = i % 3, i // 3
        last = i == depth - 1
        mx, mz = mods_x[i], mods_z[i]
        if kind == 0:
            qkv = _norm_proj(u, mx, mz, norm1_g[i], na_w_qkv[j].astype(BF16), n_ctx, BF16)
            o = _na_attention(qkv, _na_bias_table(na_rpb[j]), n_ctx, with_ctx=not last)
            u = _na_out(u, o, mx, mz, na_w_o[j].astype(BF16), n_ctx, latent_only=last)
        elif kind == 1:
            n_main = 2 * GLA_DK + 2 * GLA_DV
            w_in = gla_w_in[j]
            w_lr = jnp.zeros((d, LANES), F32).at[:, :2 * GLA_LOW_RANK].set(w_in[:, n_main:])
            w_dec = jnp.zeros((LANES, 2 * GLA_DK), F32).at[:2 * GLA_LOW_RANK].set(_block_rows(gla_w_dec2[j]))
            p, gates = _gla_proj(u, mx, mz, norm1_g[i], w_in[:, :n_main].astype(BF16), w_lr.astype(BF16),
                                 w_dec.astype(BF16), gla_b_dec[j].reshape(1, 2 * GLA_DK), n_ctx)
            o = _gla_scan(p, gates, n_ctx, nb)
            u = _gla_out(u, o, p, gla_norm_g[j], mx, mz, gla_w_o[j].astype(BF16), n_ctx)
        else:
            g1 = jnp.zeros((d, 2 * LANES), F32).at[:, :RW_GATE_LORA].set(rw_g1[j])
            g2 = jnp.zeros((2 * LANES, d), F32).at[:RW_GATE_LORA].set(rw_g2[j])
            wts = dict(
                mix=rw_mix[j], w_rkv=rw_w_rkv[j].astype(BF16),
                w1=jnp.concatenate([rw_w1[j, 0], rw_w1[j, 1]], axis=1).astype(BF16),
                w2=_block_rows(rw_w2[j]).astype(BF16), w0=rw_w0[j].reshape(1, 2 * d),
                a1=jnp.concatenate([rw_a1[j, 0], rw_a1[j, 1]], axis=1).astype(BF16),
                a2=_block_rows(rw_a2[j]).astype(BF16), a0=rw_a0[j].reshape(1, 2 * d),
                g1=g1.astype(BF16), g2=g2.astype(BF16))
            r, k, v, gate, lw, a = _rw_proj(u, mx, mz, norm1_g[i], wts, n_ctx)
            y0, y1 = _rw_mix(r, k, v, lw, a, rw_k_k[j], rw_k_a[j], n_ctx, nb)
            u = _rw_out(u, y0, y1, r, k, v, gate, a, rw_k_a[j], rw_r_k[j], rw_ln_g[j], rw_ln_b[j],
                        mx, mz, rw_w_o[j].astype(BF16), n_ctx)
        if last and u.shape[1] != seq:
            u = u[:, n_ctx:]
        u = _mlp(u, mx, mz, norm2_g[i], mlp_w1[i].astype(BF16), mlp_w2[i].astype(BF16), final_g,
                 0 if last else n_ctx, final=last)
    return u
```

```python
import functools
import math

import jax
import jax.numpy as jnp
from jax import lax
from jax.experimental import pallas as pl
from jax.experimental.pallas import tpu as pltpu

F32 = jnp.float32
BF16 = jnp.bfloat16

D_MODEL = 1024
D_FF = 4 * D_MODEL
NORM_EPS = 1e-6
GRID_W = 64

NA_HEAD_DIM = 64
NA_HEADS = D_MODEL // NA_HEAD_DIM
NA_WIN_R = 8
NA_WIN_C = 16
NA_MASK = -1e30

GLA_HEADS = 4
GLA_DK = D_MODEL // 2
GLA_DV = D_MODEL
GLA_HK = GLA_DK // GLA_HEADS
GLA_HV = GLA_DV // GLA_HEADS
GLA_LOW_RANK = 16
GLA_GATE_NORM = 16.0
CHUNK = 64

RW_HEAD = 64
RW_GATE_LORA = 160
RW_LORA = 64
RW_GN_EPS = 64e-5

LANES = 128
VMEM_LIMIT = 56 * 1024 * 1024


def _cparams(*sem):
    return pltpu.CompilerParams(dimension_semantics=sem, vmem_limit_bytes=VMEM_LIMIT)


def _pick_tile(n, target, mult=16):
    best = None
    for t in range(mult, min(n, target) + 1, mult):
        if n % t == 0:
            best = t
    assert best is not None, (n, target)
    return best


def _const_spec(shape):
    nd = len(shape)
    return pl.BlockSpec(shape, lambda *_: (0,) * nd, pipeline_mode=pl.Buffered(1))


def _dot(a, b):
    return jnp.dot(a, b, preferred_element_type=F32)


def _dot_nt(a, b):
    return lax.dot_general(a, b, (((1,), (1,)), ((), ())), preferred_element_type=F32)


def _sigmoid(x):
    return 0.5 + 0.5 * jnp.tanh(0.5 * x)


def _softplus(x):
    return jnp.maximum(x, 0.0) + jnp.log(1.0 + jnp.exp(-jnp.abs(x)))


def _rms(u, g):
    return u * lax.rsqrt(jnp.mean(u * u, axis=-1, keepdims=True) + NORM_EPS) * g


def _mod_rows(mx, mz, idx, row0, rows, n_ctx):
    vx = mx[idx:idx + 1]
    if n_ctx == 0:
        return vx
    if n_ctx % rows == 0:
        return jnp.where(row0 < n_ctx, mz[idx:idx + 1], vx)
    r = row0 + lax.broadcasted_iota(jnp.int32, (rows, 1), 0)
    return jnp.where(r < n_ctx, mz[idx:idx + 1], vx)


def _norm_mod(u, g, mx, mz, row0, n_ctx, i_shift, i_scale):
    rows = u.shape[0]
    shift = _mod_rows(mx, mz, i_shift, row0, rows, n_ctx)
    scale = _mod_rows(mx, mz, i_scale, row0, rows, n_ctx)
    return _rms(u, g) * (1.0 + scale) + shift


def _pair_sum(x, lane_lo):
    s0 = jnp.sum(jnp.where(lane_lo, x, 0.0), axis=-1, keepdims=True)
    s1 = jnp.sum(jnp.where(lane_lo, 0.0, x), axis=-1, keepdims=True)
    return jnp.where(lane_lo, s0, s1)


def _ada_kernel(c_ref, w_ref, b_ref, o_ref):
    c = c_ref[...]
    sc = (c * _sigmoid(c)).astype(BF16)
    o_ref[0] = _dot(sc, w_ref[0]) + b_ref[0]


def _ada_tables(c_all, ada_w, ada_b):
    depth, d, n = ada_w.shape
    rows = c_all.shape[0]
    tn = 1536
    return pl.pallas_call(
        _ada_kernel,
        grid=(depth, n // tn),
        in_specs=[
            pl.BlockSpec((rows, d), lambda l, j: (0, 0)),
            pl.BlockSpec((1, d, tn), lambda l, j: (l, 0, j)),
            pl.BlockSpec((1, 1, tn), lambda l, j: (l, 0, j)),
        ],
        out_specs=pl.BlockSpec((1, rows, tn), lambda l, j: (l, 0, j)),
        out_shape=jax.ShapeDtypeStruct((depth, rows, n), F32),
        compiler_params=_cparams("parallel", "parallel"),
        name="ada_tables",
    )(c_all, ada_w.astype(BF16), ada_b.reshape(depth, 1, n))


def _proj_kernel(u_ref, mx_ref, mz_ref, g_ref, w_ref, o_ref, *, tm, n_ctx, nsplit):
    t = pl.program_id(1)
    h = _norm_mod(u_ref[0], g_ref[...], mx_ref[0], mz_ref[...], t * tm, n_ctx, 0, 1).astype(BF16)
    n = w_ref.shape[1] // nsplit
    for s in range(nsplit):
        o_ref[0, :, s * n:(s + 1) * n] = _dot(h, w_ref[:, s * n:(s + 1) * n]).astype(o_ref.dtype)


def _norm_proj(u, mx, mz, g, w, n_ctx, out_dtype):
    b, t, d = u.shape
    n = w.shape[1]
    tm = _pick_tile(t, 544)
    return pl.pallas_call(
        functools.partial(_proj_kernel, tm=tm, n_ctx=n_ctx, nsplit=n // D_MODEL),
        grid=(b, t // tm),
        in_specs=[
            pl.BlockSpec((1, tm, d), lambda i, j: (i, j, 0)),
            pl.BlockSpec((1, 6, d), lambda i, j: (i, 0, 0)),
            _const_spec((6, d)),
            _const_spec((1, d)),
            _const_spec((d, n)),
        ],
        out_specs=pl.BlockSpec((1, tm, n), lambda i, j: (i, j, 0)),
        out_shape=jax.ShapeDtypeStruct((b, t, n), out_dtype),
        compiler_params=_cparams("parallel", "parallel"),
        name="norm_proj",
    )(u, mx, mz, g.reshape(1, d), w)


def _na_bias_table(rpb):
    w = GRID_W
    cols = jnp.arange(w)
    c_start = jnp.clip(cols - NA_WIN_C // 2, 0, w - NA_WIN_C)
    col_ok = (cols[None, :] >= c_start[:, None]) & (cols[None, :] < c_start[:, None] + NA_WIN_C)
    col_idx = jnp.clip(cols[None, :] - cols[:, None], -(NA_WIN_C - 1), NA_WIN_C - 1) + NA_WIN_C - 1
    onehot = col_idx[:, :, None] == jnp.arange(2 * NA_WIN_C - 1)
    m = jnp.sum(jnp.where(onehot[None, None], rpb.astype(F32)[:, :, None, None, :], 0.0), axis=-1)
    m = jnp.where(col_ok[None, None], m, NA_MASK)
    return jnp.concatenate([m[:, :-1], m[:, 1:]], axis=-1)


def _na_kernel(q_ref, k_ref, v_ref, bias_ref, o_ref, *, n_ctx, n_rows, rb_off, with_ctx, rq):
    rb0 = pl.program_id(2) * rq + rb_off
    nctxb = n_ctx // GRID_W
    scale = NA_HEAD_DIM ** -0.5
    assert scale == 0.125, "the score scale is folded into bf16 q, exact only for a power of two"
    lane_lo = lax.broadcasted_iota(jnp.int32, (1, LANES), 1) < NA_HEAD_DIM
    head_masks = (jnp.where(lane_lo, scale, 0.0).astype(BF16), jnp.where(lane_lo, 0.0, scale).astype(BF16))
    n_pairs = q_ref.shape[2] // LANES
    strip = NA_WIN_R * GRID_W
    jobs_idx = [(qi, p) for qi in range(rq) for p in range(n_pairs)]

    def stacked_q(qi, p):
        q2 = q_ref[0, qi * GRID_W:(qi + 1) * GRID_W, p * LANES:(p + 1) * LANES]
        return jnp.concatenate([q2 * head_masks[0], q2 * head_masks[1]], axis=0)

    def attend(jobs):
        scores = [[_dot_nt(q, k) if b is None else _dot_nt(q, k) + b
                   for k, b in zip(ks, bs)] for q, ks, _, bs in jobs]
        tops = []
        for sc in scores:
            m = sc[0].max(axis=-1, keepdims=True)
            for s in sc[1:]:
                m = jnp.maximum(m, s.max(axis=-1, keepdims=True))
            tops.append(m)
        exps = [[jnp.exp(s - m) for s in sc] for sc, m in zip(scores, tops)]
        dens = []
        for es in exps:
            den = es[0].sum(axis=-1, keepdims=True)
            for e in es[1:]:
                den = den + e.sum(axis=-1, keepdims=True)
            dens.append(den)
        for (qi, p), (_, _, vs, _), es, den in zip(jobs_idx, jobs, exps, dens):
            o = _dot(es[0].astype(BF16), vs[0])
            for e, v in zip(es[1:], vs[1:]):
                o = o + _dot(e.astype(BF16), v)
            o = o / den
            o2 = jnp.where(lane_lo, o[:GRID_W], o[GRID_W:])
            o_ref[0, qi * GRID_W:(qi + 1) * GRID_W, p * LANES:(p + 1) * LANES] = o2.astype(o_ref.dtype)

    def latent_rows():
        jobs = []
        for qi, p in jobs_idx:
            ls = slice(p * LANES, (p + 1) * LANES)
            r = rb0 + qi - nctxb
            r0 = jnp.clip(r - NA_WIN_R // 2, 0, n_rows - NA_WIN_R)
            start = pl.multiple_of(n_ctx + r0 * GRID_W, GRID_W)
            ri0 = NA_WIN_R - 1 - (r - r0)
            bias = jnp.concatenate(
                [jnp.concatenate([bias_ref[2 * p + hh, ri0 + 2 * m] for m in range(NA_WIN_R // 2)], axis=1)
                 for hh in range(2)], axis=0)
            jobs.append((stacked_q(qi, p),
                         [k_ref[0, pl.ds(start, strip), ls], k_ref[0, 0:n_ctx, ls]],
                         [v_ref[0, pl.ds(start, strip), ls], v_ref[0, 0:n_ctx, ls]],
                         [bias, None]))
        attend(jobs)

    def context_rows():
        jobs = []
        for qi, p in jobs_idx:
            ls = slice(p * LANES, (p + 1) * LANES)
            jobs.append((stacked_q(qi, p), [k_ref[0, 0:n_ctx, ls]], [v_ref[0, 0:n_ctx, ls]], [None]))
        attend(jobs)

    if with_ctx:
        pl.when(rb0 < nctxb)(context_rows)
        pl.when(rb0 >= nctxb)(latent_rows)
    else:
        latent_rows()


def _na_attention(qkv, bias, n_ctx, with_ctx):
    b, t, _ = qkv.shape
    seq = t - n_ctx
    n_rows = seq // GRID_W
    assert seq % GRID_W == 0 and n_ctx % GRID_W == 0 and n_rows >= NA_WIN_R
    hw = 2 * LANES
    ng = D_MODEL // hw
    nctxb = n_ctx // GRID_W
    rq = 4 if nctxb % 4 == 0 and n_rows % 4 == 0 else 2
    assert nctxb % rq == 0 and n_rows % rq == 0
    rb_off = 0 if with_ctx else nctxb
    nrb = (t // GRID_W - rb_off) // rq
    t_out = t if with_ctx else seq
    qrows = rq * GRID_W

    return pl.pallas_call(
        functools.partial(_na_kernel, n_ctx=n_ctx, n_rows=n_rows, rb_off=rb_off, with_ctx=with_ctx, rq=rq),
        grid=(b, ng, nrb),
        in_specs=[
            pl.BlockSpec((1, qrows, hw), lambda i, g, j: (i, j + rb_off // rq, g)),
            pl.BlockSpec((1, t, hw), lambda i, g, j: (i, 0, ng + g)),
            pl.BlockSpec((1, t, hw), lambda i, g, j: (i, 0, 2 * ng + g)),
            pl.BlockSpec((hw // NA_HEAD_DIM, 2 * NA_WIN_R - 2, GRID_W, LANES), lambda i, g, j: (g, 0, 0, 0)),
        ],
        out_specs=pl.BlockSpec((1, qrows, hw), lambda i, g, j: (i, j, g)),
        out_shape=jax.ShapeDtypeStruct((b, t_out, D_MODEL), BF16),
        compiler_params=_cparams("parallel", "parallel", "arbitrary"),
        name="na_attention",
    )(qkv, qkv, qkv, bias)


def _gated_residual(u, o_bf16_parts, w_ref, mx_ref, mz_ref, row0, n_ctx):
    acc = None
    k0 = 0
    for part in o_bf16_parts:
        kw = part.shape[1]
        term = _dot(part, w_ref[k0:k0 + kw, :])
        acc = term if acc is None else acc + term
        k0 += kw
    gate = _mod_rows(mx_ref[0], mz_ref[...], 2, row0, u.shape[0], n_ctx)
    return u + gate * acc


def _na_out_kernel(u_ref, o_ref, mx_ref, mz_ref, w_ref, out_ref, *, tm, n_ctx, blk_off):
    row0 = (pl.program_id(1) + blk_off) * tm
    out_ref[0] = _gated_residual(u_ref[0], [o_ref[0]], w_ref, mx_ref, mz_ref, row0, n_ctx)


def _na_out(u, o, mx, mz, w_o, n_ctx, latent_only):
    b, t, d = u.shape
    if latent_only:
        tm = _pick_tile(n_ctx, 256)
        assert (t - n_ctx) % tm == 0
        blk_off = n_ctx // tm
        nt = (t - n_ctx) // tm
    else:
        tm = _pick_tile(t, 544)
        blk_off = 0
        nt = t // tm
    return pl.pallas_call(
        functools.partial(_na_out_kernel, tm=tm, n_ctx=n_ctx, blk_off=blk_off),
        grid=(b, nt),
        in_specs=[
            pl.BlockSpec((1, tm, d), lambda i, j: (i, j + blk_off, 0)),
            pl.BlockSpec((1, tm, d), lambda i, j: (i, j, 0)),
            pl.BlockSpec((1, 6, d), lambda i, j: (i, 0, 0)),
            _const_spec((6, d)),
            _const_spec((d, d)),
        ],
        out_specs=pl.BlockSpec((1, tm, d), lambda i, j: (i, j, 0)),
        out_shape=jax.ShapeDtypeStruct((b, nt * tm, d), F32),
        compiler_params=_cparams("parallel", "parallel"),
        name="na_out",
    )(u, o, mx, mz, w_o)


def _mlp_kernel(u_ref, mx_ref, mz_ref, g_ref, w1_ref, w2_ref, fg_ref, out_ref, *, tm, n_ctx, fc, final):
    row0 = pl.program_id(1) * tm
    u = u_ref[0]
    mx = mx_ref[0]
    mz = mz_ref[...]
    h = _norm_mod(u, g_ref[...], mx, mz, row0, n_ctx, 3, 4).astype(BF16)
    acc = None
    for c in range(D_FF // fc):
        a = _dot(h, w1_ref[:, c * fc:(c + 1) * fc])
        a = jnp.square(jnp.maximum(a, 0.0)).astype(BF16)
        term = _dot(a, w2_ref[c * fc:(c + 1) * fc, :])
        acc = term if acc is None else acc + term
    y = u + _mod_rows(mx, mz, 5, row0, tm, n_ctx) * acc
    if final:
        y = _rms(y, fg_ref[...])
    out_ref[0] = y


def _mlp(u, mx, mz, g, w1, w2, final_g, n_ctx, final):
    b, t, d = u.shape
    tm = _pick_tile(t, 544)
    return pl.pallas_call(
        functools.partial(_mlp_kernel, tm=tm, n_ctx=n_ctx, fc=1024, final=final),
        grid=(b, t // tm),
        in_specs=[
            pl.BlockSpec((1, tm, d), lambda i, j: (i, j, 0)),
            pl.BlockSpec((1, 6, d), lambda i, j: (i, 0, 0)),
            _const_spec((6, d)),
            _const_spec((1, d)),
            _const_spec((d, D_FF)),
            _const_spec((D_FF, d)),
            _const_spec((1, d)),
        ],
        out_specs=pl.BlockSpec((1, tm, d), lambda i, j: (i, j, 0)),
        out_shape=jax.ShapeDtypeStruct((b, t, d), F32),
        compiler_params=_cparams("parallel", "parallel"),
        name="mlp",
    )(u, mx, mz, g.reshape(1, d), w1, w2, final_g.reshape(1, d))


def _gla_proj_kernel(u_ref, mx_ref, mz_ref, g_ref, w_ref, wlr_ref, wdec_ref, bdec_ref,
                     p_ref, gate_ref, *, tm, n_ctx):
    t = pl.program_id(1)
    h = _norm_mod(u_ref[0], g_ref[...], mx_ref[0], mz_ref[...], t * tm, n_ctx, 0, 1).astype(BF16)
    n = D_MODEL
    for s in range(w_ref.shape[1] // n):
        p_ref[0, :, s * n:(s + 1) * n] = _dot(h, w_ref[:, s * n:(s + 1) * n])
    lr = _dot(h, wlr_ref[...]).astype(BF16)
    z = _dot(lr, wdec_ref[...]) + bdec_ref[...]
    gate_ref[0] = -_softplus(-z) * (1.0 / GLA_GATE_NORM)


def _gla_proj(u, mx, mz, g, w_main, w_lr, w_dec, b_dec, n_ctx):
    b, t, d = u.shape
    n = w_main.shape[1]
    tm = _pick_tile(t, 544)
    return pl.pallas_call(
        functools.partial(_gla_proj_kernel, tm=tm, n_ctx=n_ctx),
        grid=(b, t // tm),
        in_specs=[
            pl.BlockSpec((1, tm, d), lambda i, j: (i, j, 0)),
            pl.BlockSpec((1, 6, d), lambda i, j: (i, 0, 0)),
            _const_spec((6, d)),
            _const_spec((1, d)),
            _const_spec((d, n)),
            _const_spec((d, LANES)),
            _const_spec((LANES, 2 * GLA_DK)),
            _const_spec((1, 2 * GLA_DK)),
        ],
        out_specs=[
            pl.BlockSpec((1, tm, n), lambda i, j: (i, j, 0)),
            pl.BlockSpec((1, tm, 2 * GLA_DK), lambda i, j: (i, j, 0)),
        ],
        out_shape=[
            jax.ShapeDtypeStruct((b, t, n), F32),
            jax.ShapeDtypeStruct((b, t, 2 * GLA_DK), F32),
        ],
        compiler_params=_cparams("parallel", "parallel"),
        name="gla_proj",
    )(u, mx, mz, g.reshape(1, d), w_main, w_lr, w_dec, b_dec)


def _tri(n, upper, strict):
    r = lax.broadcasted_iota(jnp.int32, (n, n), 0)
    c = lax.broadcasted_iota(jnp.int32, (n, n), 1)
    if upper:
        return (r < c) if strict else (r <= c)
    return (r > c) if strict else (r >= c)


def _split3(x):
    hi = x.astype(BF16)
    r1 = x - hi.astype(F32)
    mid = r1.astype(BF16)
    lo = (r1 - mid.astype(F32)).astype(BF16)
    return hi, mid, lo


def _tri_cumsum(tri_b, x):
    hi, mid, lo = _split3(x)
    return _dot(tri_b, hi) + _dot(tri_b, mid) + _dot(tri_b, lo)


def _gla_scan_kernel(q_ref, k_ref, v_ref, g0_ref, g1_ref, o_ref, s_ref, *, n_ctx_chunks, n_chunks, nb):
    qscale = GLA_HK ** -0.5
    g_refs = (g0_ref, g1_ref)
    incl = (_tri(CHUNK, False, False), _tri(CHUNK, True, False))
    tri_b = tuple(jnp.where(m, 1.0, 0.0).astype(BF16) for m in incl)
    o_ref[...] = jnp.zeros_like(o_ref)

    def step(first):
        units = [(d, first[d] + (i if d == 0 else -i)) for i in range(nb) for d in range(2)]
        rows = [pl.ds(pl.multiple_of(c * CHUNK, CHUNK), CHUNK) for _, c in units]
        q = [q_ref[0, r, :] for r in rows]
        k = [k_ref[0, r, :] for r in rows]
        v = [v_ref[0, r, :].astype(BF16) for r in rows]
        g = [g_refs[d][0, r, :] for (d, _), r in zip(units, rows)]
        bcum = [_tri_cumsum(tri_b[d], gi) for (d, _), gi in zip(units, g)]
        b_last_row = [jnp.sum(gi, axis=0, keepdims=True) for gi in g]
        dec_col = [jnp.exp(jnp.sum(gi.T, axis=1, keepdims=True)) for gi in g]
        q_e = [(qi * jnp.exp(bi) * qscale).astype(BF16) for qi, bi in zip(q, bcum)]
        k_e = [(ki * jnp.exp(-bi)).astype(BF16) for ki, bi in zip(k, bcum)]
        k_dec_t = [(ki * jnp.exp(bl - bi)).T.astype(BF16) for ki, bl, bi in zip(k, b_last_row, bcum)]
        a = [jnp.where(incl[d], _dot_nt(qe, ke), 0.0).astype(BF16) for (d, _), qe, ke in zip(units, q_e, k_e)]
        o_intra = [_dot(ai, vi) for ai, vi in zip(a, v)]
        s_inc = [_dot(kt, vi) for kt, vi in zip(k_dec_t, v)]
        s = [s_ref[0], s_ref[1]]
        for i, (d, _) in enumerate(units):
            o_ref[0, rows[i], :] += o_intra[i] + _dot(q_e[i], s[d].astype(BF16))
            s[d] = dec_col[i] * s[d] + s_inc[i]
        s_ref[0] = s[0]
        s_ref[1] = s[1]

    s_ref[...] = jnp.zeros_like(s_ref)
    n_lat = n_chunks - n_ctx_chunks

    def ctx_body(i, carry):
        step((i * nb, n_ctx_chunks - 1 - i * nb))
        return carry

    def lat_body(i, carry):
        step((n_ctx_chunks + i * nb, n_chunks - 1 - i * nb))
        return carry

    lax.fori_loop(0, n_ctx_chunks // nb, ctx_body, 0)
    lax.fori_loop(0, n_lat // nb, lat_body, 0)


def _gla_scan(p, gates, n_ctx, nb):
    b, t, _ = p.shape
    nkb = GLA_DK // GLA_HK
    return pl.pallas_call(
        functools.partial(_gla_scan_kernel, n_ctx_chunks=n_ctx // CHUNK, n_chunks=t // CHUNK, nb=nb),
        grid=(b, GLA_HEADS),
        in_specs=[
            pl.BlockSpec((1, t, GLA_HK), lambda i, h: (i, 0, h)),
            pl.BlockSpec((1, t, GLA_HK), lambda i, h: (i, 0, nkb + h)),
            pl.BlockSpec((1, t, GLA_HV), lambda i, h: (i, 0, (2 * GLA_DK) // GLA_HV + h)),
            pl.BlockSpec((1, t, GLA_HK), lambda i, h: (i, 0, h)),
            pl.BlockSpec((1, t, GLA_HK), lambda i, h: (i, 0, nkb + h)),
        ],
        out_specs=pl.BlockSpec((1, t, GLA_HV), lambda i, h: (i, 0, h)),
        out_shape=jax.ShapeDtypeStruct((b, t, GLA_DV), F32),
        scratch_shapes=[pltpu.VMEM((2, GLA_HK, GLA_HV), F32)],
        compiler_params=_cparams("parallel", "parallel"),
        name="gla_scan",
    )(p, p, p, gates, gates)


def _gla_out_kernel(u_ref, o_ref, gt_ref, ng_ref, mx_ref, mz_ref, w_ref, out_ref, *, tm, n_ctx):
    row0 = pl.program_id(1) * tm
    parts = []
    for h in range(GLA_HEADS):
        ls = slice(h * GLA_HV, (h + 1) * GLA_HV)
        ov = o_ref[0, :, ls]
        ov = ov * lax.rsqrt(jnp.mean(ov * ov, axis=-1, keepdims=True) + NORM_EPS) * ng_ref[...]
        gt = gt_ref[0, :, ls]
        parts.append((ov * (gt * _sigmoid(gt))).astype(BF16))
    out_ref[0] = _gated_residual(u_ref[0], parts, w_ref, mx_ref, mz_ref, row0, n_ctx)


def _gla_out(u, o, p, norm_g, mx, mz, w_o, n_ctx):
    b, t, d = u.shape
    tm = _pick_tile(t, 544)
    gt_blk = (2 * GLA_DK + GLA_DV) // d
    return pl.pallas_call(
        functools.partial(_gla_out_kernel, tm=tm, n_ctx=n_ctx),
        grid=(b, t // tm),
        in_specs=[
            pl.BlockSpec((1, tm, d), lambda i, j: (i, j, 0)),
            pl.BlockSpec((1, tm, d), lambda i, j: (i, j, 0)),
            pl.BlockSpec((1, tm, d), lambda i, j: (i, j, gt_blk)),
            _const_spec((1, GLA_HV)),
            pl.BlockSpec((1, 6, d), lambda i, j: (i, 0, 0)),
            _const_spec((6, d)),
            _const_spec((d, d)),
        ],
        out_specs=pl.BlockSpec((1, tm, d), lambda i, j: (i, j, 0)),
        out_shape=jax.ShapeDtypeStruct((b, t, d), F32),
        compiler_params=_cparams("parallel", "parallel"),
        name="gla_out",
    )(u, o, p, norm_g.reshape(1, GLA_HV), mx, mz, w_o)


def _rw_proj_kernel(u_ref, up_ref, un_ref, mx_ref, mz_ref, g_ref, mix_ref, wrkv_ref, w1_ref, w2_ref,
                    w0_ref, a1_ref, a2_ref, a0_ref, g1_ref, g2_ref, ka_ref, rk_ref,
                    r_ref, k_ref, v_ref, gate_ref, lw_ref, a_ref, bonus_ref, *, tm, n_ctx, t_total):
    row0 = pl.program_id(1) * tm
    g = g_ref[...]
    mx = mx_ref[0]
    mz = mz_ref[...]
    h = _norm_mod(u_ref[0], g, mx, mz, row0, n_ctx, 0, 1)
    h_prev = _norm_mod(up_ref[0], g, mx, mz, row0 - 8, n_ctx, 0, 1)[7:8]
    h_next = _norm_mod(un_ref[0], g, mx, mz, row0 + tm, n_ctx, 0, 1)[0:1]
    idx = lax.broadcasted_iota(jnp.int32, (tm, 1), 0)
    rows = row0 + idx
    h_dn = jnp.where(idx == 0, h_prev, pltpu.roll(h, 1, 0))
    h_dn = jnp.where((rows == 0) | (rows == n_ctx), 0.0, h_dn)
    h_up = jnp.where(idx == tm - 1, h_next, pltpu.roll(h, tm - 1, 0))
    h_up = jnp.where((rows == n_ctx - 1) | (rows == t_total - 1), 0.0, h_up)
    xx = 0.5 * (h_dn + h_up) - h
    xr, xw, xk, xv, xa, xg = ((h + xx * mix_ref[m:m + 1]).astype(BF16) for m in range(6))
    r = _dot(xr, wrkv_ref[0])
    k = _dot(xk, wrkv_ref[1])
    v = _dot(xv, wrkv_ref[2])
    r_ref[0] = r
    k_ref[0] = k
    v_ref[0] = v
    gate_ref[0] = _dot(_sigmoid(_dot(xg, g1_ref[...])).astype(BF16), g2_ref[...])
    tw = jnp.tanh(_dot(xw, w1_ref[...])).astype(BF16)
    lw_ref[0] = -math.exp(-0.5) * _sigmoid(w0_ref[...] + _dot(tw, w2_ref[...]))
    ta = _dot(xa, a1_ref[...]).astype(BF16)
    a = _sigmoid(a0_ref[...] + _dot(ta, a2_ref[...]))
    a_ref[0] = a
    lane_lo = lax.broadcasted_iota(jnp.int32, (1, LANES), 1) < RW_HEAD
    d = D_MODEL
    for p in range(d // LANES):
        ls = slice(p * LANES, (p + 1) * LANES)
        kd_sum = k[:, ls] * (2.0 + (a[:, ls] + a[:, d + p * LANES:d + (p + 1) * LANES] - 2.0) * ka_ref[:, ls])
        bonus_ref[0, :, ls] = _pair_sum(r[:, ls] * kd_sum * rk_ref[:, ls], lane_lo) * v[:, ls]


def _rw_proj(u, mx, mz, g, wts, n_ctx):
    b, t, d = u.shape
    tm = _pick_tile(t, 256)
    nb8 = t // 8
    tb = tm // 8
    full = lambda n: jax.ShapeDtypeStruct((b, t, n), F32)
    row_spec = lambda n: pl.BlockSpec((1, tm, n), lambda i, j: (i, j, 0))
    return pl.pallas_call(
        functools.partial(_rw_proj_kernel, tm=tm, n_ctx=n_ctx, t_total=t),
        grid=(b, t // tm),
        in_specs=[
            row_spec(d),
            pl.BlockSpec((1, 8, d), lambda i, j: (i, jnp.maximum(j * tb - 1, 0), 0)),
            pl.BlockSpec((1, 8, d), lambda i, j: (i, jnp.minimum((j + 1) * tb, nb8 - 1), 0)),
            pl.BlockSpec((1, 6, d), lambda i, j: (i, 0, 0)),
            _const_spec((6, d)),
            _const_spec((1, d)),
            _const_spec((6, d)),
            _const_spec((3, d, d)),
            _const_spec((d, 2 * RW_LORA)),
            _const_spec((2 * RW_LORA, 2 * d)),
            _const_spec((1, 2 * d)),
            _const_spec((d, 2 * RW_LORA)),
            _const_spec((2 * RW_LORA, 2 * d)),
            _const_spec((1, 2 * d)),
            _const_spec((d, 2 * LANES)),
            _const_spec((2 * LANES, d)),
            _const_spec((1, d)),
            _const_spec((1, d)),
        ],
        out_specs=[row_spec(d), row_spec(d), row_spec(d), row_spec(d), row_spec(2 * d), row_spec(2 * d),
                   row_spec(d)],
        out_shape=[full(d), full(d), full(d), full(d), full(2 * d), full(2 * d), full(d)],
        compiler_params=_cparams("parallel", "parallel"),
        name="rw_proj",
    )(u, u, u, mx, mz, g.reshape(1, d), wts["mix"], wts["w_rkv"], wts["w1"], wts["w2"], wts["w0"],
      wts["a1"], wts["a2"], wts["a0"], wts["g1"], wts["g2"], wts["k_a"], wts["r_k"])


def _rw_pair_terms(units, side_work=()):
    C = CHUNK
    lane_lo = lax.broadcasted_iota(jnp.int32, (1, LANES), 1) < RW_HEAD
    row2 = lax.broadcasted_iota(jnp.int32, (C, LANES), 0)
    col2 = lax.broadcasted_iota(jnp.int32, (C, LANES), 1) & (C - 1)
    strict2 = (row2 > col2, row2 < col2)
    incl2 = (row2 >= col2, row2 <= col2)
    tri_b = tuple(jnp.where(_tri(C, upper, False), 1.0, 0.0).astype(BF16) for upper in (False, True))
    eye2 = jnp.where(row2 == col2, 1.0, 0.0)
    nu = len(units)

    def bd(x):
        zero = jnp.zeros_like(x)
        return jnp.concatenate([jnp.where(lane_lo, x, zero), jnp.where(lane_lo, zero, x)], axis=0)

    def hilo(x):
        hi = x.astype(BF16)
        return hi, (x - hi.astype(F32)).astype(BF16)

    side = iter(side_work)

    def stage_done():
        thunk = next(side, None)
        if thunk is not None:
            thunk()

    pre = []
    for r2, k2, v2, lw2, a2, desc, kkw, kaw in units:
        d = int(desc)
        kk = k2 * kkw
        kk = kk * lax.rsqrt(jnp.maximum(_pair_sum(kk * kk, lane_lo), 1e-24))
        kd = k2 * (1.0 + (a2 - 1.0) * kaw)
        bvec = kk * a2
        cum = _tri_cumsum(tri_b[d], lw2)
        tot = jnp.sum(lw2, axis=0, keepdims=True)
        e_neg = jnp.exp(-cum)
        e_rem = jnp.exp(tot - cum)
        pre.append(dict(
            d=d, tot=tot, v_b=v2.astype(BF16),
            a_t=-kk * jnp.exp(cum - lw2), r_t=r2 * jnp.exp(cum),
            b_h=(bvec * e_neg).astype(BF16), k_h=(kd * e_neg).astype(BF16),
            b_rem_t=(bvec * e_rem).T.astype(BF16), k_rem_t=(kd * e_rem).T.astype(BF16)))

    ar = [jnp.concatenate([p["a_t"], p["r_t"]], axis=0).astype(BF16) for p in pre]
    p_b = [_dot_nt(ar[u], bd(pre[u]["b_h"])) for u in range(nu)]
    p_k = [_dot_nt(ar[u], bd(pre[u]["k_h"])) for u in range(nu)]
    l_ab = [jnp.where(strict2[pre[u]["d"]], p_b[u][:C], 0.0) for u in range(nu)]
    m_rb = [jnp.where(incl2[pre[u]["d"]], p_b[u][C:], 0.0).astype(BF16) for u in range(nu)]
    l_ak = [jnp.where(strict2[pre[u]["d"]], p_k[u][:C], 0.0).astype(BF16) for u in range(nu)]
    m_rk = [jnp.where(incl2[pre[u]["d"]], p_k[u][C:], 0.0).astype(BF16) for u in range(nu)]
    stage_done()

    xo = l_ab
    l_hl = [hilo(l) for l in xo]
    r0 = [_dot(jnp.concatenate([hi, lo], axis=0), bd(hi)) for hi, lo in l_hl]
    r0b = [_dot(hi, bd(lo)) for hi, lo in l_hl]
    pw = [a[:C] + a[C:] + b for a, b in zip(r0, r0b)]
    lhs_hl = l_hl
    for _ in range(2):
        p_hl = [hilo(p) for p in pw]
        ra = [_dot(jnp.concatenate([ph, plo, xh, xlo], axis=0), bd(ph))
              for (ph, plo), (xh, xlo) in zip(p_hl, lhs_hl)]
        rb = [_dot(jnp.concatenate([ph, xh], axis=0), bd(plo)) for (ph, plo), (xh, _) in zip(p_hl, lhs_hl)]
        xo = [x + p + (a[2 * C:3 * C] + a[3 * C:] + b[C:]) for x, p, a, b in zip(xo, pw, ra, rb)]
        pw = [a[:C] + a[C:2 * C] + b[:C] for a, b in zip(ra, rb)]
        lhs_hl = [hilo(x) for x in xo]
        stage_done()
    for _ in range(2):
        p_b16 = [p.astype(BF16) for p in pw]
        ra = [_dot(jnp.concatenate([p, x.astype(BF16)], axis=0), bd(p)) for p, x in zip(p_b16, xo)]
        xo = [x + p + a[C:] for x, p, a in zip(xo, pw, ra)]
        pw = [a[:C] for a in ra]
        stage_done()
    xo = [x + p + _dot(x.astype(BF16), bd(p.astype(BF16))) for x, p in zip(xo, pw)]
    t_inv = [(eye2 + x).astype(BF16) for x in xo]

    lvyk = [_dot(jnp.concatenate([l_ak[u], m_rk[u]], axis=0), bd(pre[u]["v_b"])) for u in range(nu)]
    kv = [_dot(pre[u]["k_rem_t"], pre[u]["v_b"]) for u in range(nu)]
    tw = [_dot(t_inv[u], jnp.concatenate([bd(ar[u][:C]), bd(lvyk[u][:C].astype(BF16))], axis=1))
          for u in range(nu)]
    qy = [_dot(m_rb[u], jnp.concatenate([bd(tw[u][:, :LANES].astype(BF16)),
                                         bd(tw[u][:, LANES:].astype(BF16))], axis=1)) for u in range(nu)]
    gh = [_dot(pre[u]["b_rem_t"], tw[u].astype(BF16)) for u in range(nu)]

    r = lax.broadcasted_iota(jnp.int32, (LANES, LANES), 0)
    c = lax.broadcasted_iota(jnp.int32, (LANES, LANES), 1)
    same_head = (r < RW_HEAD) == (c < RW_HEAD)
    out = []
    for u in range(nu):
        qp = pre[u]["r_t"] + qy[u][:, :LANES]
        y0 = qy[u][:, LANES:] + lvyk[u][C:]
        g_mat = jnp.where(same_head, gh[u][:, :LANES], 0.0) + jnp.where(r == c, jnp.exp(pre[u]["tot"]), 0.0)
        h_mat = jnp.where(same_head, gh[u][:, LANES:] + kv[u], 0.0)
        out.append((qp, y0, g_mat, h_mat))
    return out


def _rw_mix_kernel(r0_ref, k0_ref, v0_ref, lw0_ref, a0_ref, r1_ref, k1_ref, v1_ref, lw1_ref, a1_ref,
                   kk_ref, ka_ref, y0_ref, y1_ref, st_ref, qp_s, yc_s, g_s, h_s, *, nb, pp):
    def zero_scratch(_, carry):
        for ref in (st_ref, qp_s, yc_s, g_s, h_s):
            ref[...] = jnp.zeros_like(ref)
        return carry

    lax.fori_loop(0, jnp.where(pl.program_id(2) == 0, 1, 0), zero_scratch, 0)

    y_refs = (y0_ref, y1_ref)
    chains = [(d, p) for d in range(2) for p in range(pp)]
    states = {dp: st_ref[dp[0], dp[1]] for dp in chains}
    prev_terms = {(d, p, c): (qp_s[d, p, c], yc_s[d, p, c], g_s[d, p, c], h_s[d, p, c])
                  for d, p in chains for c in range(nb)}

    def link(ci):
        def emit():
            for d, p in chains:
                c = ci if d == 0 else nb - 1 - ci
                qp, yc, g_mat, h_mat = prev_terms[d, p, c]
                st_b = states[d, p].astype(BF16)
                y_refs[d][0, c * CHUNK:(c + 1) * CHUNK, p * LANES:(p + 1) * LANES] = _dot(qp, st_b) + yc
                states[d, p] = _dot(g_mat, st_b) + h_mat
        return emit

    ins = ((r0_ref, k0_ref, v0_ref, lw0_ref, a0_ref), (r1_ref, k1_ref, v1_ref, lw1_ref, a1_ref))
    units = []
    index = []
    for p in range(pp):
        ls = slice(p * LANES, (p + 1) * LANES)
        for c in range(nb):
            rows = slice(c * CHUNK, (c + 1) * CHUNK)
            for d in range(2):
                units.append(tuple(ref[0, rows, ls] for ref in ins[d]) + (d == 1, kk_ref[:, ls], ka_ref[:, ls]))
                index.append((d, p, c))
    links = [link(ci) for ci in range(nb)]
    assert nb <= 5, "one recurrence link per stage boundary of _rw_pair_terms"
    results = _rw_pair_terms(units, side_work=links)
    for d, p in chains:
        st_ref[d, p] = states[d, p]
    for (d, p, c), (qp, y0, g_mat, h_mat) in zip(index, results):
        qp_s[d, p, c] = qp.astype(BF16)
        yc_s[d, p, c] = y0
        g_s[d, p, c] = g_mat.astype(BF16)
        h_s[d, p, c] = h_mat


def _rw_mix(r, k, v, lw, a, k_k, k_a, n_ctx, nb):
    b, t, d = r.shape
    npairs = d // LANES
    rows = nb * CHUNK
    nblk = t // rows
    ncb = n_ctx // rows

    def rev(j):
        return jnp.where(j < ncb, ncb - 1 - j, nblk - 1 - (j - ncb))

    cur = lambda j: jnp.minimum(j, nblk - 1)
    prev = lambda j: jnp.maximum(j - 1, 0)
    pp = 2
    pw = pp * LANES
    ng = npairs // pp
    fwd = lambda off, blk: pl.BlockSpec((1, rows, pw), lambda i, p, j: (i, blk(j), off + p))
    bwd = lambda off, blk: pl.BlockSpec((1, rows, pw), lambda i, p, j: (i, rev(blk(j)), off + p))
    par = pl.BlockSpec((1, pw), lambda i, p, j: (0, p))
    return pl.pallas_call(
        functools.partial(_rw_mix_kernel, nb=nb, pp=pp),
        grid=(b, ng, nblk + 1),
        in_specs=[fwd(0, cur), fwd(0, cur), fwd(0, cur), fwd(0, cur), fwd(0, cur),
                  bwd(0, cur), bwd(0, cur), bwd(0, cur), bwd(ng, cur), bwd(ng, cur), par, par],
        out_specs=[fwd(0, prev), bwd(0, prev)],
        out_shape=[jax.ShapeDtypeStruct((b, t, d), F32), jax.ShapeDtypeStruct((b, t, d), F32)],
        scratch_shapes=[pltpu.VMEM((2, pp, LANES, LANES), F32),
                        pltpu.VMEM((2, pp, nb, CHUNK, LANES), BF16), pltpu.VMEM((2, pp, nb, CHUNK, LANES), F32),
                        pltpu.VMEM((2, pp, nb, LANES, LANES), BF16), pltpu.VMEM((2, pp, nb, LANES, LANES), F32)],
        compiler_params=_cparams("parallel", "parallel", "arbitrary"),
        name="rw_mix",
    )(r, k, v, lw, a, r, k, v, lw, a, k_k.reshape(1, d), k_a.reshape(1, d))


def _rw_out_kernel(u_ref, y0_ref, y1_ref, bonus_ref, gate_ref, lng_ref, lnb_ref, mx_ref, mz_ref, w_ref,
                   out_ref, *, tm, n_ctx):
    row0 = pl.program_id(1) * tm
    lane_lo = lax.broadcasted_iota(jnp.int32, (1, LANES), 1) < RW_HEAD
    parts = []
    for p in range(D_MODEL // LANES):
        ls = slice(p * LANES, (p + 1) * LANES)
        y = y0_ref[0, :, ls] + y1_ref[0, :, ls]
        mu = _pair_sum(y, lane_lo) * (1.0 / RW_HEAD)
        yc = y - mu
        var = _pair_sum(yc * yc, lane_lo) * (1.0 / RW_HEAD)
        yn = yc * lax.rsqrt(var + RW_GN_EPS) * lng_ref[:, ls] + lnb_ref[:, ls]
        parts.append(((yn + bonus_ref[0, :, ls]) * gate_ref[0, :, ls]).astype(BF16))
    out_ref[0] = _gated_residual(u_ref[0], parts, w_ref, mx_ref, mz_ref, row0, n_ctx)


def _rw_out(u, y0, y1, bonus, gate, ln_g, ln_b, mx, mz, w_o, n_ctx):
    b, t, d = u.shape
    tm = _pick_tile(t, 544)
    tok = pl.BlockSpec((1, tm, d), lambda i, j: (i, j, 0))
    vec = _const_spec((1, d))
    return pl.pallas_call(
        functools.partial(_rw_out_kernel, tm=tm, n_ctx=n_ctx),
        grid=(b, t // tm),
        in_specs=[tok] * 5 + [vec, vec, pl.BlockSpec((1, 6, d), lambda i, j: (i, 0, 0)),
                              _const_spec((6, d)), _const_spec((d, d))],
        out_specs=tok,
        out_shape=jax.ShapeDtypeStruct((b, t, d), F32),
        compiler_params=_cparams("parallel", "parallel"),
        name="rw_out",
    )(u, y0, y1, bonus, gate, ln_g.reshape(1, d), ln_b.reshape(1, d), mx, mz, w_o)


def _block_rows(w_pair):
    _, k, n = w_pair.shape
    z = jnp.zeros((k, n), w_pair.dtype)
    return jnp.concatenate([jnp.concatenate([w_pair[0], z], axis=1),
                            jnp.concatenate([z, w_pair[1]], axis=1)], axis=0)


def kernel(x, c, ctx, c_ctx, ada_w, ada_b, norm1_g, norm2_g, mlp_w1, mlp_w2, final_g, na_w_qkv, na_w_o, na_rpb, gla_w_in, gla_w_dec2, gla_b_dec, gla_norm_g, gla_w_o, rw_mix, rw_w_rkv, rw_w0, rw_w1, rw_w2, rw_a0, rw_a1, rw_a2, rw_g1, rw_g2, rw_k_k, rw_k_a, rw_r_k, rw_ln_g, rw_ln_b, rw_w_o):
    b, seq, d = x.shape
    n_ctx = ctx.shape[1]
    depth = ada_w.shape[0]
    assert d == D_MODEL and n_ctx % CHUNK == 0 and seq % CHUNK == 0

    u = jnp.concatenate([ctx.astype(x.dtype), x], axis=1)

    rows = -(-(b + 1) // 8) * 8
    c_all = jnp.zeros((rows, d), F32).at[:b].set(c).at[b].set(c_ctx)
    tables = _ada_tables(c_all, ada_w, ada_b)
    mods_x = tables[:, :b].reshape(depth, b, 6, d)
    mods_z = tables[:, b].reshape(depth, 6, d)

    nb = max(n for n in (4, 2, 1) if (n_ctx // CHUNK) % n == 0 and (seq // CHUNK) % n == 0)

    for i in range(depth):
        kind, j = i % 3, i // 3
        last = i == depth - 1
        mx, mz = mods_x[i], mods_z[i]
        if kind == 0:
            qkv = _norm_proj(u, mx, mz, norm1_g[i], na_w_qkv[j].astype(BF16), n_ctx, BF16)
            o = _na_attention(qkv, _na_bias_table(na_rpb[j]), n_ctx, with_ctx=not last)
            u = _na_out(u, o, mx, mz, na_w_o[j].astype(BF16), n_ctx, latent_only=last)
        elif kind == 1:
            n_main = 2 * GLA_DK + 2 * GLA_DV
            w_in = gla_w_in[j]
            w_lr = jnp.zeros((d, LANES), F32).at[:, :2 * GLA_LOW_RANK].set(w_in[:, n_main:])
            w_dec = jnp.zeros((LANES, 2 * GLA_DK), F32).at[:2 * GLA_LOW_RANK].set(_block_rows(gla_w_dec2[j]))
            p, gates = _gla_proj(u, mx, mz, norm1_g[i], w_in[:, :n_main].astype(BF16), w_lr.astype(BF16),
                                 w_dec.astype(BF16), gla_b_dec[j].reshape(1, 2 * GLA_DK), n_ctx)
            o = _gla_scan(p, gates, n_ctx, nb)
            u = _gla_out(u, o, p, gla_norm_g[j], mx, mz, gla_w_o[j].astype(BF16), n_ctx)
        else:
            g1 = jnp.zeros((d, 2 * LANES), F32).at[:, :RW_GATE_LORA].set(rw_g1[j])
            g2 = jnp.zeros((2 * LANES, d), F32).at[:RW_GATE_LORA].set(rw_g2[j])
            wts = dict(
                mix=rw_mix[j], w_rkv=rw_w_rkv[j].astype(BF16),
                w1=jnp.concatenate([rw_w1[j, 0], rw_w1[j, 1]], axis=1).astype(BF16),
                w2=_block_rows(rw_w2[j]).astype(BF16), w0=rw_w0[j].reshape(1, 2 * d),
                a1=jnp.concatenate([rw_a1[j, 0], rw_a1[j, 1]], axis=1).astype(BF16),
                a2=_block_rows(rw_a2[j]).astype(BF16), a0=rw_a0[j].reshape(1, 2 * d),
                g1=g1.astype(BF16), g2=g2.astype(BF16),
                k_a=rw_k_a[j].reshape(1, d), r_k=rw_r_k[j].reshape(1, d))
            r, k, v, gate, lw, a, bonus = _rw_proj(u, mx, mz, norm1_g[i], wts, n_ctx)
            y0, y1 = _rw_mix(r, k, v, lw, a, rw_k_k[j], rw_k_a[j], n_ctx, nb)
            u = _rw_out(u, y0, y1, bonus, gate, rw_ln_g[j], rw_ln_b[j], mx, mz, rw_w_o[j].astype(BF16), n_ctx)
        if last and u.shape[1] != seq:
            u = u[:, n_ctx:]
        u = _mlp(u, mx, mz, norm2_g[i], mlp_w1[i].astype(BF16), mlp_w2[i].astype(BF16), final_g,
                 0 if last else n_ctx, final=last)
    return u
```

```python
import functools
import math

import jax
import jax.numpy as jnp
from jax import lax
from jax.experimental import pallas as pl
from jax.experimental.pallas import tpu as pltpu

F32 = jnp.float32
BF16 = jnp.bfloat16

D_MODEL = 1024
D_FF = 4 * D_MODEL
NORM_EPS = 1e-6
GRID_W = 64

NA_HEAD_DIM = 64
NA_HEADS = D_MODEL // NA_HEAD_DIM
NA_WIN_R = 8
NA_WIN_C = 16
NA_MASK = -1e30

GLA_HEADS = 4
GLA_DK = D_MODEL // 2
GLA_DV = D_MODEL
GLA_HK = GLA_DK // GLA_HEADS
GLA_HV = GLA_DV // GLA_HEADS
GLA_LOW_RANK = 16
GLA_GATE_NORM = 16.0
CHUNK = 64

RW_HEAD = 64
RW_GATE_LORA = 160
RW_LORA = 64
RW_GN_EPS = 64e-5

LANES = 128
VMEM_LIMIT = 56 * 1024 * 1024


def _cparams(*sem):
    return pltpu.CompilerParams(dimension_semantics=sem, vmem_limit_bytes=VMEM_LIMIT)


def _pick_tile(n, target, mult=16):
    best = None
    for t in range(mult, min(n, target) + 1, mult):
        if n % t == 0:
            best = t
    assert best is not None, (n, target)
    return best


def _const_spec(shape):
    nd = len(shape)
    return pl.BlockSpec(shape, lambda *_: (0,) * nd, pipeline_mode=pl.Buffered(1))


def _dot(a, b):
    return jnp.dot(a, b, preferred_element_type=F32)


def _dot_nt(a, b):
    return lax.dot_general(a, b, (((1,), (1,)), ((), ())), preferred_element_type=F32)


def _sigmoid(x):
    return 0.5 + 0.5 * jnp.tanh(0.5 * x)


def _softplus(x):
    return jnp.maximum(x, 0.0) + jnp.log(1.0 + jnp.exp(-jnp.abs(x)))


def _rms(u, g):
    return u * lax.rsqrt(jnp.mean(u * u, axis=-1, keepdims=True) + NORM_EPS) * g


def _mod_rows(mx, mz, idx, row0, rows, n_ctx):
    vx = mx[idx:idx + 1]
    if n_ctx == 0:
        return vx
    if n_ctx % rows == 0:
        return jnp.where(row0 < n_ctx, mz[idx:idx + 1], vx)
    r = row0 + lax.broadcasted_iota(jnp.int32, (rows, 1), 0)
    return jnp.where(r < n_ctx, mz[idx:idx + 1], vx)


def _norm_mod(u, g, mx, mz, row0, n_ctx, i_shift, i_scale):
    rows = u.shape[0]
    shift = _mod_rows(mx, mz, i_shift, row0, rows, n_ctx)
    scale = _mod_rows(mx, mz, i_scale, row0, rows, n_ctx)
    return _rms(u, g) * (1.0 + scale) + shift


def _pair_sum(x, lane_lo):
    s0 = jnp.sum(jnp.where(lane_lo, x, 0.0), axis=-1, keepdims=True)
    s1 = jnp.sum(jnp.where(lane_lo, 0.0, x), axis=-1, keepdims=True)
    return jnp.where(lane_lo, s0, s1)


def _ada_kernel(c_ref, w_ref, b_ref, o_ref):
    c = c_ref[...]
    sc = (c * _sigmoid(c)).astype(BF16)
    o_ref[0] = _dot(sc, w_ref[0]) + b_ref[0]


def _ada_tables(c_all, ada_w, ada_b):
    depth, d, n = ada_w.shape
    rows = c_all.shape[0]
    tn = 1536
    return pl.pallas_call(
        _ada_kernel,
        grid=(depth, n // tn),
        in_specs=[
            pl.BlockSpec((rows, d), lambda l, j: (0, 0)),
            pl.BlockSpec((1, d, tn), lambda l, j: (l, 0, j)),
            pl.BlockSpec((1, 1, tn), lambda l, j: (l, 0, j)),
        ],
        out_specs=pl.BlockSpec((1, rows, tn), lambda l, j: (l, 0, j)),
        out_shape=jax.ShapeDtypeStruct((depth, rows, n), F32),
        compiler_params=_cparams("parallel", "parallel"),
        name="ada_tables",
    )(c_all, ada_w.astype(BF16), ada_b.reshape(depth, 1, n))


def _proj_kernel(u_ref, mx_ref, mz_ref, g_ref, w_ref, o_ref, *, tm, n_ctx, nsplit):
    t = pl.program_id(1)
    h = _norm_mod(u_ref[0], g_ref[...], mx_ref[0], mz_ref[...], t * tm, n_ctx, 0, 1).astype(BF16)
    n = w_ref.shape[1] // nsplit
    for s in range(nsplit):
        o_ref[0, :, s * n:(s + 1) * n] = _dot(h, w_ref[:, s * n:(s + 1) * n]).astype(o_ref.dtype)


def _norm_proj(u, mx, mz, g, w, n_ctx, out_dtype):
    b, t, d = u.shape
    n = w.shape[1]
    tm = _pick_tile(t, 544)
    return pl.pallas_call(
        functools.partial(_proj_kernel, tm=tm, n_ctx=n_ctx, nsplit=n // D_MODEL),
        grid=(b, t // tm),
        in_specs=[
            pl.BlockSpec((1, tm, d), lambda i, j: (i, j, 0)),
            pl.BlockSpec((1, 6, d), lambda i, j: (i, 0, 0)),
            _const_spec((6, d)),
            _const_spec((1, d)),
            _const_spec((d, n)),
        ],
        out_specs=pl.BlockSpec((1, tm, n), lambda i, j: (i, j, 0)),
        out_shape=jax.ShapeDtypeStruct((b, t, n), out_dtype),
        compiler_params=_cparams("parallel", "parallel"),
        name="norm_proj",
    )(u, mx, mz, g.reshape(1, d), w)


def _na_bias_table(rpb):
    w = GRID_W
    cols = jnp.arange(w)
    c_start = jnp.clip(cols - NA_WIN_C // 2, 0, w - NA_WIN_C)
    col_ok = (cols[None, :] >= c_start[:, None]) & (cols[None, :] < c_start[:, None] + NA_WIN_C)
    col_idx = jnp.clip(cols[None, :] - cols[:, None], -(NA_WIN_C - 1), NA_WIN_C - 1) + NA_WIN_C - 1
    onehot = col_idx[:, :, None] == jnp.arange(2 * NA_WIN_C - 1)
    m = jnp.sum(jnp.where(onehot[None, None], rpb.astype(F32)[:, :, None, None, :], 0.0), axis=-1)
    m = jnp.where(col_ok[None, None], m, NA_MASK)
    return jnp.concatenate([m[:, :-1], m[:, 1:]], axis=-1)


def _na_kernel(q_ref, k_ref, v_ref, bias_ref, o_ref, *, n_ctx, n_rows, with_ctx, rq):
    nctxb = n_ctx // GRID_W
    out_off = 0 if with_ctx else n_ctx
    scale = NA_HEAD_DIM ** -0.5
    assert scale == 0.125, "the score scale is folded into bf16 q, exact only for a power of two"
    lane_lo = lax.broadcasted_iota(jnp.int32, (1, LANES), 1) < NA_HEAD_DIM
    head_masks = (jnp.where(lane_lo, scale, 0.0).astype(BF16), jnp.where(lane_lo, 0.0, scale).astype(BF16))
    n_pairs = q_ref.shape[2] // LANES
    strip = NA_WIN_R * GRID_W
    jobs_idx = [(qi, p) for qi in range(rq) for p in range(n_pairs)]

    def q_rows(tok0, qi):
        return pl.ds(pl.multiple_of(tok0 + qi * GRID_W, GRID_W), GRID_W)

    def stacked_q(tok0, qi, p):
        q2 = q_ref[0, q_rows(tok0, qi), p * LANES:(p + 1) * LANES]
        return jnp.concatenate([q2 * head_masks[0], q2 * head_masks[1]], axis=0)

    def attend(tok0, jobs):
        scores = [[_dot_nt(q, k) if b is None else _dot_nt(q, k) + b
                   for k, b in zip(ks, bs)] for q, ks, _, bs in jobs]
        tops = []
        for sc in scores:
            m = sc[0].max(axis=-1, keepdims=True)
            for s in sc[1:]:
                m = jnp.maximum(m, s.max(axis=-1, keepdims=True))
            tops.append(m)
        exps = [[jnp.exp(s - m) for s in sc] for sc, m in zip(scores, tops)]
        dens = []
        for es in exps:
            den = es[0].sum(axis=-1, keepdims=True)
            for e in es[1:]:
                den = den + e.sum(axis=-1, keepdims=True)
            dens.append(den)
        for (qi, p), (_, _, vs, _), es, den in zip(jobs_idx, jobs, exps, dens):
            o = _dot(es[0].astype(BF16), vs[0])
            for e, v in zip(es[1:], vs[1:]):
                o = o + _dot(e.astype(BF16), v)
            o = o / den
            o2 = jnp.where(lane_lo, o[:GRID_W], o[GRID_W:])
            o_ref[0, q_rows(tok0 - out_off, qi), p * LANES:(p + 1) * LANES] = o2.astype(o_ref.dtype)

    def latent_rows(step, carry):
        tok0 = n_ctx + step * (rq * GRID_W)
        jobs = []
        for qi, p in jobs_idx:
            ls = slice(p * LANES, (p + 1) * LANES)
            r = step * rq + qi
            r0 = jnp.clip(r - NA_WIN_R // 2, 0, n_rows - NA_WIN_R)
            start = pl.multiple_of(n_ctx + r0 * GRID_W, GRID_W)
            ri0 = NA_WIN_R - 1 - (r - r0)
            bias = jnp.concatenate(
                [jnp.concatenate([bias_ref[2 * p + hh, ri0 + 2 * m] for m in range(NA_WIN_R // 2)], axis=1)
                 for hh in range(2)], axis=0)
            jobs.append((stacked_q(tok0, qi, p),
                         [k_ref[0, pl.ds(start, strip), ls], k_ref[0, 0:n_ctx, ls]],
                         [v_ref[0, pl.ds(start, strip), ls], v_ref[0, 0:n_ctx, ls]],
                         [bias, None]))
        attend(tok0, jobs)
        return carry

    def context_rows(step, carry):
        tok0 = step * (rq * GRID_W)
        jobs = []
        for qi, p in jobs_idx:
            ls = slice(p * LANES, (p + 1) * LANES)
            jobs.append((stacked_q(tok0, qi, p), [k_ref[0, 0:n_ctx, ls]], [v_ref[0, 0:n_ctx, ls]], [None]))
        attend(tok0, jobs)
        return carry

    if with_ctx:
        lax.fori_loop(0, nctxb // rq, context_rows, 0)
    lax.fori_loop(0, n_rows // rq, latent_rows, 0)


def _na_attention(qkv, bias, n_ctx, with_ctx):
    b, t, _ = qkv.shape
    seq = t - n_ctx
    n_rows = seq // GRID_W
    assert seq % GRID_W == 0 and n_ctx % GRID_W == 0 and n_rows >= NA_WIN_R
    hw = 2 * LANES
    ng = D_MODEL // hw
    nctxb = n_ctx // GRID_W
    rq = 4 if nctxb % 4 == 0 and n_rows % 4 == 0 else 2
    assert nctxb % rq == 0 and n_rows % rq == 0
    t_out = t if with_ctx else seq

    return pl.pallas_call(
        functools.partial(_na_kernel, n_ctx=n_ctx, n_rows=n_rows, with_ctx=with_ctx, rq=rq),
        grid=(b, ng),
        in_specs=[
            pl.BlockSpec((1, t, hw), lambda i, g: (i, 0, g)),
            pl.BlockSpec((1, t, hw), lambda i, g: (i, 0, ng + g)),
            pl.BlockSpec((1, t, hw), lambda i, g: (i, 0, 2 * ng + g)),
            pl.BlockSpec((hw // NA_HEAD_DIM, 2 * NA_WIN_R - 2, GRID_W, LANES), lambda i, g: (g, 0, 0, 0)),
        ],
        out_specs=pl.BlockSpec((1, t_out, hw), lambda i, g: (i, 0, g)),
        out_shape=jax.ShapeDtypeStruct((b, t_out, D_MODEL), BF16),
        compiler_params=_cparams("parallel", "parallel"),
        name="na_attention",
    )(qkv, qkv, qkv, bias)


def _gated_residual(u, o_bf16_parts, w_ref, mx_ref, mz_ref, row0, n_ctx):
    acc = None
    k0 = 0
    for part in o_bf16_parts:
        kw = part.shape[1]
        term = _dot(part, w_ref[k0:k0 + kw, :])
        acc = term if acc is None else acc + term
        k0 += kw
    gate = _mod_rows(mx_ref[0], mz_ref[...], 2, row0, u.shape[0], n_ctx)
    return u + gate * acc


def _na_out_kernel(u_ref, o_ref, mx_ref, mz_ref, w_ref, out_ref, *, tm, n_ctx, blk_off):
    row0 = (pl.program_id(1) + blk_off) * tm
    out_ref[0] = _gated_residual(u_ref[0], [o_ref[0]], w_ref, mx_ref, mz_ref, row0, n_ctx)


def _na_out(u, o, mx, mz, w_o, n_ctx, latent_only):
    b, t, d = u.shape
    if latent_only:
        tm = _pick_tile(n_ctx, 256)
        assert (t - n_ctx) % tm == 0
        blk_off = n_ctx // tm
        nt = (t - n_ctx) // tm
    else:
        tm = _pick_tile(t, 544)
        blk_off = 0
        nt = t // tm
    return pl.pallas_call(
        functools.partial(_na_out_kernel, tm=tm, n_ctx=n_ctx, blk_off=blk_off),
        grid=(b, nt),
        in_specs=[
            pl.BlockSpec((1, tm, d), lambda i, j: (i, j + blk_off, 0)),
            pl.BlockSpec((1, tm, d), lambda i, j: (i, j, 0)),
            pl.BlockSpec((1, 6, d), lambda i, j: (i, 0, 0)),
            _const_spec((6, d)),
            _const_spec((d, d)),
        ],
        out_specs=pl.BlockSpec((1, tm, d), lambda i, j: (i, j, 0)),
        out_shape=jax.ShapeDtypeStruct((b, nt * tm, d), F32),
        compiler_params=_cparams("parallel", "parallel"),
        name="na_out",
    )(u, o, mx, mz, w_o)


def _mlp_kernel(u_ref, mx_ref, mz_ref, g_ref, w1_ref, w2_ref, fg_ref, out_ref, *, tm, n_ctx, fc, final):
    row0 = pl.program_id(1) * tm
    u = u_ref[0]
    mx = mx_ref[0]
    mz = mz_ref[...]
    h = _norm_mod(u, g_ref[...], mx, mz, row0, n_ctx, 3, 4).astype(BF16)
    acc = None
    for c in range(D_FF // fc):
        a = _dot(h, w1_ref[:, c * fc:(c + 1) * fc])
        a = jnp.square(jnp.maximum(a, 0.0)).astype(BF16)
        term = _dot(a, w2_ref[c * fc:(c + 1) * fc, :])
        acc = term if acc is None else acc + term
    y = u + _mod_rows(mx, mz, 5, row0, tm, n_ctx) * acc
    if final:
        y = _rms(y, fg_ref[...])
    out_ref[0] = y


def _mlp(u, mx, mz, g, w1, w2, final_g, n_ctx, final):
    b, t, d = u.shape
    tm = _pick_tile(t, 544)
    return pl.pallas_call(
        functools.partial(_mlp_kernel, tm=tm, n_ctx=n_ctx, fc=1024, final=final),
        grid=(b, t // tm),
        in_specs=[
            pl.BlockSpec((1, tm, d), lambda i, j: (i, j, 0)),
            pl.BlockSpec((1, 6, d), lambda i, j: (i, 0, 0)),
            _const_spec((6, d)),
            _const_spec((1, d)),
            _const_spec((d, D_FF)),
            _const_spec((D_FF, d)),
            _const_spec((1, d)),
        ],
        out_specs=pl.BlockSpec((1, tm, d), lambda i, j: (i, j, 0)),
        out_shape=jax.ShapeDtypeStruct((b, t, d), F32),
        compiler_params=_cparams("parallel", "parallel"),
        name="mlp",
    )(u, mx, mz, g.reshape(1, d), w1, w2, final_g.reshape(1, d))


def _gla_proj_kernel(u_ref, mx_ref, mz_ref, g_ref, w_ref, wlr_ref, wdec_ref, bdec_ref,
                     p_ref, gate_ref, *, tm, n_ctx):
    t = pl.program_id(1)
    h = _norm_mod(u_ref[0], g_ref[...], mx_ref[0], mz_ref[...], t * tm, n_ctx, 0, 1).astype(BF16)
    n = D_MODEL
    for s in range(w_ref.shape[1] // n):
        p_ref[0, :, s * n:(s + 1) * n] = _dot(h, w_ref[:, s * n:(s + 1) * n])
    lr = _dot(h, wlr_ref[...]).astype(BF16)
    z = _dot(lr, wdec_ref[...]) + bdec_ref[...]
    gate_ref[0] = -_softplus(-z) * (1.0 / GLA_GATE_NORM)


def _gla_proj(u, mx, mz, g, w_main, w_lr, w_dec, b_dec, n_ctx):
    b, t, d = u.shape
    n = w_main.shape[1]
    tm = _pick_tile(t, 544)
    return pl.pallas_call(
        functools.partial(_gla_proj_kernel, tm=tm, n_ctx=n_ctx),
        grid=(b, t // tm),
        in_specs=[
            pl.BlockSpec((1, tm, d), lambda i, j: (i, j, 0)),
            pl.BlockSpec((1, 6, d), lambda i, j: (i, 0, 0)),
            _const_spec((6, d)),
            _const_spec((1, d)),
            _const_spec((d, n)),
            _const_spec((d, LANES)),
            _const_spec((LANES, 2 * GLA_DK)),
            _const_spec((1, 2 * GLA_DK)),
        ],
        out_specs=[
            pl.BlockSpec((1, tm, n), lambda i, j: (i, j, 0)),
            pl.BlockSpec((1, tm, 2 * GLA_DK), lambda i, j: (i, j, 0)),
        ],
        out_shape=[
            jax.ShapeDtypeStruct((b, t, n), F32),
            jax.ShapeDtypeStruct((b, t, 2 * GLA_DK), F32),
        ],
        compiler_params=_cparams("parallel", "parallel"),
        name="gla_proj",
    )(u, mx, mz, g.reshape(1, d), w_main, w_lr, w_dec, b_dec)


def _tri(n, upper, strict):
    r = lax.broadcasted_iota(jnp.int32, (n, n), 0)
    c = lax.broadcasted_iota(jnp.int32, (n, n), 1)
    if upper:
        return (r < c) if strict else (r <= c)
    return (r > c) if strict else (r >= c)


def _split3(x):
    hi = x.astype(BF16)
    r1 = x - hi.astype(F32)
    mid = r1.astype(BF16)
    lo = (r1 - mid.astype(F32)).astype(BF16)
    return hi, mid, lo


def _tri_cumsum(tri_b, x):
    hi, mid, lo = _split3(x)
    return _dot(tri_b, hi) + _dot(tri_b, mid) + _dot(tri_b, lo)


def _gla_scan_kernel(q_ref, k_ref, v_ref, g0_ref, g1_ref, o_ref, s_ref, *, n_ctx_chunks, n_chunks, nb):
    qscale = GLA_HK ** -0.5
    g_refs = (g0_ref, g1_ref)
    incl = (_tri(CHUNK, False, False), _tri(CHUNK, True, False))
    tri_b = tuple(jnp.where(m, 1.0, 0.0).astype(BF16) for m in incl)
    o_ref[...] = jnp.zeros_like(o_ref)

    def step(first):
        units = [(d, first[d] + (i if d == 0 else -i)) for i in range(nb) for d in range(2)]
        rows = [pl.ds(pl.multiple_of(c * CHUNK, CHUNK), CHUNK) for _, c in units]
        q = [q_ref[0, r, :] for r in rows]
        k = [k_ref[0, r, :] for r in rows]
        v = [v_ref[0, r, :].astype(BF16) for r in rows]
        g = [g_refs[d][0, r, :] for (d, _), r in zip(units, rows)]
        bcum = [_tri_cumsum(tri_b[d], gi) for (d, _), gi in zip(units, g)]
        b_last_row = [jnp.sum(gi, axis=0, keepdims=True) for gi in g]
        dec_col = [jnp.exp(jnp.sum(gi.T, axis=1, keepdims=True)) for gi in g]
        q_e = [(qi * jnp.exp(bi) * qscale).astype(BF16) for qi, bi in zip(q, bcum)]
        k_e = [(ki * jnp.exp(-bi)).astype(BF16) for ki, bi in zip(k, bcum)]
        k_dec_t = [(ki * jnp.exp(bl - bi)).T.astype(BF16) for ki, bl, bi in zip(k, b_last_row, bcum)]
        a = [jnp.where(incl[d], _dot_nt(qe, ke), 0.0).astype(BF16) for (d, _), qe, ke in zip(units, q_e, k_e)]
        o_intra = [_dot(ai, vi) for ai, vi in zip(a, v)]
        s_inc = [_dot(kt, vi) for kt, vi in zip(k_dec_t, v)]
        s = [s_ref[0], s_ref[1]]
        for i, (d, _) in enumerate(units):
            o_ref[0, rows[i], :] += o_intra[i] + _dot(q_e[i], s[d].astype(BF16))
            s[d] = dec_col[i] * s[d] + s_inc[i]
        s_ref[0] = s[0]
        s_ref[1] = s[1]

    s_ref[...] = jnp.zeros_like(s_ref)
    n_lat = n_chunks - n_ctx_chunks

    def ctx_body(i, carry):
        step((i * nb, n_ctx_chunks - 1 - i * nb))
        return carry

    def lat_body(i, carry):
        step((n_ctx_chunks + i * nb, n_chunks - 1 - i * nb))
        return carry

    lax.fori_loop(0, n_ctx_chunks // nb, ctx_body, 0)
    lax.fori_loop(0, n_lat // nb, lat_body, 0)


def _gla_scan(p, gates, n_ctx, nb):
    b, t, _ = p.shape
    nkb = GLA_DK // GLA_HK
    return pl.pallas_call(
        functools.partial(_gla_scan_kernel, n_ctx_chunks=n_ctx // CHUNK, n_chunks=t // CHUNK, nb=nb),
        grid=(b, GLA_HEADS),
        in_specs=[
            pl.BlockSpec((1, t, GLA_HK), lambda i, h: (i, 0, h)),
            pl.BlockSpec((1, t, GLA_HK), lambda i, h: (i, 0, nkb + h)),
            pl.BlockSpec((1, t, GLA_HV), lambda i, h: (i, 0, (2 * GLA_DK) // GLA_HV + h)),
            pl.BlockSpec((1, t, GLA_HK), lambda i, h: (i, 0, h)),
            pl.BlockSpec((1, t, GLA_HK), lambda i, h: (i, 0, nkb + h)),
        ],
        out_specs=pl.BlockSpec((1, t, GLA_HV), lambda i, h: (i, 0, h)),
        out_shape=jax.ShapeDtypeStruct((b, t, GLA_DV), F32),
        scratch_shapes=[pltpu.VMEM((2, GLA_HK, GLA_HV), F32)],
        compiler_params=_cparams("parallel", "parallel"),
        name="gla_scan",
    )(p, p, p, gates, gates)


def _gla_out_kernel(u_ref, o_ref, gt_ref, ng_ref, mx_ref, mz_ref, w_ref, out_ref, *, tm, n_ctx):
    row0 = pl.program_id(1) * tm
    parts = []
    for h in range(GLA_HEADS):
        ls = slice(h * GLA_HV, (h + 1) * GLA_HV)
        ov = o_ref[0, :, ls]
        ov = ov * lax.rsqrt(jnp.mean(ov * ov, axis=-1, keepdims=True) + NORM_EPS) * ng_ref[...]
        gt = gt_ref[0, :, ls]
        parts.append((ov * (gt * _sigmoid(gt))).astype(BF16))
    out_ref[0] = _gated_residual(u_ref[0], parts, w_ref, mx_ref, mz_ref, row0, n_ctx)


def _gla_out(u, o, p, norm_g, mx, mz, w_o, n_ctx):
    b, t, d = u.shape
    tm = _pick_tile(t, 544)
    gt_blk = (2 * GLA_DK + GLA_DV) // d
    return pl.pallas_call(
        functools.partial(_gla_out_kernel, tm=tm, n_ctx=n_ctx),
        grid=(b, t // tm),
        in_specs=[
            pl.BlockSpec((1, tm, d), lambda i, j: (i, j, 0)),
            pl.BlockSpec((1, tm, d), lambda i, j: (i, j, 0)),
            pl.BlockSpec((1, tm, d), lambda i, j: (i, j, gt_blk)),
            _const_spec((1, GLA_HV)),
            pl.BlockSpec((1, 6, d), lambda i, j: (i, 0, 0)),
            _const_spec((6, d)),
            _const_spec((d, d)),
        ],
        out_specs=pl.BlockSpec((1, tm, d), lambda i, j: (i, j, 0)),
        out_shape=jax.ShapeDtypeStruct((b, t, d), F32),
        compiler_params=_cparams("parallel", "parallel"),
        name="gla_out",
    )(u, o, p, norm_g.reshape(1, GLA_HV), mx, mz, w_o)


def _rw_proj_kernel(u_ref, up_ref, un_ref, mx_ref, mz_ref, g_ref, mix_ref, wrkv_ref, w1_ref, w2_ref,
                    w0_ref, a1_ref, a2_ref, a0_ref, g1_ref, g2_ref, ka_ref, rk_ref,
                    r_ref, k_ref, v_ref, gate_ref, lw_ref, a_ref, bonus_ref, *, tm, n_ctx, t_total):
    row0 = pl.program_id(1) * tm
    g = g_ref[...]
    mx = mx_ref[0]
    mz = mz_ref[...]
    h = _norm_mod(u_ref[0], g, mx, mz, row0, n_ctx, 0, 1)
    h_prev = _norm_mod(up_ref[0], g, mx, mz, row0 - 8, n_ctx, 0, 1)[7:8]
    h_next = _norm_mod(un_ref[0], g, mx, mz, row0 + tm, n_ctx, 0, 1)[0:1]
    idx = lax.broadcasted_iota(jnp.int32, (tm, 1), 0)
    rows = row0 + idx
    h_dn = jnp.where(idx == 0, h_prev, pltpu.roll(h, 1, 0))
    h_dn = jnp.where((rows == 0) | (rows == n_ctx), 0.0, h_dn)
    h_up = jnp.where(idx == tm - 1, h_next, pltpu.roll(h, tm - 1, 0))
    h_up = jnp.where((rows == n_ctx - 1) | (rows == t_total - 1), 0.0, h_up)
    xx = 0.5 * (h_dn + h_up) - h
    xr, xw, xk, xv, xa, xg = ((h + xx * mix_ref[m:m + 1]).astype(BF16) for m in range(6))
    r = _dot(xr, wrkv_ref[0])
    k = _dot(xk, wrkv_ref[1])
    v = _dot(xv, wrkv_ref[2])
    r_ref[0] = r
    k_ref[0] = k
    v_ref[0] = v
    gate_ref[0] = _dot(_sigmoid(_dot(xg, g1_ref[...])).astype(BF16), g2_ref[...])
    tw = jnp.tanh(_dot(xw, w1_ref[...])).astype(BF16)
    lw_ref[0] = -math.exp(-0.5) * _sigmoid(w0_ref[...] + _dot(tw, w2_ref[...]))
    ta = _dot(xa, a1_ref[...]).astype(BF16)
    a = _sigmoid(a0_ref[...] + _dot(ta, a2_ref[...]))
    a_ref[0] = a
    lane_lo = lax.broadcasted_iota(jnp.int32, (1, LANES), 1) < RW_HEAD
    d = D_MODEL
    for p in range(d // LANES):
        ls = slice(p * LANES, (p + 1) * LANES)
        kd_sum = k[:, ls] * (2.0 + (a[:, ls] + a[:, d + p * LANES:d + (p + 1) * LANES] - 2.0) * ka_ref[:, ls])
        bonus_ref[0, :, ls] = _pair_sum(r[:, ls] * kd_sum * rk_ref[:, ls], lane_lo) * v[:, ls]


def _rw_proj(u, mx, mz, g, wts, n_ctx):
    b, t, d = u.shape
    tm = _pick_tile(t, 256)
    nb8 = t // 8
    tb = tm // 8
    full = lambda n: jax.ShapeDtypeStruct((b, t, n), F32)
    row_spec = lambda n: pl.BlockSpec((1, tm, n), lambda i, j: (i, j, 0))
    return pl.pallas_call(
        functools.partial(_rw_proj_kernel, tm=tm, n_ctx=n_ctx, t_total=t),
        grid=(b, t // tm),
        in_specs=[
            row_spec(d),
            pl.BlockSpec((1, 8, d), lambda i, j: (i, jnp.maximum(j * tb - 1, 0), 0)),
            pl.BlockSpec((1, 8, d), lambda i, j: (i, jnp.minimum((j + 1) * tb, nb8 - 1), 0)),
            pl.BlockSpec((1, 6, d), lambda i, j: (i, 0, 0)),
            _const_spec((6, d)),
            _const_spec((1, d)),
            _const_spec((6, d)),
            _const_spec((3, d, d)),
            _const_spec((d, 2 * RW_LORA)),
            _const_spec((2 * RW_LORA, 2 * d)),
            _const_spec((1, 2 * d)),
            _const_spec((d, 2 * RW_LORA)),
            _const_spec((2 * RW_LORA, 2 * d)),
            _const_spec((1, 2 * d)),
            _const_spec((d, 2 * LANES)),
            _const_spec((2 * LANES, d)),
            _const_spec((1, d)),
            _const_spec((1, d)),
        ],
        out_specs=[row_spec(d), row_spec(d), row_spec(d), row_spec(d), row_spec(2 * d), row_spec(2 * d),
                   row_spec(d)],
        out_shape=[full(d), full(d), full(d), full(d), full(2 * d), full(2 * d), full(d)],
        compiler_params=_cparams("parallel", "parallel"),
        name="rw_proj",
    )(u, u, u, mx, mz, g.reshape(1, d), wts["mix"], wts["w_rkv"], wts["w1"], wts["w2"], wts["w0"],
      wts["a1"], wts["a2"], wts["a0"], wts["g1"], wts["g2"], wts["k_a"], wts["r_k"])


def _rw_pair_terms(units, side_work=()):
    C = CHUNK
    lane_lo = lax.broadcasted_iota(jnp.int32, (1, LANES), 1) < RW_HEAD
    row2 = lax.broadcasted_iota(jnp.int32, (C, LANES), 0)
    col2 = lax.broadcasted_iota(jnp.int32, (C, LANES), 1) & (C - 1)
    strict2 = (row2 > col2, row2 < col2)
    incl2 = (row2 >= col2, row2 <= col2)
    tri_b = tuple(jnp.where(_tri(C, upper, False), 1.0, 0.0).astype(BF16) for upper in (False, True))
    eye2 = jnp.where(row2 == col2, 1.0, 0.0)
    nu = len(units)

    def bd(x):
        zero = jnp.zeros_like(x)
        return jnp.concatenate([jnp.where(lane_lo, x, zero), jnp.where(lane_lo, zero, x)], axis=0)

    def hilo(x):
        hi = x.astype(BF16)
        return hi, (x - hi.astype(F32)).astype(BF16)

    side = iter(side_work)

    def stage_done():
        thunk = next(side, None)
        if thunk is not None:
            thunk()

    pre = []
    for r2, k2, v2, lw2, a2, desc, kkw, kaw in units:
        d = int(desc)
        kk = k2 * kkw
        kk = kk * lax.rsqrt(jnp.maximum(_pair_sum(kk * kk, lane_lo), 1e-24))
        kd = k2 * (1.0 + (a2 - 1.0) * kaw)
        bvec = kk * a2
        cum = _tri_cumsum(tri_b[d], lw2)
        tot = jnp.sum(lw2, axis=0, keepdims=True)
        e_neg = jnp.exp(-cum)
        e_rem = jnp.exp(tot - cum)
        pre.append(dict(
            d=d, tot=tot, v_b=v2.astype(BF16),
            a_t=-kk * jnp.exp(cum - lw2), r_t=r2 * jnp.exp(cum),
            b_h=(bvec * e_neg).astype(BF16), k_h=(kd * e_neg).astype(BF16),
            b_rem_t=(bvec * e_rem).T.astype(BF16), k_rem_t=(kd * e_rem).T.astype(BF16)))

    ar = [jnp.concatenate([p["a_t"], p["r_t"]], axis=0).astype(BF16) for p in pre]
    p_b = [_dot_nt(ar[u], bd(pre[u]["b_h"])) for u in range(nu)]
    p_k = [_dot_nt(ar[u], bd(pre[u]["k_h"])) for u in range(nu)]
    l_ab = [jnp.where(strict2[pre[u]["d"]], p_b[u][:C], 0.0) for u in range(nu)]
    m_rb = [jnp.where(incl2[pre[u]["d"]], p_b[u][C:], 0.0).astype(BF16) for u in range(nu)]
    l_ak = [jnp.where(strict2[pre[u]["d"]], p_k[u][:C], 0.0).astype(BF16) for u in range(nu)]
    m_rk = [jnp.where(incl2[pre[u]["d"]], p_k[u][C:], 0.0).astype(BF16) for u in range(nu)]
    stage_done()

    xo = l_ab
    l_hl = [hilo(l) for l in xo]
    r0 = [_dot(jnp.concatenate([hi, lo], axis=0), bd(hi)) for hi, lo in l_hl]
    r0b = [_dot(hi, bd(lo)) for hi, lo in l_hl]
    pw = [a[:C] + a[C:] + b for a, b in zip(r0, r0b)]
    lhs_hl = l_hl
    for _ in range(2):
        p_hl = [hilo(p) for p in pw]
        ra = [_dot(jnp.concatenate([ph, plo, xh, xlo], axis=0), bd(ph))
              for (ph, plo), (xh, xlo) in zip(p_hl, lhs_hl)]
        rb = [_dot(jnp.concatenate([ph, xh], axis=0), bd(plo)) for (ph, plo), (xh, _) in zip(p_hl, lhs_hl)]
        xo = [x + p + (a[2 * C:3 * C] + a[3 * C:] + b[C:]) for x, p, a, b in zip(xo, pw, ra, rb)]
        pw = [a[:C] + a[C:2 * C] + b[:C] for a, b in zip(ra, rb)]
        lhs_hl = [hilo(x) for x in xo]
        stage_done()
    for _ in range(2):
        p_b16 = [p.astype(BF16) for p in pw]
        ra = [_dot(jnp.concatenate([p, x.astype(BF16)], axis=0), bd(p)) for p, x in zip(p_b16, xo)]
        xo = [x + p + a[C:] for x, p, a in zip(xo, pw, ra)]
        pw = [a[:C] for a in ra]
        stage_done()
    xo = [x + p + _dot(x.astype(BF16), bd(p.astype(BF16))) for x, p in zip(xo, pw)]
    t_inv = [(eye2 + x).astype(BF16) for x in xo]

    lvyk = [_dot(jnp.concatenate([l_ak[u], m_rk[u]], axis=0), bd(pre[u]["v_b"])) for u in range(nu)]
    kv = [_dot(pre[u]["k_rem_t"], pre[u]["v_b"]) for u in range(nu)]
    tw = [_dot(t_inv[u], jnp.concatenate([bd(ar[u][:C]), bd(lvyk[u][:C].astype(BF16))], axis=1))
          for u in range(nu)]
    qy = [_dot(m_rb[u], jnp.concatenate([bd(tw[u][:, :LANES].astype(BF16)),
                                         bd(tw[u][:, LANES:].astype(BF16))], axis=1)) for u in range(nu)]
    gh = [_dot(pre[u]["b_rem_t"], tw[u].astype(BF16)) for u in range(nu)]

    r = lax.broadcasted_iota(jnp.int32, (LANES, LANES), 0)
    c = lax.broadcasted_iota(jnp.int32, (LANES, LANES), 1)
    same_head = (r < RW_HEAD) == (c < RW_HEAD)
    out = []
    for u in range(nu):
        qp = pre[u]["r_t"] + qy[u][:, :LANES]
        y0 = qy[u][:, LANES:] + lvyk[u][C:]
        g_mat = jnp.where(same_head, gh[u][:, :LANES], 0.0) + jnp.where(r == c, jnp.exp(pre[u]["tot"]), 0.0)
        h_mat = jnp.where(same_head, gh[u][:, LANES:] + kv[u], 0.0)
        out.append((qp, y0, g_mat, h_mat))
    return out


def _rw_mix_kernel(r0_ref, k0_ref, v0_ref, lw0_ref, a0_ref, r1_ref, k1_ref, v1_ref, lw1_ref, a1_ref,
                   kk_ref, ka_ref, y0_ref, y1_ref, st_ref, qp_s, yc_s, g_s, h_s, *, nb, pp):
    def zero_scratch(_, carry):
        for ref in (st_ref, qp_s, yc_s, g_s, h_s):
            ref[...] = jnp.zeros_like(ref)
        return carry

    lax.fori_loop(0, jnp.where(pl.program_id(2) == 0, 1, 0), zero_scratch, 0)

    y_refs = (y0_ref, y1_ref)
    chains = [(d, p) for d in range(2) for p in range(pp)]
    states = {dp: st_ref[dp[0], dp[1]] for dp in chains}
    prev_terms = {(d, p, c): (qp_s[d, p, c], yc_s[d, p, c], g_s[d, p, c], h_s[d, p, c])
                  for d, p in chains for c in range(nb)}

    def link(ci):
        def emit():
            for d, p in chains:
                c = ci if d == 0 else nb - 1 - ci
                qp, yc, g_mat, h_mat = prev_terms[d, p, c]
                st_b = states[d, p].astype(BF16)
                y_refs[d][0, c * CHUNK:(c + 1) * CHUNK, p * LANES:(p + 1) * LANES] = _dot(qp, st_b) + yc
                states[d, p] = _dot(g_mat, st_b) + h_mat
        return emit

    ins = ((r0_ref, k0_ref, v0_ref, lw0_ref, a0_ref), (r1_ref, k1_ref, v1_ref, lw1_ref, a1_ref))
    units = []
    index = []
    for p in range(pp):
        ls = slice(p * LANES, (p + 1) * LANES)
        for c in range(nb):
            rows = slice(c * CHUNK, (c + 1) * CHUNK)
            for d in range(2):
                units.append(tuple(ref[0, rows, ls] for ref in ins[d]) + (d == 1, kk_ref[:, ls], ka_ref[:, ls]))
                index.append((d, p, c))
    links = [link(ci) for ci in range(nb)]
    assert nb <= 5, "one recurrence link per stage boundary of _rw_pair_terms"
    results = _rw_pair_terms(units, side_work=links)
    for d, p in chains:
        st_ref[d, p] = states[d, p]
    for (d, p, c), (qp, y0, g_mat, h_mat) in zip(index, results):
        qp_s[d, p, c] = qp.astype(BF16)
        yc_s[d, p, c] = y0
        g_s[d, p, c] = g_mat.astype(BF16)
        h_s[d, p, c] = h_mat


def _rw_mix(r, k, v, lw, a, k_k, k_a, n_ctx, nb):
    b, t, d = r.shape
    npairs = d // LANES
    rows = nb * CHUNK
    nblk = t // rows
    ncb = n_ctx // rows

    def rev(j):
        return jnp.where(j < ncb, ncb - 1 - j, nblk - 1 - (j - ncb))

    cur = lambda j: jnp.minimum(j, nblk - 1)
    prev = lambda j: jnp.maximum(j - 1, 0)
    pp = 2
    pw = pp * LANES
    ng = npairs // pp
    fwd = lambda off, blk: pl.BlockSpec((1, rows, pw), lambda i, p, j: (i, blk(j), off + p))
    bwd = lambda off, blk: pl.BlockSpec((1, rows, pw), lambda i, p, j: (i, rev(blk(j)), off + p))
    par = pl.BlockSpec((1, pw), lambda i, p, j: (0, p))
    return pl.pallas_call(
        functools.partial(_rw_mix_kernel, nb=nb, pp=pp),
        grid=(b, ng, nblk + 1),
        in_specs=[fwd(0, cur), fwd(0, cur), fwd(0, cur), fwd(0, cur), fwd(0, cur),
                  bwd(0, cur), bwd(0, cur), bwd(0, cur), bwd(ng, cur), bwd(ng, cur), par, par],
        out_specs=[fwd(0, prev), bwd(0, prev)],
        out_shape=[jax.ShapeDtypeStruct((b, t, d), F32), jax.ShapeDtypeStruct((b, t, d), F32)],
        scratch_shapes=[pltpu.VMEM((2, pp, LANES, LANES), F32),
                        pltpu.VMEM((2, pp, nb, CHUNK, LANES), BF16), pltpu.VMEM((2, pp, nb, CHUNK, LANES), F32),
                        pltpu.VMEM((2, pp, nb, LANES, LANES), BF16), pltpu.VMEM((2, pp, nb, LANES, LANES), F32)],
        compiler_params=_cparams("parallel", "parallel", "arbitrary"),
        name="rw_mix",
    )(r, k, v, lw, a, r, k, v, lw, a, k_k.reshape(1, d), k_a.reshape(1, d))


def _rw_out_kernel(u_ref, y0_ref, y1_ref, bonus_ref, gate_ref, lng_ref, lnb_ref, mx_ref, mz_ref, w_ref,
                   out_ref, *, tm, n_ctx):
    row0 = pl.program_id(1) * tm
    lane_lo = lax.broadcasted_iota(jnp.int32, (1, LANES), 1) < RW_HEAD
    parts = []
    for p in range(D_MODEL // LANES):
        ls = slice(p * LANES, (p + 1) * LANES)
        y = y0_ref[0, :, ls] + y1_ref[0, :, ls]
        mu = _pair_sum(y, lane_lo) * (1.0 / RW_HEAD)
        yc = y - mu
        var = _pair_sum(yc * yc, lane_lo) * (1.0 / RW_HEAD)
        yn = yc * lax.rsqrt(var + RW_GN_EPS) * lng_ref[:, ls] + lnb_ref[:, ls]
        parts.append(((yn + bonus_ref[0, :, ls]) * gate_ref[0, :, ls]).astype(BF16))
    out_ref[0] = _gated_residual(u_ref[0], parts, w_ref, mx_ref, mz_ref, row0, n_ctx)


def _rw_out(u, y0, y1, bonus, gate, ln_g, ln_b, mx, mz, w_o, n_ctx):
    b, t, d = u.shape
    tm = _pick_tile(t, 544)
    tok = pl.BlockSpec((1, tm, d), lambda i, j: (i, j, 0))
    vec = _const_spec((1, d))
    return pl.pallas_call(
        functools.partial(_rw_out_kernel, tm=tm, n_ctx=n_ctx),
        grid=(b, t // tm),
        in_specs=[tok] * 5 + [vec, vec, pl.BlockSpec((1, 6, d), lambda i, j: (i, 0, 0)),
                              _const_spec((6, d)), _const_spec((d, d))],
        out_specs=tok,
        out_shape=jax.ShapeDtypeStruct((b, t, d), F32),
        compiler_params=_cparams("parallel", "parallel"),
        name="rw_out",
    )(u, y0, y1, bonus, gate, ln_g.reshape(1, d), ln_b.reshape(1, d), mx, mz, w_o)


def _block_rows(w_pair):
    _, k, n = w_pair.shape
    z = jnp.zeros((k, n), w_pair.dtype)
    return jnp.concatenate([jnp.concatenate([w_pair[0], z], axis=1),
                            jnp.concatenate([z, w_pair[1]], axis=1)], axis=0)


def kernel(x, c, ctx, c_ctx, ada_w, ada_b, norm1_g, norm2_g, mlp_w1, mlp_w2, final_g, na_w_qkv, na_w_o, na_rpb, gla_w_in, gla_w_dec2, gla_b_dec, gla_norm_g, gla_w_o, rw_mix, rw_w_rkv, rw_w0, rw_w1, rw_w2, rw_a0, rw_a1, rw_a2, rw_g1, rw_g2, rw_k_k, rw_k_a, rw_r_k, rw_ln_g, rw_ln_b, rw_w_o):
    b, seq, d = x.shape
    n_ctx = ctx.shape[1]
    depth = ada_w.shape[0]
    assert d == D_MODEL and n_ctx % CHUNK == 0 and seq % CHUNK == 0

    u = jnp.concatenate([ctx.astype(x.dtype), x], axis=1)

    rows = -(-(b + 1) // 8) * 8
    c_all = jnp.zeros((rows, d), F32).at[:b].set(c).at[b].set(c_ctx)
    tables = _ada_tables(c_all, ada_w, ada_b)
    mods_x = tables[:, :b].reshape(depth, b, 6, d)
    mods_z = tables[:, b].reshape(depth, 6, d)

    nb = max(n for n in (4, 2, 1) if (n_ctx // CHUNK) % n == 0 and (seq // CHUNK) % n == 0)

    for i in range(depth):
        kind, j = i % 3, i // 3
        last = i == depth - 1
        mx, mz = mods_x[i], mods_z[i]
        if kind == 0:
            qkv = _norm_proj(u, mx, mz, norm1_g[i], na_w_qkv[j].astype(BF16), n_ctx, BF16)
            o = _na_attention(qkv, _na_bias_table(na_rpb[j]), n_ctx, with_ctx=not last)
            u = _na_out(u, o, mx, mz, na_w_o[j].astype(BF16), n_ctx, latent_only=last)
        elif kind == 1:
            n_main = 2 * GLA_DK + 2 * GLA_DV
            w_in = gla_w_in[j]
            w_lr = jnp.zeros((d, LANES), F32).at[:, :2 * GLA_LOW_RANK].set(w_in[:, n_main:])
            w_dec = jnp.zeros((LANES, 2 * GLA_DK), F32).at[:2 * GLA_LOW_RANK].set(_block_rows(gla_w_dec2[j]))
            p, gates = _gla_proj(u, mx, mz, norm1_g[i], w_in[:, :n_main].astype(BF16), w_lr.astype(BF16),
                                 w_dec.astype(BF16), gla_b_dec[j].reshape(1, 2 * GLA_DK), n_ctx)
            o = _gla_scan(p, gates, n_ctx, nb)
            u = _gla_out(u, o, p, gla_norm_g[j], mx, mz, gla_w_o[j].astype(BF16), n_ctx)
        else:
            g1 = jnp.zeros((d, 2 * LANES), F32).at[:, :RW_GATE_LORA].set(rw_g1[j])
            g2 = jnp.zeros((2 * LANES, d), F32).at[:RW_GATE_LORA].set(rw_g2[j])
            wts = dict(
                mix=rw_mix[j], w_rkv=rw_w_rkv[j].astype(BF16),
                w1=jnp.concatenate([rw_w1[j, 0], rw_w1[j, 1]], axis=1).astype(BF16),
                w2=_block_rows(rw_w2[j]).astype(BF16), w0=rw_w0[j].reshape(1, 2 * d),
                a1=jnp.concatenate([rw_a1[j, 0], rw_a1[j, 1]], axis=1).astype(BF16),
                a2=_block_rows(rw_a2[j]).astype(BF16), a0=rw_a0[j].reshape(1, 2 * d),
                g1=g1.astype(BF16), g2=g2.astype(BF16),
                k_a=rw_k_a[j].reshape(1, d), r_k=rw_r_k[j].reshape(1, d))
            r, k, v, gate, lw, a, bonus = _rw_proj(u, mx, mz, norm1_g[i], wts, n_ctx)
            y0, y1 = _rw_mix(r, k, v, lw, a, rw_k_k[j], rw_k_a[j], n_ctx, nb)
            u = _rw_out(u, y0, y1, bonus, gate, rw_ln_g[j], rw_ln_b[j], mx, mz, rw_w_o[j].astype(BF16), n_ctx)
        if last and u.shape[1] != seq:
            u = u[:, n_ctx:]
        u = _mlp(u, mx, mz, norm2_g[i], mlp_w1[i].astype(BF16), mlp_w2[i].astype(BF16), final_g,
                 0 if last else n_ctx, final=last)
    return u
```

```python
import functools
import math

import jax
import jax.numpy as jnp
from jax import lax
from jax.experimental import pallas as pl
from jax.experimental.pallas import tpu as pltpu

F32 = jnp.float32
BF16 = jnp.bfloat16

D_MODEL = 1024
D_FF = 4 * D_MODEL
NORM_EPS = 1e-6
GRID_W = 64

NA_HEAD_DIM = 64
NA_HEADS = D_MODEL // NA_HEAD_DIM
NA_WIN_R = 8
NA_WIN_C = 16
NA_MASK = -1e30

GLA_HEADS = 4
GLA_DK = D_MODEL // 2
GLA_DV = D_MODEL
GLA_HK = GLA_DK // GLA_HEADS
GLA_HV = GLA_DV // GLA_HEADS
GLA_LOW_RANK = 16
GLA_GATE_NORM = 16.0
CHUNK = 64

RW_HEAD = 64
RW_GATE_LORA = 160
RW_LORA = 64
RW_GN_EPS = 64e-5

LANES = 128
VMEM_LIMIT = 56 * 1024 * 1024


def _cparams(*sem):
    return pltpu.CompilerParams(dimension_semantics=sem, vmem_limit_bytes=VMEM_LIMIT)


def _pick_tile(n, target, mult=16):
    best = None
    for t in range(mult, min(n, target) + 1, mult):
        if n % t == 0:
            best = t
    assert best is not None, (n, target)
    return best


def _const_spec(shape):
    nd = len(shape)
    return pl.BlockSpec(shape, lambda *_: (0,) * nd, pipeline_mode=pl.Buffered(1))


def _dot(a, b):
    return jnp.dot(a, b, preferred_element_type=F32)


def _dot_nt(a, b):
    return lax.dot_general(a, b, (((1,), (1,)), ((), ())), preferred_element_type=F32)


def _sigmoid(x):
    return 0.5 + 0.5 * jnp.tanh(0.5 * x)


def _softplus(x):
    return jnp.maximum(x, 0.0) + jnp.log(1.0 + jnp.exp(-jnp.abs(x)))


def _rms(u, g):
    return u * lax.rsqrt(jnp.mean(u * u, axis=-1, keepdims=True) + NORM_EPS) * g


def _mod_rows(mx, mz, idx, row0, rows, n_ctx):
    vx = mx[idx:idx + 1]
    if n_ctx == 0:
        return vx
    if n_ctx % rows == 0:
        return jnp.where(row0 < n_ctx, mz[idx:idx + 1], vx)
    r = row0 + lax.broadcasted_iota(jnp.int32, (rows, 1), 0)
    return jnp.where(r < n_ctx, mz[idx:idx + 1], vx)


def _norm_mod(u, g, mx, mz, row0, n_ctx, i_shift, i_scale):
    rows = u.shape[0]
    shift = _mod_rows(mx, mz, i_shift, row0, rows, n_ctx)
    scale = _mod_rows(mx, mz, i_scale, row0, rows, n_ctx)
    return _rms(u, g) * (1.0 + scale) + shift


def _pair_sum(x, lane_lo):
    s0 = jnp.sum(jnp.where(lane_lo, x, 0.0), axis=-1, keepdims=True)
    s1 = jnp.sum(jnp.where(lane_lo, 0.0, x), axis=-1, keepdims=True)
    return jnp.where(lane_lo, s0, s1)


def _ada_kernel(c_ref, w_ref, b_ref, o_ref):
    c = c_ref[...]
    sc = (c * _sigmoid(c)).astype(BF16)
    o_ref[0] = _dot(sc, w_ref[0]) + b_ref[0]


def _ada_tables(c_all, ada_w, ada_b):
    depth, d, n = ada_w.shape
    rows = c_all.shape[0]
    tn = 1536
    return pl.pallas_call(
        _ada_kernel,
        grid=(depth, n // tn),
        in_specs=[
            pl.BlockSpec((rows, d), lambda l, j: (0, 0)),
            pl.BlockSpec((1, d, tn), lambda l, j: (l, 0, j)),
            pl.BlockSpec((1, 1, tn), lambda l, j: (l, 0, j)),
        ],
        out_specs=pl.BlockSpec((1, rows, tn), lambda l, j: (l, 0, j)),
        out_shape=jax.ShapeDtypeStruct((depth, rows, n), F32),
        compiler_params=_cparams("parallel", "parallel"),
        name="ada_tables",
    )(c_all, ada_w.astype(BF16), ada_b.reshape(depth, 1, n))


def _proj_kernel(u_ref, mx_ref, mz_ref, g_ref, w_ref, o_ref, *, tm, n_ctx, nsplit):
    t = pl.program_id(1)
    h = _norm_mod(u_ref[0], g_ref[...], mx_ref[0], mz_ref[...], t * tm, n_ctx, 0, 1).astype(BF16)
    n = w_ref.shape[1] // nsplit
    for s in range(nsplit):
        o_ref[0, :, s * n:(s + 1) * n] = _dot(h, w_ref[:, s * n:(s + 1) * n]).astype(o_ref.dtype)


def _norm_proj(u, mx, mz, g, w, n_ctx, out_dtype):
    b, t, d = u.shape
    n = w.shape[1]
    tm = _pick_tile(t, 544)
    return pl.pallas_call(
        functools.partial(_proj_kernel, tm=tm, n_ctx=n_ctx, nsplit=n // D_MODEL),
        grid=(b, t // tm),
        in_specs=[
            pl.BlockSpec((1, tm, d), lambda i, j: (i, j, 0)),
            pl.BlockSpec((1, 6, d), lambda i, j: (i, 0, 0)),
            _const_spec((6, d)),
            _const_spec((1, d)),
            _const_spec((d, n)),
        ],
        out_specs=pl.BlockSpec((1, tm, n), lambda i, j: (i, j, 0)),
        out_shape=jax.ShapeDtypeStruct((b, t, n), out_dtype),
        compiler_params=_cparams("parallel", "parallel"),
        name="norm_proj",
    )(u, mx, mz, g.reshape(1, d), w)


def _na_bias_table(rpb):
    w = GRID_W
    cols = jnp.arange(w)
    c_start = jnp.clip(cols - NA_WIN_C // 2, 0, w - NA_WIN_C)
    col_ok = (cols[None, :] >= c_start[:, None]) & (cols[None, :] < c_start[:, None] + NA_WIN_C)
    col_idx = jnp.clip(cols[None, :] - cols[:, None], -(NA_WIN_C - 1), NA_WIN_C - 1) + NA_WIN_C - 1
    onehot = col_idx[:, :, None] == jnp.arange(2 * NA_WIN_C - 1)
    m = jnp.sum(jnp.where(onehot[None, None], rpb.astype(F32)[:, :, None, None, :], 0.0), axis=-1)
    m = jnp.where(col_ok[None, None], m, NA_MASK)
    return jnp.concatenate([m[:, :-1], m[:, 1:]], axis=-1)


def _na_kernel(q_ref, k_ref, v_ref, bias_ref, o_ref, *, n_ctx, n_rows, with_ctx, rq):
    nctxb = n_ctx // GRID_W
    out_off = 0 if with_ctx else n_ctx
    scale = NA_HEAD_DIM ** -0.5
    assert scale == 0.125, "the score scale is folded into bf16 q, exact only for a power of two"
    lane_lo = lax.broadcasted_iota(jnp.int32, (1, LANES), 1) < NA_HEAD_DIM
    head_masks = (jnp.where(lane_lo, scale, 0.0).astype(BF16), jnp.where(lane_lo, 0.0, scale).astype(BF16))
    n_pairs = q_ref.shape[2] // LANES
    strip = NA_WIN_R * GRID_W
    jobs_idx = [(qi, p) for qi in range(rq) for p in range(n_pairs)]

    def q_rows(tok0, qi):
        return pl.ds(pl.multiple_of(tok0 + qi * GRID_W, GRID_W), GRID_W)

    def stacked_q(tok0, qi, p):
        q2 = q_ref[0, q_rows(tok0, qi), p * LANES:(p + 1) * LANES]
        return jnp.concatenate([q2 * head_masks[0], q2 * head_masks[1]], axis=0)

    def attend(tok0, jobs):
        scores = [[_dot_nt(q, k) if b is None else _dot_nt(q, k) + b
                   for k, b in zip(ks, bs)] for q, ks, _, bs in jobs]
        tops = []
        for sc in scores:
            m = sc[0].max(axis=-1, keepdims=True)
            for s in sc[1:]:
                m = jnp.maximum(m, s.max(axis=-1, keepdims=True))
            tops.append(m)
        exps = [[jnp.exp(s - m) for s in sc] for sc, m in zip(scores, tops)]
        dens = []
        for es in exps:
            den = es[0].sum(axis=-1, keepdims=True)
            for e in es[1:]:
                den = den + e.sum(axis=-1, keepdims=True)
            dens.append(den)
        for (qi, p), (_, _, vs, _), es, den in zip(jobs_idx, jobs, exps, dens):
            o = _dot(es[0].astype(BF16), vs[0])
            for e, v in zip(es[1:], vs[1:]):
                o = o + _dot(e.astype(BF16), v)
            o = o / den
            o2 = jnp.where(lane_lo, o[:GRID_W], o[GRID_W:])
            o_ref[0, q_rows(tok0 - out_off, qi), p * LANES:(p + 1) * LANES] = o2.astype(o_ref.dtype)

    def latent_rows(step, carry):
        tok0 = n_ctx + step * (rq * GRID_W)
        jobs = []
        for qi, p in jobs_idx:
            ls = slice(p * LANES, (p + 1) * LANES)
            r = step * rq + qi
            r0 = jnp.clip(r - NA_WIN_R // 2, 0, n_rows - NA_WIN_R)
            start = pl.multiple_of(n_ctx + r0 * GRID_W, GRID_W)
            ri0 = NA_WIN_R - 1 - (r - r0)
            bias = jnp.concatenate(
                [jnp.concatenate([bias_ref[2 * p + hh, ri0 + 2 * m] for m in range(NA_WIN_R // 2)], axis=1)
                 for hh in range(2)], axis=0)
            jobs.append((stacked_q(tok0, qi, p),
                         [k_ref[0, pl.ds(start, strip), ls], k_ref[0, 0:n_ctx, ls]],
                         [v_ref[0, pl.ds(start, strip), ls], v_ref[0, 0:n_ctx, ls]],
                         [bias, None]))
        attend(tok0, jobs)
        return carry

    def context_rows(step, carry):
        tok0 = step * (rq * GRID_W)
        jobs = []
        for qi, p in jobs_idx:
            ls = slice(p * LANES, (p + 1) * LANES)
            jobs.append((stacked_q(tok0, qi, p), [k_ref[0, 0:n_ctx, ls]], [v_ref[0, 0:n_ctx, ls]], [None]))
        attend(tok0, jobs)
        return carry

    if with_ctx:
        lax.fori_loop(0, nctxb // rq, context_rows, 0)
    lax.fori_loop(0, n_rows // rq, latent_rows, 0)


def _na_attention(qkv, bias, n_ctx, with_ctx):
    b, t, _ = qkv.shape
    seq = t - n_ctx
    n_rows = seq // GRID_W
    assert seq % GRID_W == 0 and n_ctx % GRID_W == 0 and n_rows >= NA_WIN_R
    hw = 2 * LANES
    ng = D_MODEL // hw
    nctxb = n_ctx // GRID_W
    rq = 4 if nctxb % 4 == 0 and n_rows % 4 == 0 else 2
    assert nctxb % rq == 0 and n_rows % rq == 0
    t_out = t if with_ctx else seq

    return pl.pallas_call(
        functools.partial(_na_kernel, n_ctx=n_ctx, n_rows=n_rows, with_ctx=with_ctx, rq=rq),
        grid=(b, ng),
        in_specs=[
            pl.BlockSpec((1, t, hw), lambda i, g: (i, 0, g)),
            pl.BlockSpec((1, t, hw), lambda i, g: (i, 0, ng + g)),
            pl.BlockSpec((1, t, hw), lambda i, g: (i, 0, 2 * ng + g)),
            pl.BlockSpec((hw // NA_HEAD_DIM, 2 * NA_WIN_R - 2, GRID_W, LANES), lambda i, g: (g, 0, 0, 0)),
        ],
        out_specs=pl.BlockSpec((1, t_out, hw), lambda i, g: (i, 0, g)),
        out_shape=jax.ShapeDtypeStruct((b, t_out, D_MODEL), BF16),
        compiler_params=_cparams("parallel", "parallel"),
        name="na_attention",
    )(qkv, qkv, qkv, bias)


def _gated_residual(u, o_bf16_parts, w_ref, mx_ref, mz_ref, row0, n_ctx):
    acc = None
    k0 = 0
    for part in o_bf16_parts:
        kw = part.shape[1]
        term = _dot(part, w_ref[k0:k0 + kw, :])
        acc = term if acc is None else acc + term
        k0 += kw
    gate = _mod_rows(mx_ref[0], mz_ref[...], 2, row0, u.shape[0], n_ctx)
    return u + gate * acc


def _na_out_kernel(u_ref, o_ref, mx_ref, mz_ref, w_ref, out_ref, *, tm, n_ctx, blk_off):
    row0 = (pl.program_id(1) + blk_off) * tm
    out_ref[0] = _gated_residual(u_ref[0], [o_ref[0]], w_ref, mx_ref, mz_ref, row0, n_ctx)


def _na_out(u, o, mx, mz, w_o, n_ctx, latent_only):
    b, t, d = u.shape
    if latent_only:
        tm = _pick_tile(n_ctx, 256)
        assert (t - n_ctx) % tm == 0
        blk_off = n_ctx // tm
        nt = (t - n_ctx) // tm
    else:
        tm = _pick_tile(t, 544)
        blk_off = 0
        nt = t // tm
    return pl.pallas_call(
        functools.partial(_na_out_kernel, tm=tm, n_ctx=n_ctx, blk_off=blk_off),
        grid=(b, nt),
        in_specs=[
            pl.BlockSpec((1, tm, d), lambda i, j: (i, j + blk_off, 0)),
            pl.BlockSpec((1, tm, d), lambda i, j: (i, j, 0)),
            pl.BlockSpec((1, 6, d), lambda i, j: (i, 0, 0)),
            _const_spec((6, d)),
            _const_spec((d, d)),
        ],
        out_specs=pl.BlockSpec((1, tm, d), lambda i, j: (i, j, 0)),
        out_shape=jax.ShapeDtypeStruct((b, nt * tm, d), F32),
        compiler_params=_cparams("parallel", "parallel"),
        name="na_out",
    )(u, o, mx, mz, w_o)


def _mlp_kernel(u_ref, mx_ref, mz_ref, g_ref, w1_ref, w2_ref, fg_ref, out_ref, *, tm, n_ctx, fc, final):
    row0 = pl.program_id(1) * tm
    u = u_ref[0]
    mx = mx_ref[0]
    mz = mz_ref[...]
    h = _norm_mod(u, g_ref[...], mx, mz, row0, n_ctx, 3, 4).astype(BF16)
    acc = None
    for c in range(D_FF // fc):
        a = _dot(h, w1_ref[:, c * fc:(c + 1) * fc])
        a = jnp.square(jnp.maximum(a, 0.0)).astype(BF16)
        term = _dot(a, w2_ref[c * fc:(c + 1) * fc, :])
        acc = term if acc is None else acc + term
    y = u + _mod_rows(mx, mz, 5, row0, tm, n_ctx) * acc
    if final:
        y = _rms(y, fg_ref[...])
    out_ref[0] = y


def _mlp(u, mx, mz, g, w1, w2, final_g, n_ctx, final):
    b, t, d = u.shape
    tm = _pick_tile(t, 544)
    return pl.pallas_call(
        functools.partial(_mlp_kernel, tm=tm, n_ctx=n_ctx, fc=1024, final=final),
        grid=(b, t // tm),
        in_specs=[
            pl.BlockSpec((1, tm, d), lambda i, j: (i, j, 0)),
            pl.BlockSpec((1, 6, d), lambda i, j: (i, 0, 0)),
            _const_spec((6, d)),
            _const_spec((1, d)),
            _const_spec((d, D_FF)),
            _const_spec((D_FF, d)),
            _const_spec((1, d)),
        ],
        out_specs=pl.BlockSpec((1, tm, d), lambda i, j: (i, j, 0)),
        out_shape=jax.ShapeDtypeStruct((b, t, d), F32),
        compiler_params=_cparams("parallel", "parallel"),
        name="mlp",
    )(u, mx, mz, g.reshape(1, d), w1, w2, final_g.reshape(1, d))


def _gla_proj_kernel(u_ref, mx_ref, mz_ref, g_ref, w_ref, wlr_ref, wdec_ref, bdec_ref,
                     p_ref, gate_ref, *, tm, n_ctx):
    t = pl.program_id(1)
    h = _norm_mod(u_ref[0], g_ref[...], mx_ref[0], mz_ref[...], t * tm, n_ctx, 0, 1).astype(BF16)
    n = D_MODEL
    for s in range(w_ref.shape[1] // n):
        p_ref[0, :, s * n:(s + 1) * n] = _dot(h, w_ref[:, s * n:(s + 1) * n])
    lr = _dot(h, wlr_ref[...]).astype(BF16)
    z = _dot(lr, wdec_ref[...]) + bdec_ref[...]
    gate_ref[0] = -_softplus(-z) * (1.0 / GLA_GATE_NORM)


def _gla_proj(u, mx, mz, g, w_main, w_lr, w_dec, b_dec, n_ctx):
    b, t, d = u.shape
    n = w_main.shape[1]
    tm = _pick_tile(t, 544)
    return pl.pallas_call(
        functools.partial(_gla_proj_kernel, tm=tm, n_ctx=n_ctx),
        grid=(b, t // tm),
        in_specs=[
            pl.BlockSpec((1, tm, d), lambda i, j: (i, j, 0)),
            pl.BlockSpec((1, 6, d), lambda i, j: (i, 0, 0)),
            _const_spec((6, d)),
            _const_spec((1, d)),
            _const_spec((d, n)),
            _const_spec((d, LANES)),
            _const_spec((LANES, 2 * GLA_DK)),
            _const_spec((1, 2 * GLA_DK)),
        ],
        out_specs=[
            pl.BlockSpec((1, tm, n), lambda i, j: (i, j, 0)),
            pl.BlockSpec((1, tm, 2 * GLA_DK), lambda i, j: (i, j, 0)),
        ],
        out_shape=[
            jax.ShapeDtypeStruct((b, t, n), F32),
            jax.ShapeDtypeStruct((b, t, 2 * GLA_DK), F32),
        ],
        compiler_params=_cparams("parallel", "parallel"),
        name="gla_proj",
    )(u, mx, mz, g.reshape(1, d), w_main, w_lr, w_dec, b_dec)


def _tri(n, upper, strict):
    r = lax.broadcasted_iota(jnp.int32, (n, n), 0)
    c = lax.broadcasted_iota(jnp.int32, (n, n), 1)
    if upper:
        return (r < c) if strict else (r <= c)
    return (r > c) if strict else (r >= c)


def _cumsum_rows(x, descending):
    n = x.shape[0]
    row = lax.broadcasted_iota(jnp.int32, (n, 1), 0)
    s = 1
    while s < n:
        if descending:
            x = x + jnp.where(row < n - s, pltpu.roll(x, n - s, 0), 0.0)
        else:
            x = x + jnp.where(row >= s, pltpu.roll(x, s, 0), 0.0)
        s *= 2
    return x


def _gla_scan_kernel(q_ref, k_ref, v_ref, g0_ref, g1_ref, o_ref, s_ref, *, n_ctx_chunks, n_chunks, nb):
    qscale = GLA_HK ** -0.5
    g_refs = (g0_ref, g1_ref)
    incl = (_tri(CHUNK, False, False), _tri(CHUNK, True, False))
    o_ref[...] = jnp.zeros_like(o_ref)

    def step(first):
        units = [(d, first[d] + (i if d == 0 else -i)) for i in range(nb) for d in range(2)]
        rows = [pl.ds(pl.multiple_of(c * CHUNK, CHUNK), CHUNK) for _, c in units]
        q = [q_ref[0, r, :] for r in rows]
        k = [k_ref[0, r, :] for r in rows]
        v = [v_ref[0, r, :].astype(BF16) for r in rows]
        g = [g_refs[d][0, r, :] for (d, _), r in zip(units, rows)]
        bcum = [_cumsum_rows(gi, d == 1) for (d, _), gi in zip(units, g)]
        b_last_row = [jnp.sum(gi, axis=0, keepdims=True) for gi in g]
        dec_col = [jnp.exp(jnp.sum(gi.T, axis=1, keepdims=True)) for gi in g]
        q_e = [(qi * jnp.exp(bi) * qscale).astype(BF16) for qi, bi in zip(q, bcum)]
        k_e = [(ki * jnp.exp(-bi)).astype(BF16) for ki, bi in zip(k, bcum)]
        k_dec_t = [(ki * jnp.exp(bl - bi)).T.astype(BF16) for ki, bl, bi in zip(k, b_last_row, bcum)]
        a = [jnp.where(incl[d], _dot_nt(qe, ke), 0.0).astype(BF16) for (d, _), qe, ke in zip(units, q_e, k_e)]
        o_intra = [_dot(ai, vi) for ai, vi in zip(a, v)]
        s_inc = [_dot(kt, vi) for kt, vi in zip(k_dec_t, v)]
        s = [s_ref[0], s_ref[1]]
        for i, (d, _) in enumerate(units):
            o_ref[0, rows[i], :] += o_intra[i] + _dot(q_e[i], s[d].astype(BF16))
            s[d] = dec_col[i] * s[d] + s_inc[i]
        s_ref[0] = s[0]
        s_ref[1] = s[1]

    s_ref[...] = jnp.zeros_like(s_ref)
    n_lat = n_chunks - n_ctx_chunks

    def ctx_body(i, carry):
        step((i * nb, n_ctx_chunks - 1 - i * nb))
        return carry

    def lat_body(i, carry):
        step((n_ctx_chunks + i * nb, n_chunks - 1 - i * nb))
        return carry

    lax.fori_loop(0, n_ctx_chunks // nb, ctx_body, 0)
    lax.fori_loop(0, n_lat // nb, lat_body, 0)


def _gla_scan(p, gates, n_ctx, nb):
    b, t, _ = p.shape
    nkb = GLA_DK // GLA_HK
    return pl.pallas_call(
        functools.partial(_gla_scan_kernel, n_ctx_chunks=n_ctx // CHUNK, n_chunks=t // CHUNK, nb=nb),
        grid=(b, GLA_HEADS),
        in_specs=[
            pl.BlockSpec((1, t, GLA_HK), lambda i, h: (i, 0, h)),
            pl.BlockSpec((1, t, GLA_HK), lambda i, h: (i, 0, nkb + h)),
            pl.BlockSpec((1, t, GLA_HV), lambda i, h: (i, 0, (2 * GLA_DK) // GLA_HV + h)),
            pl.BlockSpec((1, t, GLA_HK), lambda i, h: (i, 0, h)),
            pl.BlockSpec((1, t, GLA_HK), lambda i, h: (i, 0, nkb + h)),
        ],
        out_specs=pl.BlockSpec((1, t, GLA_HV), lambda i, h: (i, 0, h)),
        out_shape=jax.ShapeDtypeStruct((b, t, GLA_DV), F32),
        scratch_shapes=[pltpu.VMEM((2, GLA_HK, GLA_HV), F32)],
        compiler_params=_cparams("parallel", "parallel"),
        name="gla_scan",
    )(p, p, p, gates, gates)


def _gla_out_kernel(u_ref, o_ref, gt_ref, ng_ref, mx_ref, mz_ref, w_ref, out_ref, *, tm, n_ctx):
    row0 = pl.program_id(1) * tm
    parts = []
    for h in range(GLA_HEADS):
        ls = slice(h * GLA_HV, (h + 1) * GLA_HV)
        ov = o_ref[0, :, ls]
        ov = ov * lax.rsqrt(jnp.mean(ov * ov, axis=-1, keepdims=True) + NORM_EPS) * ng_ref[...]
        gt = gt_ref[0, :, ls]
        parts.append((ov * (gt * _sigmoid(gt))).astype(BF16))
    out_ref[0] = _gated_residual(u_ref[0], parts, w_ref, mx_ref, mz_ref, row0, n_ctx)


def _gla_out(u, o, p, norm_g, mx, mz, w_o, n_ctx):
    b, t, d = u.shape
    tm = _pick_tile(t, 544)
    gt_blk = (2 * GLA_DK + GLA_DV) // d
    return pl.pallas_call(
        functools.partial(_gla_out_kernel, tm=tm, n_ctx=n_ctx),
        grid=(b, t // tm),
        in_specs=[
            pl.BlockSpec((1, tm, d), lambda i, j: (i, j, 0)),
            pl.BlockSpec((1, tm, d), lambda i, j: (i, j, 0)),
            pl.BlockSpec((1, tm, d), lambda i, j: (i, j, gt_blk)),
            _const_spec((1, GLA_HV)),
            pl.BlockSpec((1, 6, d), lambda i, j: (i, 0, 0)),
            _const_spec((6, d)),
            _const_spec((d, d)),
        ],
        out_specs=pl.BlockSpec((1, tm, d), lambda i, j: (i, j, 0)),
        out_shape=jax.ShapeDtypeStruct((b, t, d), F32),
        compiler_params=_cparams("parallel", "parallel"),
        name="gla_out",
    )(u, o, p, norm_g.reshape(1, GLA_HV), mx, mz, w_o)


def _rw_proj_kernel(u_ref, up_ref, un_ref, mx_ref, mz_ref, g_ref, mix_ref, wrkv_ref, w1_ref, w2_ref,
                    w0_ref, a1_ref, a2_ref, a0_ref, g1_ref, g2_ref, ka_ref, rk_ref,
                    r_ref, k_ref, v_ref, gate_ref, lw_ref, a_ref, bonus_ref, *, tm, n_ctx, t_total):
    row0 = pl.program_id(1) * tm
    g = g_ref[...]
    mx = mx_ref[0]
    mz = mz_ref[...]
    h = _norm_mod(u_ref[0], g, mx, mz, row0, n_ctx, 0, 1)
    h_prev = _norm_mod(up_ref[0], g, mx, mz, row0 - 8, n_ctx, 0, 1)[7:8]
    h_next = _norm_mod(un_ref[0], g, mx, mz, row0 + tm, n_ctx, 0, 1)[0:1]
    idx = lax.broadcasted_iota(jnp.int32, (tm, 1), 0)
    rows = row0 + idx
    h_dn = jnp.where(idx == 0, h_prev, pltpu.roll(h, 1, 0))
    h_dn = jnp.where((rows == 0) | (rows == n_ctx), 0.0, h_dn)
    h_up = jnp.where(idx == tm - 1, h_next, pltpu.roll(h, tm - 1, 0))
    h_up = jnp.where((rows == n_ctx - 1) | (rows == t_total - 1), 0.0, h_up)
    xx = 0.5 * (h_dn + h_up) - h
    xr, xw, xk, xv, xa, xg = ((h + xx * mix_ref[m:m + 1]).astype(BF16) for m in range(6))
    r = _dot(xr, wrkv_ref[0])
    k = _dot(xk, wrkv_ref[1])
    v = _dot(xv, wrkv_ref[2])
    r_ref[0] = r
    k_ref[0] = k
    v_ref[0] = v
    gate_ref[0] = _dot(_sigmoid(_dot(xg, g1_ref[...])).astype(BF16), g2_ref[...])
    tw = jnp.tanh(_dot(xw, w1_ref[...])).astype(BF16)
    lw_ref[0] = -math.exp(-0.5) * _sigmoid(w0_ref[...] + _dot(tw, w2_ref[...]))
    ta = _dot(xa, a1_ref[...]).astype(BF16)
    a = _sigmoid(a0_ref[...] + _dot(ta, a2_ref[...]))
    a_ref[0] = a
    lane_lo = lax.broadcasted_iota(jnp.int32, (1, LANES), 1) < RW_HEAD
    d = D_MODEL
    for p in range(d // LANES):
        ls = slice(p * LANES, (p + 1) * LANES)
        kd_sum = k[:, ls] * (2.0 + (a[:, ls] + a[:, d + p * LANES:d + (p + 1) * LANES] - 2.0) * ka_ref[:, ls])
        bonus_ref[0, :, ls] = _pair_sum(r[:, ls] * kd_sum * rk_ref[:, ls], lane_lo) * v[:, ls]


def _rw_proj(u, mx, mz, g, wts, n_ctx):
    b, t, d = u.shape
    tm = _pick_tile(t, 256)
    nb8 = t // 8
    tb = tm // 8
    full = lambda n: jax.ShapeDtypeStruct((b, t, n), F32)
    row_spec = lambda n: pl.BlockSpec((1, tm, n), lambda i, j: (i, j, 0))
    return pl.pallas_call(
        functools.partial(_rw_proj_kernel, tm=tm, n_ctx=n_ctx, t_total=t),
        grid=(b, t // tm),
        in_specs=[
            row_spec(d),
            pl.BlockSpec((1, 8, d), lambda i, j: (i, jnp.maximum(j * tb - 1, 0), 0)),
            pl.BlockSpec((1, 8, d), lambda i, j: (i, jnp.minimum((j + 1) * tb, nb8 - 1), 0)),
            pl.BlockSpec((1, 6, d), lambda i, j: (i, 0, 0)),
            _const_spec((6, d)),
            _const_spec((1, d)),
            _const_spec((6, d)),
            _const_spec((3, d, d)),
            _const_spec((d, 2 * RW_LORA)),
            _const_spec((2 * RW_LORA, 2 * d)),
            _const_spec((1, 2 * d)),
            _const_spec((d, 2 * RW_LORA)),
            _const_spec((2 * RW_LORA, 2 * d)),
            _const_spec((1, 2 * d)),
            _const_spec((d, 2 * LANES)),
            _const_spec((2 * LANES, d)),
            _const_spec((1, d)),
            _const_spec((1, d)),
        ],
        out_specs=[row_spec(d), row_spec(d), row_spec(d), row_spec(d), row_spec(2 * d), row_spec(2 * d),
                   row_spec(d)],
        out_shape=[full(d), full(d), full(d), full(d), full(2 * d), full(2 * d), full(d)],
        compiler_params=_cparams("parallel", "parallel"),
        name="rw_proj",
    )(u, u, u, mx, mz, g.reshape(1, d), wts["mix"], wts["w_rkv"], wts["w1"], wts["w2"], wts["w0"],
      wts["a1"], wts["a2"], wts["a0"], wts["g1"], wts["g2"], wts["k_a"], wts["r_k"])


def _rw_pair_terms(units, side_work=()):
    C = CHUNK
    lane_lo = lax.broadcasted_iota(jnp.int32, (1, LANES), 1) < RW_HEAD
    row2 = lax.broadcasted_iota(jnp.int32, (C, LANES), 0)
    col2 = lax.broadcasted_iota(jnp.int32, (C, LANES), 1) & (C - 1)
    strict2 = (row2 > col2, row2 < col2)
    incl2 = (row2 >= col2, row2 <= col2)
    eye2 = jnp.where(row2 == col2, 1.0, 0.0)
    nu = len(units)

    def bd(x):
        zero = jnp.zeros_like(x)
        return jnp.concatenate([jnp.where(lane_lo, x, zero), jnp.where(lane_lo, zero, x)], axis=0)

    def hilo(x):
        hi = x.astype(BF16)
        return hi, (x - hi.astype(F32)).astype(BF16)

    side = iter(side_work)

    def stage_done():
        thunk = next(side, None)
        if thunk is not None:
            thunk()

    pre = []
    for r2, k2, v2, lw2, a2, desc, kkw, kaw in units:
        d = int(desc)
        kk = k2 * kkw
        kk = kk * lax.rsqrt(jnp.maximum(_pair_sum(kk * kk, lane_lo), 1e-24))
        kd = k2 * (1.0 + (a2 - 1.0) * kaw)
        bvec = kk * a2
        cum = _cumsum_rows(lw2, desc)
        tot = jnp.sum(lw2, axis=0, keepdims=True)
        e_neg = jnp.exp(-cum)
        e_rem = jnp.exp(tot - cum)
        pre.append(dict(
            d=d, tot=tot, v_b=v2.astype(BF16),
            a_t=-kk * jnp.exp(cum - lw2), r_t=r2 * jnp.exp(cum),
            b_h=(bvec * e_neg).astype(BF16), k_h=(kd * e_neg).astype(BF16),
            b_rem_t=(bvec * e_rem).T.astype(BF16), k_rem_t=(kd * e_rem).T.astype(BF16)))

    ar = [jnp.concatenate([p["a_t"], p["r_t"]], axis=0).astype(BF16) for p in pre]
    p_bk = [_dot_nt(ar[u], jnp.concatenate([bd(pre[u]["b_h"]), bd(pre[u]["k_h"])], axis=0)) for u in range(nu)]
    l_ab = [jnp.where(strict2[pre[u]["d"]], p_bk[u][:C, :LANES], 0.0) for u in range(nu)]
    m_rb = [jnp.where(incl2[pre[u]["d"]], p_bk[u][C:, :LANES], 0.0).astype(BF16) for u in range(nu)]
    l_ak = [jnp.where(strict2[pre[u]["d"]], p_bk[u][:C, LANES:], 0.0).astype(BF16) for u in range(nu)]
    m_rk = [jnp.where(incl2[pre[u]["d"]], p_bk[u][C:, LANES:], 0.0).astype(BF16) for u in range(nu)]
    stage_done()

    xo = l_ab
    l_b16 = [l.astype(BF16) for l in l_ab]
    pw = [_dot(l, bd(l)) for l in l_b16]
    for _ in range(3):
        p_b16 = [p.astype(BF16) for p in pw]
        ra = [_dot(jnp.concatenate([p, x.astype(BF16)], axis=0), bd(p)) for p, x in zip(p_b16, xo)]
        xo = [x + p + a[C:] for x, p, a in zip(xo, pw, ra)]
        pw = [a[:C] for a in ra]
        stage_done()
    xo = [x + p + _dot(x.astype(BF16), bd(p.astype(BF16))) for x, p in zip(xo, pw)]
    l_hl = [hilo(l) for l in l_ab]
    x_hl = [hilo(x) for x in xo]
    lx = [_dot(jnp.concatenate([lh, ll], axis=0), bd(xh)) for (lh, ll), (xh, _) in zip(l_hl, x_hl)]
    lxb = [_dot(lh, bd(xlo)) for (lh, _), (_, xlo) in zip(l_hl, x_hl)]
    res = [(l - x) + (a[:C] + a[C:] + b) for l, x, a, b in zip(l_ab, xo, lx, lxb)]
    stage_done()
    xo = [x + r + _dot(xh, bd(r.astype(BF16))) for x, r, (xh, _) in zip(xo, res, x_hl)]
    t_inv = [(eye2 + x).astype(BF16) for x in xo]

    lvyk = [_dot(jnp.concatenate([l_ak[u], m_rk[u]], axis=0), bd(pre[u]["v_b"])) for u in range(nu)]
    kv = [_dot(pre[u]["k_rem_t"], pre[u]["v_b"]) for u in range(nu)]
    tw = [_dot(t_inv[u], jnp.concatenate([bd(ar[u][:C]), bd(lvyk[u][:C].astype(BF16))], axis=1))
          for u in range(nu)]
    qy = [_dot(m_rb[u], jnp.concatenate([bd(tw[u][:, :LANES].astype(BF16)),
                                         bd(tw[u][:, LANES:].astype(BF16))], axis=1)) for u in range(nu)]
    gh = [_dot(pre[u]["b_rem_t"], tw[u].astype(BF16)) for u in range(nu)]

    r = lax.broadcasted_iota(jnp.int32, (LANES, LANES), 0)
    c = lax.broadcasted_iota(jnp.int32, (LANES, LANES), 1)
    same_head = (r < RW_HEAD) == (c < RW_HEAD)
    out = []
    for u in range(nu):
        qp = pre[u]["r_t"] + qy[u][:, :LANES]
        y0 = qy[u][:, LANES:] + lvyk[u][C:]
        g_mat = jnp.where(same_head, gh[u][:, :LANES], 0.0) + jnp.where(r == c, jnp.exp(pre[u]["tot"]), 0.0)
        h_mat = jnp.where(same_head, gh[u][:, LANES:] + kv[u], 0.0)
        out.append((qp, y0, g_mat, h_mat))
    return out


def _rw_mix_kernel(r0_ref, k0_ref, v0_ref, lw0_ref, a0_ref, r1_ref, k1_ref, v1_ref, lw1_ref, a1_ref,
                   kk_ref, ka_ref, y0_ref, y1_ref, st_ref, qp_s, yc_s, g_s, h_s, *, nb, pp):
    def zero_scratch(_, carry):
        for ref in (st_ref, qp_s, yc_s, g_s, h_s):
            ref[...] = jnp.zeros_like(ref)
        return carry

    lax.fori_loop(0, jnp.where(pl.program_id(2) == 0, 1, 0), zero_scratch, 0)

    y_refs = (y0_ref, y1_ref)
    chains = [(d, p) for d in range(2) for p in range(pp)]
    states = {dp: st_ref[dp[0], dp[1]] for dp in chains}
    prev_terms = {(d, p, c): (qp_s[d, p, c], yc_s[d, p, c], g_s[d, p, c], h_s[d, p, c])
                  for d, p in chains for c in range(nb)}

    def link(ci):
        def emit():
            for d, p in chains:
                c = ci if d == 0 else nb - 1 - ci
                qp, yc, g_mat, h_mat = prev_terms[d, p, c]
                st_b = states[d, p].astype(BF16)
                y_refs[d][0, c * CHUNK:(c + 1) * CHUNK, p * LANES:(p + 1) * LANES] = _dot(qp, st_b) + yc
                states[d, p] = _dot(g_mat, st_b) + h_mat
        return emit

    ins = ((r0_ref, k0_ref, v0_ref, lw0_ref, a0_ref), (r1_ref, k1_ref, v1_ref, lw1_ref, a1_ref))
    units = []
    index = []
    for p in range(pp):
        ls = slice(p * LANES, (p + 1) * LANES)
        for c in range(nb):
            rows = slice(c * CHUNK, (c + 1) * CHUNK)
            for d in range(2):
                units.append(tuple(ref[0, rows, ls] for ref in ins[d]) + (d == 1, kk_ref[:, ls], ka_ref[:, ls]))
                index.append((d, p, c))
    links = [link(ci) for ci in range(nb)]
    assert nb <= 5, "one recurrence link per stage boundary of _rw_pair_terms"
    results = _rw_pair_terms(units, side_work=links)
    for d, p in chains:
        st_ref[d, p] = states[d, p]
    for (d, p, c), (qp, y0, g_mat, h_mat) in zip(index, results):
        qp_s[d, p, c] = qp.astype(BF16)
        yc_s[d, p, c] = y0
        g_s[d, p, c] = g_mat.astype(BF16)
        h_s[d, p, c] = h_mat


def _rw_mix(r, k, v, lw, a, k_k, k_a, n_ctx, nb):
    b, t, d = r.shape
    npairs = d // LANES
    rows = nb * CHUNK
    nblk = t // rows
    ncb = n_ctx // rows

    def rev(j):
        return jnp.where(j < ncb, ncb - 1 - j, nblk - 1 - (j - ncb))

    cur = lambda j: jnp.minimum(j, nblk - 1)
    prev = lambda j: jnp.maximum(j - 1, 0)
    pp = 2
    pw = pp * LANES
    ng = npairs // pp
    fwd = lambda off, blk: pl.BlockSpec((1, rows, pw), lambda i, p, j: (i, blk(j), off + p))
    bwd = lambda off, blk: pl.BlockSpec((1, rows, pw), lambda i, p, j: (i, rev(blk(j)), off + p))
    par = pl.BlockSpec((1, pw), lambda i, p, j: (0, p))
    return pl.pallas_call(
        functools.partial(_rw_mix_kernel, nb=nb, pp=pp),
        grid=(b, ng, nblk + 1),
        in_specs=[fwd(0, cur), fwd(0, cur), fwd(0, cur), fwd(0, cur), fwd(0, cur),
                  bwd(0, cur), bwd(0, cur), bwd(0, cur), bwd(ng, cur), bwd(ng, cur), par, par],
        out_specs=[fwd(0, prev), bwd(0, prev)],
        out_shape=[jax.ShapeDtypeStruct((b, t, d), F32), jax.ShapeDtypeStruct((b, t, d), F32)],
        scratch_shapes=[pltpu.VMEM((2, pp, LANES, LANES), F32),
                        pltpu.VMEM((2, pp, nb, CHUNK, LANES), BF16), pltpu.VMEM((2, pp, nb, CHUNK, LANES), F32),
                        pltpu.VMEM((2, pp, nb, LANES, LANES), BF16), pltpu.VMEM((2, pp, nb, LANES, LANES), F32)],
        compiler_params=_cparams("parallel", "parallel", "arbitrary"),
        name="rw_mix",
    )(r, k, v, lw, a, r, k, v, lw, a, k_k.reshape(1, d), k_a.reshape(1, d))


def _rw_out_kernel(u_ref, y0_ref, y1_ref, bonus_ref, gate_ref, lng_ref, lnb_ref, mx_ref, mz_ref, w_ref,
                   out_ref, *, tm, n_ctx):
    row0 = pl.program_id(1) * tm
    lane_lo = lax.broadcasted_iota(jnp.int32, (1, LANES), 1) < RW_HEAD
    parts = []
    for p in range(D_MODEL // LANES):
        ls = slice(p * LANES, (p + 1) * LANES)
        y = y0_ref[0, :, ls] + y1_ref[0, :, ls]
        mu = _pair_sum(y, lane_lo) * (1.0 / RW_HEAD)
        yc = y - mu
        var = _pair_sum(yc * yc, lane_lo) * (1.0 / RW_HEAD)
        yn = yc * lax.rsqrt(var + RW_GN_EPS) * lng_ref[:, ls] + lnb_ref[:, ls]
        parts.append(((yn + bonus_ref[0, :, ls]) * gate_ref[0, :, ls]).astype(BF16))
    out_ref[0] = _gated_residual(u_ref[0], parts, w_ref, mx_ref, mz_ref, row0, n_ctx)


def _rw_out(u, y0, y1, bonus, gate, ln_g, ln_b, mx, mz, w_o, n_ctx):
    b, t, d = u.shape
    tm = _pick_tile(t, 544)
    tok = pl.BlockSpec((1, tm, d), lambda i, j: (i, j, 0))
    vec = _const_spec((1, d))
    return pl.pallas_call(
        functools.partial(_rw_out_kernel, tm=tm, n_ctx=n_ctx),
        grid=(b, t // tm),
        in_specs=[tok] * 5 + [vec, vec, pl.BlockSpec((1, 6, d), lambda i, j: (i, 0, 0)),
                              _const_spec((6, d)), _const_spec((d, d))],
        out_specs=tok,
        out_shape=jax.ShapeDtypeStruct((b, t, d), F32),
        compiler_params=_cparams("parallel", "parallel"),
        name="rw_out",
    )(u, y0, y1, bonus, gate, ln_g.reshape(1, d), ln_b.reshape(1, d), mx, mz, w_o)


def _block_rows(w_pair):
    _, k, n = w_pair.shape
    z = jnp.zeros((k, n), w_pair.dtype)
    return jnp.concatenate([jnp.concatenate([w_pair[0], z], axis=1),
                            jnp.concatenate([z, w_pair[1]], axis=1)], axis=0)


def kernel(x, c, ctx, c_ctx, ada_w, ada_b, norm1_g, norm2_g, mlp_w1, mlp_w2, final_g, na_w_qkv, na_w_o, na_rpb, gla_w_in, gla_w_dec2, gla_b_dec, gla_norm_g, gla_w_o, rw_mix, rw_w_rkv, rw_w0, rw_w1, rw_w2, rw_a0, rw_a1, rw_a2, rw_g1, rw_g2, rw_k_k, rw_k_a, rw_r_k, rw_ln_g, rw_ln_b, rw_w_o):
    b, seq, d = x.shape
    n_ctx = ctx.shape[1]
    depth = ada_w.shape[0]
    assert d == D_MODEL and n_ctx % CHUNK == 0 and seq % CHUNK == 0

    u = jnp.concatenate([ctx.astype(x.dtype), x], axis=1)

    rows = -(-(b + 1) // 8) * 8
    c_all = jnp.zeros((rows, d), F32).at[:b].set(c).at[b].set(c_ctx)
    tables = _ada_tables(c_all, ada_w, ada_b)
    mods_x = tables[:, :b].reshape(depth, b, 6, d)
    mods_z = tables[:, b].reshape(depth, 6, d)

    nb = max(n for n in (4, 2, 1) if (n_ctx // CHUNK) % n == 0 and (seq // CHUNK) % n == 0)

    for i in range(depth):
        kind, j = i % 3, i // 3
        last = i == depth - 1
        mx, mz = mods_x[i], mods_z[i]
        if kind == 0:
            qkv = _norm_proj(u, mx, mz, norm1_g[i], na_w_qkv[j].astype(BF16), n_ctx, BF16)
            o = _na_attention(qkv, _na_bias_table(na_rpb[j]), n_ctx, with_ctx=not last)
            u = _na_out(u, o, mx, mz, na_w_o[j].astype(BF16), n_ctx, latent_only=last)
        elif kind == 1:
            n_main = 2 * GLA_DK + 2 * GLA_DV
            w_in = gla_w_in[j]
            w_lr = jnp.zeros((d, LANES), F32).at[:, :2 * GLA_LOW_RANK].set(w_in[:, n_main:])
            w_dec = jnp.zeros((LANES, 2 * GLA_DK), F32).at[:2 * GLA_LOW_RANK].set(_block_rows(gla_w_dec2[j]))
            p, gates = _gla_proj(u, mx, mz, norm1_g[i], w_in[:, :n_main].astype(BF16), w_lr.astype(BF16),
                                 w_dec.astype(BF16), gla_b_dec[j].reshape(1, 2 * GLA_DK), n_ctx)
            o = _gla_scan(p, gates, n_ctx, nb)
            u = _gla_out(u, o, p, gla_norm_g[j], mx, mz, gla_w_o[j].astype(BF16), n_ctx)
        else:
            g1 = jnp.zeros((d, 2 * LANES), F32).at[:, :RW_GATE_LORA].set(rw_g1[j])
            g2 = jnp.zeros((2 * LANES, d), F32).at[:RW_GATE_LORA].set(rw_g2[j])
            wts = dict(
                mix=rw_mix[j], w_rkv=rw_w_rkv[j].astype(BF16),
                w1=jnp.concatenate([rw_w1[j, 0], rw_w1[j, 1]], axis=1).astype(BF16),
                w2=_block_rows(rw_w2[j]).astype(BF16), w0=rw_w0[j].reshape(1, 2 * d),
                a1=jnp.concatenate([rw_a1[j, 0], rw_a1[j, 1]], axis=1).astype(BF16),
                a2=_block_rows(rw_a2[j]).astype(BF16), a0=rw_a0[j].reshape(1, 2 * d),
                g1=g1.astype(BF16), g2=g2.astype(BF16),
                k_a=rw_k_a[j].reshape(1, d), r_k=rw_r_k[j].reshape(1, d))
            r, k, v, gate, lw, a, bonus = _rw_proj(u, mx, mz, norm1_g[i], wts, n_ctx)
            y0, y1 = _rw_mix(r, k, v, lw, a, rw_k_k[j], rw_k_a[j], n_ctx, nb)
            u = _rw_out(u, y0, y1, bonus, gate, rw_ln_g[j], rw_ln_b[j], mx, mz, rw_w_o[j].astype(BF16), n_ctx)
        if last and u.shape[1] != seq:
            u = u[:, n_ctx:]
        u = _mlp(u, mx, mz, norm2_g[i], mlp_w1[i].astype(BF16), mlp_w2[i].astype(BF16), final_g,
                 0 if last else n_ctx, final=last)
    return u
```

```python
import functools
import math

import jax
import jax.numpy as jnp
from jax import lax
from jax.experimental import pallas as pl
from jax.experimental.pallas import tpu as pltpu

F32 = jnp.float32
BF16 = jnp.bfloat16

D_MODEL = 1024
D_FF = 4 * D_MODEL
NORM_EPS = 1e-6
GRID_W = 64

NA_HEAD_DIM = 64
NA_HEADS = D_MODEL // NA_HEAD_DIM
NA_WIN_R = 8
NA_WIN_C = 16
NA_MASK = -1e30

GLA_HEADS = 4
GLA_DK = D_MODEL // 2
GLA_DV = D_MODEL
GLA_HK = GLA_DK // GLA_HEADS
GLA_HV = GLA_DV // GLA_HEADS
GLA_LOW_RANK = 16
GLA_GATE_NORM = 16.0
CHUNK = 64

RW_HEAD = 64
RW_GATE_LORA = 160
RW_LORA = 64
RW_GN_EPS = 64e-5

LANES = 128
VMEM_LIMIT = 56 * 1024 * 1024


def _cparams(*sem):
    return pltpu.CompilerParams(dimension_semantics=sem, vmem_limit_bytes=VMEM_LIMIT)


def _pick_tile(n, target, mult=16):
    best = None
    for t in range(mult, min(n, target) + 1, mult):
        if n % t == 0:
            best = t
    assert best is not None, (n, target)
    return best


def _const_spec(shape):
    nd = len(shape)
    return pl.BlockSpec(shape, lambda *_: (0,) * nd, pipeline_mode=pl.Buffered(1))


def _dot(a, b):
    return jnp.dot(a, b, preferred_element_type=F32)


def _dot_nt(a, b):
    return lax.dot_general(a, b, (((1,), (1,)), ((), ())), preferred_element_type=F32)


def _sigmoid(x):
    return 0.5 + 0.5 * jnp.tanh(0.5 * x)


def _softplus(x):
    return jnp.maximum(x, 0.0) + jnp.log(1.0 + jnp.exp(-jnp.abs(x)))


def _rms(u, g):
    return u * lax.rsqrt(jnp.mean(u * u, axis=-1, keepdims=True) + NORM_EPS) * g


def _mod_rows(mx, mz, idx, row0, rows, n_ctx):
    vx = mx[idx:idx + 1]
    if n_ctx == 0:
        return vx
    if n_ctx % rows == 0:
        return jnp.where(row0 < n_ctx, mz[idx:idx + 1], vx)
    r = row0 + lax.broadcasted_iota(jnp.int32, (rows, 1), 0)
    return jnp.where(r < n_ctx, mz[idx:idx + 1], vx)


def _norm_mod(u, g, mx, mz, row0, n_ctx, i_shift, i_scale):
    rows = u.shape[0]
    shift = _mod_rows(mx, mz, i_shift, row0, rows, n_ctx)
    scale = _mod_rows(mx, mz, i_scale, row0, rows, n_ctx)
    return _rms(u, g) * (1.0 + scale) + shift


def _pair_sum(x, lane_lo):
    s0 = jnp.sum(jnp.where(lane_lo, x, 0.0), axis=-1, keepdims=True)
    s1 = jnp.sum(jnp.where(lane_lo, 0.0, x), axis=-1, keepdims=True)
    return jnp.where(lane_lo, s0, s1)


def _ada_kernel(c_ref, w_ref, b_ref, o_ref):
    c = c_ref[...]
    sc = (c * _sigmoid(c)).astype(BF16)
    o_ref[0] = _dot(sc, w_ref[0].astype(BF16)) + b_ref[0]


def _ada_tables(c_all, ada_w, ada_b):
    depth, d, n = ada_w.shape
    rows = c_all.shape[0]
    tn = 1536
    return pl.pallas_call(
        _ada_kernel,
        grid=(depth, n // tn),
        in_specs=[
            pl.BlockSpec((rows, d), lambda l, j: (0, 0)),
            pl.BlockSpec((1, d, tn), lambda l, j: (l, 0, j)),
            pl.BlockSpec((1, 1, tn), lambda l, j: (l, 0, j)),
        ],
        out_specs=pl.BlockSpec((1, rows, tn), lambda l, j: (l, 0, j)),
        out_shape=jax.ShapeDtypeStruct((depth, rows, n), F32),
        compiler_params=_cparams("parallel", "parallel"),
        name="ada_tables",
    )(c_all, ada_w, ada_b.reshape(depth, 1, n))


def _proj_kernel(u_ref, mx_ref, mz_ref, g_ref, w_ref, o_ref, *, tm, n_ctx, nsplit):
    t = pl.program_id(1)
    h = _norm_mod(u_ref[0], g_ref[...], mx_ref[0], mz_ref[...], t * tm, n_ctx, 0, 1).astype(BF16)
    n = w_ref.shape[1] // nsplit
    for s in range(nsplit):
        o_ref[0, :, s * n:(s + 1) * n] = _dot(h, w_ref[:, s * n:(s + 1) * n]).astype(o_ref.dtype)


def _norm_proj(u, mx, mz, g, w, n_ctx, out_dtype):
    b, t, d = u.shape
    n = w.shape[1]
    tm = _pick_tile(t, 544)
    return pl.pallas_call(
        functools.partial(_proj_kernel, tm=tm, n_ctx=n_ctx, nsplit=n // D_MODEL),
        grid=(b, t // tm),
        in_specs=[
            pl.BlockSpec((1, tm, d), lambda i, j: (i, j, 0)),
            pl.BlockSpec((1, 6, d), lambda i, j: (i, 0, 0)),
            _const_spec((6, d)),
            _const_spec((1, d)),
            _const_spec((d, n)),
        ],
        out_specs=pl.BlockSpec((1, tm, n), lambda i, j: (i, j, 0)),
        out_shape=jax.ShapeDtypeStruct((b, t, n), out_dtype),
        compiler_params=_cparams("parallel", "parallel"),
        name="norm_proj",
    )(u, mx, mz, g.reshape(1, d), w)


def _entry_proj_kernel(ctx_ref, x_ref, mx_ref, mz_ref, g_ref, w_ref, o_ref, u_ref, *, tm, nsplit):
    j = pl.program_id(1)
    u = jnp.where(j == 0, ctx_ref[0].astype(F32), x_ref[0])
    u_ref[0] = u
    h = _norm_mod(u, g_ref[...], mx_ref[0], mz_ref[...], j * tm, tm, 0, 1).astype(BF16)
    n = w_ref.shape[1] // nsplit
    for s in range(nsplit):
        o_ref[0, :, s * n:(s + 1) * n] = _dot(h, w_ref[:, s * n:(s + 1) * n]).astype(o_ref.dtype)


def _entry_proj(x, ctx, mx, mz, g, w, out_dtype):
    b, seq, d = x.shape
    tm = ctx.shape[1]
    assert seq % tm == 0 and tm % 16 == 0
    n = w.shape[1]
    t = tm + seq
    return pl.pallas_call(
        functools.partial(_entry_proj_kernel, tm=tm, nsplit=n // D_MODEL),
        grid=(b, t // tm),
        in_specs=[
            pl.BlockSpec((1, tm, d), lambda i, j: (i, 0, 0)),
            pl.BlockSpec((1, tm, d), lambda i, j: (i, jnp.maximum(j - 1, 0), 0)),
            pl.BlockSpec((1, 6, d), lambda i, j: (i, 0, 0)),
            _const_spec((6, d)),
            _const_spec((1, d)),
            _const_spec((d, n)),
        ],
        out_specs=[pl.BlockSpec((1, tm, n), lambda i, j: (i, j, 0)),
                   pl.BlockSpec((1, tm, d), lambda i, j: (i, j, 0))],
        out_shape=[jax.ShapeDtypeStruct((b, t, n), out_dtype), jax.ShapeDtypeStruct((b, t, d), F32)],
        compiler_params=_cparams("parallel", "parallel"),
        name="entry_proj",
    )(ctx, x, mx, mz, g.reshape(1, d), w)


def _na_bias_table(rpb):
    w = GRID_W
    cols = jnp.arange(w)
    c_start = jnp.clip(cols - NA_WIN_C // 2, 0, w - NA_WIN_C)
    col_ok = (cols[None, :] >= c_start[:, None]) & (cols[None, :] < c_start[:, None] + NA_WIN_C)
    col_idx = jnp.clip(cols[None, :] - cols[:, None], -(NA_WIN_C - 1), NA_WIN_C - 1) + NA_WIN_C - 1
    onehot = col_idx[:, :, None] == jnp.arange(2 * NA_WIN_C - 1)
    m = jnp.sum(jnp.where(onehot[None, None], rpb.astype(F32)[:, :, None, None, :], 0.0), axis=-1)
    m = jnp.where(col_ok[None, None], m, NA_MASK)
    return jnp.concatenate([m[:, :-1], m[:, 1:]], axis=-1)


def _na_kernel(q_ref, k_ref, v_ref, bias_ref, o_ref, *, n_ctx, n_rows, with_ctx, rq):
    nctxb = n_ctx // GRID_W
    out_off = 0 if with_ctx else n_ctx
    scale = NA_HEAD_DIM ** -0.5
    assert scale == 0.125, "the score scale is folded into bf16 q, exact only for a power of two"
    lane_lo = lax.broadcasted_iota(jnp.int32, (1, LANES), 1) < NA_HEAD_DIM
    head_masks = (jnp.where(lane_lo, scale, 0.0).astype(BF16), jnp.where(lane_lo, 0.0, scale).astype(BF16))
    n_pairs = q_ref.shape[2] // LANES
    strip = NA_WIN_R * GRID_W
    jobs_idx = [(qi, p) for qi in range(rq) for p in range(n_pairs)]

    def q_rows(tok0, qi):
        return pl.ds(pl.multiple_of(tok0 + qi * GRID_W, GRID_W), GRID_W)

    def stacked_q(tok0, qi, p):
        q2 = q_ref[0, q_rows(tok0, qi), p * LANES:(p + 1) * LANES]
        return jnp.concatenate([q2 * head_masks[0], q2 * head_masks[1]], axis=0)

    def attend(tok0, jobs):
        scores = [[_dot_nt(q, k) if b is None else _dot_nt(q, k) + b
                   for k, b in zip(ks, bs)] for q, ks, _, bs in jobs]
        tops = []
        for sc in scores:
            m = sc[0].max(axis=-1, keepdims=True)
            for s in sc[1:]:
                m = jnp.maximum(m, s.max(axis=-1, keepdims=True))
            tops.append(m)
        exps = [[jnp.exp(s - m) for s in sc] for sc, m in zip(scores, tops)]
        dens = []
        for es in exps:
            den = es[0].sum(axis=-1, keepdims=True)
            for e in es[1:]:
                den = den + e.sum(axis=-1, keepdims=True)
            dens.append(den)
        for (qi, p), (_, _, vs, _), es, den in zip(jobs_idx, jobs, exps, dens):
            o = _dot(es[0].astype(BF16), vs[0])
            for e, v in zip(es[1:], vs[1:]):
                o = o + _dot(e.astype(BF16), v)
            o = o / den
            o2 = jnp.where(lane_lo, o[:GRID_W], o[GRID_W:])
            o_ref[0, q_rows(tok0 - out_off, qi), p * LANES:(p + 1) * LANES] = o2.astype(o_ref.dtype)

    def latent_rows(step, carry):
        tok0 = n_ctx + step * (rq * GRID_W)
        jobs = []
        for qi, p in jobs_idx:
            ls = slice(p * LANES, (p + 1) * LANES)
            r = step * rq + qi
            r0 = jnp.clip(r - NA_WIN_R // 2, 0, n_rows - NA_WIN_R)
            start = pl.multiple_of(n_ctx + r0 * GRID_W, GRID_W)
            ri0 = NA_WIN_R - 1 - (r - r0)
            bias = jnp.concatenate(
                [jnp.concatenate([bias_ref[2 * p + hh, ri0 + 2 * m] for m in range(NA_WIN_R // 2)], axis=1)
                 for hh in range(2)], axis=0)
            jobs.append((stacked_q(tok0, qi, p),
                         [k_ref[0, pl.ds(start, strip), ls], k_ref[0, 0:n_ctx, ls]],
                         [v_ref[0, pl.ds(start, strip), ls], v_ref[0, 0:n_ctx, ls]],
                         [bias, None]))
        attend(tok0, jobs)
        return carry

    def context_rows(step, carry):
        tok0 = step * (rq * GRID_W)
        jobs = []
        for qi, p in jobs_idx:
            ls = slice(p * LANES, (p + 1) * LANES)
            jobs.append((stacked_q(tok0, qi, p), [k_ref[0, 0:n_ctx, ls]], [v_ref[0, 0:n_ctx, ls]], [None]))
        attend(tok0, jobs)
        return carry

    if with_ctx:
        lax.fori_loop(0, nctxb // rq, context_rows, 0)
    lax.fori_loop(0, n_rows // rq, latent_rows, 0)


def _na_attention(qkv, bias, n_ctx, with_ctx):
    b, t, _ = qkv.shape
    seq = t - n_ctx
    n_rows = seq // GRID_W
    assert seq % GRID_W == 0 and n_ctx % GRID_W == 0 and n_rows >= NA_WIN_R
    hw = 2 * LANES
    ng = D_MODEL // hw
    nctxb = n_ctx // GRID_W
    rq = 4 if nctxb % 4 == 0 and n_rows % 4 == 0 else 2
    assert nctxb % rq == 0 and n_rows % rq == 0
    t_out = t if with_ctx else seq

    return pl.pallas_call(
        functools.partial(_na_kernel, n_ctx=n_ctx, n_rows=n_rows, with_ctx=with_ctx, rq=rq),
        grid=(b, ng),
        in_specs=[
            pl.BlockSpec((1, t, hw), lambda i, g: (i, 0, g)),
            pl.BlockSpec((1, t, hw), lambda i, g: (i, 0, ng + g)),
            pl.BlockSpec((1, t, hw), lambda i, g: (i, 0, 2 * ng + g)),
            pl.BlockSpec((hw // NA_HEAD_DIM, 2 * NA_WIN_R - 2, GRID_W, LANES), lambda i, g: (g, 0, 0, 0)),
        ],
        out_specs=pl.BlockSpec((1, t_out, hw), lambda i, g: (i, 0, g)),
        out_shape=jax.ShapeDtypeStruct((b, t_out, D_MODEL), BF16),
        compiler_params=_cparams("parallel", "parallel"),
        name="na_attention",
    )(qkv, qkv, qkv, bias)


def _gated_residual(u, o_bf16_parts, w_ref, mx_ref, mz_ref, row0, n_ctx):
    acc = None
    k0 = 0
    for part in o_bf16_parts:
        kw = part.shape[1]
        term = _dot(part, w_ref[k0:k0 + kw, :])
        acc = term if acc is None else acc + term
        k0 += kw
    gate = _mod_rows(mx_ref[0], mz_ref[...], 2, row0, u.shape[0], n_ctx)
    return u + gate * acc


def _na_out_kernel(u_ref, o_ref, mx_ref, mz_ref, w_ref, out_ref, *, tm, n_ctx, blk_off):
    row0 = (pl.program_id(1) + blk_off) * tm
    out_ref[0] = _gated_residual(u_ref[0], [o_ref[0]], w_ref, mx_ref, mz_ref, row0, n_ctx)


def _na_out(u, o, mx, mz, w_o, n_ctx, latent_only):
    b, t, d = u.shape
    if latent_only:
        tm = _pick_tile(n_ctx, 256)
        assert (t - n_ctx) % tm == 0
        blk_off = n_ctx // tm
        nt = (t - n_ctx) // tm
    else:
        tm = _pick_tile(t, 544)
        blk_off = 0
        nt = t // tm
    return pl.pallas_call(
        functools.partial(_na_out_kernel, tm=tm, n_ctx=n_ctx, blk_off=blk_off),
        grid=(b, nt),
        in_specs=[
            pl.BlockSpec((1, tm, d), lambda i, j: (i, j + blk_off, 0)),
            pl.BlockSpec((1, tm, d), lambda i, j: (i, j, 0)),
            pl.BlockSpec((1, 6, d), lambda i, j: (i, 0, 0)),
            _const_spec((6, d)),
            _const_spec((d, d)),
        ],
        out_specs=pl.BlockSpec((1, tm, d), lambda i, j: (i, j, 0)),
        out_shape=jax.ShapeDtypeStruct((b, nt * tm, d), F32),
        compiler_params=_cparams("parallel", "parallel"),
        name="na_out",
    )(u, o, mx, mz, w_o)


MLP_FC = 1024


def _mlp_apply(u, mx, mz, g, w1_ref, w2_ref, row0, n_ctx):
    h = _norm_mod(u, g, mx, mz, row0, n_ctx, 3, 4).astype(BF16)
    acc = None
    for c in range(D_FF // MLP_FC):
        a = _dot(h, w1_ref[:, c * MLP_FC:(c + 1) * MLP_FC])
        a = jnp.square(jnp.maximum(a, 0.0)).astype(BF16)
        term = _dot(a, w2_ref[c * MLP_FC:(c + 1) * MLP_FC, :])
        acc = term if acc is None else acc + term
    return u + _mod_rows(mx, mz, 5, row0, u.shape[0], n_ctx) * acc


def _mlp_kernel(u_ref, mx_ref, mz_ref, g_ref, w1_ref, w2_ref, fg_ref, out_ref, *, tm, n_ctx, final):
    row0 = pl.program_id(1) * tm
    y = _mlp_apply(u_ref[0], mx_ref[0], mz_ref[...], g_ref[...], w1_ref, w2_ref, row0, n_ctx)
    if final:
        y = _rms(y, fg_ref[...])
    out_ref[0] = y


def _mixer_mlp_kernel(*refs, kind, tm, n_ctx):
    if kind == "na":
        u_ref, o_ref, mx_ref, mz_ref, wo_ref, g_ref, w1_ref, w2_ref, out_ref = refs
        parts = [o_ref[0]]
    else:
        u_ref, o_ref, gt_ref, ng_ref, mx_ref, mz_ref, wo_ref, g_ref, w1_ref, w2_ref, out_ref = refs
        parts = _gla_gated_parts(o_ref, gt_ref, ng_ref)
    row0 = pl.program_id(1) * tm
    u = _gated_residual(u_ref[0], parts, wo_ref, mx_ref, mz_ref, row0, n_ctx)
    out_ref[0] = _mlp_apply(u, mx_ref[0], mz_ref[...], g_ref[...], w1_ref, w2_ref, row0, n_ctx)


def _mixer_mlp(kind, u, mixer_inputs, mixer_specs, mx, mz, w_o, g, w1, w2, n_ctx):
    b, t, d = u.shape
    tm = _pick_tile(t, 544)
    tok = pl.BlockSpec((1, tm, d), lambda i, j: (i, j, 0))
    return pl.pallas_call(
        functools.partial(_mixer_mlp_kernel, kind=kind, tm=tm, n_ctx=n_ctx),
        grid=(b, t // tm),
        in_specs=[tok] + [spec(tm) for spec in mixer_specs] + [
            pl.BlockSpec((1, 6, d), lambda i, j: (i, 0, 0)),
            _const_spec((6, d)),
            _const_spec((d, d)),
            _const_spec((1, d)),
            _const_spec((d, D_FF)),
            _const_spec((D_FF, d)),
        ],
        out_specs=tok,
        out_shape=jax.ShapeDtypeStruct((b, t, d), F32),
        compiler_params=_cparams("parallel", "parallel"),
        name=kind + "_out_mlp",
    )(u, *mixer_inputs, mx, mz, w_o, g.reshape(1, d), w1, w2)


def _mlp(u, mx, mz, g, w1, w2, final_g, n_ctx, final):
    b, t, d = u.shape
    tm = _pick_tile(t, 544)
    return pl.pallas_call(
        functools.partial(_mlp_kernel, tm=tm, n_ctx=n_ctx, final=final),
        grid=(b, t // tm),
        in_specs=[
            pl.BlockSpec((1, tm, d), lambda i, j: (i, j, 0)),
            pl.BlockSpec((1, 6, d), lambda i, j: (i, 0, 0)),
            _const_spec((6, d)),
            _const_spec((1, d)),
            _const_spec((d, D_FF)),
            _const_spec((D_FF, d)),
            _const_spec((1, d)),
        ],
        out_specs=pl.BlockSpec((1, tm, d), lambda i, j: (i, j, 0)),
        out_shape=jax.ShapeDtypeStruct((b, t, d), F32),
        compiler_params=_cparams("parallel", "parallel"),
        name="mlp",
    )(u, mx, mz, g.reshape(1, d), w1, w2, final_g.reshape(1, d))


def _gla_proj_kernel(u_ref, mx_ref, mz_ref, g_ref, w_ref, wlr_ref, wdec_ref, bdec_ref,
                     p_ref, gate_ref, *, tm, n_ctx):
    t = pl.program_id(1)
    h = _norm_mod(u_ref[0], g_ref[...], mx_ref[0], mz_ref[...], t * tm, n_ctx, 0, 1).astype(BF16)
    n = D_MODEL
    for s in range(w_ref.shape[1] // n):
        p_ref[0, :, s * n:(s + 1) * n] = _dot(h, w_ref[:, s * n:(s + 1) * n])
    lr = _dot(h, wlr_ref[...]).astype(BF16)
    z = _dot(lr, wdec_ref[...]) + bdec_ref[...]
    gate_ref[0] = -_softplus(-z) * (1.0 / GLA_GATE_NORM)


def _gla_proj(u, mx, mz, g, w_main, w_lr, w_dec, b_dec, n_ctx):
    b, t, d = u.shape
    n = w_main.shape[1]
    tm = _pick_tile(t, 544)
    return pl.pallas_call(
        functools.partial(_gla_proj_kernel, tm=tm, n_ctx=n_ctx),
        grid=(b, t // tm),
        in_specs=[
            pl.BlockSpec((1, tm, d), lambda i, j: (i, j, 0)),
            pl.BlockSpec((1, 6, d), lambda i, j: (i, 0, 0)),
            _const_spec((6, d)),
            _const_spec((1, d)),
            _const_spec((d, n)),
            _const_spec((d, LANES)),
            _const_spec((LANES, 2 * GLA_DK)),
            _const_spec((1, 2 * GLA_DK)),
        ],
        out_specs=[
            pl.BlockSpec((1, tm, n), lambda i, j: (i, j, 0)),
            pl.BlockSpec((1, tm, 2 * GLA_DK), lambda i, j: (i, j, 0)),
        ],
        out_shape=[
            jax.ShapeDtypeStruct((b, t, n), F32),
            jax.ShapeDtypeStruct((b, t, 2 * GLA_DK), F32),
        ],
        compiler_params=_cparams("parallel", "parallel"),
        name="gla_proj",
    )(u, mx, mz, g.reshape(1, d), w_main, w_lr, w_dec, b_dec)


def _tri(n, upper, strict):
    r = lax.broadcasted_iota(jnp.int32, (n, n), 0)
    c = lax.broadcasted_iota(jnp.int32, (n, n), 1)
    if upper:
        return (r < c) if strict else (r <= c)
    return (r > c) if strict else (r >= c)


def _cumsum_rows(x, descending):
    n = x.shape[0]
    row = lax.broadcasted_iota(jnp.int32, (n, 1), 0)
    s = 1
    while s < n:
        if descending:
            x = x + jnp.where(row < n - s, pltpu.roll(x, n - s, 0), 0.0)
        else:
            x = x + jnp.where(row >= s, pltpu.roll(x, s, 0), 0.0)
        s *= 2
    return x


def _gla_scan_kernel(q_ref, k_ref, v_ref, g0_ref, g1_ref, o_ref, s_ref, *, n_ctx_chunks, n_chunks, nb):
    qscale = GLA_HK ** -0.5
    g_refs = (g0_ref, g1_ref)
    incl = (_tri(CHUNK, False, False), _tri(CHUNK, True, False))
    o_ref[...] = jnp.zeros_like(o_ref)

    def step(first):
        units = [(d, first[d] + (i if d == 0 else -i)) for i in range(nb) for d in range(2)]
        rows = [pl.ds(pl.multiple_of(c * CHUNK, CHUNK), CHUNK) for _, c in units]
        q = [q_ref[0, r, :] for r in rows]
        k = [k_ref[0, r, :] for r in rows]
        v = [v_ref[0, r, :].astype(BF16) for r in rows]
        g = [g_refs[d][0, r, :] for (d, _), r in zip(units, rows)]
        bcum = [_cumsum_rows(gi, d == 1) for (d, _), gi in zip(units, g)]
        b_last_row = [jnp.sum(gi, axis=0, keepdims=True) for gi in g]
        dec_col = [jnp.exp(jnp.sum(gi.T, axis=1, keepdims=True)) for gi in g]
        q_e = [(qi * jnp.exp(bi) * qscale).astype(BF16) for qi, bi in zip(q, bcum)]
        k_e = [(ki * jnp.exp(-bi)).astype(BF16) for ki, bi in zip(k, bcum)]
        k_dec_t = [(ki * jnp.exp(bl - bi)).T.astype(BF16) for ki, bl, bi in zip(k, b_last_row, bcum)]
        a = [jnp.where(incl[d], _dot_nt(qe, ke), 0.0).astype(BF16) for (d, _), qe, ke in zip(units, q_e, k_e)]
        o_intra = [_dot(ai, vi) for ai, vi in zip(a, v)]
        s_inc = [_dot(kt, vi) for kt, vi in zip(k_dec_t, v)]
        s = [s_ref[0], s_ref[1]]
        for i, (d, _) in enumerate(units):
            o_ref[0, rows[i], :] += o_intra[i] + _dot(q_e[i], s[d].astype(BF16))
            s[d] = dec_col[i] * s[d] + s_inc[i]
        s_ref[0] = s[0]
        s_ref[1] = s[1]

    s_ref[...] = jnp.zeros_like(s_ref)
    n_lat = n_chunks - n_ctx_chunks

    def ctx_body(i, carry):
        step((i * nb, n_ctx_chunks - 1 - i * nb))
        return carry

    def lat_body(i, carry):
        step((n_ctx_chunks + i * nb, n_chunks - 1 - i * nb))
        return carry

    lax.fori_loop(0, n_ctx_chunks // nb, ctx_body, 0)
    lax.fori_loop(0, n_lat // nb, lat_body, 0)


def _gla_scan(p, gates, n_ctx, nb):
    b, t, _ = p.shape
    nkb = GLA_DK // GLA_HK
    return pl.pallas_call(
        functools.partial(_gla_scan_kernel, n_ctx_chunks=n_ctx // CHUNK, n_chunks=t // CHUNK, nb=nb),
        grid=(b, GLA_HEADS),
        in_specs=[
            pl.BlockSpec((1, t, GLA_HK), lambda i, h: (i, 0, h)),
            pl.BlockSpec((1, t, GLA_HK), lambda i, h: (i, 0, nkb + h)),
            pl.BlockSpec((1, t, GLA_HV), lambda i, h: (i, 0, (2 * GLA_DK) // GLA_HV + h)),
            pl.BlockSpec((1, t, GLA_HK), lambda i, h: (i, 0, h)),
            pl.BlockSpec((1, t, GLA_HK), lambda i, h: (i, 0, nkb + h)),
        ],
        out_specs=pl.BlockSpec((1, t, GLA_HV), lambda i, h: (i, 0, h)),
        out_shape=jax.ShapeDtypeStruct((b, t, GLA_DV), F32),
        scratch_shapes=[pltpu.VMEM((2, GLA_HK, GLA_HV), F32)],
        compiler_params=_cparams("parallel", "parallel"),
        name="gla_scan",
    )(p, p, p, gates, gates)


def _gla_gated_parts(o_ref, gt_ref, ng_ref):
    parts = []
    for h in range(GLA_HEADS):
        ls = slice(h * GLA_HV, (h + 1) * GLA_HV)
        ov = o_ref[0, :, ls]
        ov = ov * lax.rsqrt(jnp.mean(ov * ov, axis=-1, keepdims=True) + NORM_EPS) * ng_ref[...]
        gt = gt_ref[0, :, ls]
        parts.append((ov * (gt * _sigmoid(gt))).astype(BF16))
    return parts


def _gla_out_kernel(u_ref, o_ref, gt_ref, ng_ref, mx_ref, mz_ref, w_ref, out_ref, *, tm, n_ctx):
    row0 = pl.program_id(1) * tm
    parts = _gla_gated_parts(o_ref, gt_ref, ng_ref)
    out_ref[0] = _gated_residual(u_ref[0], parts, w_ref, mx_ref, mz_ref, row0, n_ctx)


def _gla_out(u, o, p, norm_g, mx, mz, w_o, n_ctx):
    b, t, d = u.shape
    tm = _pick_tile(t, 544)
    gt_blk = (2 * GLA_DK + GLA_DV) // d
    return pl.pallas_call(
        functools.partial(_gla_out_kernel, tm=tm, n_ctx=n_ctx),
        grid=(b, t // tm),
        in_specs=[
            pl.BlockSpec((1, tm, d), lambda i, j: (i, j, 0)),
            pl.BlockSpec((1, tm, d), lambda i, j: (i, j, 0)),
            pl.BlockSpec((1, tm, d), lambda i, j: (i, j, gt_blk)),
            _const_spec((1, GLA_HV)),
            pl.BlockSpec((1, 6, d), lambda i, j: (i, 0, 0)),
            _const_spec((6, d)),
            _const_spec((d, d)),
        ],
        out_specs=pl.BlockSpec((1, tm, d), lambda i, j: (i, j, 0)),
        out_shape=jax.ShapeDtypeStruct((b, t, d), F32),
        compiler_params=_cparams("parallel", "parallel"),
        name="gla_out",
    )(u, o, p, norm_g.reshape(1, GLA_HV), mx, mz, w_o)


def _rw_proj_kernel(u_ref, up_ref, un_ref, mx_ref, mz_ref, g_ref, mix_ref, wrkv_ref, w1_ref, w2_ref,
                    w0_ref, a1_ref, a2_ref, a0_ref, g1_ref, g2_ref, ka_ref, rk_ref,
                    r_ref, k_ref, v_ref, gate_ref, lw_ref, a_ref, bonus_ref, *, tm, n_ctx, t_total):
    row0 = pl.program_id(1) * tm
    g = g_ref[...]
    mx = mx_ref[0]
    mz = mz_ref[...]
    h = _norm_mod(u_ref[0], g, mx, mz, row0, n_ctx, 0, 1)
    h_prev = _norm_mod(up_ref[0], g, mx, mz, row0 - 8, n_ctx, 0, 1)[7:8]
    h_next = _norm_mod(un_ref[0], g, mx, mz, row0 + tm, n_ctx, 0, 1)[0:1]
    idx = lax.broadcasted_iota(jnp.int32, (tm, 1), 0)
    rows = row0 + idx
    h_dn = jnp.where(idx == 0, h_prev, pltpu.roll(h, 1, 0))
    h_dn = jnp.where((rows == 0) | (rows == n_ctx), 0.0, h_dn)
    h_up = jnp.where(idx == tm - 1, h_next, pltpu.roll(h, tm - 1, 0))
    h_up = jnp.where((rows == n_ctx - 1) | (rows == t_total - 1), 0.0, h_up)
    xx = 0.5 * (h_dn + h_up) - h
    xr, xw, xk, xv, xa, xg = ((h + xx * mix_ref[m:m + 1]).astype(BF16) for m in range(6))
    r = _dot(xr, wrkv_ref[0])
    k = _dot(xk, wrkv_ref[1])
    v = _dot(xv, wrkv_ref[2])
    r_ref[0] = r
    k_ref[0] = k
    v_ref[0] = v
    gate_ref[0] = _dot(_sigmoid(_dot(xg, g1_ref[...])).astype(BF16), g2_ref[...])
    tw = jnp.tanh(_dot(xw, w1_ref[...])).astype(BF16)
    lw_ref[0] = -math.exp(-0.5) * _sigmoid(w0_ref[...] + _dot(tw, w2_ref[...]))
    ta = _dot(xa, a1_ref[...]).astype(BF16)
    a = _sigmoid(a0_ref[...] + _dot(ta, a2_ref[...]))
    a_ref[0] = a
    lane_lo = lax.broadcasted_iota(jnp.int32, (1, LANES), 1) < RW_HEAD
    d = D_MODEL
    for p in range(d // LANES):
        ls = slice(p * LANES, (p + 1) * LANES)
        kd_sum = k[:, ls] * (2.0 + (a[:, ls] + a[:, d + p * LANES:d + (p + 1) * LANES] - 2.0) * ka_ref[:, ls])
        bonus_ref[0, :, ls] = _pair_sum(r[:, ls] * kd_sum * rk_ref[:, ls], lane_lo) * v[:, ls]


def _rw_proj(u, mx, mz, g, wts, n_ctx):
    b, t, d = u.shape
    tm = _pick_tile(t, 256)
    nb8 = t // 8
    tb = tm // 8
    full = lambda n: jax.ShapeDtypeStruct((b, t, n), F32)
    row_spec = lambda n: pl.BlockSpec((1, tm, n), lambda i, j: (i, j, 0))
    return pl.pallas_call(
        functools.partial(_rw_proj_kernel, tm=tm, n_ctx=n_ctx, t_total=t),
        grid=(b, t // tm),
        in_specs=[
            row_spec(d),
            pl.BlockSpec((1, 8, d), lambda i, j: (i, jnp.maximum(j * tb - 1, 0), 0)),
            pl.BlockSpec((1, 8, d), lambda i, j: (i, jnp.minimum((j + 1) * tb, nb8 - 1), 0)),
            pl.BlockSpec((1, 6, d), lambda i, j: (i, 0, 0)),
            _const_spec((6, d)),
            _const_spec((1, d)),
            _const_spec((6, d)),
            _const_spec((3, d, d)),
            _const_spec((d, 2 * RW_LORA)),
            _const_spec((2 * RW_LORA, 2 * d)),
            _const_spec((1, 2 * d)),
            _const_spec((d, 2 * RW_LORA)),
            _const_spec((2 * RW_LORA, 2 * d)),
            _const_spec((1, 2 * d)),
            _const_spec((d, 2 * LANES)),
            _const_spec((2 * LANES, d)),
            _const_spec((1, d)),
            _const_spec((1, d)),
        ],
        out_specs=[row_spec(d), row_spec(d), row_spec(d), row_spec(d), row_spec(2 * d), row_spec(2 * d),
                   row_spec(d)],
        out_shape=[full(d), full(d), full(d), full(d), full(2 * d), full(2 * d), full(d)],
        compiler_params=_cparams("parallel", "parallel"),
        name="rw_proj",
    )(u, u, u, mx, mz, g.reshape(1, d), wts["mix"], wts["w_rkv"], wts["w1"], wts["w2"], wts["w0"],
      wts["a1"], wts["a2"], wts["a0"], wts["g1"], wts["g2"], wts["k_a"], wts["r_k"])


def _rw_pair_terms(units, side_work=()):
    C = CHUNK
    lane_lo = lax.broadcasted_iota(jnp.int32, (1, LANES), 1) < RW_HEAD
    row2 = lax.broadcasted_iota(jnp.int32, (C, LANES), 0)
    col2 = lax.broadcasted_iota(jnp.int32, (C, LANES), 1) & (C - 1)
    strict2 = (row2 > col2, row2 < col2)
    incl2 = (row2 >= col2, row2 <= col2)
    eye2 = jnp.where(row2 == col2, 1.0, 0.0)
    nu = len(units)

    def bd(x):
        zero = jnp.zeros_like(x)
        return jnp.concatenate([jnp.where(lane_lo, x, zero), jnp.where(lane_lo, zero, x)], axis=0)

    def hilo(x):
        hi = x.astype(BF16)
        return hi, (x - hi.astype(F32)).astype(BF16)

    side = iter(side_work)

    def stage_done():
        thunk = next(side, None)
        if thunk is not None:
            thunk()

    pre = []
    for r2, k2, v2, lw2, a2, desc, kkw, kaw in units:
        d = int(desc)
        kk = k2 * kkw
        kk = kk * lax.rsqrt(jnp.maximum(_pair_sum(kk * kk, lane_lo), 1e-24))
        kd = k2 * (1.0 + (a2 - 1.0) * kaw)
        bvec = kk * a2
        cum = _cumsum_rows(lw2, desc)
        tot = jnp.sum(lw2, axis=0, keepdims=True)
        e_neg = jnp.exp(-cum)
        e_rem = jnp.exp(tot - cum)
        pre.append(dict(
            d=d, tot=tot, v_b=v2.astype(BF16),
            a_t=-kk * jnp.exp(cum - lw2), r_t=r2 * jnp.exp(cum),
            b_h=(bvec * e_neg).astype(BF16), k_h=(kd * e_neg).astype(BF16),
            b_rem_t=(bvec * e_rem).T.astype(BF16), k_rem_t=(kd * e_rem).T.astype(BF16)))

    ar = [jnp.concatenate([p["a_t"], p["r_t"]], axis=0).astype(BF16) for p in pre]
    p_bk = [_dot_nt(ar[u], jnp.concatenate([bd(pre[u]["b_h"]), bd(pre[u]["k_h"])], axis=0)) for u in range(nu)]
    l_ab = [jnp.where(strict2[pre[u]["d"]], p_bk[u][:C, :LANES], 0.0) for u in range(nu)]
    m_rb = [jnp.where(incl2[pre[u]["d"]], p_bk[u][C:, :LANES], 0.0).astype(BF16) for u in range(nu)]
    l_ak = [jnp.where(strict2[pre[u]["d"]], p_bk[u][:C, LANES:], 0.0).astype(BF16) for u in range(nu)]
    m_rk = [jnp.where(incl2[pre[u]["d"]], p_bk[u][C:, LANES:], 0.0).astype(BF16) for u in range(nu)]
    stage_done()

    xo = l_ab
    l_b16 = [l.astype(BF16) for l in l_ab]
    pw = [_dot(l, bd(l)) for l in l_b16]
    for _ in range(3):
        p_b16 = [p.astype(BF16) for p in pw]
        ra = [_dot(jnp.concatenate([p, x.astype(BF16)], axis=0), bd(p)) for p, x in zip(p_b16, xo)]
        xo = [x + p + a[C:] for x, p, a in zip(xo, pw, ra)]
        pw = [a[:C] for a in ra]
        stage_done()
    xo = [x + p + _dot(x.astype(BF16), bd(p.astype(BF16))) for x, p in zip(xo, pw)]
    l_hl = [hilo(l) for l in l_ab]
    x_hl = [hilo(x) for x in xo]
    lx = [_dot(jnp.concatenate([lh, ll], axis=0), bd(xh)) for (lh, ll), (xh, _) in zip(l_hl, x_hl)]
    lxb = [_dot(lh, bd(xlo)) for (lh, _), (_, xlo) in zip(l_hl, x_hl)]
    res = [(l - x) + (a[:C] + a[C:] + b) for l, x, a, b in zip(l_ab, xo, lx, lxb)]
    stage_done()
    xo = [x + r + _dot(xh, bd(r.astype(BF16))) for x, r, (xh, _) in zip(xo, res, x_hl)]
    t_inv = [(eye2 + x).astype(BF16) for x in xo]

    lvyk = [_dot(jnp.concatenate([l_ak[u], m_rk[u]], axis=0), bd(pre[u]["v_b"])) for u in range(nu)]
    kv = [_dot(pre[u]["k_rem_t"], pre[u]["v_b"]) for u in range(nu)]
    tw = [_dot(t_inv[u], jnp.concatenate([bd(ar[u][:C]), bd(lvyk[u][:C].astype(BF16))], axis=1))
          for u in range(nu)]
    qy = [_dot(m_rb[u], jnp.concatenate([bd(tw[u][:, :LANES].astype(BF16)),
                                         bd(tw[u][:, LANES:].astype(BF16))], axis=1)) for u in range(nu)]
    gh = [_dot(pre[u]["b_rem_t"], tw[u].astype(BF16)) for u in range(nu)]

    r = lax.broadcasted_iota(jnp.int32, (LANES, LANES), 0)
    c = lax.broadcasted_iota(jnp.int32, (LANES, LANES), 1)
    same_head = (r < RW_HEAD) == (c < RW_HEAD)
    out = []
    for u in range(nu):
        qp = pre[u]["r_t"] + qy[u][:, :LANES]
        y0 = qy[u][:, LANES:] + lvyk[u][C:]
        g_mat = jnp.where(same_head, gh[u][:, :LANES], 0.0) + jnp.where(r == c, jnp.exp(pre[u]["tot"]), 0.0)
        h_mat = jnp.where(same_head, gh[u][:, LANES:] + kv[u], 0.0)
        out.append((qp, y0, g_mat, h_mat))
    return out


def _rw_mix_kernel(r0_ref, k0_ref, v0_ref, lw0_ref, a0_ref, r1_ref, k1_ref, v1_ref, lw1_ref, a1_ref,
                   kk_ref, ka_ref, y0_ref, y1_ref, st_ref, qp_s, yc_s, g_s, h_s, *, nb, pp):
    def zero_scratch(_, carry):
        for ref in (st_ref, qp_s, yc_s, g_s, h_s):
            ref[...] = jnp.zeros_like(ref)
        return carry

    lax.fori_loop(0, jnp.where(pl.program_id(2) == 0, 1, 0), zero_scratch, 0)

    y_refs = (y0_ref, y1_ref)
    chains = [(d, p) for d in range(2) for p in range(pp)]
    states = {dp: st_ref[dp[0], dp[1]] for dp in chains}
    prev_terms = {(d, p, c): (qp_s[d, p, c], yc_s[d, p, c], g_s[d, p, c], h_s[d, p, c])
                  for d, p in chains for c in range(nb)}

    def link(ci):
        def emit():
            for d, p in chains:
                c = ci if d == 0 else nb - 1 - ci
                qp, yc, g_mat, h_mat = prev_terms[d, p, c]
                st_b = states[d, p].astype(BF16)
                y_refs[d][0, c * CHUNK:(c + 1) * CHUNK, p * LANES:(p + 1) * LANES] = _dot(qp, st_b) + yc
                states[d, p] = _dot(g_mat, st_b) + h_mat
        return emit

    ins = ((r0_ref, k0_ref, v0_ref, lw0_ref, a0_ref), (r1_ref, k1_ref, v1_ref, lw1_ref, a1_ref))
    units = []
    index = []
    for p in range(pp):
        ls = slice(p * LANES, (p + 1) * LANES)
        for c in range(nb):
            rows = slice(c * CHUNK, (c + 1) * CHUNK)
            for d in range(2):
                units.append(tuple(ref[0, rows, ls] for ref in ins[d]) + (d == 1, kk_ref[:, ls], ka_ref[:, ls]))
                index.append((d, p, c))
    links = [link(ci) for ci in range(nb)]
    assert nb <= 5, "one recurrence link per stage boundary of _rw_pair_terms"
    results = _rw_pair_terms(units, side_work=links)
    for d, p in chains:
        st_ref[d, p] = states[d, p]
    for (d, p, c), (qp, y0, g_mat, h_mat) in zip(index, results):
        qp_s[d, p, c] = qp.astype(BF16)
        yc_s[d, p, c] = y0
        g_s[d, p, c] = g_mat.astype(BF16)
        h_s[d, p, c] = h_mat


def _rw_mix(r, k, v, lw, a, k_k, k_a, n_ctx, nb):
    b, t, d = r.shape
    npairs = d // LANES
    rows = nb * CHUNK
    nblk = t // rows
    ncb = n_ctx // rows

    def rev(j):
        return jnp.where(j < ncb, ncb - 1 - j, nblk - 1 - (j - ncb))

    cur = lambda j: jnp.minimum(j, nblk - 1)
    prev = lambda j: jnp.maximum(j - 1, 0)
    pp = 2
    pw = pp * LANES
    ng = npairs // pp
    fwd = lambda off, blk: pl.BlockSpec((1, rows, pw), lambda i, p, j: (i, blk(j), off + p))
    bwd = lambda off, blk: pl.BlockSpec((1, rows, pw), lambda i, p, j: (i, rev(blk(j)), off + p))
    par = pl.BlockSpec((1, pw), lambda i, p, j: (0, p))
    return pl.pallas_call(
        functools.partial(_rw_mix_kernel, nb=nb, pp=pp),
        grid=(b, ng, nblk + 1),
        in_specs=[fwd(0, cur), fwd(0, cur), fwd(0, cur), fwd(0, cur), fwd(0, cur),
                  bwd(0, cur), bwd(0, cur), bwd(0, cur), bwd(ng, cur), bwd(ng, cur), par, par],
        out_specs=[fwd(0, prev), bwd(0, prev)],
        out_shape=[jax.ShapeDtypeStruct((b, t, d), F32), jax.ShapeDtypeStruct((b, t, d), F32)],
        scratch_shapes=[pltpu.VMEM((2, pp, LANES, LANES), F32),
                        pltpu.VMEM((2, pp, nb, CHUNK, LANES), BF16), pltpu.VMEM((2, pp, nb, CHUNK, LANES), F32),
                        pltpu.VMEM((2, pp, nb, LANES, LANES), BF16), pltpu.VMEM((2, pp, nb, LANES, LANES), F32)],
        compiler_params=_cparams("parallel", "parallel", "arbitrary"),
        name="rw_mix",
    )(r, k, v, lw, a, r, k, v, lw, a, k_k.reshape(1, d), k_a.reshape(1, d))


def _rw_out_kernel(u_ref, y0_ref, y1_ref, bonus_ref, gate_ref, lng_ref, lnb_ref, mx_ref, mz_ref, w_ref,
                   out_ref, *, tm, n_ctx):
    row0 = pl.program_id(1) * tm
    lane_lo = lax.broadcasted_iota(jnp.int32, (1, LANES), 1) < RW_HEAD
    parts = []
    for p in range(D_MODEL // LANES):
        ls = slice(p * LANES, (p + 1) * LANES)
        y = y0_ref[0, :, ls] + y1_ref[0, :, ls]
        mu = _pair_sum(y, lane_lo) * (1.0 / RW_HEAD)
        yc = y - mu
        var = _pair_sum(yc * yc, lane_lo) * (1.0 / RW_HEAD)
        yn = yc * lax.rsqrt(var + RW_GN_EPS) * lng_ref[:, ls] + lnb_ref[:, ls]
        parts.append(((yn + bonus_ref[0, :, ls]) * gate_ref[0, :, ls]).astype(BF16))
    out_ref[0] = _gated_residual(u_ref[0], parts, w_ref, mx_ref, mz_ref, row0, n_ctx)


def _rw_out(u, y0, y1, bonus, gate, ln_g, ln_b, mx, mz, w_o, n_ctx):
    b, t, d = u.shape
    tm = _pick_tile(t, 544)
    tok = pl.BlockSpec((1, tm, d), lambda i, j: (i, j, 0))
    vec = _const_spec((1, d))
    return pl.pallas_call(
        functools.partial(_rw_out_kernel, tm=tm, n_ctx=n_ctx),
        grid=(b, t // tm),
        in_specs=[tok] * 5 + [vec, vec, pl.BlockSpec((1, 6, d), lambda i, j: (i, 0, 0)),
                              _const_spec((6, d)), _const_spec((d, d))],
        out_specs=tok,
        out_shape=jax.ShapeDtypeStruct((b, t, d), F32),
        compiler_params=_cparams("parallel", "parallel"),
        name="rw_out",
    )(u, y0, y1, bonus, gate, ln_g.reshape(1, d), ln_b.reshape(1, d), mx, mz, w_o)


def _block_rows(w_pair):
    _, k, n = w_pair.shape
    z = jnp.zeros((k, n), w_pair.dtype)
    return jnp.concatenate([jnp.concatenate([w_pair[0], z], axis=1),
                            jnp.concatenate([z, w_pair[1]], axis=1)], axis=0)


def kernel(x, c, ctx, c_ctx, ada_w, ada_b, norm1_g, norm2_g, mlp_w1, mlp_w2, final_g, na_w_qkv, na_w_o, na_rpb, gla_w_in, gla_w_dec2, gla_b_dec, gla_norm_g, gla_w_o, rw_mix, rw_w_rkv, rw_w0, rw_w1, rw_w2, rw_a0, rw_a1, rw_a2, rw_g1, rw_g2, rw_k_k, rw_k_a, rw_r_k, rw_ln_g, rw_ln_b, rw_w_o):
    b, seq, d = x.shape
    n_ctx = ctx.shape[1]
    depth = ada_w.shape[0]
    assert d == D_MODEL and n_ctx % CHUNK == 0 and seq % CHUNK == 0

    entry_fused = seq % n_ctx == 0 and n_ctx % 16 == 0
    u = None if entry_fused else jnp.concatenate([ctx.astype(x.dtype), x], axis=1)

    rows = -(-(b + 1) // 8) * 8
    c_all = jnp.zeros((rows, d), F32).at[:b].set(c).at[b].set(c_ctx)
    tables = _ada_tables(c_all, ada_w, ada_b)
    mods_x = tables[:, :b].reshape(depth, b, 6, d)
    mods_z = tables[:, b].reshape(depth, 6, d)

    nb = max(n for n in (4, 2, 1) if (n_ctx // CHUNK) % n == 0 and (seq // CHUNK) % n == 0)

    for i in range(depth):
        kind, j = i % 3, i // 3
        last = i == depth - 1
        mx, mz = mods_x[i], mods_z[i]
        w1, w2 = mlp_w1[i].astype(BF16), mlp_w2[i].astype(BF16)
        tok = lambda tm: pl.BlockSpec((1, tm, d), lambda bi, ti: (bi, ti, 0))
        mlp_done = False
        if kind == 0:
            if u is None:
                qkv, u = _entry_proj(x, ctx, mx, mz, norm1_g[i], na_w_qkv[j].astype(BF16), BF16)
            else:
                qkv = _norm_proj(u, mx, mz, norm1_g[i], na_w_qkv[j].astype(BF16), n_ctx, BF16)
            o = _na_attention(qkv, _na_bias_table(na_rpb[j]), n_ctx, with_ctx=not last)
            if last:
                u = _na_out(u, o, mx, mz, na_w_o[j].astype(BF16), n_ctx, latent_only=True)
            else:
                u = _mixer_mlp("na", u, [o], [tok], mx, mz, na_w_o[j].astype(BF16), norm2_g[i], w1, w2, n_ctx)
                mlp_done = True
        elif kind == 1:
            n_main = 2 * GLA_DK + 2 * GLA_DV
            w_in = gla_w_in[j]
            w_lr = jnp.zeros((d, LANES), F32).at[:, :2 * GLA_LOW_RANK].set(w_in[:, n_main:])
            w_dec = jnp.zeros((LANES, 2 * GLA_DK), F32).at[:2 * GLA_LOW_RANK].set(_block_rows(gla_w_dec2[j]))
            p, gates = _gla_proj(u, mx, mz, norm1_g[i], w_in[:, :n_main].astype(BF16), w_lr.astype(BF16),
                                 w_dec.astype(BF16), gla_b_dec[j].reshape(1, 2 * GLA_DK), n_ctx)
            o = _gla_scan(p, gates, n_ctx, nb)
            if last:
                u = _gla_out(u, o, p, gla_norm_g[j], mx, mz, gla_w_o[j].astype(BF16), n_ctx)
            else:
                gt_blk = (2 * GLA_DK + GLA_DV) // d
                gt_spec = lambda tm: pl.BlockSpec((1, tm, d), lambda bi, ti: (bi, ti, gt_blk))
                u = _mixer_mlp("gla", u, [o, p, gla_norm_g[j].reshape(1, GLA_HV)],
                               [tok, gt_spec, lambda tm: _const_spec((1, GLA_HV))],
                               mx, mz, gla_w_o[j].astype(BF16), norm2_g[i], w1, w2, n_ctx)
                mlp_done = True
        else:
            g1 = jnp.zeros((d, 2 * LANES), F32).at[:, :RW_GATE_LORA].set(rw_g1[j])
            g2 = jnp.zeros((2 * LANES, d), F32).at[:RW_GATE_LORA].set(rw_g2[j])
            wts = dict(
                mix=rw_mix[j], w_rkv=rw_w_rkv[j].astype(BF16),
                w1=jnp.concatenate([rw_w1[j, 0], rw_w1[j, 1]], axis=1).astype(BF16),
                w2=_block_rows(rw_w2[j]).astype(BF16), w0=rw_w0[j].reshape(1, 2 * d),
                a1=jnp.concatenate([rw_a1[j, 0], rw_a1[j, 1]], axis=1).astype(BF16),
                a2=_block_rows(rw_a2[j]).astype(BF16), a0=rw_a0[j].reshape(1, 2 * d),
                g1=g1.astype(BF16), g2=g2.astype(BF16),
                k_a=rw_k_a[j].reshape(1, d), r_k=rw_r_k[j].reshape(1, d))
            r, k, v, gate, lw, a, bonus = _rw_proj(u, mx, mz, norm1_g[i], wts, n_ctx)
            y0, y1 = _rw_mix(r, k, v, lw, a, rw_k_k[j], rw_k_a[j], n_ctx, nb)
            u = _rw_out(u, y0, y1, bonus, gate, rw_ln_g[j], rw_ln_b[j], mx, mz, rw_w_o[j].astype(BF16), n_ctx)
        if last and u.shape[1] != seq:
            u = u[:, n_ctx:]
        if not mlp_done:
            u = _mlp(u, mx, mz, norm2_g[i], w1, w2, final_g, 0 if last else n_ctx, final=last)
    return u
```

```python
import functools
import math

import jax
import jax.numpy as jnp
from jax import lax
from jax.experimental import pallas as pl
from jax.experimental.pallas import tpu as pltpu

F32 = jnp.float32
BF16 = jnp.bfloat16

D_MODEL = 1024
D_FF = 4 * D_MODEL
NORM_EPS = 1e-6
GRID_W = 64

NA_HEAD_DIM = 64
NA_HEADS = D_MODEL // NA_HEAD_DIM
NA_WIN_R = 8
NA_WIN_C = 16
NA_MASK = -1e30

GLA_HEADS = 4
GLA_DK = D_MODEL // 2
GLA_DV = D_MODEL
GLA_HK = GLA_DK // GLA_HEADS
GLA_HV = GLA_DV // GLA_HEADS
GLA_LOW_RANK = 16
GLA_GATE_NORM = 16.0
CHUNK = 64

RW_HEAD = 64
RW_GATE_LORA = 160
RW_LORA = 64
RW_GN_EPS = 64e-5

LANES = 128
VMEM_LIMIT = 56 * 1024 * 1024


def _cparams(*sem):
    return pltpu.CompilerParams(dimension_semantics=sem, vmem_limit_bytes=VMEM_LIMIT)


def _pick_tile(n, target, mult=16):
    best = None
    for t in range(mult, min(n, target) + 1, mult):
        if n % t == 0:
            best = t
    assert best is not None, (n, target)
    return best


def _const_spec(shape):
    nd = len(shape)
    return pl.BlockSpec(shape, lambda *_: (0,) * nd, pipeline_mode=pl.Buffered(1))


def _dot(a, b):
    return jnp.dot(a, b, preferred_element_type=F32)


def _dot_nt(a, b):
    return lax.dot_general(a, b, (((1,), (1,)), ((), ())), preferred_element_type=F32)


def _sigmoid(x):
    return 0.5 + 0.5 * jnp.tanh(0.5 * x)


def _softplus(x):
    return jnp.maximum(x, 0.0) + jnp.log(1.0 + jnp.exp(-jnp.abs(x)))


def _rms(u, g):
    return u * lax.rsqrt(jnp.mean(u * u, axis=-1, keepdims=True) + NORM_EPS) * g


def _mod_rows(mx, mz, idx, row0, rows, n_ctx):
    vx = mx[idx:idx + 1]
    if n_ctx == 0:
        return vx
    if n_ctx % rows == 0:
        return jnp.where(row0 < n_ctx, mz[idx:idx + 1], vx)
    r = row0 + lax.broadcasted_iota(jnp.int32, (rows, 1), 0)
    return jnp.where(r < n_ctx, mz[idx:idx + 1], vx)


def _norm_mod(u, g, mx, mz, row0, n_ctx, i_shift, i_scale):
    rows = u.shape[0]
    shift = _mod_rows(mx, mz, i_shift, row0, rows, n_ctx)
    scale = _mod_rows(mx, mz, i_scale, row0, rows, n_ctx)
    return _rms(u, g) * (1.0 + scale) + shift


def _pair_sum(x, lane_lo):
    s0 = jnp.sum(jnp.where(lane_lo, x, 0.0), axis=-1, keepdims=True)
    s1 = jnp.sum(jnp.where(lane_lo, 0.0, x), axis=-1, keepdims=True)
    return jnp.where(lane_lo, s0, s1)


def _ada_kernel(c_ref, w_ref, b_ref, o_ref):
    c = c_ref[...]
    sc = (c * _sigmoid(c)).astype(BF16)
    o_ref[0] = _dot(sc, w_ref[0].astype(BF16)) + b_ref[0]


def _ada_tables(c_all, ada_w, ada_b):
    depth, d, n = ada_w.shape
    rows = c_all.shape[0]
    tn = 1536
    return pl.pallas_call(
        _ada_kernel,
        grid=(depth, n // tn),
        in_specs=[
            pl.BlockSpec((rows, d), lambda l, j: (0, 0)),
            pl.BlockSpec((1, d, tn), lambda l, j: (l, 0, j)),
            pl.BlockSpec((1, 1, tn), lambda l, j: (l, 0, j)),
        ],
        out_specs=pl.BlockSpec((1, rows, tn), lambda l, j: (l, 0, j)),
        out_shape=jax.ShapeDtypeStruct((depth, rows, n), F32),
        compiler_params=_cparams("parallel", "parallel"),
        name="ada_tables",
    )(c_all, ada_w, ada_b.reshape(depth, 1, n))


def _proj_kernel(u_ref, mx_ref, mz_ref, g_ref, w_ref, o_ref, *, tm, n_ctx, nsplit):
    t = pl.program_id(1)
    h = _norm_mod(u_ref[0], g_ref[...], mx_ref[0], mz_ref[...], t * tm, n_ctx, 0, 1).astype(BF16)
    n = w_ref.shape[1] // nsplit
    for s in range(nsplit):
        o_ref[0, :, s * n:(s + 1) * n] = _dot(h, w_ref[:, s * n:(s + 1) * n]).astype(o_ref.dtype)


def _norm_proj(u, mx, mz, g, w, n_ctx, out_dtype):
    b, t, d = u.shape
    n = w.shape[1]
    tm = _pick_tile(t, 544)
    return pl.pallas_call(
        functools.partial(_proj_kernel, tm=tm, n_ctx=n_ctx, nsplit=n // D_MODEL),
        grid=(b, t // tm),
        in_specs=[
            pl.BlockSpec((1, tm, d), lambda i, j: (i, j, 0)),
            pl.BlockSpec((1, 6, d), lambda i, j: (i, 0, 0)),
            _const_spec((6, d)),
            _const_spec((1, d)),
            _const_spec((d, n)),
        ],
        out_specs=pl.BlockSpec((1, tm, n), lambda i, j: (i, j, 0)),
        out_shape=jax.ShapeDtypeStruct((b, t, n), out_dtype),
        compiler_params=_cparams("parallel", "parallel"),
        name="norm_proj",
    )(u, mx, mz, g.reshape(1, d), w)


def _entry_proj_kernel(ctx_ref, x_ref, mx_ref, mz_ref, g_ref, w_ref, o_ref, u_ref, *, tm, nsplit):
    j = pl.program_id(1)
    u = jnp.where(j == 0, ctx_ref[0].astype(F32), x_ref[0])
    u_ref[0] = u
    h = _norm_mod(u, g_ref[...], mx_ref[0], mz_ref[...], j * tm, tm, 0, 1).astype(BF16)
    n = w_ref.shape[1] // nsplit
    for s in range(nsplit):
        o_ref[0, :, s * n:(s + 1) * n] = _dot(h, w_ref[:, s * n:(s + 1) * n]).astype(o_ref.dtype)


def _entry_proj(x, ctx, mx, mz, g, w, out_dtype):
    b, seq, d = x.shape
    tm = ctx.shape[1]
    assert seq % tm == 0 and tm % 16 == 0
    n = w.shape[1]
    t = tm + seq
    return pl.pallas_call(
        functools.partial(_entry_proj_kernel, tm=tm, nsplit=n // D_MODEL),
        grid=(b, t // tm),
        in_specs=[
            pl.BlockSpec((1, tm, d), lambda i, j: (i, 0, 0)),
            pl.BlockSpec((1, tm, d), lambda i, j: (i, jnp.maximum(j - 1, 0), 0)),
            pl.BlockSpec((1, 6, d), lambda i, j: (i, 0, 0)),
            _const_spec((6, d)),
            _const_spec((1, d)),
            _const_spec((d, n)),
        ],
        out_specs=[pl.BlockSpec((1, tm, n), lambda i, j: (i, j, 0)),
                   pl.BlockSpec((1, tm, d), lambda i, j: (i, j, 0))],
        out_shape=[jax.ShapeDtypeStruct((b, t, n), out_dtype), jax.ShapeDtypeStruct((b, t, d), F32)],
        compiler_params=_cparams("parallel", "parallel"),
        name="entry_proj",
    )(ctx, x, mx, mz, g.reshape(1, d), w)


def _na_bias_table(rpb):
    w = GRID_W
    cols = jnp.arange(w)
    c_start = jnp.clip(cols - NA_WIN_C // 2, 0, w - NA_WIN_C)
    col_ok = (cols[None, :] >= c_start[:, None]) & (cols[None, :] < c_start[:, None] + NA_WIN_C)
    col_idx = jnp.clip(cols[None, :] - cols[:, None], -(NA_WIN_C - 1), NA_WIN_C - 1) + NA_WIN_C - 1
    onehot = col_idx[:, :, None] == jnp.arange(2 * NA_WIN_C - 1)
    m = jnp.sum(jnp.where(onehot[None, None], rpb.astype(F32)[:, :, None, None, :], 0.0), axis=-1)
    m = jnp.where(col_ok[None, None], m, NA_MASK)
    return jnp.concatenate([m[:, :-1], m[:, 1:]], axis=-1)


def _na_kernel(q_ref, k_ref, v_ref, bias_ref, o_ref, *, n_ctx, n_rows, with_ctx, rq):
    nctxb = n_ctx // GRID_W
    out_off = 0 if with_ctx else n_ctx
    scale = NA_HEAD_DIM ** -0.5
    assert scale == 0.125, "the score scale is folded into bf16 q, exact only for a power of two"
    lane_lo = lax.broadcasted_iota(jnp.int32, (1, LANES), 1) < NA_HEAD_DIM
    head_masks = (jnp.where(lane_lo, scale, 0.0).astype(BF16), jnp.where(lane_lo, 0.0, scale).astype(BF16))
    n_pairs = q_ref.shape[2] // LANES
    strip = NA_WIN_R * GRID_W
    jobs_idx = [(qi, p) for qi in range(rq) for p in range(n_pairs)]

    def q_rows(tok0, qi):
        return pl.ds(pl.multiple_of(tok0 + qi * GRID_W, GRID_W), GRID_W)

    def stacked_q(tok0, qi, p):
        q2 = q_ref[0, q_rows(tok0, qi), p * LANES:(p + 1) * LANES]
        return jnp.concatenate([q2 * head_masks[0], q2 * head_masks[1]], axis=0)

    def attend(tok0, jobs):
        scores = [[_dot_nt(q, k) if b is None else _dot_nt(q, k) + b
                   for k, b in zip(ks, bs)] for q, ks, _, bs in jobs]
        tops = []
        for sc in scores:
            m = sc[0].max(axis=-1, keepdims=True)
            for s in sc[1:]:
                m = jnp.maximum(m, s.max(axis=-1, keepdims=True))
            tops.append(m)
        exps = [[jnp.exp(s - m) for s in sc] for sc, m in zip(scores, tops)]
        dens = []
        for es in exps:
            den = es[0].sum(axis=-1, keepdims=True)
            for e in es[1:]:
                den = den + e.sum(axis=-1, keepdims=True)
            dens.append(den)
        for (qi, p), (_, _, vs, _), es, den in zip(jobs_idx, jobs, exps, dens):
            o = _dot(es[0].astype(BF16), vs[0])
            for e, v in zip(es[1:], vs[1:]):
                o = o + _dot(e.astype(BF16), v)
            o = o / den
            o2 = jnp.where(lane_lo, o[:GRID_W], o[GRID_W:])
            o_ref[0, q_rows(tok0 - out_off, qi), p * LANES:(p + 1) * LANES] = o2.astype(o_ref.dtype)

    def latent_rows(step, carry):
        tok0 = n_ctx + step * (rq * GRID_W)
        jobs = []
        for qi, p in jobs_idx:
            ls = slice(p * LANES, (p + 1) * LANES)
            r = step * rq + qi
            r0 = jnp.clip(r - NA_WIN_R // 2, 0, n_rows - NA_WIN_R)
            start = pl.multiple_of(n_ctx + r0 * GRID_W, GRID_W)
            ri0 = NA_WIN_R - 1 - (r - r0)
            bias = jnp.concatenate(
                [jnp.concatenate([bias_ref[2 * p + hh, ri0 + 2 * m] for m in range(NA_WIN_R // 2)], axis=1)
                 for hh in range(2)], axis=0)
            jobs.append((stacked_q(tok0, qi, p),
                         [k_ref[0, pl.ds(start, strip), ls], k_ref[0, 0:n_ctx, ls]],
                         [v_ref[0, pl.ds(start, strip), ls], v_ref[0, 0:n_ctx, ls]],
                         [bias, None]))
        attend(tok0, jobs)
        return carry

    def context_rows(step, carry):
        tok0 = step * (rq * GRID_W)
        jobs = []
        for qi, p in jobs_idx:
            ls = slice(p * LANES, (p + 1) * LANES)
            jobs.append((stacked_q(tok0, qi, p), [k_ref[0, 0:n_ctx, ls]], [v_ref[0, 0:n_ctx, ls]], [None]))
        attend(tok0, jobs)
        return carry

    if with_ctx:
        lax.fori_loop(0, nctxb // rq, context_rows, 0)
    lax.fori_loop(0, n_rows // rq, latent_rows, 0)


def _na_attention(qkv, bias, n_ctx, with_ctx):
    b, t, _ = qkv.shape
    seq = t - n_ctx
    n_rows = seq // GRID_W
    assert seq % GRID_W == 0 and n_ctx % GRID_W == 0 and n_rows >= NA_WIN_R
    hw = 2 * LANES
    ng = D_MODEL // hw
    nctxb = n_ctx // GRID_W
    rq = 4 if nctxb % 4 == 0 and n_rows % 4 == 0 else 2
    assert nctxb % rq == 0 and n_rows % rq == 0
    t_out = t if with_ctx else seq

    return pl.pallas_call(
        functools.partial(_na_kernel, n_ctx=n_ctx, n_rows=n_rows, with_ctx=with_ctx, rq=rq),
        grid=(b, ng),
        in_specs=[
            pl.BlockSpec((1, t, hw), lambda i, g: (i, 0, g)),
            pl.BlockSpec((1, t, hw), lambda i, g: (i, 0, ng + g)),
            pl.BlockSpec((1, t, hw), lambda i, g: (i, 0, 2 * ng + g)),
            pl.BlockSpec((hw // NA_HEAD_DIM, 2 * NA_WIN_R - 2, GRID_W, LANES), lambda i, g: (g, 0, 0, 0)),
        ],
        out_specs=pl.BlockSpec((1, t_out, hw), lambda i, g: (i, 0, g)),
        out_shape=jax.ShapeDtypeStruct((b, t_out, D_MODEL), BF16),
        compiler_params=_cparams("parallel", "parallel"),
        name="na_attention",
    )(qkv, qkv, qkv, bias)


def _gated_residual(u, o_bf16_parts, w_ref, mx_ref, mz_ref, row0, n_ctx):
    acc = None
    k0 = 0
    for part in o_bf16_parts:
        kw = part.shape[1]
        term = _dot(part, w_ref[k0:k0 + kw, :])
        acc = term if acc is None else acc + term
        k0 += kw
    gate = _mod_rows(mx_ref[0], mz_ref[...], 2, row0, u.shape[0], n_ctx)
    return u + gate * acc


def _na_out_kernel(u_ref, o_ref, mx_ref, mz_ref, w_ref, out_ref, *, tm, n_ctx, blk_off):
    row0 = (pl.program_id(1) + blk_off) * tm
    out_ref[0] = _gated_residual(u_ref[0], [o_ref[0]], w_ref, mx_ref, mz_ref, row0, n_ctx)


def _na_out(u, o, mx, mz, w_o, n_ctx, latent_only):
    b, t, d = u.shape
    if latent_only:
        tm = _pick_tile(n_ctx, 256)
        assert (t - n_ctx) % tm == 0
        blk_off = n_ctx // tm
        nt = (t - n_ctx) // tm
    else:
        tm = _pick_tile(t, 544)
        blk_off = 0
        nt = t // tm
    return pl.pallas_call(
        functools.partial(_na_out_kernel, tm=tm, n_ctx=n_ctx, blk_off=blk_off),
        grid=(b, nt),
        in_specs=[
            pl.BlockSpec((1, tm, d), lambda i, j: (i, j + blk_off, 0)),
            pl.BlockSpec((1, tm, d), lambda i, j: (i, j, 0)),
            pl.BlockSpec((1, 6, d), lambda i, j: (i, 0, 0)),
            _const_spec((6, d)),
            _const_spec((d, d)),
        ],
        out_specs=pl.BlockSpec((1, tm, d), lambda i, j: (i, j, 0)),
        out_shape=jax.ShapeDtypeStruct((b, nt * tm, d), F32),
        compiler_params=_cparams("parallel", "parallel"),
        name="na_out",
    )(u, o, mx, mz, w_o)


MLP_FC = 1024


def _mlp_apply(u, mx, mz, g, w1_ref, w2_ref, row0, n_ctx):
    h = _norm_mod(u, g, mx, mz, row0, n_ctx, 3, 4).astype(BF16)
    acc = None
    for c in range(D_FF // MLP_FC):
        a = _dot(h, w1_ref[:, c * MLP_FC:(c + 1) * MLP_FC])
        a = jnp.square(jnp.maximum(a, 0.0)).astype(BF16)
        term = _dot(a, w2_ref[c * MLP_FC:(c + 1) * MLP_FC, :])
        acc = term if acc is None else acc + term
    return u + _mod_rows(mx, mz, 5, row0, u.shape[0], n_ctx) * acc


def _mlp_kernel(u_ref, mx_ref, mz_ref, g_ref, w1_ref, w2_ref, fg_ref, out_ref, *, tm, n_ctx, final):
    row0 = pl.program_id(1) * tm
    y = _mlp_apply(u_ref[0], mx_ref[0], mz_ref[...], g_ref[...], w1_ref, w2_ref, row0, n_ctx)
    if final:
        y = _rms(y, fg_ref[...])
    out_ref[0] = y


def _mixer_mlp_kernel(*refs, kind, tm, n_ctx):
    u_ref, mixer_refs = refs[0], refs[1:-7]
    mx_ref, mz_ref, wo_ref, g_ref, w1_ref, w2_ref, out_ref = refs[-7:]
    if kind == "na":
        parts = [mixer_refs[0][0]]
    elif kind == "gla":
        parts = _gla_gated_parts(*mixer_refs)
    else:
        parts = _rw_gated_parts(*mixer_refs)
    row0 = pl.program_id(1) * tm
    u = _gated_residual(u_ref[0], parts, wo_ref, mx_ref, mz_ref, row0, n_ctx)
    out_ref[0] = _mlp_apply(u, mx_ref[0], mz_ref[...], g_ref[...], w1_ref, w2_ref, row0, n_ctx)


def _mixer_mlp(kind, u, mixer_inputs, mixer_specs, mx, mz, w_o, g, w1, w2, n_ctx, tile_rows=544):
    b, t, d = u.shape
    tm = _pick_tile(t, tile_rows)
    tok = pl.BlockSpec((1, tm, d), lambda i, j: (i, j, 0))
    return pl.pallas_call(
        functools.partial(_mixer_mlp_kernel, kind=kind, tm=tm, n_ctx=n_ctx),
        grid=(b, t // tm),
        in_specs=[tok] + [spec(tm) for spec in mixer_specs] + [
            pl.BlockSpec((1, 6, d), lambda i, j: (i, 0, 0)),
            _const_spec((6, d)),
            _const_spec((d, d)),
            _const_spec((1, d)),
            _const_spec((d, D_FF)),
            _const_spec((D_FF, d)),
        ],
        out_specs=tok,
        out_shape=jax.ShapeDtypeStruct((b, t, d), F32),
        compiler_params=_cparams("parallel", "parallel"),
        name=kind + "_out_mlp",
    )(u, *mixer_inputs, mx, mz, w_o, g.reshape(1, d), w1, w2)


def _mlp(u, mx, mz, g, w1, w2, final_g, n_ctx, final):
    b, t, d = u.shape
    tm = _pick_tile(t, 544)
    return pl.pallas_call(
        functools.partial(_mlp_kernel, tm=tm, n_ctx=n_ctx, final=final),
        grid=(b, t // tm),
        in_specs=[
            pl.BlockSpec((1, tm, d), lambda i, j: (i, j, 0)),
            pl.BlockSpec((1, 6, d), lambda i, j: (i, 0, 0)),
            _const_spec((6, d)),
            _const_spec((1, d)),
            _const_spec((d, D_FF)),
            _const_spec((D_FF, d)),
            _const_spec((1, d)),
        ],
        out_specs=pl.BlockSpec((1, tm, d), lambda i, j: (i, j, 0)),
        out_shape=jax.ShapeDtypeStruct((b, t, d), F32),
        compiler_params=_cparams("parallel", "parallel"),
        name="mlp",
    )(u, mx, mz, g.reshape(1, d), w1, w2, final_g.reshape(1, d))


def _gla_proj_kernel(u_ref, mx_ref, mz_ref, g_ref, w_ref, wlr_ref, wdec_ref, bdec_ref,
                     p_ref, gate_ref, *, tm, n_ctx):
    t = pl.program_id(1)
    h = _norm_mod(u_ref[0], g_ref[...], mx_ref[0], mz_ref[...], t * tm, n_ctx, 0, 1).astype(BF16)
    n = D_MODEL
    for s in range(w_ref.shape[1] // n):
        p_ref[0, :, s * n:(s + 1) * n] = _dot(h, w_ref[:, s * n:(s + 1) * n])
    lr = _dot(h, wlr_ref[...]).astype(BF16)
    z = _dot(lr, wdec_ref[...]) + bdec_ref[...]
    gate_ref[0] = -_softplus(-z) * (1.0 / GLA_GATE_NORM)


def _gla_proj(u, mx, mz, g, w_main, w_lr, w_dec, b_dec, n_ctx):
    b, t, d = u.shape
    n = w_main.shape[1]
    tm = _pick_tile(t, 544)
    return pl.pallas_call(
        functools.partial(_gla_proj_kernel, tm=tm, n_ctx=n_ctx),
        grid=(b, t // tm),
        in_specs=[
            pl.BlockSpec((1, tm, d), lambda i, j: (i, j, 0)),
            pl.BlockSpec((1, 6, d), lambda i, j: (i, 0, 0)),
            _const_spec((6, d)),
            _const_spec((1, d)),
            _const_spec((d, n)),
            _const_spec((d, LANES)),
            _const_spec((LANES, 2 * GLA_DK)),
            _const_spec((1, 2 * GLA_DK)),
        ],
        out_specs=[
            pl.BlockSpec((1, tm, n), lambda i, j: (i, j, 0)),
            pl.BlockSpec((1, tm, 2 * GLA_DK), lambda i, j: (i, j, 0)),
        ],
        out_shape=[
            jax.ShapeDtypeStruct((b, t, n), F32),
            jax.ShapeDtypeStruct((b, t, 2 * GLA_DK), F32),
        ],
        compiler_params=_cparams("parallel", "parallel"),
        name="gla_proj",
    )(u, mx, mz, g.reshape(1, d), w_main, w_lr, w_dec, b_dec)


def _tri(n, upper, strict):
    r = lax.broadcasted_iota(jnp.int32, (n, n), 0)
    c = lax.broadcasted_iota(jnp.int32, (n, n), 1)
    if upper:
        return (r < c) if strict else (r <= c)
    return (r > c) if strict else (r >= c)


def _cumsum_rows(x, descending):
    n = x.shape[0]
    row = lax.broadcasted_iota(jnp.int32, (n, 1), 0)
    s = 1
    while s < n:
        if descending:
            x = x + jnp.where(row < n - s, pltpu.roll(x, n - s, 0), 0.0)
        else:
            x = x + jnp.where(row >= s, pltpu.roll(x, s, 0), 0.0)
        s *= 2
    return x


def _gla_scan_kernel(q_ref, k_ref, v_ref, g0_ref, g1_ref, o_ref, s_ref, *, n_ctx_chunks, n_chunks, nb):
    qscale = GLA_HK ** -0.5
    g_refs = (g0_ref, g1_ref)
    incl = (_tri(CHUNK, False, False), _tri(CHUNK, True, False))
    o_ref[...] = jnp.zeros_like(o_ref)

    def step(first):
        units = [(d, first[d] + (i if d == 0 else -i)) for i in range(nb) for d in range(2)]
        rows = [pl.ds(pl.multiple_of(c * CHUNK, CHUNK), CHUNK) for _, c in units]
        q = [q_ref[0, r, :] for r in rows]
        k = [k_ref[0, r, :] for r in rows]
        v = [v_ref[0, r, :].astype(BF16) for r in rows]
        g = [g_refs[d][0, r, :] for (d, _), r in zip(units, rows)]
        bcum = [_cumsum_rows(gi, d == 1) for (d, _), gi in zip(units, g)]
        b_last_row = [jnp.sum(gi, axis=0, keepdims=True) for gi in g]
        dec_col = [jnp.exp(jnp.sum(gi.T, axis=1, keepdims=True)) for gi in g]
        q_e = [(qi * jnp.exp(bi) * qscale).astype(BF16) for qi, bi in zip(q, bcum)]
        k_e = [(ki * jnp.exp(-bi)).astype(BF16) for ki, bi in zip(k, bcum)]
        k_dec_t = [(ki * jnp.exp(bl - bi)).T.astype(BF16) for ki, bl, bi in zip(k, b_last_row, bcum)]
        a = [jnp.where(incl[d], _dot_nt(qe, ke), 0.0).astype(BF16) for (d, _), qe, ke in zip(units, q_e, k_e)]
        o_intra = [_dot(ai, vi) for ai, vi in zip(a, v)]
        s_inc = [_dot(kt, vi) for kt, vi in zip(k_dec_t, v)]
        s = [s_ref[0], s_ref[1]]
        for i, (d, _) in enumerate(units):
            o_ref[0, rows[i], :] += o_intra[i] + _dot(q_e[i], s[d].astype(BF16))
            s[d] = dec_col[i] * s[d] + s_inc[i]
        s_ref[0] = s[0]
        s_ref[1] = s[1]

    s_ref[...] = jnp.zeros_like(s_ref)
    n_lat = n_chunks - n_ctx_chunks

    def ctx_body(i, carry):
        step((i * nb, n_ctx_chunks - 1 - i * nb))
        return carry

    def lat_body(i, carry):
        step((n_ctx_chunks + i * nb, n_chunks - 1 - i * nb))
        return carry

    lax.fori_loop(0, n_ctx_chunks // nb, ctx_body, 0)
    lax.fori_loop(0, n_lat // nb, lat_body, 0)


def _gla_scan(p, gates, n_ctx, nb):
    b, t, _ = p.shape
    nkb = GLA_DK // GLA_HK
    return pl.pallas_call(
        functools.partial(_gla_scan_kernel, n_ctx_chunks=n_ctx // CHUNK, n_chunks=t // CHUNK, nb=nb),
        grid=(b, GLA_HEADS),
        in_specs=[
            pl.BlockSpec((1, t, GLA_HK), lambda i, h: (i, 0, h)),
            pl.BlockSpec((1, t, GLA_HK), lambda i, h: (i, 0, nkb + h)),
            pl.BlockSpec((1, t, GLA_HV), lambda i, h: (i, 0, (2 * GLA_DK) // GLA_HV + h)),
            pl.BlockSpec((1, t, GLA_HK), lambda i, h: (i, 0, h)),
            pl.BlockSpec((1, t, GLA_HK), lambda i, h: (i, 0, nkb + h)),
        ],
        out_specs=pl.BlockSpec((1, t, GLA_HV), lambda i, h: (i, 0, h)),
        out_shape=jax.ShapeDtypeStruct((b, t, GLA_DV), F32),
        scratch_shapes=[pltpu.VMEM((2, GLA_HK, GLA_HV), F32)],
        compiler_params=_cparams("parallel", "parallel"),
        name="gla_scan",
    )(p, p, p, gates, gates)


def _gla_gated_parts(o_ref, gt_ref, ng_ref):
    parts = []
    for h in range(GLA_HEADS):
        ls = slice(h * GLA_HV, (h + 1) * GLA_HV)
        ov = o_ref[0, :, ls]
        ov = ov * lax.rsqrt(jnp.mean(ov * ov, axis=-1, keepdims=True) + NORM_EPS) * ng_ref[...]
        gt = gt_ref[0, :, ls]
        parts.append((ov * (gt * _sigmoid(gt))).astype(BF16))
    return parts


def _gla_out_kernel(u_ref, o_ref, gt_ref, ng_ref, mx_ref, mz_ref, w_ref, out_ref, *, tm, n_ctx):
    row0 = pl.program_id(1) * tm
    parts = _gla_gated_parts(o_ref, gt_ref, ng_ref)
    out_ref[0] = _gated_residual(u_ref[0], parts, w_ref, mx_ref, mz_ref, row0, n_ctx)


def _gla_out(u, o, p, norm_g, mx, mz, w_o, n_ctx):
    b, t, d = u.shape
    tm = _pick_tile(t, 544)
    gt_blk = (2 * GLA_DK + GLA_DV) // d
    return pl.pallas_call(
        functools.partial(_gla_out_kernel, tm=tm, n_ctx=n_ctx),
        grid=(b, t // tm),
        in_specs=[
            pl.BlockSpec((1, tm, d), lambda i, j: (i, j, 0)),
            pl.BlockSpec((1, tm, d), lambda i, j: (i, j, 0)),
            pl.BlockSpec((1, tm, d), lambda i, j: (i, j, gt_blk)),
            _const_spec((1, GLA_HV)),
            pl.BlockSpec((1, 6, d), lambda i, j: (i, 0, 0)),
            _const_spec((6, d)),
            _const_spec((d, d)),
        ],
        out_specs=pl.BlockSpec((1, tm, d), lambda i, j: (i, j, 0)),
        out_shape=jax.ShapeDtypeStruct((b, t, d), F32),
        compiler_params=_cparams("parallel", "parallel"),
        name="gla_out",
    )(u, o, p, norm_g.reshape(1, GLA_HV), mx, mz, w_o)


def _rw_proj_kernel(u_ref, up_ref, un_ref, mx_ref, mz_ref, g_ref, mix_ref, wrkv_ref, w1_ref, w2_ref,
                    w0_ref, a1_ref, a2_ref, a0_ref, g1_ref, g2_ref, ka_ref, rk_ref,
                    r_ref, k_ref, v_ref, gate_ref, lw_ref, a_ref, bonus_ref, *, tm, n_ctx, t_total):
    row0 = pl.program_id(1) * tm
    g = g_ref[...]
    mx = mx_ref[0]
    mz = mz_ref[...]
    h = _norm_mod(u_ref[0], g, mx, mz, row0, n_ctx, 0, 1)
    h_prev = _norm_mod(up_ref[0], g, mx, mz, row0 - 8, n_ctx, 0, 1)[7:8]
    h_next = _norm_mod(un_ref[0], g, mx, mz, row0 + tm, n_ctx, 0, 1)[0:1]
    idx = lax.broadcasted_iota(jnp.int32, (tm, 1), 0)
    rows = row0 + idx
    h_dn = jnp.where(idx == 0, h_prev, pltpu.roll(h, 1, 0))
    h_dn = jnp.where((rows == 0) | (rows == n_ctx), 0.0, h_dn)
    h_up = jnp.where(idx == tm - 1, h_next, pltpu.roll(h, tm - 1, 0))
    h_up = jnp.where((rows == n_ctx - 1) | (rows == t_total - 1), 0.0, h_up)
    xx = 0.5 * (h_dn + h_up) - h
    xr, xw, xk, xv, xa, xg = ((h + xx * mix_ref[m:m + 1]).astype(BF16) for m in range(6))
    r = _dot(xr, wrkv_ref[0])
    k = _dot(xk, wrkv_ref[1])
    v = _dot(xv, wrkv_ref[2])
    r_ref[0] = r
    k_ref[0] = k
    v_ref[0] = v
    gate_ref[0] = _dot(_sigmoid(_dot(xg, g1_ref[...])).astype(BF16), g2_ref[...])
    tw = jnp.tanh(_dot(xw, w1_ref[...])).astype(BF16)
    lw_ref[0] = -math.exp(-0.5) * _sigmoid(w0_ref[...] + _dot(tw, w2_ref[...]))
    ta = _dot(xa, a1_ref[...]).astype(BF16)
    a = _sigmoid(a0_ref[...] + _dot(ta, a2_ref[...]))
    a_ref[0] = a
    lane_lo = lax.broadcasted_iota(jnp.int32, (1, LANES), 1) < RW_HEAD
    d = D_MODEL
    for p in range(d // LANES):
        ls = slice(p * LANES, (p + 1) * LANES)
        kd_sum = k[:, ls] * (2.0 + (a[:, ls] + a[:, d + p * LANES:d + (p + 1) * LANES] - 2.0) * ka_ref[:, ls])
        bonus_ref[0, :, ls] = _pair_sum(r[:, ls] * kd_sum * rk_ref[:, ls], lane_lo) * v[:, ls]


def _rw_proj(u, mx, mz, g, wts, n_ctx):
    b, t, d = u.shape
    tm = _pick_tile(t, 256)
    nb8 = t // 8
    tb = tm // 8
    full = lambda n: jax.ShapeDtypeStruct((b, t, n), F32)
    row_spec = lambda n: pl.BlockSpec((1, tm, n), lambda i, j: (i, j, 0))
    return pl.pallas_call(
        functools.partial(_rw_proj_kernel, tm=tm, n_ctx=n_ctx, t_total=t),
        grid=(b, t // tm),
        in_specs=[
            row_spec(d),
            pl.BlockSpec((1, 8, d), lambda i, j: (i, jnp.maximum(j * tb - 1, 0), 0)),
            pl.BlockSpec((1, 8, d), lambda i, j: (i, jnp.minimum((j + 1) * tb, nb8 - 1), 0)),
            pl.BlockSpec((1, 6, d), lambda i, j: (i, 0, 0)),
            _const_spec((6, d)),
            _const_spec((1, d)),
            _const_spec((6, d)),
            _const_spec((3, d, d)),
            _const_spec((d, 2 * RW_LORA)),
            _const_spec((2 * RW_LORA, 2 * d)),
            _const_spec((1, 2 * d)),
            _const_spec((d, 2 * RW_LORA)),
            _const_spec((2 * RW_LORA, 2 * d)),
            _const_spec((1, 2 * d)),
            _const_spec((d, 2 * LANES)),
            _const_spec((2 * LANES, d)),
            _const_spec((1, d)),
            _const_spec((1, d)),
        ],
        out_specs=[row_spec(d), row_spec(d), row_spec(d), row_spec(d), row_spec(2 * d), row_spec(2 * d),
                   row_spec(d)],
        out_shape=[full(d), full(d), full(d), full(d), full(2 * d), full(2 * d), full(d)],
        compiler_params=_cparams("parallel", "parallel"),
        name="rw_proj",
    )(u, u, u, mx, mz, g.reshape(1, d), wts["mix"], wts["w_rkv"], wts["w1"], wts["w2"], wts["w0"],
      wts["a1"], wts["a2"], wts["a0"], wts["g1"], wts["g2"], wts["k_a"], wts["r_k"])


def _rw_pair_terms(units, side_work=()):
    C = CHUNK
    lane_lo = lax.broadcasted_iota(jnp.int32, (1, LANES), 1) < RW_HEAD
    row2 = lax.broadcasted_iota(jnp.int32, (C, LANES), 0)
    col2 = lax.broadcasted_iota(jnp.int32, (C, LANES), 1) & (C - 1)
    strict2 = (row2 > col2, row2 < col2)
    incl2 = (row2 >= col2, row2 <= col2)
    eye2 = jnp.where(row2 == col2, 1.0, 0.0)
    nu = len(units)

    def bd(x):
        zero = jnp.zeros_like(x)
        return jnp.concatenate([jnp.where(lane_lo, x, zero), jnp.where(lane_lo, zero, x)], axis=0)

    def hilo(x):
        hi = x.astype(BF16)
        return hi, (x - hi.astype(F32)).astype(BF16)

    side = iter(side_work)

    def stage_done():
        thunk = next(side, None)
        if thunk is not None:
            thunk()

    pre = []
    for r2, k2, v2, lw2, a2, desc, kkw, kaw in units:
        d = int(desc)
        kk = k2 * kkw
        kk = kk * lax.rsqrt(jnp.maximum(_pair_sum(kk * kk, lane_lo), 1e-24))
        kd = k2 * (1.0 + (a2 - 1.0) * kaw)
        bvec = kk * a2
        cum = _cumsum_rows(lw2, desc)
        tot = jnp.sum(lw2, axis=0, keepdims=True)
        e_neg = jnp.exp(-cum)
        e_rem = jnp.exp(tot - cum)
        pre.append(dict(
            d=d, tot=tot, v_b=v2.astype(BF16),
            a_t=-kk * jnp.exp(cum - lw2), r_t=r2 * jnp.exp(cum),
            b_h=(bvec * e_neg).astype(BF16), k_h=(kd * e_neg).astype(BF16),
            b_rem_t=(bvec * e_rem).T.astype(BF16), k_rem_t=(kd * e_rem).T.astype(BF16)))

    ar = [jnp.concatenate([p["a_t"], p["r_t"]], axis=0).astype(BF16) for p in pre]
    p_bk = [_dot_nt(ar[u], jnp.concatenate([bd(pre[u]["b_h"]), bd(pre[u]["k_h"])], axis=0)) for u in range(nu)]
    l_ab = [jnp.where(strict2[pre[u]["d"]], p_bk[u][:C, :LANES], 0.0) for u in range(nu)]
    m_rb = [jnp.where(incl2[pre[u]["d"]], p_bk[u][C:, :LANES], 0.0).astype(BF16) for u in range(nu)]
    l_ak = [jnp.where(strict2[pre[u]["d"]], p_bk[u][:C, LANES:], 0.0).astype(BF16) for u in range(nu)]
    m_rk = [jnp.where(incl2[pre[u]["d"]], p_bk[u][C:, LANES:], 0.0).astype(BF16) for u in range(nu)]
    stage_done()

    xo = l_ab
    l_b16 = [l.astype(BF16) for l in l_ab]
    pw = [_dot(l, bd(l)) for l in l_b16]
    for _ in range(3):
        p_b16 = [p.astype(BF16) for p in pw]
        ra = [_dot(jnp.concatenate([p, x.astype(BF16)], axis=0), bd(p)) for p, x in zip(p_b16, xo)]
        xo = [x + p + a[C:] for x, p, a in zip(xo, pw, ra)]
        pw = [a[:C] for a in ra]
        stage_done()
    xo = [x + p + _dot(x.astype(BF16), bd(p.astype(BF16))) for x, p in zip(xo, pw)]
    l_hl = [hilo(l) for l in l_ab]
    x_hl = [hilo(x) for x in xo]
    lx = [_dot(jnp.concatenate([lh, ll], axis=0), bd(xh)) for (lh, ll), (xh, _) in zip(l_hl, x_hl)]
    lxb = [_dot(lh, bd(xlo)) for (lh, _), (_, xlo) in zip(l_hl, x_hl)]
    res = [(l - x) + (a[:C] + a[C:] + b) for l, x, a, b in zip(l_ab, xo, lx, lxb)]
    stage_done()
    xo = [x + r + _dot(xh, bd(r.astype(BF16))) for x, r, (xh, _) in zip(xo, res, x_hl)]
    t_inv = [(eye2 + x).astype(BF16) for x in xo]

    lvyk = [_dot(jnp.concatenate([l_ak[u], m_rk[u]], axis=0), bd(pre[u]["v_b"])) for u in range(nu)]
    kv = [_dot(pre[u]["k_rem_t"], pre[u]["v_b"]) for u in range(nu)]
    tw = [_dot(t_inv[u], jnp.concatenate([bd(ar[u][:C]), bd(lvyk[u][:C].astype(BF16))], axis=1))
          for u in range(nu)]
    qy = [_dot(m_rb[u], jnp.concatenate([bd(tw[u][:, :LANES].astype(BF16)),
                                         bd(tw[u][:, LANES:].astype(BF16))], axis=1)) for u in range(nu)]
    gh = [_dot(pre[u]["b_rem_t"], tw[u].astype(BF16)) for u in range(nu)]

    r = lax.broadcasted_iota(jnp.int32, (LANES, LANES), 0)
    c = lax.broadcasted_iota(jnp.int32, (LANES, LANES), 1)
    same_head = (r < RW_HEAD) == (c < RW_HEAD)
    out = []
    for u in range(nu):
        qp = pre[u]["r_t"] + qy[u][:, :LANES]
        y0 = qy[u][:, LANES:] + lvyk[u][C:]
        g_mat = jnp.where(same_head, gh[u][:, :LANES], 0.0) + jnp.where(r == c, jnp.exp(pre[u]["tot"]), 0.0)
        h_mat = jnp.where(same_head, gh[u][:, LANES:] + kv[u], 0.0)
        out.append((qp, y0, g_mat, h_mat))
    return out


def _rw_mix_kernel(r0_ref, k0_ref, v0_ref, lw0_ref, a0_ref, r1_ref, k1_ref, v1_ref, lw1_ref, a1_ref,
                   kk_ref, ka_ref, y0_ref, y1_ref, st_ref, qp_s, yc_s, g_s, h_s, *, nb, pp):
    def zero_scratch(_, carry):
        for ref in (st_ref, qp_s, yc_s, g_s, h_s):
            ref[...] = jnp.zeros_like(ref)
        return carry

    lax.fori_loop(0, jnp.where(pl.program_id(2) == 0, 1, 0), zero_scratch, 0)

    y_refs = (y0_ref, y1_ref)
    chains = [(d, p) for d in range(2) for p in range(pp)]
    states = {dp: st_ref[dp[0], dp[1]] for dp in chains}
    prev_terms = {(d, p, c): (qp_s[d, p, c], yc_s[d, p, c], g_s[d, p, c], h_s[d, p, c])
                  for d, p in chains for c in range(nb)}

    def link(ci):
        def emit():
            for d, p in chains:
                c = ci if d == 0 else nb - 1 - ci
                qp, yc, g_mat, h_mat = prev_terms[d, p, c]
                st_b = states[d, p].astype(BF16)
                y_refs[d][0, c * CHUNK:(c + 1) * CHUNK, p * LANES:(p + 1) * LANES] = _dot(qp, st_b) + yc
                states[d, p] = _dot(g_mat, st_b) + h_mat
        return emit

    ins = ((r0_ref, k0_ref, v0_ref, lw0_ref, a0_ref), (r1_ref, k1_ref, v1_ref, lw1_ref, a1_ref))
    units = []
    index = []
    for p in range(pp):
        ls = slice(p * LANES, (p + 1) * LANES)
        for c in range(nb):
            rows = slice(c * CHUNK, (c + 1) * CHUNK)
            for d in range(2):
                units.append(tuple(ref[0, rows, ls] for ref in ins[d]) + (d == 1, kk_ref[:, ls], ka_ref[:, ls]))
                index.append((d, p, c))
    links = [link(ci) for ci in range(nb)]
    assert nb <= 5, "one recurrence link per stage boundary of _rw_pair_terms"
    results = _rw_pair_terms(units, side_work=links)
    for d, p in chains:
        st_ref[d, p] = states[d, p]
    for (d, p, c), (qp, y0, g_mat, h_mat) in zip(index, results):
        qp_s[d, p, c] = qp.astype(BF16)
        yc_s[d, p, c] = y0
        g_s[d, p, c] = g_mat.astype(BF16)
        h_s[d, p, c] = h_mat


def _rw_mix(r, k, v, lw, a, k_k, k_a, n_ctx, nb):
    b, t, d = r.shape
    npairs = d // LANES
    rows = nb * CHUNK
    nblk = t // rows
    ncb = n_ctx // rows

    def rev(j):
        return jnp.where(j < ncb, ncb - 1 - j, nblk - 1 - (j - ncb))

    cur = lambda j: jnp.minimum(j, nblk - 1)
    prev = lambda j: jnp.maximum(j - 1, 0)
    pp = 2
    pw = pp * LANES
    ng = npairs // pp
    fwd = lambda off, blk: pl.BlockSpec((1, rows, pw), lambda i, p, j: (i, blk(j), off + p))
    bwd = lambda off, blk: pl.BlockSpec((1, rows, pw), lambda i, p, j: (i, rev(blk(j)), off + p))
    par = pl.BlockSpec((1, pw), lambda i, p, j: (0, p))
    return pl.pallas_call(
        functools.partial(_rw_mix_kernel, nb=nb, pp=pp),
        grid=(b, ng, nblk + 1),
        in_specs=[fwd(0, cur), fwd(0, cur), fwd(0, cur), fwd(0, cur), fwd(0, cur),
                  bwd(0, cur), bwd(0, cur), bwd(0, cur), bwd(ng, cur), bwd(ng, cur), par, par],
        out_specs=[fwd(0, prev), bwd(0, prev)],
        out_shape=[jax.ShapeDtypeStruct((b, t, d), F32), jax.ShapeDtypeStruct((b, t, d), F32)],
        scratch_shapes=[pltpu.VMEM((2, pp, LANES, LANES), F32),
                        pltpu.VMEM((2, pp, nb, CHUNK, LANES), BF16), pltpu.VMEM((2, pp, nb, CHUNK, LANES), F32),
                        pltpu.VMEM((2, pp, nb, LANES, LANES), BF16), pltpu.VMEM((2, pp, nb, LANES, LANES), F32)],
        compiler_params=_cparams("parallel", "parallel", "arbitrary"),
        name="rw_mix",
    )(r, k, v, lw, a, r, k, v, lw, a, k_k.reshape(1, d), k_a.reshape(1, d))


def _rw_gated_parts(y0_ref, y1_ref, bonus_ref, gate_ref, lng_ref, lnb_ref):
    lane_lo = lax.broadcasted_iota(jnp.int32, (1, LANES), 1) < RW_HEAD
    parts = []
    for p in range(D_MODEL // LANES):
        ls = slice(p * LANES, (p + 1) * LANES)
        y = y0_ref[0, :, ls] + y1_ref[0, :, ls]
        mu = _pair_sum(y, lane_lo) * (1.0 / RW_HEAD)
        yc = y - mu
        var = _pair_sum(yc * yc, lane_lo) * (1.0 / RW_HEAD)
        yn = yc * lax.rsqrt(var + RW_GN_EPS) * lng_ref[:, ls] + lnb_ref[:, ls]
        parts.append(((yn + bonus_ref[0, :, ls]) * gate_ref[0, :, ls]).astype(BF16))
    return parts


def _rw_out_kernel(u_ref, y0_ref, y1_ref, bonus_ref, gate_ref, lng_ref, lnb_ref, mx_ref, mz_ref, w_ref,
                   out_ref, *, tm, n_ctx):
    row0 = pl.program_id(1) * tm
    parts = _rw_gated_parts(y0_ref, y1_ref, bonus_ref, gate_ref, lng_ref, lnb_ref)
    out_ref[0] = _gated_residual(u_ref[0], parts, w_ref, mx_ref, mz_ref, row0, n_ctx)


def _rw_out(u, y0, y1, bonus, gate, ln_g, ln_b, mx, mz, w_o, n_ctx):
    b, t, d = u.shape
    tm = _pick_tile(t, 544)
    tok = pl.BlockSpec((1, tm, d), lambda i, j: (i, j, 0))
    vec = _const_spec((1, d))
    return pl.pallas_call(
        functools.partial(_rw_out_kernel, tm=tm, n_ctx=n_ctx),
        grid=(b, t // tm),
        in_specs=[tok] * 5 + [vec, vec, pl.BlockSpec((1, 6, d), lambda i, j: (i, 0, 0)),
                              _const_spec((6, d)), _const_spec((d, d))],
        out_specs=tok,
        out_shape=jax.ShapeDtypeStruct((b, t, d), F32),
        compiler_params=_cparams("parallel", "parallel"),
        name="rw_out",
    )(u, y0, y1, bonus, gate, ln_g.reshape(1, d), ln_b.reshape(1, d), mx, mz, w_o)


def _block_rows(w_pair):
    _, k, n = w_pair.shape
    z = jnp.zeros((k, n), w_pair.dtype)
    return jnp.concatenate([jnp.concatenate([w_pair[0], z], axis=1),
                            jnp.concatenate([z, w_pair[1]], axis=1)], axis=0)


def kernel(x, c, ctx, c_ctx, ada_w, ada_b, norm1_g, norm2_g, mlp_w1, mlp_w2, final_g, na_w_qkv, na_w_o, na_rpb, gla_w_in, gla_w_dec2, gla_b_dec, gla_norm_g, gla_w_o, rw_mix, rw_w_rkv, rw_w0, rw_w1, rw_w2, rw_a0, rw_a1, rw_a2, rw_g1, rw_g2, rw_k_k, rw_k_a, rw_r_k, rw_ln_g, rw_ln_b, rw_w_o):
    b, seq, d = x.shape
    n_ctx = ctx.shape[1]
    depth = ada_w.shape[0]
    assert d == D_MODEL and n_ctx % CHUNK == 0 and seq % CHUNK == 0

    entry_fused = seq % n_ctx == 0 and n_ctx % 16 == 0
    u = None if entry_fused else jnp.concatenate([ctx.astype(x.dtype), x], axis=1)

    rows = -(-(b + 1) // 8) * 8
    c_all = jnp.zeros((rows, d), F32).at[:b].set(c).at[b].set(c_ctx)
    tables = _ada_tables(c_all, ada_w, ada_b)
    mods_x = tables[:, :b].reshape(depth, b, 6, d)
    mods_z = tables[:, b].reshape(depth, 6, d)

    nb = max(n for n in (4, 2, 1) if (n_ctx // CHUNK) % n == 0 and (seq // CHUNK) % n == 0)

    for i in range(depth):
        kind, j = i % 3, i // 3
        last = i == depth - 1
        mx, mz = mods_x[i], mods_z[i]
        w1, w2 = mlp_w1[i].astype(BF16), mlp_w2[i].astype(BF16)
        tok = lambda tm: pl.BlockSpec((1, tm, d), lambda bi, ti: (bi, ti, 0))
        mlp_done = False
        if kind == 0:
            if u is None:
                qkv, u = _entry_proj(x, ctx, mx, mz, norm1_g[i], na_w_qkv[j].astype(BF16), BF16)
            else:
                qkv = _norm_proj(u, mx, mz, norm1_g[i], na_w_qkv[j].astype(BF16), n_ctx, BF16)
            o = _na_attention(qkv, _na_bias_table(na_rpb[j]), n_ctx, with_ctx=not last)
            if last:
                u = _na_out(u, o, mx, mz, na_w_o[j].astype(BF16), n_ctx, latent_only=True)
            else:
                u = _mixer_mlp("na", u, [o], [tok], mx, mz, na_w_o[j].astype(BF16), norm2_g[i], w1, w2, n_ctx)
                mlp_done = True
        elif kind == 1:
            n_main = 2 * GLA_DK + 2 * GLA_DV
            w_in = gla_w_in[j]
            w_lr = jnp.zeros((d, LANES), F32).at[:, :2 * GLA_LOW_RANK].set(w_in[:, n_main:])
            w_dec = jnp.zeros((LANES, 2 * GLA_DK), F32).at[:2 * GLA_LOW_RANK].set(_block_rows(gla_w_dec2[j]))
            p, gates = _gla_proj(u, mx, mz, norm1_g[i], w_in[:, :n_main].astype(BF16), w_lr.astype(BF16),
                                 w_dec.astype(BF16), gla_b_dec[j].reshape(1, 2 * GLA_DK), n_ctx)
            o = _gla_scan(p, gates, n_ctx, nb)
            if last:
                u = _gla_out(u, o, p, gla_norm_g[j], mx, mz, gla_w_o[j].astype(BF16), n_ctx)
            else:
                gt_blk = (2 * GLA_DK + GLA_DV) // d
                gt_spec = lambda tm: pl.BlockSpec((1, tm, d), lambda bi, ti: (bi, ti, gt_blk))
                u = _mixer_mlp("gla", u, [o, p, gla_norm_g[j].reshape(1, GLA_HV)],
                               [tok, gt_spec, lambda tm: _const_spec((1, GLA_HV))],
                               mx, mz, gla_w_o[j].astype(BF16), norm2_g[i], w1, w2, n_ctx)
                mlp_done = True
        else:
            g1 = jnp.zeros((d, 2 * LANES), F32).at[:, :RW_GATE_LORA].set(rw_g1[j])
            g2 = jnp.zeros((2 * LANES, d), F32).at[:RW_GATE_LORA].set(rw_g2[j])
            wts = dict(
                mix=rw_mix[j], w_rkv=rw_w_rkv[j].astype(BF16),
                w1=jnp.concatenate([rw_w1[j, 0], rw_w1[j, 1]], axis=1).astype(BF16),
                w2=_block_rows(rw_w2[j]).astype(BF16), w0=rw_w0[j].reshape(1, 2 * d),
                a1=jnp.concatenate([rw_a1[j, 0], rw_a1[j, 1]], axis=1).astype(BF16),
                a2=_block_rows(rw_a2[j]).astype(BF16), a0=rw_a0[j].reshape(1, 2 * d),
                g1=g1.astype(BF16), g2=g2.astype(BF16),
                k_a=rw_k_a[j].reshape(1, d), r_k=rw_r_k[j].reshape(1, d))
            r, k, v, gate, lw, a, bonus = _rw_proj(u, mx, mz, norm1_g[i], wts, n_ctx)
            y0, y1 = _rw_mix(r, k, v, lw, a, rw_k_k[j], rw_k_a[j], n_ctx, nb)
            if last:
                u = _rw_out(u, y0, y1, bonus, gate, rw_ln_g[j], rw_ln_b[j], mx, mz, rw_w_o[j].astype(BF16), n_ctx)
            else:
                vec = lambda tm: _const_spec((1, d))
                u = _mixer_mlp("rw", u, [y0, y1, bonus, gate, rw_ln_g[j].reshape(1, d), rw_ln_b[j].reshape(1, d)],
                               [tok, tok, tok, tok, vec, vec], mx, mz, rw_w_o[j].astype(BF16), norm2_g[i],
                               w1, w2, n_ctx, tile_rows=544)
                mlp_done = True
        if last and u.shape[1] != seq:
            u = u[:, n_ctx:]
        if not mlp_done:
            u = _mlp(u, mx, mz, norm2_g[i], w1, w2, final_g, 0 if last else n_ctx, final=last)
    return u
```

```python
import functools
import math

import jax
import jax.numpy as jnp
from jax import lax
from jax.experimental import pallas as pl
from jax.experimental.pallas import tpu as pltpu

F32 = jnp.float32
BF16 = jnp.bfloat16

D_MODEL = 1024
D_FF = 4 * D_MODEL
NORM_EPS = 1e-6
GRID_W = 64

NA_HEAD_DIM = 64
NA_HEADS = D_MODEL // NA_HEAD_DIM
NA_WIN_R = 8
NA_WIN_C = 16
NA_MASK = -1e30

GLA_HEADS = 4
GLA_DK = D_MODEL // 2
GLA_DV = D_MODEL
GLA_HK = GLA_DK // GLA_HEADS
GLA_HV = GLA_DV // GLA_HEADS
GLA_LOW_RANK = 16
GLA_GATE_NORM = 16.0
CHUNK = 64

RW_HEAD = 64
RW_GATE_LORA = 160
RW_LORA = 64
RW_GN_EPS = 64e-5

LANES = 128
VMEM_LIMIT = 56 * 1024 * 1024


def _cparams(*sem):
    return pltpu.CompilerParams(dimension_semantics=sem, vmem_limit_bytes=VMEM_LIMIT)


def _pick_tile(n, target, mult=16):
    best = None
    for t in range(mult, min(n, target) + 1, mult):
        if n % t == 0:
            best = t
    assert best is not None, (n, target)
    return best


def _const_spec(shape):
    nd = len(shape)
    return pl.BlockSpec(shape, lambda *_: (0,) * nd, pipeline_mode=pl.Buffered(1))


def _dot(a, b):
    return jnp.dot(a, b, preferred_element_type=F32)


def _dot_nt(a, b):
    return lax.dot_general(a, b, (((1,), (1,)), ((), ())), preferred_element_type=F32)


def _sigmoid(x):
    return 0.5 + 0.5 * jnp.tanh(0.5 * x)


def _softplus(x):
    return jnp.maximum(x, 0.0) + jnp.log(1.0 + jnp.exp(-jnp.abs(x)))


def _rms(u, g):
    return u * lax.rsqrt(jnp.mean(u * u, axis=-1, keepdims=True) + NORM_EPS) * g


def _mod_rows(mx, mz, idx, row0, rows, n_ctx):
    vx = mx[idx:idx + 1]
    if n_ctx == 0:
        return vx
    if n_ctx % rows == 0:
        return jnp.where(row0 < n_ctx, mz[idx:idx + 1], vx)
    r = row0 + lax.broadcasted_iota(jnp.int32, (rows, 1), 0)
    return jnp.where(r < n_ctx, mz[idx:idx + 1], vx)


def _norm_mod(u, g, mx, mz, row0, n_ctx, i_shift, i_scale):
    rows = u.shape[0]
    shift = _mod_rows(mx, mz, i_shift, row0, rows, n_ctx)
    scale = _mod_rows(mx, mz, i_scale, row0, rows, n_ctx)
    return _rms(u, g) * (1.0 + scale) + shift


def _pair_sum(x, lane_lo):
    s0 = jnp.sum(jnp.where(lane_lo, x, 0.0), axis=-1, keepdims=True)
    s1 = jnp.sum(jnp.where(lane_lo, 0.0, x), axis=-1, keepdims=True)
    return jnp.where(lane_lo, s0, s1)


def _ada_kernel(c_ref, w_ref, b_ref, o_ref):
    c = c_ref[...]
    sc = (c * _sigmoid(c)).astype(BF16)
    o_ref[0] = _dot(sc, w_ref[0].astype(BF16)) + b_ref[0]


def _ada_tables(c_all, ada_w, ada_b):
    depth, d, n = ada_w.shape
    rows = c_all.shape[0]
    tn = 1536
    return pl.pallas_call(
        _ada_kernel,
        grid=(depth, n // tn),
        in_specs=[
            pl.BlockSpec((rows, d), lambda l, j: (0, 0)),
            pl.BlockSpec((1, d, tn), lambda l, j: (l, 0, j)),
            pl.BlockSpec((1, 1, tn), lambda l, j: (l, 0, j)),
        ],
        out_specs=pl.BlockSpec((1, rows, tn), lambda l, j: (l, 0, j)),
        out_shape=jax.ShapeDtypeStruct((depth, rows, n), F32),
        compiler_params=_cparams("parallel", "parallel"),
        name="ada_tables",
    )(c_all, ada_w, ada_b.reshape(depth, 1, n))


def _proj_kernel(u_ref, mx_ref, mz_ref, g_ref, w_ref, o_ref, *, tm, n_ctx, nsplit):
    t = pl.program_id(1)
    h = _norm_mod(u_ref[0], g_ref[...], mx_ref[0], mz_ref[...], t * tm, n_ctx, 0, 1).astype(BF16)
    n = w_ref.shape[1] // nsplit
    for s in range(nsplit):
        o_ref[0, :, s * n:(s + 1) * n] = _dot(h, w_ref[:, s * n:(s + 1) * n]).astype(o_ref.dtype)


def _norm_proj(u, mx, mz, g, w, n_ctx, out_dtype):
    b, t, d = u.shape
    n = w.shape[1]
    tm = _pick_tile(t, 544)
    return pl.pallas_call(
        functools.partial(_proj_kernel, tm=tm, n_ctx=n_ctx, nsplit=n // D_MODEL),
        grid=(b, t // tm),
        in_specs=[
            pl.BlockSpec((1, tm, d), lambda i, j: (i, j, 0)),
            pl.BlockSpec((1, 6, d), lambda i, j: (i, 0, 0)),
            _const_spec((6, d)),
            _const_spec((1, d)),
            _const_spec((d, n)),
        ],
        out_specs=pl.BlockSpec((1, tm, n), lambda i, j: (i, j, 0)),
        out_shape=jax.ShapeDtypeStruct((b, t, n), out_dtype),
        compiler_params=_cparams("parallel", "parallel"),
        name="norm_proj",
    )(u, mx, mz, g.reshape(1, d), w)


def _entry_proj_kernel(ctx_ref, x_ref, mx_ref, mz_ref, g_ref, w_ref, o_ref, u_ref, *, tm, nsplit):
    j = pl.program_id(1)
    u = jnp.where(j == 0, ctx_ref[0].astype(F32), x_ref[0])
    u_ref[0] = u
    h = _norm_mod(u, g_ref[...], mx_ref[0], mz_ref[...], j * tm, tm, 0, 1).astype(BF16)
    n = w_ref.shape[1] // nsplit
    for s in range(nsplit):
        o_ref[0, :, s * n:(s + 1) * n] = _dot(h, w_ref[:, s * n:(s + 1) * n]).astype(o_ref.dtype)


def _entry_proj(x, ctx, mx, mz, g, w, out_dtype):
    b, seq, d = x.shape
    tm = ctx.shape[1]
    assert seq % tm == 0 and tm % 16 == 0
    n = w.shape[1]
    t = tm + seq
    return pl.pallas_call(
        functools.partial(_entry_proj_kernel, tm=tm, nsplit=n // D_MODEL),
        grid=(b, t // tm),
        in_specs=[
            pl.BlockSpec((1, tm, d), lambda i, j: (i, 0, 0)),
            pl.BlockSpec((1, tm, d), lambda i, j: (i, jnp.maximum(j - 1, 0), 0)),
            pl.BlockSpec((1, 6, d), lambda i, j: (i, 0, 0)),
            _const_spec((6, d)),
            _const_spec((1, d)),
            _const_spec((d, n)),
        ],
        out_specs=[pl.BlockSpec((1, tm, n), lambda i, j: (i, j, 0)),
                   pl.BlockSpec((1, tm, d), lambda i, j: (i, j, 0))],
        out_shape=[jax.ShapeDtypeStruct((b, t, n), out_dtype), jax.ShapeDtypeStruct((b, t, d), F32)],
        compiler_params=_cparams("parallel", "parallel"),
        name="entry_proj",
    )(ctx, x, mx, mz, g.reshape(1, d), w)


def _na_bias_table(rpb):
    w = GRID_W
    cols = jnp.arange(w)
    c_start = jnp.clip(cols - NA_WIN_C // 2, 0, w - NA_WIN_C)
    col_ok = (cols[None, :] >= c_start[:, None]) & (cols[None, :] < c_start[:, None] + NA_WIN_C)
    rpb = rpb.astype(F32)
    edge = w - NA_WIN_C
    t = jnp.concatenate([jnp.repeat(rpb[..., :1], edge, axis=-1), rpb,
                         jnp.repeat(rpb[..., -1:], edge + 1, axis=-1)], axis=-1)
    m = jnp.tile(t, w)[..., :w * (2 * w - 1)].reshape(t.shape[:-1] + (w, 2 * w - 1))[..., w - 1:]
    m = jnp.where(col_ok[None, None], m, NA_MASK)
    return jnp.concatenate([m[:, :-1], m[:, 1:]], axis=-1)


def _na_kernel(q_ref, k_ref, v_ref, bias_ref, o_ref, *, n_ctx, n_rows, with_ctx, rq):
    nctxb = n_ctx // GRID_W
    out_off = 0 if with_ctx else n_ctx
    scale = NA_HEAD_DIM ** -0.5
    assert scale == 0.125, "the score scale is folded into bf16 q, exact only for a power of two"
    lane_lo = lax.broadcasted_iota(jnp.int32, (1, LANES), 1) < NA_HEAD_DIM
    head_masks = (jnp.where(lane_lo, scale, 0.0).astype(BF16), jnp.where(lane_lo, 0.0, scale).astype(BF16))
    n_pairs = q_ref.shape[2] // LANES
    strip = NA_WIN_R * GRID_W
    jobs_idx = [(qi, p) for qi in range(rq) for p in range(n_pairs)]

    def q_rows(tok0, qi):
        return pl.ds(pl.multiple_of(tok0 + qi * GRID_W, GRID_W), GRID_W)

    def stacked_q(tok0, qi, p):
        q2 = q_ref[0, q_rows(tok0, qi), p * LANES:(p + 1) * LANES]
        return jnp.concatenate([q2 * head_masks[0], q2 * head_masks[1]], axis=0)

    def attend(tok0, jobs):
        scores = [[_dot_nt(q, k) if b is None else _dot_nt(q, k) + b
                   for k, b in zip(ks, bs)] for q, ks, _, bs in jobs]
        def lane_tiles(arrays):
            return [a[:, t:t + LANES] for a in arrays for t in range(0, a.shape[1], LANES)]

        tops = [functools.reduce(jnp.maximum, lane_tiles(sc)).max(axis=-1, keepdims=True) for sc in scores]
        exps = [[jnp.exp(s - m) for s in sc] for sc, m in zip(scores, tops)]
        dens = [functools.reduce(jnp.add, lane_tiles(es)).sum(axis=-1, keepdims=True) for es in exps]
        for (qi, p), (_, _, vs, _), es, den in zip(jobs_idx, jobs, exps, dens):
            o = _dot(es[0].astype(BF16), vs[0])
            for e, v in zip(es[1:], vs[1:]):
                o = o + _dot(e.astype(BF16), v)
            o = o / den
            o2 = jnp.where(lane_lo, o[:GRID_W], o[GRID_W:])
            o_ref[0, q_rows(tok0 - out_off, qi), p * LANES:(p + 1) * LANES] = o2.astype(o_ref.dtype)

    def latent_rows(step, carry):
        tok0 = n_ctx + step * (rq * GRID_W)
        jobs = []
        for qi, p in jobs_idx:
            ls = slice(p * LANES, (p + 1) * LANES)
            r = step * rq + qi
            r0 = jnp.clip(r - NA_WIN_R // 2, 0, n_rows - NA_WIN_R)
            start = pl.multiple_of(n_ctx + r0 * GRID_W, GRID_W)
            ri0 = NA_WIN_R - 1 - (r - r0)
            bias = jnp.concatenate(
                [jnp.concatenate([bias_ref[2 * p + hh, ri0 + 2 * m] for m in range(NA_WIN_R // 2)], axis=1)
                 for hh in range(2)], axis=0)
            jobs.append((stacked_q(tok0, qi, p),
                         [k_ref[0, pl.ds(start, strip), ls], k_ref[0, 0:n_ctx, ls]],
                         [v_ref[0, pl.ds(start, strip), ls], v_ref[0, 0:n_ctx, ls]],
                         [bias, None]))
        attend(tok0, jobs)
        return carry

    def context_rows(step, carry):
        tok0 = step * (rq * GRID_W)
        jobs = []
        for qi, p in jobs_idx:
            ls = slice(p * LANES, (p + 1) * LANES)
            jobs.append((stacked_q(tok0, qi, p), [k_ref[0, 0:n_ctx, ls]], [v_ref[0, 0:n_ctx, ls]], [None]))
        attend(tok0, jobs)
        return carry

    if with_ctx:
        lax.fori_loop(0, nctxb // rq, context_rows, 0)
    lax.fori_loop(0, n_rows // rq, latent_rows, 0)


def _na_attention(qkv, bias, n_ctx, with_ctx):
    b, t, _ = qkv.shape
    seq = t - n_ctx
    n_rows = seq // GRID_W
    assert seq % GRID_W == 0 and n_ctx % GRID_W == 0 and n_rows >= NA_WIN_R
    hw = 2 * LANES
    ng = D_MODEL // hw
    nctxb = n_ctx // GRID_W
    rq = 4 if nctxb % 4 == 0 and n_rows % 4 == 0 else 2
    assert nctxb % rq == 0 and n_rows % rq == 0
    t_out = t if with_ctx else seq

    return pl.pallas_call(
        functools.partial(_na_kernel, n_ctx=n_ctx, n_rows=n_rows, with_ctx=with_ctx, rq=rq),
        grid=(b, ng),
        in_specs=[
            pl.BlockSpec((1, t, hw), lambda i, g: (i, 0, g)),
            pl.BlockSpec((1, t, hw), lambda i, g: (i, 0, ng + g)),
            pl.BlockSpec((1, t, hw), lambda i, g: (i, 0, 2 * ng + g)),
            pl.BlockSpec((hw // NA_HEAD_DIM, 2 * NA_WIN_R - 2, GRID_W, LANES), lambda i, g: (g, 0, 0, 0)),
        ],
        out_specs=pl.BlockSpec((1, t_out, hw), lambda i, g: (i, 0, g)),
        out_shape=jax.ShapeDtypeStruct((b, t_out, D_MODEL), BF16),
        compiler_params=_cparams("parallel", "parallel"),
        name="na_attention",
    )(qkv, qkv, qkv, bias)


def _gated_residual(u, o_bf16_parts, w_ref, mx_ref, mz_ref, row0, n_ctx):
    acc = None
    k0 = 0
    for part in o_bf16_parts:
        kw = part.shape[1]
        term = _dot(part, w_ref[k0:k0 + kw, :])
        acc = term if acc is None else acc + term
        k0 += kw
    gate = _mod_rows(mx_ref[0], mz_ref[...], 2, row0, u.shape[0], n_ctx)
    return u + gate * acc


def _na_out_kernel(u_ref, o_ref, mx_ref, mz_ref, w_ref, out_ref, *, tm, n_ctx, blk_off):
    row0 = (pl.program_id(1) + blk_off) * tm
    out_ref[0] = _gated_residual(u_ref[0], [o_ref[0]], w_ref, mx_ref, mz_ref, row0, n_ctx)


def _na_out(u, o, mx, mz, w_o, n_ctx, latent_only):
    b, t, d = u.shape
    if latent_only:
        tm = _pick_tile(n_ctx, 256)
        assert (t - n_ctx) % tm == 0
        blk_off = n_ctx // tm
        nt = (t - n_ctx) // tm
    else:
        tm = _pick_tile(t, 544)
        blk_off = 0
        nt = t // tm
    return pl.pallas_call(
        functools.partial(_na_out_kernel, tm=tm, n_ctx=n_ctx, blk_off=blk_off),
        grid=(b, nt),
        in_specs=[
            pl.BlockSpec((1, tm, d), lambda i, j: (i, j + blk_off, 0)),
            pl.BlockSpec((1, tm, d), lambda i, j: (i, j, 0)),
            pl.BlockSpec((1, 6, d), lambda i, j: (i, 0, 0)),
            _const_spec((6, d)),
            _const_spec((d, d)),
        ],
        out_specs=pl.BlockSpec((1, tm, d), lambda i, j: (i, j, 0)),
        out_shape=jax.ShapeDtypeStruct((b, nt * tm, d), F32),
        compiler_params=_cparams("parallel", "parallel"),
        name="na_out",
    )(u, o, mx, mz, w_o)


MLP_FC = 1024


def _mlp_apply(u, mx, mz, g, w1_ref, w2_ref, row0, n_ctx):
    h = _norm_mod(u, g, mx, mz, row0, n_ctx, 3, 4).astype(BF16)
    acc = None
    for c in range(D_FF // MLP_FC):
        a = _dot(h, w1_ref[:, c * MLP_FC:(c + 1) * MLP_FC])
        a = jnp.square(jnp.maximum(a, 0.0)).astype(BF16)
        term = _dot(a, w2_ref[c * MLP_FC:(c + 1) * MLP_FC, :])
        acc = term if acc is None else acc + term
    return u + _mod_rows(mx, mz, 5, row0, u.shape[0], n_ctx) * acc


def _mlp_kernel(u_ref, mx_ref, mz_ref, g_ref, w1_ref, w2_ref, fg_ref, out_ref, *, tm, n_ctx, final):
    row0 = pl.program_id(1) * tm
    y = _mlp_apply(u_ref[0], mx_ref[0], mz_ref[...], g_ref[...], w1_ref, w2_ref, row0, n_ctx)
    if final:
        y = _rms(y, fg_ref[...])
    out_ref[0] = y


def _mixer_mlp_kernel(*refs, kind, tm, n_ctx):
    u_ref, mixer_refs = refs[0], refs[1:-7]
    mx_ref, mz_ref, wo_ref, g_ref, w1_ref, w2_ref, out_ref = refs[-7:]
    if kind == "na":
        parts = [mixer_refs[0][0]]
    elif kind == "gla":
        parts = _gla_gated_parts(*mixer_refs)
    else:
        parts = _rw_gated_parts(*mixer_refs)
    row0 = pl.program_id(1) * tm
    u = _gated_residual(u_ref[0], parts, wo_ref, mx_ref, mz_ref, row0, n_ctx)
    out_ref[0] = _mlp_apply(u, mx_ref[0], mz_ref[...], g_ref[...], w1_ref, w2_ref, row0, n_ctx)


def _mixer_mlp(kind, u, mixer_inputs, mixer_specs, mx, mz, w_o, g, w1, w2, n_ctx, tile_rows=544):
    b, t, d = u.shape
    tm = _pick_tile(t, tile_rows)
    tok = pl.BlockSpec((1, tm, d), lambda i, j: (i, j, 0))
    return pl.pallas_call(
        functools.partial(_mixer_mlp_kernel, kind=kind, tm=tm, n_ctx=n_ctx),
        grid=(b, t // tm),
        in_specs=[tok] + [spec(tm) for spec in mixer_specs] + [
            pl.BlockSpec((1, 6, d), lambda i, j: (i, 0, 0)),
            _const_spec((6, d)),
            _const_spec((d, d)),
            _const_spec((1, d)),
            _const_spec((d, D_FF)),
            _const_spec((D_FF, d)),
        ],
        out_specs=tok,
        out_shape=jax.ShapeDtypeStruct((b, t, d), F32),
        compiler_params=_cparams("parallel", "parallel"),
        name=kind + "_out_mlp",
    )(u, *mixer_inputs, mx, mz, w_o, g.reshape(1, d), w1, w2)


def _mlp(u, mx, mz, g, w1, w2, final_g, n_ctx, final):
    b, t, d = u.shape
    tm = _pick_tile(t, 544)
    return pl.pallas_call(
        functools.partial(_mlp_kernel, tm=tm, n_ctx=n_ctx, final=final),
        grid=(b, t // tm),
        in_specs=[
            pl.BlockSpec((1, tm, d), lambda i, j: (i, j, 0)),
            pl.BlockSpec((1, 6, d), lambda i, j: (i, 0, 0)),
            _const_spec((6, d)),
            _const_spec((1, d)),
            _const_spec((d, D_FF)),
            _const_spec((D_FF, d)),
            _const_spec((1, d)),
        ],
        out_specs=pl.BlockSpec((1, tm, d), lambda i, j: (i, j, 0)),
        out_shape=jax.ShapeDtypeStruct((b, t, d), F32),
        compiler_params=_cparams("parallel", "parallel"),
        name="mlp",
    )(u, mx, mz, g.reshape(1, d), w1, w2, final_g.reshape(1, d))


def _gla_proj_kernel(u_ref, mx_ref, mz_ref, g_ref, w_ref, wlr_ref, wdec_ref, bdec_ref,
                     p_ref, gate_ref, *, tm, n_ctx):
    t = pl.program_id(1)
    h = _norm_mod(u_ref[0], g_ref[...], mx_ref[0], mz_ref[...], t * tm, n_ctx, 0, 1).astype(BF16)
    n = D_MODEL
    for s in range(w_ref.shape[1] // n):
        p_ref[0, :, s * n:(s + 1) * n] = _dot(h, w_ref[:, s * n:(s + 1) * n])
    lr = _dot(h, wlr_ref[...]).astype(BF16)
    z = _dot(lr, wdec_ref[...]) + bdec_ref[...]
    gate_ref[0] = -_softplus(-z) * (1.0 / GLA_GATE_NORM)


def _gla_proj(u, mx, mz, g, w_main, w_lr, w_dec, b_dec, n_ctx):
    b, t, d = u.shape
    n = w_main.shape[1]
    tm = _pick_tile(t, 544)
    return pl.pallas_call(
        functools.partial(_gla_proj_kernel, tm=tm, n_ctx=n_ctx),
        grid=(b, t // tm),
        in_specs=[
            pl.BlockSpec((1, tm, d), lambda i, j: (i, j, 0)),
            pl.BlockSpec((1, 6, d), lambda i, j: (i, 0, 0)),
            _const_spec((6, d)),
            _const_spec((1, d)),
            _const_spec((d, n)),
            _const_spec((d, LANES)),
            _const_spec((LANES, 2 * GLA_DK)),
            _const_spec((1, 2 * GLA_DK)),
        ],
        out_specs=[
            pl.BlockSpec((1, tm, n), lambda i, j: (i, j, 0)),
            pl.BlockSpec((1, tm, 2 * GLA_DK), lambda i, j: (i, j, 0)),
        ],
        out_shape=[
            jax.ShapeDtypeStruct((b, t, n), F32),
            jax.ShapeDtypeStruct((b, t, 2 * GLA_DK), F32),
        ],
        compiler_params=_cparams("parallel", "parallel"),
        name="gla_proj",
    )(u, mx, mz, g.reshape(1, d), w_main, w_lr, w_dec, b_dec)


def _tri(n, upper, strict):
    r = lax.broadcasted_iota(jnp.int32, (n, n), 0)
    c = lax.broadcasted_iota(jnp.int32, (n, n), 1)
    if upper:
        return (r < c) if strict else (r <= c)
    return (r > c) if strict else (r >= c)


def _cumsum_rows(x, descending):
    n = x.shape[0]
    row = lax.broadcasted_iota(jnp.int32, (n, 1), 0)
    s = 1
    while s < n:
        if descending:
            x = x + jnp.where(row < n - s, pltpu.roll(x, n - s, 0), 0.0)
        else:
            x = x + jnp.where(row >= s, pltpu.roll(x, s, 0), 0.0)
        s *= 2
    return x


def _gla_scan_kernel(q_ref, k_ref, v_ref, g0_ref, g1_ref, o_ref, s_ref, *, n_ctx_chunks, n_chunks, nb):
    qscale = GLA_HK ** -0.5
    g_refs = (g0_ref, g1_ref)
    incl = (_tri(CHUNK, False, False), _tri(CHUNK, True, False))
    o_ref[...] = jnp.zeros_like(o_ref)

    def step(first):
        units = [(d, first[d] + (i if d == 0 else -i)) for i in range(nb) for d in range(2)]
        rows = [pl.ds(pl.multiple_of(c * CHUNK, CHUNK), CHUNK) for _, c in units]
        q = [q_ref[0, r, :] for r in rows]
        k = [k_ref[0, r, :] for r in rows]
        v = [v_ref[0, r, :].astype(BF16) for r in rows]
        g = [g_refs[d][0, r, :] for (d, _), r in zip(units, rows)]
        bcum = [_cumsum_rows(gi, d == 1) for (d, _), gi in zip(units, g)]
        b_last_row = [jnp.sum(gi, axis=0, keepdims=True) for gi in g]
        dec_col = [jnp.exp(jnp.sum(gi.T, axis=1, keepdims=True)) for gi in g]
        q_e = [(qi * jnp.exp(bi) * qscale).astype(BF16) for qi, bi in zip(q, bcum)]
        k_e = [(ki * jnp.exp(-bi)).astype(BF16) for ki, bi in zip(k, bcum)]
        k_dec_t = [(ki * jnp.exp(bl - bi)).T.astype(BF16) for ki, bl, bi in zip(k, b_last_row, bcum)]
        a = [jnp.where(incl[d], _dot_nt(qe, ke), 0.0).astype(BF16) for (d, _), qe, ke in zip(units, q_e, k_e)]
        o_intra = [_dot(ai, vi) for ai, vi in zip(a, v)]
        s_inc = [_dot(kt, vi) for kt, vi in zip(k_dec_t, v)]
        s = [s_ref[0], s_ref[1]]
        for i, (d, _) in enumerate(units):
            o_ref[0, rows[i], :] += o_intra[i] + _dot(q_e[i], s[d].astype(BF16))
            s[d] = dec_col[i] * s[d] + s_inc[i]
        s_ref[0] = s[0]
        s_ref[1] = s[1]

    s_ref[...] = jnp.zeros_like(s_ref)
    n_lat = n_chunks - n_ctx_chunks

    def ctx_body(i, carry):
        step((i * nb, n_ctx_chunks - 1 - i * nb))
        return carry

    def lat_body(i, carry):
        step((n_ctx_chunks + i * nb, n_chunks - 1 - i * nb))
        return carry

    lax.fori_loop(0, n_ctx_chunks // nb, ctx_body, 0)
    lax.fori_loop(0, n_lat // nb, lat_body, 0)


def _gla_scan(p, gates, n_ctx, nb):
    b, t, _ = p.shape
    nkb = GLA_DK // GLA_HK
    return pl.pallas_call(
        functools.partial(_gla_scan_kernel, n_ctx_chunks=n_ctx // CHUNK, n_chunks=t // CHUNK, nb=nb),
        grid=(b, GLA_HEADS),
        in_specs=[
            pl.BlockSpec((1, t, GLA_HK), lambda i, h: (i, 0, h)),
            pl.BlockSpec((1, t, GLA_HK), lambda i, h: (i, 0, nkb + h)),
            pl.BlockSpec((1, t, GLA_HV), lambda i, h: (i, 0, (2 * GLA_DK) // GLA_HV + h)),
            pl.BlockSpec((1, t, GLA_HK), lambda i, h: (i, 0, h)),
            pl.BlockSpec((1, t, GLA_HK), lambda i, h: (i, 0, nkb + h)),
        ],
        out_specs=pl.BlockSpec((1, t, GLA_HV), lambda i, h: (i, 0, h)),
        out_shape=jax.ShapeDtypeStruct((b, t, GLA_DV), F32),
        scratch_shapes=[pltpu.VMEM((2, GLA_HK, GLA_HV), F32)],
        compiler_params=_cparams("parallel", "parallel"),
        name="gla_scan",
    )(p, p, p, gates, gates)


def _gla_gated_parts(o_ref, gt_ref, ng_ref):
    parts = []
    for h in range(GLA_HEADS):
        ls = slice(h * GLA_HV, (h + 1) * GLA_HV)
        ov = o_ref[0, :, ls]
        ov = ov * lax.rsqrt(jnp.mean(ov * ov, axis=-1, keepdims=True) + NORM_EPS) * ng_ref[...]
        gt = gt_ref[0, :, ls]
        parts.append((ov * (gt * _sigmoid(gt))).astype(BF16))
    return parts


def _gla_out_kernel(u_ref, o_ref, gt_ref, ng_ref, mx_ref, mz_ref, w_ref, out_ref, *, tm, n_ctx):
    row0 = pl.program_id(1) * tm
    parts = _gla_gated_parts(o_ref, gt_ref, ng_ref)
    out_ref[0] = _gated_residual(u_ref[0], parts, w_ref, mx_ref, mz_ref, row0, n_ctx)


def _gla_out(u, o, p, norm_g, mx, mz, w_o, n_ctx):
    b, t, d = u.shape
    tm = _pick_tile(t, 544)
    gt_blk = (2 * GLA_DK + GLA_DV) // d
    return pl.pallas_call(
        functools.partial(_gla_out_kernel, tm=tm, n_ctx=n_ctx),
        grid=(b, t // tm),
        in_specs=[
            pl.BlockSpec((1, tm, d), lambda i, j: (i, j, 0)),
            pl.BlockSpec((1, tm, d), lambda i, j: (i, j, 0)),
            pl.BlockSpec((1, tm, d), lambda i, j: (i, j, gt_blk)),
            _const_spec((1, GLA_HV)),
            pl.BlockSpec((1, 6, d), lambda i, j: (i, 0, 0)),
            _const_spec((6, d)),
            _const_spec((d, d)),
        ],
        out_specs=pl.BlockSpec((1, tm, d), lambda i, j: (i, j, 0)),
        out_shape=jax.ShapeDtypeStruct((b, t, d), F32),
        compiler_params=_cparams("parallel", "parallel"),
        name="gla_out",
    )(u, o, p, norm_g.reshape(1, GLA_HV), mx, mz, w_o)


def _rw_proj_kernel(u_ref, up_ref, un_ref, mx_ref, mz_ref, g_ref, mix_ref, wrkv_ref, w1_ref, w2_ref,
                    w0_ref, a1_ref, a2_ref, a0_ref, g1_ref, g2_ref, ka_ref, rk_ref,
                    r_ref, k_ref, v_ref, gate_ref, lw_ref, a_ref, bonus_ref, *, tm, n_ctx, t_total):
    row0 = pl.program_id(1) * tm
    g = g_ref[...]
    mx = mx_ref[0]
    mz = mz_ref[...]
    h = _norm_mod(u_ref[0], g, mx, mz, row0, n_ctx, 0, 1)
    h_prev = _norm_mod(up_ref[0], g, mx, mz, row0 - 8, n_ctx, 0, 1)[7:8]
    h_next = _norm_mod(un_ref[0], g, mx, mz, row0 + tm, n_ctx, 0, 1)[0:1]
    idx = lax.broadcasted_iota(jnp.int32, (tm, 1), 0)
    rows = row0 + idx
    h_dn = jnp.where(idx == 0, h_prev, pltpu.roll(h, 1, 0))
    h_dn = jnp.where((rows == 0) | (rows == n_ctx), 0.0, h_dn)
    h_up = jnp.where(idx == tm - 1, h_next, pltpu.roll(h, tm - 1, 0))
    h_up = jnp.where((rows == n_ctx - 1) | (rows == t_total - 1), 0.0, h_up)
    xx = 0.5 * (h_dn + h_up) - h
    xr, xw, xk, xv, xa, xg = ((h + xx * mix_ref[m:m + 1]).astype(BF16) for m in range(6))
    r = _dot(xr, wrkv_ref[0])
    k = _dot(xk, wrkv_ref[1])
    v = _dot(xv, wrkv_ref[2])
    r_ref[0] = r
    k_ref[0] = k
    v_ref[0] = v
    gate_ref[0] = _dot(_sigmoid(_dot(xg, g1_ref[...])).astype(BF16), g2_ref[...])
    tw = jnp.tanh(_dot(xw, w1_ref[...])).astype(BF16)
    lw_ref[0] = -math.exp(-0.5) * _sigmoid(w0_ref[...] + _dot(tw, w2_ref[...]))
    ta = _dot(xa, a1_ref[...]).astype(BF16)
    a = _sigmoid(a0_ref[...] + _dot(ta, a2_ref[...]))
    a_ref[0] = a
    lane_lo = lax.broadcasted_iota(jnp.int32, (1, LANES), 1) < RW_HEAD
    d = D_MODEL
    for p in range(d // LANES):
        ls = slice(p * LANES, (p + 1) * LANES)
        kd_sum = k[:, ls] * (2.0 + (a[:, ls] + a[:, d + p * LANES:d + (p + 1) * LANES] - 2.0) * ka_ref[:, ls])
        bonus_ref[0, :, ls] = _pair_sum(r[:, ls] * kd_sum * rk_ref[:, ls], lane_lo) * v[:, ls]


def _rw_proj(u, mx, mz, g, wts, n_ctx):
    b, t, d = u.shape
    tm = _pick_tile(t, 256)
    nb8 = t // 8
    tb = tm // 8
    full = lambda n: jax.ShapeDtypeStruct((b, t, n), F32)
    row_spec = lambda n: pl.BlockSpec((1, tm, n), lambda i, j: (i, j, 0))
    return pl.pallas_call(
        functools.partial(_rw_proj_kernel, tm=tm, n_ctx=n_ctx, t_total=t),
        grid=(b, t // tm),
        in_specs=[
            row_spec(d),
            pl.BlockSpec((1, 8, d), lambda i, j: (i, jnp.maximum(j * tb - 1, 0), 0)),
            pl.BlockSpec((1, 8, d), lambda i, j: (i, jnp.minimum((j + 1) * tb, nb8 - 1), 0)),
            pl.BlockSpec((1, 6, d), lambda i, j: (i, 0, 0)),
            _const_spec((6, d)),
            _const_spec((1, d)),
            _const_spec((6, d)),
            _const_spec((3, d, d)),
            _const_spec((d, 2 * RW_LORA)),
            _const_spec((2 * RW_LORA, 2 * d)),
            _const_spec((1, 2 * d)),
            _const_spec((d, 2 * RW_LORA)),
            _const_spec((2 * RW_LORA, 2 * d)),
            _const_spec((1, 2 * d)),
            _const_spec((d, 2 * LANES)),
            _const_spec((2 * LANES, d)),
            _const_spec((1, d)),
            _const_spec((1, d)),
        ],
        out_specs=[row_spec(d), row_spec(d), row_spec(d), row_spec(d), row_spec(2 * d), row_spec(2 * d),
                   row_spec(d)],
        out_shape=[full(d), full(d), full(d), full(d), full(2 * d), full(2 * d), full(d)],
        compiler_params=_cparams("parallel", "parallel"),
        name="rw_proj",
    )(u, u, u, mx, mz, g.reshape(1, d), wts["mix"], wts["w_rkv"], wts["w1"], wts["w2"], wts["w0"],
      wts["a1"], wts["a2"], wts["a0"], wts["g1"], wts["g2"], wts["k_a"], wts["r_k"])


def _rw_pair_terms(units, side_work=()):
    C = CHUNK
    lane_lo = lax.broadcasted_iota(jnp.int32, (1, LANES), 1) < RW_HEAD
    row2 = lax.broadcasted_iota(jnp.int32, (C, LANES), 0)
    col2 = lax.broadcasted_iota(jnp.int32, (C, LANES), 1) & (C - 1)
    strict2 = (row2 > col2, row2 < col2)
    incl2 = (row2 >= col2, row2 <= col2)
    eye2 = jnp.where(row2 == col2, 1.0, 0.0)
    nu = len(units)

    def bd(x):
        zero = jnp.zeros_like(x)
        return jnp.concatenate([jnp.where(lane_lo, x, zero), jnp.where(lane_lo, zero, x)], axis=0)

    def hilo(x):
        hi = x.astype(BF16)
        return hi, (x - hi.astype(F32)).astype(BF16)

    side = iter(side_work)

    def stage_done():
        thunk = next(side, None)
        if thunk is not None:
            thunk()

    pre = []
    for r2, k2, v2, lw2, a2, desc, kkw, kaw in units:
        d = int(desc)
        kk = k2 * kkw
        kk = kk * lax.rsqrt(jnp.maximum(_pair_sum(kk * kk, lane_lo), 1e-24))
        kd = k2 * (1.0 + (a2 - 1.0) * kaw)
        bvec = kk * a2
        cum = _cumsum_rows(lw2, desc)
        tot = jnp.sum(lw2, axis=0, keepdims=True)
        e_neg = jnp.exp(-cum)
        e_rem = jnp.exp(tot - cum)
        pre.append(dict(
            d=d, tot=tot, v_b=v2.astype(BF16),
            a_t=-kk * jnp.exp(cum - lw2), r_t=r2 * jnp.exp(cum),
            b_h=(bvec * e_neg).astype(BF16), k_h=(kd * e_neg).astype(BF16),
            b_rem_t=(bvec * e_rem).T.astype(BF16), k_rem_t=(kd * e_rem).T.astype(BF16)))

    ar = [jnp.concatenate([p["a_t"], p["r_t"]], axis=0).astype(BF16) for p in pre]
    p_bk = [_dot_nt(ar[u], jnp.concatenate([bd(pre[u]["b_h"]), bd(pre[u]["k_h"])], axis=0)) for u in range(nu)]
    l_ab = [jnp.where(strict2[pre[u]["d"]], p_bk[u][:C, :LANES], 0.0) for u in range(nu)]
    m_rb = [jnp.where(incl2[pre[u]["d"]], p_bk[u][C:, :LANES], 0.0).astype(BF16) for u in range(nu)]
    l_ak = [jnp.where(strict2[pre[u]["d"]], p_bk[u][:C, LANES:], 0.0).astype(BF16) for u in range(nu)]
    m_rk = [jnp.where(incl2[pre[u]["d"]], p_bk[u][C:, LANES:], 0.0).astype(BF16) for u in range(nu)]
    stage_done()

    xo = l_ab
    l_b16 = [l.astype(BF16) for l in l_ab]
    pw = [_dot(l, bd(l)) for l in l_b16]
    for _ in range(3):
        p_b16 = [p.astype(BF16) for p in pw]
        ra = [_dot(jnp.concatenate([p, x.astype(BF16)], axis=0), bd(p)) for p, x in zip(p_b16, xo)]
        xo = [x + p + a[C:] for x, p, a in zip(xo, pw, ra)]
        pw = [a[:C] for a in ra]
        stage_done()
    xo = [x + p + _dot(x.astype(BF16), bd(p.astype(BF16))) for x, p in zip(xo, pw)]
    l_hl = [hilo(l) for l in l_ab]
    x_hl = [hilo(x) for x in xo]
    lx = [_dot(jnp.concatenate([lh, ll], axis=0), bd(xh)) for (lh, ll), (xh, _) in zip(l_hl, x_hl)]
    lxb = [_dot(lh, bd(xlo)) for (lh, _), (_, xlo) in zip(l_hl, x_hl)]
    res = [(l - x) + (a[:C] + a[C:] + b) for l, x, a, b in zip(l_ab, xo, lx, lxb)]
    stage_done()
    xo = [x + r + _dot(xh, bd(r.astype(BF16))) for x, r, (xh, _) in zip(xo, res, x_hl)]
    t_inv = [(eye2 + x).astype(BF16) for x in xo]

    lvyk = [_dot(jnp.concatenate([l_ak[u], m_rk[u]], axis=0), bd(pre[u]["v_b"])) for u in range(nu)]
    kv = [_dot(pre[u]["k_rem_t"], pre[u]["v_b"]) for u in range(nu)]
    tw = [_dot(t_inv[u], jnp.concatenate([bd(ar[u][:C]), bd(lvyk[u][:C].astype(BF16))], axis=1))
          for u in range(nu)]
    qy = [_dot(m_rb[u], jnp.concatenate([bd(tw[u][:, :LANES].astype(BF16)),
                                         bd(tw[u][:, LANES:].astype(BF16))], axis=1)) for u in range(nu)]
    gh = [_dot(pre[u]["b_rem_t"], tw[u].astype(BF16)) for u in range(nu)]

    r = lax.broadcasted_iota(jnp.int32, (LANES, LANES), 0)
    c = lax.broadcasted_iota(jnp.int32, (LANES, LANES), 1)
    same_head = (r < RW_HEAD) == (c < RW_HEAD)
    out = []
    for u in range(nu):
        qp = pre[u]["r_t"] + qy[u][:, :LANES]
        y0 = qy[u][:, LANES:] + lvyk[u][C:]
        g_mat = jnp.where(same_head, gh[u][:, :LANES], 0.0) + jnp.where(r == c, jnp.exp(pre[u]["tot"]), 0.0)
        h_mat = jnp.where(same_head, gh[u][:, LANES:] + kv[u], 0.0)
        out.append((qp, y0, g_mat, h_mat))
    return out


def _rw_mix_kernel(r0_ref, k0_ref, v0_ref, lw0_ref, a0_ref, r1_ref, k1_ref, v1_ref, lw1_ref, a1_ref,
                   kk_ref, ka_ref, y0_ref, y1_ref, st_ref, qp_s, yc_s, g_s, h_s, *, nb, pp):
    def zero_scratch(_, carry):
        for ref in (st_ref, qp_s, yc_s, g_s, h_s):
            ref[...] = jnp.zeros_like(ref)
        return carry

    lax.fori_loop(0, jnp.where(pl.program_id(2) == 0, 1, 0), zero_scratch, 0)

    y_refs = (y0_ref, y1_ref)
    chains = [(d, p) for d in range(2) for p in range(pp)]
    states = {dp: st_ref[dp[0], dp[1]] for dp in chains}
    prev_terms = {(d, p, c): (qp_s[d, p, c], yc_s[d, p, c], g_s[d, p, c], h_s[d, p, c])
                  for d, p in chains for c in range(nb)}

    def link(ci):
        def emit():
            for d, p in chains:
                c = ci if d == 0 else nb - 1 - ci
                qp, yc, g_mat, h_mat = prev_terms[d, p, c]
                st_b = states[d, p].astype(BF16)
                y_refs[d][0, c * CHUNK:(c + 1) * CHUNK, p * LANES:(p + 1) * LANES] = _dot(qp, st_b) + yc
                states[d, p] = _dot(g_mat, st_b) + h_mat
        return emit

    ins = ((r0_ref, k0_ref, v0_ref, lw0_ref, a0_ref), (r1_ref, k1_ref, v1_ref, lw1_ref, a1_ref))
    units = []
    index = []
    for p in range(pp):
        ls = slice(p * LANES, (p + 1) * LANES)
        for c in range(nb):
            rows = slice(c * CHUNK, (c + 1) * CHUNK)
            for d in range(2):
                units.append(tuple(ref[0, rows, ls] for ref in ins[d]) + (d == 1, kk_ref[:, ls], ka_ref[:, ls]))
                index.append((d, p, c))
    links = [link(ci) for ci in range(nb)]
    assert nb <= 5, "one recurrence link per stage boundary of _rw_pair_terms"
    results = _rw_pair_terms(units, side_work=links)
    for d, p in chains:
        st_ref[d, p] = states[d, p]
    for (d, p, c), (qp, y0, g_mat, h_mat) in zip(index, results):
        qp_s[d, p, c] = qp.astype(BF16)
        yc_s[d, p, c] = y0
        g_s[d, p, c] = g_mat.astype(BF16)
        h_s[d, p, c] = h_mat


def _rw_mix(r, k, v, lw, a, k_k, k_a, n_ctx, nb):
    b, t, d = r.shape
    npairs = d // LANES
    rows = nb * CHUNK
    nblk = t // rows
    ncb = n_ctx // rows

    def rev(j):
        return jnp.where(j < ncb, ncb - 1 - j, nblk - 1 - (j - ncb))

    cur = lambda j: jnp.minimum(j, nblk - 1)
    prev = lambda j: jnp.maximum(j - 1, 0)
    pp = 2
    pw = pp * LANES
    ng = npairs // pp
    fwd = lambda off, blk: pl.BlockSpec((1, rows, pw), lambda i, p, j: (i, blk(j), off + p))
    bwd = lambda off, blk: pl.BlockSpec((1, rows, pw), lambda i, p, j: (i, rev(blk(j)), off + p))
    par = pl.BlockSpec((1, pw), lambda i, p, j: (0, p))
    return pl.pallas_call(
        functools.partial(_rw_mix_kernel, nb=nb, pp=pp),
        grid=(b, ng, nblk + 1),
        in_specs=[fwd(0, cur), fwd(0, cur), fwd(0, cur), fwd(0, cur), fwd(0, cur),
                  bwd(0, cur), bwd(0, cur), bwd(0, cur), bwd(ng, cur), bwd(ng, cur), par, par],
        out_specs=[fwd(0, prev), bwd(0, prev)],
        out_shape=[jax.ShapeDtypeStruct((b, t, d), F32), jax.ShapeDtypeStruct((b, t, d), F32)],
        scratch_shapes=[pltpu.VMEM((2, pp, LANES, LANES), F32),
                        pltpu.VMEM((2, pp, nb, CHUNK, LANES), BF16), pltpu.VMEM((2, pp, nb, CHUNK, LANES), F32),
                        pltpu.VMEM((2, pp, nb, LANES, LANES), BF16), pltpu.VMEM((2, pp, nb, LANES, LANES), F32)],
        compiler_params=_cparams("parallel", "parallel", "arbitrary"),
        name="rw_mix",
    )(r, k, v, lw, a, r, k, v, lw, a, k_k.reshape(1, d), k_a.reshape(1, d))


def _rw_gated_parts(y0_ref, y1_ref, bonus_ref, gate_ref, lng_ref, lnb_ref):
    lane_lo = lax.broadcasted_iota(jnp.int32, (1, LANES), 1) < RW_HEAD
    parts = []
    for p in range(D_MODEL // LANES):
        ls = slice(p * LANES, (p + 1) * LANES)
        y = y0_ref[0, :, ls] + y1_ref[0, :, ls]
        mu = _pair_sum(y, lane_lo) * (1.0 / RW_HEAD)
        yc = y - mu
        var = _pair_sum(yc * yc, lane_lo) * (1.0 / RW_HEAD)
        yn = yc * lax.rsqrt(var + RW_GN_EPS) * lng_ref[:, ls] + lnb_ref[:, ls]
        parts.append(((yn + bonus_ref[0, :, ls]) * gate_ref[0, :, ls]).astype(BF16))
    return parts


def _rw_out_kernel(u_ref, y0_ref, y1_ref, bonus_ref, gate_ref, lng_ref, lnb_ref, mx_ref, mz_ref, w_ref,
                   out_ref, *, tm, n_ctx):
    row0 = pl.program_id(1) * tm
    parts = _rw_gated_parts(y0_ref, y1_ref, bonus_ref, gate_ref, lng_ref, lnb_ref)
    out_ref[0] = _gated_residual(u_ref[0], parts, w_ref, mx_ref, mz_ref, row0, n_ctx)


def _rw_out(u, y0, y1, bonus, gate, ln_g, ln_b, mx, mz, w_o, n_ctx):
    b, t, d = u.shape
    tm = _pick_tile(t, 544)
    tok = pl.BlockSpec((1, tm, d), lambda i, j: (i, j, 0))
    vec = _const_spec((1, d))
    return pl.pallas_call(
        functools.partial(_rw_out_kernel, tm=tm, n_ctx=n_ctx),
        grid=(b, t // tm),
        in_specs=[tok] * 5 + [vec, vec, pl.BlockSpec((1, 6, d), lambda i, j: (i, 0, 0)),
                              _const_spec((6, d)), _const_spec((d, d))],
        out_specs=tok,
        out_shape=jax.ShapeDtypeStruct((b, t, d), F32),
        compiler_params=_cparams("parallel", "parallel"),
        name="rw_out",
    )(u, y0, y1, bonus, gate, ln_g.reshape(1, d), ln_b.reshape(1, d), mx, mz, w_o)


def _block_rows(w_pair):
    _, k, n = w_pair.shape
    z = jnp.zeros((k, n), w_pair.dtype)
    return jnp.concatenate([jnp.concatenate([w_pair[0], z], axis=1),
                            jnp.concatenate([z, w_pair[1]], axis=1)], axis=0)


def kernel(x, c, ctx, c_ctx, ada_w, ada_b, norm1_g, norm2_g, mlp_w1, mlp_w2, final_g, na_w_qkv, na_w_o, na_rpb, gla_w_in, gla_w_dec2, gla_b_dec, gla_norm_g, gla_w_o, rw_mix, rw_w_rkv, rw_w0, rw_w1, rw_w2, rw_a0, rw_a1, rw_a2, rw_g1, rw_g2, rw_k_k, rw_k_a, rw_r_k, rw_ln_g, rw_ln_b, rw_w_o):
    b, seq, d = x.shape
    n_ctx = ctx.shape[1]
    depth = ada_w.shape[0]
    assert d == D_MODEL and n_ctx % CHUNK == 0 and seq % CHUNK == 0

    entry_fused = seq % n_ctx == 0 and n_ctx % 16 == 0
    u = None if entry_fused else jnp.concatenate([ctx.astype(x.dtype), x], axis=1)

    rows = -(-(b + 1) // 8) * 8
    c_all = jnp.zeros((rows, d), F32).at[:b].set(c).at[b].set(c_ctx)
    tables = _ada_tables(c_all, ada_w, ada_b)
    mods_x = tables[:, :b].reshape(depth, b, 6, d)
    mods_z = tables[:, b].reshape(depth, 6, d)

    nb = max(n for n in (4, 2, 1) if (n_ctx // CHUNK) % n == 0 and (seq // CHUNK) % n == 0)

    for i in range(depth):
        kind, j = i % 3, i // 3
        last = i == depth - 1
        mx, mz = mods_x[i], mods_z[i]
        w1, w2 = mlp_w1[i].astype(BF16), mlp_w2[i].astype(BF16)
        tok = lambda tm: pl.BlockSpec((1, tm, d), lambda bi, ti: (bi, ti, 0))
        mlp_done = False
        if kind == 0:
            if u is None:
                qkv, u = _entry_proj(x, ctx, mx, mz, norm1_g[i], na_w_qkv[j].astype(BF16), BF16)
            else:
                qkv = _norm_proj(u, mx, mz, norm1_g[i], na_w_qkv[j].astype(BF16), n_ctx, BF16)
            o = _na_attention(qkv, _na_bias_table(na_rpb[j]), n_ctx, with_ctx=not last)
            if last:
                u = _na_out(u, o, mx, mz, na_w_o[j].astype(BF16), n_ctx, latent_only=True)
            else:
                u = _mixer_mlp("na", u, [o], [tok], mx, mz, na_w_o[j].astype(BF16), norm2_g[i], w1, w2, n_ctx)
                mlp_done = True
        elif kind == 1:
            n_main = 2 * GLA_DK + 2 * GLA_DV
            w_in = gla_w_in[j]
            w_lr = jnp.zeros((d, LANES), F32).at[:, :2 * GLA_LOW_RANK].set(w_in[:, n_main:])
            w_dec = jnp.zeros((LANES, 2 * GLA_DK), F32).at[:2 * GLA_LOW_RANK].set(_block_rows(gla_w_dec2[j]))
            p, gates = _gla_proj(u, mx, mz, norm1_g[i], w_in[:, :n_main].astype(BF16), w_lr.astype(BF16),
                                 w_dec.astype(BF16), gla_b_dec[j].reshape(1, 2 * GLA_DK), n_ctx)
            o = _gla_scan(p, gates, n_ctx, nb)
            if last:
                u = _gla_out(u, o, p, gla_norm_g[j], mx, mz, gla_w_o[j].astype(BF16), n_ctx)
            else:
                gt_blk = (2 * GLA_DK + GLA_DV) // d
                gt_spec = lambda tm: pl.BlockSpec((1, tm, d), lambda bi, ti: (bi, ti, gt_blk))
                u = _mixer_mlp("gla", u, [o, p, gla_norm_g[j].reshape(1, GLA_HV)],
                               [tok, gt_spec, lambda tm: _const_spec((1, GLA_HV))],
                               mx, mz, gla_w_o[j].astype(BF16), norm2_g[i], w1, w2, n_ctx)
                mlp_done = True
        else:
            g1 = jnp.zeros((d, 2 * LANES), F32).at[:, :RW_GATE_LORA].set(rw_g1[j])
            g2 = jnp.zeros((2 * LANES, d), F32).at[:RW_GATE_LORA].set(rw_g2[j])
            wts = dict(
                mix=rw_mix[j], w_rkv=rw_w_rkv[j].astype(BF16),
                w1=jnp.concatenate([rw_w1[j, 0], rw_w1[j, 1]], axis=1).astype(BF16),
                w2=_block_rows(rw_w2[j]).astype(BF16), w0=rw_w0[j].reshape(1, 2 * d),
                a1=jnp.concatenate([rw_a1[j, 0], rw_a1[j, 1]], axis=1).astype(BF16),
                a2=_block_rows(rw_a2[j]).astype(BF16), a0=rw_a0[j].reshape(1, 2 * d),
                g1=g1.astype(BF16), g2=g2.astype(BF16),
                k_a=rw_k_a[j].reshape(1, d), r_k=rw_r_k[j].reshape(1, d))
            r, k, v, gate, lw, a, bonus = _rw_proj(u, mx, mz, norm1_g[i], wts, n_ctx)
            y0, y1 = _rw_mix(r, k, v, lw, a, rw_k_k[j], rw_k_a[j], n_ctx, nb)
            if last:
                u = _rw_out(u, y0, y1, bonus, gate, rw_ln_g[j], rw_ln_b[j], mx, mz, rw_w_o[j].astype(BF16), n_ctx)
            else:
                vec = lambda tm: _const_spec((1, d))
                u = _mixer_mlp("rw", u, [y0, y1, bonus, gate, rw_ln_g[j].reshape(1, d), rw_ln_b[j].reshape(1, d)],
                               [tok, tok, tok, tok, vec, vec], mx, mz, rw_w_o[j].astype(BF16), norm2_g[i],
                               w1, w2, n_ctx, tile_rows=544)
                mlp_done = True
        if last and u.shape[1] != seq:
            u = u[:, n_ctx:]
        if not mlp_done:
            u = _mlp(u, mx, mz, norm2_g[i], w1, w2, final_g, 0 if last else n_ctx, final=last)
    return u
```

```python
import functools
import math

import jax
import jax.numpy as jnp
from jax import lax
from jax.experimental import pallas as pl
from jax.experimental.pallas import tpu as pltpu

F32 = jnp.float32
BF16 = jnp.bfloat16

D_MODEL = 1024
D_FF = 4 * D_MODEL
NORM_EPS = 1e-6
GRID_W = 64

NA_HEAD_DIM = 64
NA_HEADS = D_MODEL // NA_HEAD_DIM
NA_WIN_R = 8
NA_WIN_C = 16
NA_MASK = -1e30

GLA_HEADS = 4
GLA_DK = D_MODEL // 2
GLA_DV = D_MODEL
GLA_HK = GLA_DK // GLA_HEADS
GLA_HV = GLA_DV // GLA_HEADS
GLA_LOW_RANK = 16
GLA_GATE_NORM = 16.0
CHUNK = 64

RW_HEAD = 64
RW_GATE_LORA = 160
RW_LORA = 64
RW_GN_EPS = 64e-5

LANES = 128
VMEM_LIMIT = 56 * 1024 * 1024
ROW_TILE = 544
NARROW_TILE = 256


def _cparams(*sem):
    return pltpu.CompilerParams(dimension_semantics=sem, vmem_limit_bytes=VMEM_LIMIT)


def _pick_tile(n, target, mult=16):
    best = None
    for t in range(mult, min(n, target) + 1, mult):
        if n % t == 0:
            best = t
    assert best is not None, (n, target)
    return best


def _const_spec(shape):
    nd = len(shape)
    return pl.BlockSpec(shape, lambda *_: (0,) * nd, pipeline_mode=pl.Buffered(1))


def _dot(a, b):
    return jnp.dot(a, b, preferred_element_type=F32)


def _dot_nt(a, b):
    return lax.dot_general(a, b, (((1,), (1,)), ((), ())), preferred_element_type=F32)


def _sigmoid(x):
    return 0.5 + 0.5 * jnp.tanh(0.5 * x)


def _softplus(x):
    return jnp.maximum(x, 0.0) + jnp.log(1.0 + jnp.exp(-jnp.abs(x)))


def _rms(u, g):
    return u * lax.rsqrt(jnp.mean(u * u, axis=-1, keepdims=True) + NORM_EPS) * g


def _mod_rows(mx, mz, idx, row0, rows, n_ctx):
    vx = mx[idx:idx + 1]
    if n_ctx == 0:
        return vx
    if n_ctx % rows == 0:
        return jnp.where(row0 < n_ctx, mz[idx:idx + 1], vx)
    r = row0 + lax.broadcasted_iota(jnp.int32, (rows, 1), 0)
    return jnp.where(r < n_ctx, mz[idx:idx + 1], vx)


def _norm_mod(u, g, mx, mz, row0, n_ctx, i_shift, i_scale):
    rows = u.shape[0]
    shift = _mod_rows(mx, mz, i_shift, row0, rows, n_ctx)
    scale = _mod_rows(mx, mz, i_scale, row0, rows, n_ctx)
    return _rms(u, g) * (1.0 + scale) + shift


def _pair_sum(x, lane_lo):
    s0 = jnp.sum(jnp.where(lane_lo, x, 0.0), axis=-1, keepdims=True)
    s1 = jnp.sum(jnp.where(lane_lo, 0.0, x), axis=-1, keepdims=True)
    return jnp.where(lane_lo, s0, s1)


def _ada_kernel(c_ref, w_ref, b_ref, o_ref):
    c = c_ref[...]
    sc = (c * _sigmoid(c)).astype(BF16)
    o_ref[0] = _dot(sc, w_ref[0].astype(BF16)) + b_ref[0]


def _ada_tables(c_all, ada_w, ada_b):
    depth, d, n = ada_w.shape
    rows = c_all.shape[0]
    tn = 1536
    return pl.pallas_call(
        _ada_kernel,
        grid=(depth, n // tn),
        in_specs=[
            pl.BlockSpec((rows, d), lambda l, j: (0, 0)),
            pl.BlockSpec((1, d, tn), lambda l, j: (l, 0, j)),
            pl.BlockSpec((1, 1, tn), lambda l, j: (l, 0, j)),
        ],
        out_specs=pl.BlockSpec((1, rows, tn), lambda l, j: (l, 0, j)),
        out_shape=jax.ShapeDtypeStruct((depth, rows, n), F32),
        compiler_params=_cparams("parallel", "parallel"),
        name="ada_tables",
    )(c_all, ada_w, ada_b.reshape(depth, 1, n))


def _proj_kernel(u_ref, mx_ref, mz_ref, g_ref, w_ref, o_ref, *, tm, n_ctx, nsplit):
    t = pl.program_id(1)
    h = _norm_mod(u_ref[0], g_ref[...], mx_ref[0], mz_ref[...], t * tm, n_ctx, 0, 1).astype(BF16)
    n = w_ref.shape[1] // nsplit
    for s in range(nsplit):
        o_ref[0, :, s * n:(s + 1) * n] = _dot(h, w_ref[:, s * n:(s + 1) * n]).astype(o_ref.dtype)


def _norm_proj(u, mx, mz, g, w, n_ctx, out_dtype):
    b, t, d = u.shape
    n = w.shape[1]
    tm = _pick_tile(t, ROW_TILE)
    return pl.pallas_call(
        functools.partial(_proj_kernel, tm=tm, n_ctx=n_ctx, nsplit=n // D_MODEL),
        grid=(b, t // tm),
        in_specs=[
            pl.BlockSpec((1, tm, d), lambda i, j: (i, j, 0)),
            pl.BlockSpec((1, 6, d), lambda i, j: (i, 0, 0)),
            _const_spec((6, d)),
            _const_spec((1, d)),
            _const_spec((d, n)),
        ],
        out_specs=pl.BlockSpec((1, tm, n), lambda i, j: (i, j, 0)),
        out_shape=jax.ShapeDtypeStruct((b, t, n), out_dtype),
        compiler_params=_cparams("parallel", "parallel"),
        name="norm_proj",
    )(u, mx, mz, g.reshape(1, d), w)


def _entry_proj_kernel(ctx_ref, x_ref, mx_ref, mz_ref, g_ref, w_ref, o_ref, u_ref, *, tm, nsplit):
    j = pl.program_id(1)
    u = jnp.where(j == 0, ctx_ref[0].astype(F32), x_ref[0])
    u_ref[0] = u
    h = _norm_mod(u, g_ref[...], mx_ref[0], mz_ref[...], j * tm, tm, 0, 1).astype(BF16)
    n = w_ref.shape[1] // nsplit
    for s in range(nsplit):
        o_ref[0, :, s * n:(s + 1) * n] = _dot(h, w_ref[:, s * n:(s + 1) * n]).astype(o_ref.dtype)


def _entry_proj(x, ctx, mx, mz, g, w, out_dtype):
    b, seq, d = x.shape
    tm = ctx.shape[1]
    assert seq % tm == 0 and tm % 16 == 0
    n = w.shape[1]
    t = tm + seq
    return pl.pallas_call(
        functools.partial(_entry_proj_kernel, tm=tm, nsplit=n // D_MODEL),
        grid=(b, t // tm),
        in_specs=[
            pl.BlockSpec((1, tm, d), lambda i, j: (i, 0, 0)),
            pl.BlockSpec((1, tm, d), lambda i, j: (i, jnp.maximum(j - 1, 0), 0)),
            pl.BlockSpec((1, 6, d), lambda i, j: (i, 0, 0)),
            _const_spec((6, d)),
            _const_spec((1, d)),
            _const_spec((d, n)),
        ],
        out_specs=[pl.BlockSpec((1, tm, n), lambda i, j: (i, j, 0)),
                   pl.BlockSpec((1, tm, d), lambda i, j: (i, j, 0))],
        out_shape=[jax.ShapeDtypeStruct((b, t, n), out_dtype), jax.ShapeDtypeStruct((b, t, d), F32)],
        compiler_params=_cparams("parallel", "parallel"),
        name="entry_proj",
    )(ctx, x, mx, mz, g.reshape(1, d), w)


def _na_bias_table(rpb):
    w = GRID_W
    cols = jnp.arange(w)
    c_start = jnp.clip(cols - NA_WIN_C // 2, 0, w - NA_WIN_C)
    col_ok = (cols[None, :] >= c_start[:, None]) & (cols[None, :] < c_start[:, None] + NA_WIN_C)
    rpb = rpb.astype(F32)
    edge = w - NA_WIN_C
    t = jnp.concatenate([jnp.repeat(rpb[..., :1], edge, axis=-1), rpb,
                         jnp.repeat(rpb[..., -1:], edge + 1, axis=-1)], axis=-1)
    m = jnp.tile(t, w)[..., :w * (2 * w - 1)].reshape(t.shape[:-1] + (w, 2 * w - 1))[..., w - 1:]
    m = jnp.where(col_ok[None, None], m, NA_MASK)
    return jnp.concatenate([m[:, :-1], m[:, 1:]], axis=-1)


def _na_kernel(q_ref, k_ref, v_ref, bias_ref, o_ref, *, n_ctx, n_rows, with_ctx, rq):
    nctxb = n_ctx // GRID_W
    out_off = 0 if with_ctx else n_ctx
    scale = NA_HEAD_DIM ** -0.5
    assert scale == 0.125, "the score scale is folded into bf16 q, exact only for a power of two"
    lane_lo = lax.broadcasted_iota(jnp.int32, (1, LANES), 1) < NA_HEAD_DIM
    head_masks = (jnp.where(lane_lo, scale, 0.0).astype(BF16), jnp.where(lane_lo, 0.0, scale).astype(BF16))
    n_pairs = q_ref.shape[2] // LANES
    strip = NA_WIN_R * GRID_W
    jobs_idx = [(qi, p) for qi in range(rq) for p in range(n_pairs)]

    def q_rows(tok0, qi):
        return pl.ds(pl.multiple_of(tok0 + qi * GRID_W, GRID_W), GRID_W)

    def stacked_q(tok0, qi, p):
        q2 = q_ref[0, q_rows(tok0, qi), p * LANES:(p + 1) * LANES]
        return jnp.concatenate([q2 * head_masks[0], q2 * head_masks[1]], axis=0)

    def attend(tok0, jobs):
        scores = [[_dot_nt(q, k) if b is None else _dot_nt(q, k) + b
                   for k, b in zip(ks, bs)] for q, ks, _, bs in jobs]
        def lane_tiles(arrays):
            return [a[:, t:t + LANES] for a in arrays for t in range(0, a.shape[1], LANES)]

        tops = [functools.reduce(jnp.maximum, lane_tiles(sc)).max(axis=-1, keepdims=True) for sc in scores]
        exps = [[jnp.exp(s - m) for s in sc] for sc, m in zip(scores, tops)]
        dens = [functools.reduce(jnp.add, lane_tiles(es)).sum(axis=-1, keepdims=True) for es in exps]
        for (qi, p), (_, _, vs, _), es, den in zip(jobs_idx, jobs, exps, dens):
            o = _dot(es[0].astype(BF16), vs[0])
            for e, v in zip(es[1:], vs[1:]):
                o = o + _dot(e.astype(BF16), v)
            o = o / den
            o2 = jnp.where(lane_lo, o[:GRID_W], o[GRID_W:])
            o_ref[0, q_rows(tok0 - out_off, qi), p * LANES:(p + 1) * LANES] = o2.astype(o_ref.dtype)

    def latent_rows(step, carry):
        tok0 = n_ctx + step * (rq * GRID_W)
        jobs = []
        for qi, p in jobs_idx:
            ls = slice(p * LANES, (p + 1) * LANES)
            r = step * rq + qi
            r0 = jnp.clip(r - NA_WIN_R // 2, 0, n_rows - NA_WIN_R)
            start = pl.multiple_of(n_ctx + r0 * GRID_W, GRID_W)
            ri0 = NA_WIN_R - 1 - (r - r0)
            bias = jnp.concatenate(
                [jnp.concatenate([bias_ref[2 * p + hh, ri0 + 2 * m] for m in range(NA_WIN_R // 2)], axis=1)
                 for hh in range(2)], axis=0)
            jobs.append((stacked_q(tok0, qi, p),
                         [k_ref[0, pl.ds(start, strip), ls], k_ref[0, 0:n_ctx, ls]],
                         [v_ref[0, pl.ds(start, strip), ls], v_ref[0, 0:n_ctx, ls]],
                         [bias, None]))
        attend(tok0, jobs)
        return carry

    def context_rows(step, carry):
        tok0 = step * (rq * GRID_W)
        jobs = []
        for qi, p in jobs_idx:
            ls = slice(p * LANES, (p + 1) * LANES)
            jobs.append((stacked_q(tok0, qi, p), [k_ref[0, 0:n_ctx, ls]], [v_ref[0, 0:n_ctx, ls]], [None]))
        attend(tok0, jobs)
        return carry

    if with_ctx:
        lax.fori_loop(0, nctxb // rq, context_rows, 0)
    lax.fori_loop(0, n_rows // rq, latent_rows, 0)


def _na_attention(qkv, bias, n_ctx, with_ctx):
    b, t, _ = qkv.shape
    seq = t - n_ctx
    n_rows = seq // GRID_W
    assert seq % GRID_W == 0 and n_ctx % GRID_W == 0 and n_rows >= NA_WIN_R
    hw = 2 * LANES
    ng = D_MODEL // hw
    nctxb = n_ctx // GRID_W
    rq = 4 if nctxb % 4 == 0 and n_rows % 4 == 0 else 2
    assert nctxb % rq == 0 and n_rows % rq == 0
    t_out = t if with_ctx else seq

    return pl.pallas_call(
        functools.partial(_na_kernel, n_ctx=n_ctx, n_rows=n_rows, with_ctx=with_ctx, rq=rq),
        grid=(b, ng),
        in_specs=[
            pl.BlockSpec((1, t, hw), lambda i, g: (i, 0, g)),
            pl.BlockSpec((1, t, hw), lambda i, g: (i, 0, ng + g)),
            pl.BlockSpec((1, t, hw), lambda i, g: (i, 0, 2 * ng + g)),
            pl.BlockSpec((hw // NA_HEAD_DIM, 2 * NA_WIN_R - 2, GRID_W, LANES), lambda i, g: (g, 0, 0, 0)),
        ],
        out_specs=pl.BlockSpec((1, t_out, hw), lambda i, g: (i, 0, g)),
        out_shape=jax.ShapeDtypeStruct((b, t_out, D_MODEL), BF16),
        compiler_params=_cparams("parallel", "parallel"),
        name="na_attention",
    )(qkv, qkv, qkv, bias)


def _gated_residual(u, o_bf16_parts, w_ref, mx_ref, mz_ref, row0, n_ctx):
    acc = None
    k0 = 0
    for part in o_bf16_parts:
        kw = part.shape[1]
        term = _dot(part, w_ref[k0:k0 + kw, :])
        acc = term if acc is None else acc + term
        k0 += kw
    gate = _mod_rows(mx_ref[0], mz_ref[...], 2, row0, u.shape[0], n_ctx)
    return u + gate * acc


def _na_out_kernel(u_ref, o_ref, mx_ref, mz_ref, w_ref, out_ref, *, tm, n_ctx, blk_off):
    row0 = (pl.program_id(1) + blk_off) * tm
    out_ref[0] = _gated_residual(u_ref[0], [o_ref[0]], w_ref, mx_ref, mz_ref, row0, n_ctx)


def _na_out_latent(u, o, mx, mz, w_o, n_ctx):
    b, t, d = u.shape
    tm = _pick_tile(n_ctx, NARROW_TILE)
    assert (t - n_ctx) % tm == 0
    blk_off = n_ctx // tm
    nt = (t - n_ctx) // tm
    return pl.pallas_call(
        functools.partial(_na_out_kernel, tm=tm, n_ctx=n_ctx, blk_off=blk_off),
        grid=(b, nt),
        in_specs=[
            pl.BlockSpec((1, tm, d), lambda i, j: (i, j + blk_off, 0)),
            pl.BlockSpec((1, tm, d), lambda i, j: (i, j, 0)),
            pl.BlockSpec((1, 6, d), lambda i, j: (i, 0, 0)),
            _const_spec((6, d)),
            _const_spec((d, d)),
        ],
        out_specs=pl.BlockSpec((1, tm, d), lambda i, j: (i, j, 0)),
        out_shape=jax.ShapeDtypeStruct((b, nt * tm, d), F32),
        compiler_params=_cparams("parallel", "parallel"),
        name="na_out",
    )(u, o, mx, mz, w_o)


MLP_FC = 1024


def _mlp_apply(u, mx, mz, g, w1_ref, w2_ref, row0, n_ctx):
    h = _norm_mod(u, g, mx, mz, row0, n_ctx, 3, 4).astype(BF16)
    acc = None
    for c in range(D_FF // MLP_FC):
        a = _dot(h, w1_ref[:, c * MLP_FC:(c + 1) * MLP_FC])
        a = jnp.square(jnp.maximum(a, 0.0)).astype(BF16)
        term = _dot(a, w2_ref[c * MLP_FC:(c + 1) * MLP_FC, :])
        acc = term if acc is None else acc + term
    return u + _mod_rows(mx, mz, 5, row0, u.shape[0], n_ctx) * acc


def _mlp_kernel(u_ref, mx_ref, mz_ref, g_ref, w1_ref, w2_ref, fg_ref, out_ref, *, tm, n_ctx, final):
    row0 = pl.program_id(1) * tm
    y = _mlp_apply(u_ref[0], mx_ref[0], mz_ref[...], g_ref[...], w1_ref, w2_ref, row0, n_ctx)
    if final:
        y = _rms(y, fg_ref[...])
    out_ref[0] = y


def _mixer_mlp_kernel(*refs, kind, tm, n_ctx):
    u_ref, mixer_refs = refs[0], refs[1:-7]
    mx_ref, mz_ref, wo_ref, g_ref, w1_ref, w2_ref, out_ref = refs[-7:]
    if kind == "na":
        parts = [mixer_refs[0][0]]
    elif kind == "gla":
        parts = _gla_gated_parts(*mixer_refs)
    else:
        parts = _rw_gated_parts(*mixer_refs)
    row0 = pl.program_id(1) * tm
    u = _gated_residual(u_ref[0], parts, wo_ref, mx_ref, mz_ref, row0, n_ctx)
    out_ref[0] = _mlp_apply(u, mx_ref[0], mz_ref[...], g_ref[...], w1_ref, w2_ref, row0, n_ctx)


def _mixer_mlp(kind, u, mixer_inputs, mixer_specs, mx, mz, w_o, g, w1, w2, n_ctx):
    b, t, d = u.shape
    tm = _pick_tile(t, ROW_TILE)
    tok = pl.BlockSpec((1, tm, d), lambda i, j: (i, j, 0))
    return pl.pallas_call(
        functools.partial(_mixer_mlp_kernel, kind=kind, tm=tm, n_ctx=n_ctx),
        grid=(b, t // tm),
        in_specs=[tok] + [spec(tm) for spec in mixer_specs] + [
            pl.BlockSpec((1, 6, d), lambda i, j: (i, 0, 0)),
            _const_spec((6, d)),
            _const_spec((d, d)),
            _const_spec((1, d)),
            _const_spec((d, D_FF)),
            _const_spec((D_FF, d)),
        ],
        out_specs=tok,
        out_shape=jax.ShapeDtypeStruct((b, t, d), F32),
        compiler_params=_cparams("parallel", "parallel"),
        name=kind + "_out_mlp",
    )(u, *mixer_inputs, mx, mz, w_o, g.reshape(1, d), w1, w2)


def _mlp(u, mx, mz, g, w1, w2, final_g, n_ctx, final):
    b, t, d = u.shape
    tm = _pick_tile(t, ROW_TILE)
    return pl.pallas_call(
        functools.partial(_mlp_kernel, tm=tm, n_ctx=n_ctx, final=final),
        grid=(b, t // tm),
        in_specs=[
            pl.BlockSpec((1, tm, d), lambda i, j: (i, j, 0)),
            pl.BlockSpec((1, 6, d), lambda i, j: (i, 0, 0)),
            _const_spec((6, d)),
            _const_spec((1, d)),
            _const_spec((d, D_FF)),
            _const_spec((D_FF, d)),
            _const_spec((1, d)),
        ],
        out_specs=pl.BlockSpec((1, tm, d), lambda i, j: (i, j, 0)),
        out_shape=jax.ShapeDtypeStruct((b, t, d), F32),
        compiler_params=_cparams("parallel", "parallel"),
        name="mlp",
    )(u, mx, mz, g.reshape(1, d), w1, w2, final_g.reshape(1, d))


def _gla_proj_kernel(u_ref, mx_ref, mz_ref, g_ref, w_ref, wlr_ref, wdec_ref, bdec_ref,
                     p_ref, gate_ref, *, tm, n_ctx):
    t = pl.program_id(1)
    h = _norm_mod(u_ref[0], g_ref[...], mx_ref[0], mz_ref[...], t * tm, n_ctx, 0, 1).astype(BF16)
    n = D_MODEL
    for s in range(w_ref.shape[1] // n):
        p_ref[0, :, s * n:(s + 1) * n] = _dot(h, w_ref[:, s * n:(s + 1) * n])
    lr = _dot(h, wlr_ref[...]).astype(BF16)
    z = _dot(lr, wdec_ref[...]) + bdec_ref[...]
    gate_ref[0] = -_softplus(-z) * (1.0 / GLA_GATE_NORM)


def _gla_proj(u, mx, mz, g, w_main, w_lr, w_dec, b_dec, n_ctx):
    b, t, d = u.shape
    n = w_main.shape[1]
    tm = _pick_tile(t, ROW_TILE)
    return pl.pallas_call(
        functools.partial(_gla_proj_kernel, tm=tm, n_ctx=n_ctx),
        grid=(b, t // tm),
        in_specs=[
            pl.BlockSpec((1, tm, d), lambda i, j: (i, j, 0)),
            pl.BlockSpec((1, 6, d), lambda i, j: (i, 0, 0)),
            _const_spec((6, d)),
            _const_spec((1, d)),
            _const_spec((d, n)),
            _const_spec((d, LANES)),
            _const_spec((LANES, 2 * GLA_DK)),
            _const_spec((1, 2 * GLA_DK)),
        ],
        out_specs=[
            pl.BlockSpec((1, tm, n), lambda i, j: (i, j, 0)),
            pl.BlockSpec((1, tm, 2 * GLA_DK), lambda i, j: (i, j, 0)),
        ],
        out_shape=[
            jax.ShapeDtypeStruct((b, t, n), F32),
            jax.ShapeDtypeStruct((b, t, 2 * GLA_DK), F32),
        ],
        compiler_params=_cparams("parallel", "parallel"),
        name="gla_proj",
    )(u, mx, mz, g.reshape(1, d), w_main, w_lr, w_dec, b_dec)


def _tri(n, upper, strict):
    r = lax.broadcasted_iota(jnp.int32, (n, n), 0)
    c = lax.broadcasted_iota(jnp.int32, (n, n), 1)
    if upper:
        return (r < c) if strict else (r <= c)
    return (r > c) if strict else (r >= c)


def _cumsum_rows(x, descending):
    n = x.shape[0]
    row = lax.broadcasted_iota(jnp.int32, (n, 1), 0)
    s = 1
    while s < n:
        if descending:
            x = x + jnp.where(row < n - s, pltpu.roll(x, n - s, 0), 0.0)
        else:
            x = x + jnp.where(row >= s, pltpu.roll(x, s, 0), 0.0)
        s *= 2
    return x


def _gla_scan_kernel(q_ref, k_ref, v_ref, g0_ref, g1_ref, o_ref, s_ref, *, n_ctx_chunks, n_chunks, nb):
    qscale = GLA_HK ** -0.5
    g_refs = (g0_ref, g1_ref)
    incl = (_tri(CHUNK, False, False), _tri(CHUNK, True, False))
    o_ref[...] = jnp.zeros_like(o_ref)

    def step(first):
        units = [(d, first[d] + (i if d == 0 else -i)) for i in range(nb) for d in range(2)]
        rows = [pl.ds(pl.multiple_of(c * CHUNK, CHUNK), CHUNK) for _, c in units]
        q = [q_ref[0, r, :] for r in rows]
        k = [k_ref[0, r, :] for r in rows]
        v = [v_ref[0, r, :].astype(BF16) for r in rows]
        g = [g_refs[d][0, r, :] for (d, _), r in zip(units, rows)]
        bcum = [_cumsum_rows(gi, d == 1) for (d, _), gi in zip(units, g)]
        b_last_row = [jnp.sum(gi, axis=0, keepdims=True) for gi in g]
        dec_col = [jnp.exp(jnp.sum(gi.T, axis=1, keepdims=True)) for gi in g]
        q_e = [(qi * jnp.exp(bi) * qscale).astype(BF16) for qi, bi in zip(q, bcum)]
        k_e = [(ki * jnp.exp(-bi)).astype(BF16) for ki, bi in zip(k, bcum)]
        k_dec_t = [(ki * jnp.exp(bl - bi)).T.astype(BF16) for ki, bl, bi in zip(k, b_last_row, bcum)]
        a = [jnp.where(incl[d], _dot_nt(qe, ke), 0.0).astype(BF16) for (d, _), qe, ke in zip(units, q_e, k_e)]
        o_intra = [_dot(ai, vi) for ai, vi in zip(a, v)]
        s_inc = [_dot(kt, vi) for kt, vi in zip(k_dec_t, v)]
        s = [s_ref[0], s_ref[1]]
        for i, (d, _) in enumerate(units):
            o_ref[0, rows[i], :] += o_intra[i] + _dot(q_e[i], s[d].astype(BF16))
            s[d] = dec_col[i] * s[d] + s_inc[i]
        s_ref[0] = s[0]
        s_ref[1] = s[1]

    s_ref[...] = jnp.zeros_like(s_ref)
    n_lat = n_chunks - n_ctx_chunks

    def ctx_body(i, carry):
        step((i * nb, n_ctx_chunks - 1 - i * nb))
        return carry

    def lat_body(i, carry):
        step((n_ctx_chunks + i * nb, n_chunks - 1 - i * nb))
        return carry

    lax.fori_loop(0, n_ctx_chunks // nb, ctx_body, 0)
    lax.fori_loop(0, n_lat // nb, lat_body, 0)


def _gla_scan(p, gates, n_ctx, nb):
    b, t, _ = p.shape
    nkb = GLA_DK // GLA_HK
    return pl.pallas_call(
        functools.partial(_gla_scan_kernel, n_ctx_chunks=n_ctx // CHUNK, n_chunks=t // CHUNK, nb=nb),
        grid=(b, GLA_HEADS),
        in_specs=[
            pl.BlockSpec((1, t, GLA_HK), lambda i, h: (i, 0, h)),
            pl.BlockSpec((1, t, GLA_HK), lambda i, h: (i, 0, nkb + h)),
            pl.BlockSpec((1, t, GLA_HV), lambda i, h: (i, 0, (2 * GLA_DK) // GLA_HV + h)),
            pl.BlockSpec((1, t, GLA_HK), lambda i, h: (i, 0, h)),
            pl.BlockSpec((1, t, GLA_HK), lambda i, h: (i, 0, nkb + h)),
        ],
        out_specs=pl.BlockSpec((1, t, GLA_HV), lambda i, h: (i, 0, h)),
        out_shape=jax.ShapeDtypeStruct((b, t, GLA_DV), F32),
        scratch_shapes=[pltpu.VMEM((2, GLA_HK, GLA_HV), F32)],
        compiler_params=_cparams("parallel", "parallel"),
        name="gla_scan",
    )(p, p, p, gates, gates)


def _gla_gated_parts(o_ref, gt_ref, ng_ref):
    parts = []
    for h in range(GLA_HEADS):
        ls = slice(h * GLA_HV, (h + 1) * GLA_HV)
        ov = o_ref[0, :, ls]
        ov = ov * lax.rsqrt(jnp.mean(ov * ov, axis=-1, keepdims=True) + NORM_EPS) * ng_ref[...]
        gt = gt_ref[0, :, ls]
        parts.append((ov * (gt * _sigmoid(gt))).astype(BF16))
    return parts


def _gla_out_kernel(u_ref, o_ref, gt_ref, ng_ref, mx_ref, mz_ref, w_ref, out_ref, *, tm, n_ctx):
    row0 = pl.program_id(1) * tm
    parts = _gla_gated_parts(o_ref, gt_ref, ng_ref)
    out_ref[0] = _gated_residual(u_ref[0], parts, w_ref, mx_ref, mz_ref, row0, n_ctx)


def _gla_out(u, o, p, norm_g, mx, mz, w_o, n_ctx):
    b, t, d = u.shape
    tm = _pick_tile(t, ROW_TILE)
    gt_blk = (2 * GLA_DK + GLA_DV) // d
    return pl.pallas_call(
        functools.partial(_gla_out_kernel, tm=tm, n_ctx=n_ctx),
        grid=(b, t // tm),
        in_specs=[
            pl.BlockSpec((1, tm, d), lambda i, j: (i, j, 0)),
            pl.BlockSpec((1, tm, d), lambda i, j: (i, j, 0)),
            pl.BlockSpec((1, tm, d), lambda i, j: (i, j, gt_blk)),
            _const_spec((1, GLA_HV)),
            pl.BlockSpec((1, 6, d), lambda i, j: (i, 0, 0)),
            _const_spec((6, d)),
            _const_spec((d, d)),
        ],
        out_specs=pl.BlockSpec((1, tm, d), lambda i, j: (i, j, 0)),
        out_shape=jax.ShapeDtypeStruct((b, t, d), F32),
        compiler_params=_cparams("parallel", "parallel"),
        name="gla_out",
    )(u, o, p, norm_g.reshape(1, GLA_HV), mx, mz, w_o)


def _rw_proj_kernel(u_ref, up_ref, un_ref, mx_ref, mz_ref, g_ref, mix_ref, wrkv_ref, w1_ref, w2_ref,
                    w0_ref, a1_ref, a2_ref, a0_ref, g1_ref, g2_ref, ka_ref, rk_ref,
                    r_ref, k_ref, v_ref, gate_ref, lw_ref, a_ref, bonus_ref, *, tm, n_ctx, t_total):
    row0 = pl.program_id(1) * tm
    g = g_ref[...]
    mx = mx_ref[0]
    mz = mz_ref[...]
    h = _norm_mod(u_ref[0], g, mx, mz, row0, n_ctx, 0, 1)
    h_prev = _norm_mod(up_ref[0], g, mx, mz, row0 - 8, n_ctx, 0, 1)[7:8]
    h_next = _norm_mod(un_ref[0], g, mx, mz, row0 + tm, n_ctx, 0, 1)[0:1]
    idx = lax.broadcasted_iota(jnp.int32, (tm, 1), 0)
    rows = row0 + idx
    h_dn = jnp.where(idx == 0, h_prev, pltpu.roll(h, 1, 0))
    h_dn = jnp.where((rows == 0) | (rows == n_ctx), 0.0, h_dn)
    h_up = jnp.where(idx == tm - 1, h_next, pltpu.roll(h, tm - 1, 0))
    h_up = jnp.where((rows == n_ctx - 1) | (rows == t_total - 1), 0.0, h_up)
    xx = 0.5 * (h_dn + h_up) - h
    xr, xw, xk, xv, xa, xg = ((h + xx * mix_ref[m:m + 1]).astype(BF16) for m in range(6))
    r = _dot(xr, wrkv_ref[0])
    k = _dot(xk, wrkv_ref[1])
    v = _dot(xv, wrkv_ref[2])
    r_ref[0] = r
    k_ref[0] = k
    v_ref[0] = v
    gate_ref[0] = _dot(_sigmoid(_dot(xg, g1_ref[...])).astype(BF16), g2_ref[...])
    tw = jnp.tanh(_dot(xw, w1_ref[...])).astype(BF16)
    lw_ref[0] = -math.exp(-0.5) * _sigmoid(w0_ref[...] + _dot(tw, w2_ref[...]))
    ta = _dot(xa, a1_ref[...]).astype(BF16)
    a = _sigmoid(a0_ref[...] + _dot(ta, a2_ref[...]))
    a_ref[0] = a
    lane_lo = lax.broadcasted_iota(jnp.int32, (1, LANES), 1) < RW_HEAD
    d = D_MODEL
    for p in range(d // LANES):
        ls = slice(p * LANES, (p + 1) * LANES)
        kd_sum = k[:, ls] * (2.0 + (a[:, ls] + a[:, d + p * LANES:d + (p + 1) * LANES] - 2.0) * ka_ref[:, ls])
        bonus_ref[0, :, ls] = _pair_sum(r[:, ls] * kd_sum * rk_ref[:, ls], lane_lo) * v[:, ls]


def _rw_proj(u, mx, mz, g, wts, n_ctx):
    b, t, d = u.shape
    tm = _pick_tile(t, NARROW_TILE)
    nb8 = t // 8
    tb = tm // 8
    full = lambda n: jax.ShapeDtypeStruct((b, t, n), F32)
    row_spec = lambda n: pl.BlockSpec((1, tm, n), lambda i, j: (i, j, 0))
    return pl.pallas_call(
        functools.partial(_rw_proj_kernel, tm=tm, n_ctx=n_ctx, t_total=t),
        grid=(b, t // tm),
        in_specs=[
            row_spec(d),
            pl.BlockSpec((1, 8, d), lambda i, j: (i, jnp.maximum(j * tb - 1, 0), 0)),
            pl.BlockSpec((1, 8, d), lambda i, j: (i, jnp.minimum((j + 1) * tb, nb8 - 1), 0)),
            pl.BlockSpec((1, 6, d), lambda i, j: (i, 0, 0)),
            _const_spec((6, d)),
            _const_spec((1, d)),
            _const_spec((6, d)),
            _const_spec((3, d, d)),
            _const_spec((d, 2 * RW_LORA)),
            _const_spec((2 * RW_LORA, 2 * d)),
            _const_spec((1, 2 * d)),
            _const_spec((d, 2 * RW_LORA)),
            _const_spec((2 * RW_LORA, 2 * d)),
            _const_spec((1, 2 * d)),
            _const_spec((d, 2 * LANES)),
            _const_spec((2 * LANES, d)),
            _const_spec((1, d)),
            _const_spec((1, d)),
        ],
        out_specs=[row_spec(d), row_spec(d), row_spec(d), row_spec(d), row_spec(2 * d), row_spec(2 * d),
                   row_spec(d)],
        out_shape=[full(d), full(d), full(d), full(d), full(2 * d), full(2 * d), full(d)],
        compiler_params=_cparams("parallel", "parallel"),
        name="rw_proj",
    )(u, u, u, mx, mz, g.reshape(1, d), wts["mix"], wts["w_rkv"], wts["w1"], wts["w2"], wts["w0"],
      wts["a1"], wts["a2"], wts["a0"], wts["g1"], wts["g2"], wts["k_a"], wts["r_k"])


def _rw_pair_terms(units, side_work=()):
    C = CHUNK
    lane_lo = lax.broadcasted_iota(jnp.int32, (1, LANES), 1) < RW_HEAD
    row2 = lax.broadcasted_iota(jnp.int32, (C, LANES), 0)
    col2 = lax.broadcasted_iota(jnp.int32, (C, LANES), 1) & (C - 1)
    strict2 = (row2 > col2, row2 < col2)
    incl2 = (row2 >= col2, row2 <= col2)
    eye2 = jnp.where(row2 == col2, 1.0, 0.0)
    nu = len(units)

    def bd(x):
        zero = jnp.zeros_like(x)
        return jnp.concatenate([jnp.where(lane_lo, x, zero), jnp.where(lane_lo, zero, x)], axis=0)

    def hilo(x):
        hi = x.astype(BF16)
        return hi, (x - hi.astype(F32)).astype(BF16)

    side = iter(side_work)

    def stage_done():
        thunk = next(side, None)
        if thunk is not None:
            thunk()

    pre = []
    for r2, k2, v2, lw2, a2, desc, kkw, kaw in units:
        d = int(desc)
        kk = k2 * kkw
        kk = kk * lax.rsqrt(jnp.maximum(_pair_sum(kk * kk, lane_lo), 1e-24))
        kd = k2 * (1.0 + (a2 - 1.0) * kaw)
        bvec = kk * a2
        cum = _cumsum_rows(lw2, desc)
        tot = jnp.sum(lw2, axis=0, keepdims=True)
        e_neg = jnp.exp(-cum)
        e_rem = jnp.exp(tot - cum)
        pre.append(dict(
            d=d, tot=tot, v_b=v2.astype(BF16),
            a_t=-kk * jnp.exp(cum - lw2), r_t=r2 * jnp.exp(cum),
            b_h=(bvec * e_neg).astype(BF16), k_h=(kd * e_neg).astype(BF16),
            b_rem_t=(bvec * e_rem).T.astype(BF16), k_rem_t=(kd * e_rem).T.astype(BF16)))

    ar = [jnp.concatenate([p["a_t"], p["r_t"]], axis=0).astype(BF16) for p in pre]
    p_bk = [_dot_nt(ar[u], jnp.concatenate([bd(pre[u]["b_h"]), bd(pre[u]["k_h"])], axis=0)) for u in range(nu)]
    l_ab = [jnp.where(strict2[pre[u]["d"]], p_bk[u][:C, :LANES], 0.0) for u in range(nu)]
    m_rb = [jnp.where(incl2[pre[u]["d"]], p_bk[u][C:, :LANES], 0.0).astype(BF16) for u in range(nu)]
    l_ak = [jnp.where(strict2[pre[u]["d"]], p_bk[u][:C, LANES:], 0.0).astype(BF16) for u in range(nu)]
    m_rk = [jnp.where(incl2[pre[u]["d"]], p_bk[u][C:, LANES:], 0.0).astype(BF16) for u in range(nu)]
    stage_done()

    xo = l_ab
    l_b16 = [l.astype(BF16) for l in l_ab]
    pw = [_dot(l, bd(l)) for l in l_b16]
    for _ in range(3):
        p_b16 = [p.astype(BF16) for p in pw]
        ra = [_dot(jnp.concatenate([p, x.astype(BF16)], axis=0), bd(p)) for p, x in zip(p_b16, xo)]
        xo = [x + p + a[C:] for x, p, a in zip(xo, pw, ra)]
        pw = [a[:C] for a in ra]
        stage_done()
    xo = [x + p + _dot(x.astype(BF16), bd(p.astype(BF16))) for x, p in zip(xo, pw)]
    l_hl = [hilo(l) for l in l_ab]
    x_hl = [hilo(x) for x in xo]
    lx = [_dot(jnp.concatenate([lh, ll], axis=0), bd(xh)) for (lh, ll), (xh, _) in zip(l_hl, x_hl)]
    lxb = [_dot(lh, bd(xlo)) for (lh, _), (_, xlo) in zip(l_hl, x_hl)]
    res = [(l - x) + (a[:C] + a[C:] + b) for l, x, a, b in zip(l_ab, xo, lx, lxb)]
    stage_done()
    xo = [x + r + _dot(xh, bd(r.astype(BF16))) for x, r, (xh, _) in zip(xo, res, x_hl)]
    t_inv = [(eye2 + x).astype(BF16) for x in xo]

    lvyk = [_dot(jnp.concatenate([l_ak[u], m_rk[u]], axis=0), bd(pre[u]["v_b"])) for u in range(nu)]
    kv = [_dot(pre[u]["k_rem_t"], pre[u]["v_b"]) for u in range(nu)]
    tw = [_dot(t_inv[u], jnp.concatenate([bd(ar[u][:C]), bd(lvyk[u][:C].astype(BF16))], axis=1))
          for u in range(nu)]
    qy = [_dot(m_rb[u], jnp.concatenate([bd(tw[u][:, :LANES].astype(BF16)),
                                         bd(tw[u][:, LANES:].astype(BF16))], axis=1)) for u in range(nu)]
    gh = [_dot(pre[u]["b_rem_t"], tw[u].astype(BF16)) for u in range(nu)]

    r = lax.broadcasted_iota(jnp.int32, (LANES, LANES), 0)
    c = lax.broadcasted_iota(jnp.int32, (LANES, LANES), 1)
    same_head = (r < RW_HEAD) == (c < RW_HEAD)
    out = []
    for u in range(nu):
        qp = pre[u]["r_t"] + qy[u][:, :LANES]
        y0 = qy[u][:, LANES:] + lvyk[u][C:]
        g_mat = jnp.where(same_head, gh[u][:, :LANES], 0.0) + jnp.where(r == c, jnp.exp(pre[u]["tot"]), 0.0)
        h_mat = jnp.where(same_head, gh[u][:, LANES:] + kv[u], 0.0)
        out.append((qp, y0, g_mat, h_mat))
    return out


def _rw_mix_kernel(r0_ref, k0_ref, v0_ref, lw0_ref, a0_ref, r1_ref, k1_ref, v1_ref, lw1_ref, a1_ref,
                   kk_ref, ka_ref, y0_ref, y1_ref, st_ref, qp_s, yc_s, g_s, h_s, *, nb, pp):
    def zero_scratch(_, carry):
        for ref in (st_ref, qp_s, yc_s, g_s, h_s):
            ref[...] = jnp.zeros_like(ref)
        return carry

    lax.fori_loop(0, jnp.where(pl.program_id(2) == 0, 1, 0), zero_scratch, 0)

    y_refs = (y0_ref, y1_ref)
    chains = [(d, p) for d in range(2) for p in range(pp)]
    states = {dp: st_ref[dp[0], dp[1]] for dp in chains}
    prev_terms = {(d, p, c): (qp_s[d, p, c], yc_s[d, p, c], g_s[d, p, c], h_s[d, p, c])
                  for d, p in chains for c in range(nb)}

    def link(ci):
        def emit():
            for d, p in chains:
                c = ci if d == 0 else nb - 1 - ci
                qp, yc, g_mat, h_mat = prev_terms[d, p, c]
                st_b = states[d, p].astype(BF16)
                y_refs[d][0, c * CHUNK:(c + 1) * CHUNK, p * LANES:(p + 1) * LANES] = _dot(qp, st_b) + yc
                states[d, p] = _dot(g_mat, st_b) + h_mat
        return emit

    ins = ((r0_ref, k0_ref, v0_ref, lw0_ref, a0_ref), (r1_ref, k1_ref, v1_ref, lw1_ref, a1_ref))
    units = []
    index = []
    for p in range(pp):
        ls = slice(p * LANES, (p + 1) * LANES)
        for c in range(nb):
            rows = slice(c * CHUNK, (c + 1) * CHUNK)
            for d in range(2):
                units.append(tuple(ref[0, rows, ls] for ref in ins[d]) + (d == 1, kk_ref[:, ls], ka_ref[:, ls]))
                index.append((d, p, c))
    links = [link(ci) for ci in range(nb)]
    assert nb <= 5, "one recurrence link per stage boundary of _rw_pair_terms"
    results = _rw_pair_terms(units, side_work=links)
    for d, p in chains:
        st_ref[d, p] = states[d, p]
    for (d, p, c), (qp, y0, g_mat, h_mat) in zip(index, results):
        qp_s[d, p, c] = qp.astype(BF16)
        yc_s[d, p, c] = y0
        g_s[d, p, c] = g_mat.astype(BF16)
        h_s[d, p, c] = h_mat


def _rw_mix(r, k, v, lw, a, k_k, k_a, n_ctx, nb):
    b, t, d = r.shape
    npairs = d // LANES
    rows = nb * CHUNK
    nblk = t // rows
    ncb = n_ctx // rows

    def rev(j):
        return jnp.where(j < ncb, ncb - 1 - j, nblk - 1 - (j - ncb))

    cur = lambda j: jnp.minimum(j, nblk - 1)
    prev = lambda j: jnp.maximum(j - 1, 0)
    pp = 2
    pw = pp * LANES
    ng = npairs // pp
    fwd = lambda off, blk: pl.BlockSpec((1, rows, pw), lambda i, p, j: (i, blk(j), off + p))
    bwd = lambda off, blk: pl.BlockSpec((1, rows, pw), lambda i, p, j: (i, rev(blk(j)), off + p))
    par = pl.BlockSpec((1, pw), lambda i, p, j: (0, p))
    return pl.pallas_call(
        functools.partial(_rw_mix_kernel, nb=nb, pp=pp),
        grid=(b, ng, nblk + 1),
        in_specs=[fwd(0, cur), fwd(0, cur), fwd(0, cur), fwd(0, cur), fwd(0, cur),
                  bwd(0, cur), bwd(0, cur), bwd(0, cur), bwd(ng, cur), bwd(ng, cur), par, par],
        out_specs=[fwd(0, prev), bwd(0, prev)],
        out_shape=[jax.ShapeDtypeStruct((b, t, d), F32), jax.ShapeDtypeStruct((b, t, d), F32)],
        scratch_shapes=[pltpu.VMEM((2, pp, LANES, LANES), F32),
                        pltpu.VMEM((2, pp, nb, CHUNK, LANES), BF16), pltpu.VMEM((2, pp, nb, CHUNK, LANES), F32),
                        pltpu.VMEM((2, pp, nb, LANES, LANES), BF16), pltpu.VMEM((2, pp, nb, LANES, LANES), F32)],
        compiler_params=_cparams("parallel", "parallel", "arbitrary"),
        name="rw_mix",
    )(r, k, v, lw, a, r, k, v, lw, a, k_k.reshape(1, d), k_a.reshape(1, d))


def _rw_gated_parts(y0_ref, y1_ref, bonus_ref, gate_ref, lng_ref, lnb_ref):
    lane_lo = lax.broadcasted_iota(jnp.int32, (1, LANES), 1) < RW_HEAD
    parts = []
    for p in range(D_MODEL // LANES):
        ls = slice(p * LANES, (p + 1) * LANES)
        y = y0_ref[0, :, ls] + y1_ref[0, :, ls]
        mu = _pair_sum(y, lane_lo) * (1.0 / RW_HEAD)
        yc = y - mu
        var = _pair_sum(yc * yc, lane_lo) * (1.0 / RW_HEAD)
        yn = yc * lax.rsqrt(var + RW_GN_EPS) * lng_ref[:, ls] + lnb_ref[:, ls]
        parts.append(((yn + bonus_ref[0, :, ls]) * gate_ref[0, :, ls]).astype(BF16))
    return parts


def _rw_out_kernel(u_ref, y0_ref, y1_ref, bonus_ref, gate_ref, lng_ref, lnb_ref, mx_ref, mz_ref, w_ref,
                   out_ref, *, tm, n_ctx):
    row0 = pl.program_id(1) * tm
    parts = _rw_gated_parts(y0_ref, y1_ref, bonus_ref, gate_ref, lng_ref, lnb_ref)
    out_ref[0] = _gated_residual(u_ref[0], parts, w_ref, mx_ref, mz_ref, row0, n_ctx)


def _rw_out(u, y0, y1, bonus, gate, ln_g, ln_b, mx, mz, w_o, n_ctx):
    b, t, d = u.shape
    tm = _pick_tile(t, ROW_TILE)
    tok = pl.BlockSpec((1, tm, d), lambda i, j: (i, j, 0))
    vec = _const_spec((1, d))
    return pl.pallas_call(
        functools.partial(_rw_out_kernel, tm=tm, n_ctx=n_ctx),
        grid=(b, t // tm),
        in_specs=[tok] * 5 + [vec, vec, pl.BlockSpec((1, 6, d), lambda i, j: (i, 0, 0)),
                              _const_spec((6, d)), _const_spec((d, d))],
        out_specs=tok,
        out_shape=jax.ShapeDtypeStruct((b, t, d), F32),
        compiler_params=_cparams("parallel", "parallel"),
        name="rw_out",
    )(u, y0, y1, bonus, gate, ln_g.reshape(1, d), ln_b.reshape(1, d), mx, mz, w_o)


def _block_rows(w_pair):
    _, k, n = w_pair.shape
    z = jnp.zeros((k, n), w_pair.dtype)
    return jnp.concatenate([jnp.concatenate([w_pair[0], z], axis=1),
                            jnp.concatenate([z, w_pair[1]], axis=1)], axis=0)


def kernel(x, c, ctx, c_ctx, ada_w, ada_b, norm1_g, norm2_g, mlp_w1, mlp_w2, final_g, na_w_qkv, na_w_o, na_rpb, gla_w_in, gla_w_dec2, gla_b_dec, gla_norm_g, gla_w_o, rw_mix, rw_w_rkv, rw_w0, rw_w1, rw_w2, rw_a0, rw_a1, rw_a2, rw_g1, rw_g2, rw_k_k, rw_k_a, rw_r_k, rw_ln_g, rw_ln_b, rw_w_o):
    b, seq, d = x.shape
    n_ctx = ctx.shape[1]
    depth = ada_w.shape[0]
    assert d == D_MODEL and n_ctx % CHUNK == 0 and seq % CHUNK == 0

    entry_fused = seq % n_ctx == 0 and n_ctx % 16 == 0
    u = None if entry_fused else jnp.concatenate([ctx.astype(x.dtype), x], axis=1)

    rows = -(-(b + 1) // 8) * 8
    c_all = jnp.zeros((rows, d), F32).at[:b].set(c).at[b].set(c_ctx)
    tables = _ada_tables(c_all, ada_w, ada_b)
    mods_x = tables[:, :b].reshape(depth, b, 6, d)
    mods_z = tables[:, b].reshape(depth, 6, d)

    nb = max(n for n in (4, 2, 1) if (n_ctx // CHUNK) % n == 0 and (seq // CHUNK) % n == 0)

    for i in range(depth):
        kind, j = i % 3, i // 3
        last = i == depth - 1
        mx, mz = mods_x[i], mods_z[i]
        w1, w2 = mlp_w1[i].astype(BF16), mlp_w2[i].astype(BF16)
        tok = lambda tm: pl.BlockSpec((1, tm, d), lambda bi, ti: (bi, ti, 0))
        mlp_done = False
        if kind == 0:
            if u is None:
                qkv, u = _entry_proj(x, ctx, mx, mz, norm1_g[i], na_w_qkv[j].astype(BF16), BF16)
            else:
                qkv = _norm_proj(u, mx, mz, norm1_g[i], na_w_qkv[j].astype(BF16), n_ctx, BF16)
            o = _na_attention(qkv, _na_bias_table(na_rpb[j]), n_ctx, with_ctx=not last)
            if last:
                u = _na_out_latent(u, o, mx, mz, na_w_o[j].astype(BF16), n_ctx)
            else:
                u = _mixer_mlp("na", u, [o], [tok], mx, mz, na_w_o[j].astype(BF16), norm2_g[i], w1, w2, n_ctx)
                mlp_done = True
        elif kind == 1:
            n_main = 2 * GLA_DK + 2 * GLA_DV
            w_in = gla_w_in[j]
            w_lr = jnp.zeros((d, LANES), F32).at[:, :2 * GLA_LOW_RANK].set(w_in[:, n_main:])
            w_dec = jnp.zeros((LANES, 2 * GLA_DK), F32).at[:2 * GLA_LOW_RANK].set(_block_rows(gla_w_dec2[j]))
            p, gates = _gla_proj(u, mx, mz, norm1_g[i], w_in[:, :n_main].astype(BF16), w_lr.astype(BF16),
                                 w_dec.astype(BF16), gla_b_dec[j].reshape(1, 2 * GLA_DK), n_ctx)
            o = _gla_scan(p, gates, n_ctx, nb)
            if last:
                u = _gla_out(u, o, p, gla_norm_g[j], mx, mz, gla_w_o[j].astype(BF16), n_ctx)
            else:
                gt_blk = (2 * GLA_DK + GLA_DV) // d
                gt_spec = lambda tm: pl.BlockSpec((1, tm, d), lambda bi, ti: (bi, ti, gt_blk))
                u = _mixer_mlp("gla", u, [o, p, gla_norm_g[j].reshape(1, GLA_HV)],
                               [tok, gt_spec, lambda tm: _const_spec((1, GLA_HV))],
                               mx, mz, gla_w_o[j].astype(BF16), norm2_g[i], w1, w2, n_ctx)
                mlp_done = True
        else:
            g1 = jnp.zeros((d, 2 * LANES), F32).at[:, :RW_GATE_LORA].set(rw_g1[j])
            g2 = jnp.zeros((2 * LANES, d), F32).at[:RW_GATE_LORA].set(rw_g2[j])
            wts = dict(
                mix=rw_mix[j], w_rkv=rw_w_rkv[j].astype(BF16),
                w1=jnp.concatenate([rw_w1[j, 0], rw_w1[j, 1]], axis=1).astype(BF16),
                w2=_block_rows(rw_w2[j]).astype(BF16), w0=rw_w0[j].reshape(1, 2 * d),
                a1=jnp.concatenate([rw_a1[j, 0], rw_a1[j, 1]], axis=1).astype(BF16),
                a2=_block_rows(rw_a2[j]).astype(BF16), a0=rw_a0[j].reshape(1, 2 * d),
                g1=g1.astype(BF16), g2=g2.astype(BF16),
                k_a=rw_k_a[j].reshape(1, d), r_k=rw_r_k[j].reshape(1, d))
            r, k, v, gate, lw, a, bonus = _rw_proj(u, mx, mz, norm1_g[i], wts, n_ctx)
            y0, y1 = _rw_mix(r, k, v, lw, a, rw_k_k[j], rw_k_a[j], n_ctx, nb)
            if last:
                u = _rw_out(u, y0, y1, bonus, gate, rw_ln_g[j], rw_ln_b[j], mx, mz, rw_w_o[j].astype(BF16), n_ctx)
            else:
                vec = lambda tm: _const_spec((1, d))
                u = _mixer_mlp("rw", u, [y0, y1, bonus, gate, rw_ln_g[j].reshape(1, d), rw_ln_b[j].reshape(1, d)],
                               [tok, tok, tok, tok, vec, vec], mx, mz, rw_w_o[j].astype(BF16), norm2_g[i],
                               w1, w2, n_ctx)
                mlp_done = True
        if last and u.shape[1] != seq:
            u = u[:, n_ctx:]
        if not mlp_done:
            u = _mlp(u, mx, mz, norm2_g[i], w1, w2, final_g, 0 if last else n_ctx, final=last)
    return u
```

```python
import functools
import math

import jax
import jax.numpy as jnp
from jax import lax
from jax.experimental import pallas as pl
from jax.experimental.pallas import tpu as pltpu

F32 = jnp.float32
BF16 = jnp.bfloat16

D_MODEL = 1024
D_FF = 4 * D_MODEL
NORM_EPS = 1e-6
GRID_W = 64

NA_HEAD_DIM = 64
NA_HEADS = D_MODEL // NA_HEAD_DIM
NA_WIN_R = 8
NA_WIN_C = 16
NA_MASK = -1e30

GLA_HEADS = 4
GLA_DK = D_MODEL // 2
GLA_DV = D_MODEL
GLA_HK = GLA_DK // GLA_HEADS
GLA_HV = GLA_DV // GLA_HEADS
GLA_LOW_RANK = 16
GLA_GATE_NORM = 16.0
CHUNK = 64

RW_HEAD = 64
RW_GATE_LORA = 160
RW_LORA = 64
RW_GN_EPS = 64e-5

LANES = 128
VMEM_LIMIT = 56 * 1024 * 1024
ROW_TILE = 544
NARROW_TILE = 256


def _cparams(*sem):
    return pltpu.CompilerParams(dimension_semantics=sem, vmem_limit_bytes=VMEM_LIMIT)


def _pick_tile(n, target, mult=16):
    best = None
    for t in range(mult, min(n, target) + 1, mult):
        if n % t == 0:
            best = t
    assert best is not None, (n, target)
    return best


def _const_spec(shape):
    nd = len(shape)
    return pl.BlockSpec(shape, lambda *_: (0,) * nd, pipeline_mode=pl.Buffered(1))


def _dot(a, b):
    return jnp.dot(a, b, preferred_element_type=F32)


def _dot_nt(a, b):
    return lax.dot_general(a, b, (((1,), (1,)), ((), ())), preferred_element_type=F32)


def _sigmoid(x):
    return 0.5 + 0.5 * jnp.tanh(0.5 * x)


def _softplus(x):
    return jnp.maximum(x, 0.0) + jnp.log(1.0 + jnp.exp(-jnp.abs(x)))


def _rms(u, g):
    return u * lax.rsqrt(jnp.mean(u * u, axis=-1, keepdims=True) + NORM_EPS) * g


def _mod_rows(mx, mz, idx, row0, rows, n_ctx):
    vx = mx[idx:idx + 1]
    if n_ctx == 0:
        return vx
    if n_ctx % rows == 0:
        return jnp.where(row0 < n_ctx, mz[idx:idx + 1], vx)
    r = row0 + lax.broadcasted_iota(jnp.int32, (rows, 1), 0)
    return jnp.where(r < n_ctx, mz[idx:idx + 1], vx)


def _norm_mod(u, g, mx, mz, row0, n_ctx, i_shift, i_scale):
    rows = u.shape[0]
    shift = _mod_rows(mx, mz, i_shift, row0, rows, n_ctx)
    scale = _mod_rows(mx, mz, i_scale, row0, rows, n_ctx)
    return _rms(u, g) * (1.0 + scale) + shift


def _pair_sum(x, lane_lo):
    s0 = jnp.sum(jnp.where(lane_lo, x, 0.0), axis=-1, keepdims=True)
    s1 = jnp.sum(jnp.where(lane_lo, 0.0, x), axis=-1, keepdims=True)
    return jnp.where(lane_lo, s0, s1)


def _ada_kernel(c_ref, w_ref, b_ref, o_ref):
    c = c_ref[...]
    sc = (c * _sigmoid(c)).astype(BF16)
    o_ref[0] = _dot(sc, w_ref[0].astype(BF16)) + b_ref[0]


def _ada_tables(c_all, ada_w, ada_b):
    depth, d, n = ada_w.shape
    rows = c_all.shape[0]
    tn = 1536
    return pl.pallas_call(
        _ada_kernel,
        grid=(depth, n // tn),
        in_specs=[
            pl.BlockSpec((rows, d), lambda l, j: (0, 0)),
            pl.BlockSpec((1, d, tn), lambda l, j: (l, 0, j)),
            pl.BlockSpec((1, 1, tn), lambda l, j: (l, 0, j)),
        ],
        out_specs=pl.BlockSpec((1, rows, tn), lambda l, j: (l, 0, j)),
        out_shape=jax.ShapeDtypeStruct((depth, rows, n), F32),
        compiler_params=_cparams("parallel", "parallel"),
        name="ada_tables",
    )(c_all, ada_w, ada_b.reshape(depth, 1, n))


def _proj_kernel(u_ref, mx_ref, mz_ref, g_ref, w_ref, o_ref, *, tm, n_ctx, nsplit):
    t = pl.program_id(1)
    h = _norm_mod(u_ref[0], g_ref[...], mx_ref[0], mz_ref[...], t * tm, n_ctx, 0, 1).astype(BF16)
    n = w_ref.shape[1] // nsplit
    for s in range(nsplit):
        o_ref[0, :, s * n:(s + 1) * n] = _dot(h, w_ref[:, s * n:(s + 1) * n]).astype(o_ref.dtype)


def _norm_proj(u, mx, mz, g, w, n_ctx, out_dtype):
    b, t, d = u.shape
    n = w.shape[1]
    tm = _pick_tile(t, ROW_TILE)
    return pl.pallas_call(
        functools.partial(_proj_kernel, tm=tm, n_ctx=n_ctx, nsplit=n // D_MODEL),
        grid=(b, t // tm),
        in_specs=[
            pl.BlockSpec((1, tm, d), lambda i, j: (i, j, 0)),
            pl.BlockSpec((1, 6, d), lambda i, j: (i, 0, 0)),
            _const_spec((6, d)),
            _const_spec((1, d)),
            _const_spec((d, n)),
        ],
        out_specs=pl.BlockSpec((1, tm, n), lambda i, j: (i, j, 0)),
        out_shape=jax.ShapeDtypeStruct((b, t, n), out_dtype),
        compiler_params=_cparams("parallel", "parallel"),
        name="norm_proj",
    )(u, mx, mz, g.reshape(1, d), w)


def _entry_proj_kernel(ctx_ref, x_ref, mx_ref, mz_ref, g_ref, w_ref, o_ref, u_ref, *, tm, nsplit):
    j = pl.program_id(1)
    u = jnp.where(j == 0, ctx_ref[0].astype(F32), x_ref[0])
    u_ref[0] = u
    h = _norm_mod(u, g_ref[...], mx_ref[0], mz_ref[...], j * tm, tm, 0, 1).astype(BF16)
    n = w_ref.shape[1] // nsplit
    for s in range(nsplit):
        o_ref[0, :, s * n:(s + 1) * n] = _dot(h, w_ref[:, s * n:(s + 1) * n]).astype(o_ref.dtype)


def _entry_proj(x, ctx, mx, mz, g, w, out_dtype):
    b, seq, d = x.shape
    tm = ctx.shape[1]
    assert seq % tm == 0 and tm % 16 == 0
    n = w.shape[1]
    t = tm + seq
    return pl.pallas_call(
        functools.partial(_entry_proj_kernel, tm=tm, nsplit=n // D_MODEL),
        grid=(b, t // tm),
        in_specs=[
            pl.BlockSpec((1, tm, d), lambda i, j: (i, 0, 0)),
            pl.BlockSpec((1, tm, d), lambda i, j: (i, jnp.maximum(j - 1, 0), 0)),
            pl.BlockSpec((1, 6, d), lambda i, j: (i, 0, 0)),
            _const_spec((6, d)),
            _const_spec((1, d)),
            _const_spec((d, n)),
        ],
        out_specs=[pl.BlockSpec((1, tm, n), lambda i, j: (i, j, 0)),
                   pl.BlockSpec((1, tm, d), lambda i, j: (i, j, 0))],
        out_shape=[jax.ShapeDtypeStruct((b, t, n), out_dtype), jax.ShapeDtypeStruct((b, t, d), F32)],
        compiler_params=_cparams("parallel", "parallel"),
        name="entry_proj",
    )(ctx, x, mx, mz, g.reshape(1, d), w)


def _na_bias_table(rpb):
    w = GRID_W
    cols = jnp.arange(w)
    c_start = jnp.clip(cols - NA_WIN_C // 2, 0, w - NA_WIN_C)
    col_ok = (cols[None, :] >= c_start[:, None]) & (cols[None, :] < c_start[:, None] + NA_WIN_C)
    rpb = rpb.astype(F32)
    edge = w - NA_WIN_C
    t = jnp.concatenate([jnp.repeat(rpb[..., :1], edge, axis=-1), rpb,
                         jnp.repeat(rpb[..., -1:], edge + 1, axis=-1)], axis=-1)
    m = jnp.tile(t, w)[..., :w * (2 * w - 1)].reshape(t.shape[:-1] + (w, 2 * w - 1))[..., w - 1:]
    m = jnp.where(col_ok[None, None], m, NA_MASK)
    return jnp.concatenate([m[:, :-1], m[:, 1:]], axis=-1)


def _na_kernel(q_ref, k_ref, v_ref, bias_ref, o_ref, *, n_ctx, n_rows, with_ctx, rq):
    nctxb = n_ctx // GRID_W
    out_off = 0 if with_ctx else n_ctx
    scale = NA_HEAD_DIM ** -0.5
    assert scale == 0.125, "the score scale is folded into bf16 q, exact only for a power of two"
    lane_lo = lax.broadcasted_iota(jnp.int32, (1, LANES), 1) < NA_HEAD_DIM
    head_masks = (jnp.where(lane_lo, scale, 0.0).astype(BF16), jnp.where(lane_lo, 0.0, scale).astype(BF16))
    n_pairs = q_ref.shape[2] // LANES
    strip = NA_WIN_R * GRID_W
    jobs_idx = [(qi, p) for qi in range(rq) for p in range(n_pairs)]

    def q_rows(tok0, qi):
        return pl.ds(pl.multiple_of(tok0 + qi * GRID_W, GRID_W), GRID_W)

    def stacked_q(tok0, qi, p):
        q2 = q_ref[0, q_rows(tok0, qi), p * LANES:(p + 1) * LANES]
        return jnp.concatenate([q2 * head_masks[0], q2 * head_masks[1]], axis=0)

    def attend(tok0, jobs):
        scores = [[_dot_nt(q, k) if b is None else _dot_nt(q, k) + b
                   for k, b in zip(ks, bs)] for q, ks, _, bs in jobs]
        def lane_tiles(arrays):
            return [a[:, t:t + LANES] for a in arrays for t in range(0, a.shape[1], LANES)]

        tops = [functools.reduce(jnp.maximum, lane_tiles(sc)).max(axis=-1, keepdims=True) for sc in scores]
        exps = [[jnp.exp(s - m) for s in sc] for sc, m in zip(scores, tops)]
        dens = [functools.reduce(jnp.add, lane_tiles(es)).sum(axis=-1, keepdims=True) for es in exps]
        for (qi, p), (_, _, vs, _), es, den in zip(jobs_idx, jobs, exps, dens):
            o = _dot(es[0].astype(BF16), vs[0])
            for e, v in zip(es[1:], vs[1:]):
                o = o + _dot(e.astype(BF16), v)
            o = o / den
            o2 = jnp.where(lane_lo, o[:GRID_W], o[GRID_W:])
            o_ref[0, q_rows(tok0 - out_off, qi), p * LANES:(p + 1) * LANES] = o2.astype(o_ref.dtype)

    def latent_rows(step, carry):
        tok0 = n_ctx + step * (rq * GRID_W)
        jobs = []
        for qi, p in jobs_idx:
            ls = slice(p * LANES, (p + 1) * LANES)
            r = step * rq + qi
            r0 = jnp.clip(r - NA_WIN_R // 2, 0, n_rows - NA_WIN_R)
            start = pl.multiple_of(n_ctx + r0 * GRID_W, GRID_W)
            ri0 = NA_WIN_R - 1 - (r - r0)
            bias = jnp.concatenate(
                [jnp.concatenate([bias_ref[2 * p + hh, ri0 + 2 * m] for m in range(NA_WIN_R // 2)], axis=1)
                 for hh in range(2)], axis=0)
            jobs.append((stacked_q(tok0, qi, p),
                         [k_ref[0, pl.ds(start, strip), ls], k_ref[0, 0:n_ctx, ls]],
                         [v_ref[0, pl.ds(start, strip), ls], v_ref[0, 0:n_ctx, ls]],
                         [bias, None]))
        attend(tok0, jobs)
        return carry

    def context_rows(step, carry):
        tok0 = step * (rq * GRID_W)
        jobs = []
        for qi, p in jobs_idx:
            ls = slice(p * LANES, (p + 1) * LANES)
            jobs.append((stacked_q(tok0, qi, p), [k_ref[0, 0:n_ctx, ls]], [v_ref[0, 0:n_ctx, ls]], [None]))
        attend(tok0, jobs)
        return carry

    if with_ctx:
        lax.fori_loop(0, nctxb // rq, context_rows, 0)
    lax.fori_loop(0, n_rows // rq, latent_rows, 0)


def _na_attention(qkv, bias, n_ctx, with_ctx):
    b, t, _ = qkv.shape
    seq = t - n_ctx
    n_rows = seq // GRID_W
    assert seq % GRID_W == 0 and n_ctx % GRID_W == 0 and n_rows >= NA_WIN_R
    hw = 2 * LANES
    ng = D_MODEL // hw
    nctxb = n_ctx // GRID_W
    rq = 4 if nctxb % 4 == 0 and n_rows % 4 == 0 else 2
    assert nctxb % rq == 0 and n_rows % rq == 0
    t_out = t if with_ctx else seq

    return pl.pallas_call(
        functools.partial(_na_kernel, n_ctx=n_ctx, n_rows=n_rows, with_ctx=with_ctx, rq=rq),
        grid=(b, ng),
        in_specs=[
            pl.BlockSpec((1, t, hw), lambda i, g: (i, 0, g)),
            pl.BlockSpec((1, t, hw), lambda i, g: (i, 0, ng + g)),
            pl.BlockSpec((1, t, hw), lambda i, g: (i, 0, 2 * ng + g)),
            pl.BlockSpec((hw // NA_HEAD_DIM, 2 * NA_WIN_R - 2, GRID_W, LANES), lambda i, g: (g, 0, 0, 0)),
        ],
        out_specs=pl.BlockSpec((1, t_out, hw), lambda i, g: (i, 0, g)),
        out_shape=jax.ShapeDtypeStruct((b, t_out, D_MODEL), BF16),
        compiler_params=_cparams("parallel", "parallel"),
        name="na_attention",
    )(qkv, qkv, qkv, bias)


def _gated_residual(u, o_bf16_parts, w_ref, mx_ref, mz_ref, row0, n_ctx):
    acc = None
    k0 = 0
    for part in o_bf16_parts:
        kw = part.shape[1]
        term = _dot(part, w_ref[k0:k0 + kw, :])
        acc = term if acc is None else acc + term
        k0 += kw
    gate = _mod_rows(mx_ref[0], mz_ref[...], 2, row0, u.shape[0], n_ctx)
    return u + gate * acc


def _na_out_kernel(u_ref, o_ref, mx_ref, mz_ref, w_ref, out_ref, *, tm, n_ctx, blk_off):
    row0 = (pl.program_id(1) + blk_off) * tm
    out_ref[0] = _gated_residual(u_ref[0], [o_ref[0]], w_ref, mx_ref, mz_ref, row0, n_ctx)


def _na_out_latent(u, o, mx, mz, w_o, n_ctx):
    b, t, d = u.shape
    tm = _pick_tile(n_ctx, NARROW_TILE)
    assert (t - n_ctx) % tm == 0
    blk_off = n_ctx // tm
    nt = (t - n_ctx) // tm
    return pl.pallas_call(
        functools.partial(_na_out_kernel, tm=tm, n_ctx=n_ctx, blk_off=blk_off),
        grid=(b, nt),
        in_specs=[
            pl.BlockSpec((1, tm, d), lambda i, j: (i, j + blk_off, 0)),
            pl.BlockSpec((1, tm, d), lambda i, j: (i, j, 0)),
            pl.BlockSpec((1, 6, d), lambda i, j: (i, 0, 0)),
            _const_spec((6, d)),
            _const_spec((d, d)),
        ],
        out_specs=pl.BlockSpec((1, tm, d), lambda i, j: (i, j, 0)),
        out_shape=jax.ShapeDtypeStruct((b, nt * tm, d), F32),
        compiler_params=_cparams("parallel", "parallel"),
        name="na_out",
    )(u, o, mx, mz, w_o)


MLP_FC = 1024


def _mlp_apply(u, mx, mz, g, w1_ref, w2_ref, row0, n_ctx):
    h = _norm_mod(u, g, mx, mz, row0, n_ctx, 3, 4).astype(BF16)
    acc = None
    for c in range(D_FF // MLP_FC):
        a = _dot(h, w1_ref[:, c * MLP_FC:(c + 1) * MLP_FC])
        a = jnp.square(jnp.maximum(a, 0.0)).astype(BF16)
        term = _dot(a, w2_ref[c * MLP_FC:(c + 1) * MLP_FC, :])
        acc = term if acc is None else acc + term
    return u + _mod_rows(mx, mz, 5, row0, u.shape[0], n_ctx) * acc


def _mlp_kernel(u_ref, mx_ref, mz_ref, g_ref, w1_ref, w2_ref, fg_ref, out_ref, *, tm, n_ctx, final):
    row0 = pl.program_id(1) * tm
    y = _mlp_apply(u_ref[0], mx_ref[0], mz_ref[...], g_ref[...], w1_ref, w2_ref, row0, n_ctx)
    if final:
        y = _rms(y, fg_ref[...])
    out_ref[0] = y


def _mixer_mlp_kernel(*refs, kind, tm, n_ctx):
    u_ref, mixer_refs = refs[0], refs[1:-7]
    mx_ref, mz_ref, wo_ref, g_ref, w1_ref, w2_ref, out_ref = refs[-7:]
    if kind == "na":
        parts = [mixer_refs[0][0]]
    elif kind == "gla":
        parts = _gla_gated_parts(*mixer_refs)
    else:
        parts = _rw_gated_parts(*mixer_refs)
    row0 = pl.program_id(1) * tm
    u = _gated_residual(u_ref[0], parts, wo_ref, mx_ref, mz_ref, row0, n_ctx)
    out_ref[0] = _mlp_apply(u, mx_ref[0], mz_ref[...], g_ref[...], w1_ref, w2_ref, row0, n_ctx)


def _mixer_mlp(kind, u, mixer_inputs, mixer_specs, mx, mz, w_o, g, w1, w2, n_ctx):
    b, t, d = u.shape
    tm = _pick_tile(t, ROW_TILE)
    tok = pl.BlockSpec((1, tm, d), lambda i, j: (i, j, 0))
    return pl.pallas_call(
        functools.partial(_mixer_mlp_kernel, kind=kind, tm=tm, n_ctx=n_ctx),
        grid=(b, t // tm),
        in_specs=[tok] + [spec(tm) for spec in mixer_specs] + [
            pl.BlockSpec((1, 6, d), lambda i, j: (i, 0, 0)),
            _const_spec((6, d)),
            _const_spec((d, d)),
            _const_spec((1, d)),
            _const_spec((d, D_FF)),
            _const_spec((D_FF, d)),
        ],
        out_specs=tok,
        out_shape=jax.ShapeDtypeStruct((b, t, d), F32),
        compiler_params=_cparams("parallel", "parallel"),
        name=kind + "_out_mlp",
    )(u, *mixer_inputs, mx, mz, w_o, g.reshape(1, d), w1, w2)


def _mlp(u, mx, mz, g, w1, w2, final_g, n_ctx, final):
    b, t, d = u.shape
    tm = _pick_tile(t, ROW_TILE)
    return pl.pallas_call(
        functools.partial(_mlp_kernel, tm=tm, n_ctx=n_ctx, final=final),
        grid=(b, t // tm),
        in_specs=[
            pl.BlockSpec((1, tm, d), lambda i, j: (i, j, 0)),
            pl.BlockSpec((1, 6, d), lambda i, j: (i, 0, 0)),
            _const_spec((6, d)),
            _const_spec((1, d)),
            _const_spec((d, D_FF)),
            _const_spec((D_FF, d)),
            _const_spec((1, d)),
        ],
        out_specs=pl.BlockSpec((1, tm, d), lambda i, j: (i, j, 0)),
        out_shape=jax.ShapeDtypeStruct((b, t, d), F32),
        compiler_params=_cparams("parallel", "parallel"),
        name="mlp",
    )(u, mx, mz, g.reshape(1, d), w1, w2, final_g.reshape(1, d))


def _gla_proj_kernel(u_ref, mx_ref, mz_ref, g_ref, w_ref, wlr_ref, wdec_ref, bdec_ref,
                     p_ref, gate_ref, *, tm, n_ctx):
    t = pl.program_id(1)
    h = _norm_mod(u_ref[0], g_ref[...], mx_ref[0], mz_ref[...], t * tm, n_ctx, 0, 1).astype(BF16)
    n = D_MODEL
    for s in range(w_ref.shape[1] // n):
        p_ref[0, :, s * n:(s + 1) * n] = _dot(h, w_ref[:, s * n:(s + 1) * n])
    lr = _dot(h, wlr_ref[...]).astype(BF16)
    z = _dot(lr, wdec_ref[...]) + bdec_ref[...]
    gate_ref[0] = -_softplus(-z) * (1.0 / GLA_GATE_NORM)


def _gla_proj(u, mx, mz, g, w_main, w_lr, w_dec, b_dec, n_ctx):
    b, t, d = u.shape
    n = w_main.shape[1]
    tm = _pick_tile(t, ROW_TILE)
    return pl.pallas_call(
        functools.partial(_gla_proj_kernel, tm=tm, n_ctx=n_ctx),
        grid=(b, t // tm),
        in_specs=[
            pl.BlockSpec((1, tm, d), lambda i, j: (i, j, 0)),
            pl.BlockSpec((1, 6, d), lambda i, j: (i, 0, 0)),
            _const_spec((6, d)),
            _const_spec((1, d)),
            _const_spec((d, n)),
            _const_spec((d, LANES)),
            _const_spec((LANES, 2 * GLA_DK)),
            _const_spec((1, 2 * GLA_DK)),
        ],
        out_specs=[
            pl.BlockSpec((1, tm, n), lambda i, j: (i, j, 0)),
            pl.BlockSpec((1, tm, 2 * GLA_DK), lambda i, j: (i, j, 0)),
        ],
        out_shape=[
            jax.ShapeDtypeStruct((b, t, n), F32),
            jax.ShapeDtypeStruct((b, t, 2 * GLA_DK), F32),
        ],
        compiler_params=_cparams("parallel", "parallel"),
        name="gla_proj",
    )(u, mx, mz, g.reshape(1, d), w_main, w_lr, w_dec, b_dec)


def _tri(n, upper, strict):
    r = lax.broadcasted_iota(jnp.int32, (n, n), 0)
    c = lax.broadcasted_iota(jnp.int32, (n, n), 1)
    if upper:
        return (r < c) if strict else (r <= c)
    return (r > c) if strict else (r >= c)


def _cumsum_rows(x, descending):
    n = x.shape[0]
    row = lax.broadcasted_iota(jnp.int32, (n, 1), 0)
    s = 1
    while s < n:
        if descending:
            x = x + jnp.where(row < n - s, pltpu.roll(x, n - s, 0), 0.0)
        else:
            x = x + jnp.where(row >= s, pltpu.roll(x, s, 0), 0.0)
        s *= 2
    return x


def _gla_scan_kernel(q_ref, k_ref, v_ref, g0_ref, g1_ref, o_ref, s_ref, *, n_ctx_chunks, n_chunks, nb):
    qscale = GLA_HK ** -0.5
    g_refs = (g0_ref, g1_ref)
    incl = (_tri(CHUNK, False, False), _tri(CHUNK, True, False))
    o_ref[...] = jnp.zeros_like(o_ref)

    def step(first):
        units = [(d, first[d] + (i if d == 0 else -i)) for i in range(nb) for d in range(2)]
        rows = [pl.ds(pl.multiple_of(c * CHUNK, CHUNK), CHUNK) for _, c in units]
        q = [q_ref[0, r, :] for r in rows]
        k = [k_ref[0, r, :] for r in rows]
        v = [v_ref[0, r, :].astype(BF16) for r in rows]
        g = [g_refs[d][0, r, :] for (d, _), r in zip(units, rows)]
        bcum = [_cumsum_rows(gi, d == 1) for (d, _), gi in zip(units, g)]
        b_last_row = [jnp.sum(gi, axis=0, keepdims=True) for gi in g]
        dec_col = [jnp.exp(jnp.sum(gi.T, axis=1, keepdims=True)) for gi in g]
        q_e = [(qi * jnp.exp(bi) * qscale).astype(BF16) for qi, bi in zip(q, bcum)]
        k_e = [(ki * jnp.exp(-bi)).astype(BF16) for ki, bi in zip(k, bcum)]
        k_dec_t = [(ki * jnp.exp(bl - bi)).T.astype(BF16) for ki, bl, bi in zip(k, b_last_row, bcum)]
        a = [jnp.where(incl[d], _dot_nt(qe, ke), 0.0).astype(BF16) for (d, _), qe, ke in zip(units, q_e, k_e)]
        o_intra = [_dot(ai, vi) for ai, vi in zip(a, v)]
        s_inc = [_dot(kt, vi) for kt, vi in zip(k_dec_t, v)]
        s = [s_ref[0], s_ref[1]]
        for i, (d, _) in enumerate(units):
            o_ref[0, rows[i], :] += o_intra[i] + _dot(q_e[i], s[d].astype(BF16))
            s[d] = dec_col[i] * s[d] + s_inc[i]
        s_ref[0] = s[0]
        s_ref[1] = s[1]

    s_ref[...] = jnp.zeros_like(s_ref)
    n_lat = n_chunks - n_ctx_chunks

    def ctx_body(i, carry):
        step((i * nb, n_ctx_chunks - 1 - i * nb))
        return carry

    def lat_body(i, carry):
        step((n_ctx_chunks + i * nb, n_chunks - 1 - i * nb))
        return carry

    lax.fori_loop(0, n_ctx_chunks // nb, ctx_body, 0)
    lax.fori_loop(0, n_lat // nb, lat_body, 0)


def _gla_scan(p, gates, n_ctx, nb):
    b, t, _ = p.shape
    nkb = GLA_DK // GLA_HK
    return pl.pallas_call(
        functools.partial(_gla_scan_kernel, n_ctx_chunks=n_ctx // CHUNK, n_chunks=t // CHUNK, nb=nb),
        grid=(b, GLA_HEADS),
        in_specs=[
            pl.BlockSpec((1, t, GLA_HK), lambda i, h: (i, 0, h)),
            pl.BlockSpec((1, t, GLA_HK), lambda i, h: (i, 0, nkb + h)),
            pl.BlockSpec((1, t, GLA_HV), lambda i, h: (i, 0, (2 * GLA_DK) // GLA_HV + h)),
            pl.BlockSpec((1, t, GLA_HK), lambda i, h: (i, 0, h)),
            pl.BlockSpec((1, t, GLA_HK), lambda i, h: (i, 0, nkb + h)),
        ],
        out_specs=pl.BlockSpec((1, t, GLA_HV), lambda i, h: (i, 0, h)),
        out_shape=jax.ShapeDtypeStruct((b, t, GLA_DV), F32),
        scratch_shapes=[pltpu.VMEM((2, GLA_HK, GLA_HV), F32)],
        compiler_params=_cparams("parallel", "parallel"),
        name="gla_scan",
    )(p, p, p, gates, gates)


def _gla_gated_parts(o_ref, gt_ref, ng_ref):
    parts = []
    for h in range(GLA_HEADS):
        ls = slice(h * GLA_HV, (h + 1) * GLA_HV)
        ov = o_ref[0, :, ls]
        ov = ov * lax.rsqrt(jnp.mean(ov * ov, axis=-1, keepdims=True) + NORM_EPS) * ng_ref[...]
        gt = gt_ref[0, :, ls]
        parts.append((ov * (gt * _sigmoid(gt))).astype(BF16))
    return parts


def _gla_out_kernel(u_ref, o_ref, gt_ref, ng_ref, mx_ref, mz_ref, w_ref, out_ref, *, tm, n_ctx):
    row0 = pl.program_id(1) * tm
    parts = _gla_gated_parts(o_ref, gt_ref, ng_ref)
    out_ref[0] = _gated_residual(u_ref[0], parts, w_ref, mx_ref, mz_ref, row0, n_ctx)


def _gla_out(u, o, p, norm_g, mx, mz, w_o, n_ctx):
    b, t, d = u.shape
    tm = _pick_tile(t, ROW_TILE)
    gt_blk = (2 * GLA_DK + GLA_DV) // d
    return pl.pallas_call(
        functools.partial(_gla_out_kernel, tm=tm, n_ctx=n_ctx),
        grid=(b, t // tm),
        in_specs=[
            pl.BlockSpec((1, tm, d), lambda i, j: (i, j, 0)),
            pl.BlockSpec((1, tm, d), lambda i, j: (i, j, 0)),
            pl.BlockSpec((1, tm, d), lambda i, j: (i, j, gt_blk)),
            _const_spec((1, GLA_HV)),
            pl.BlockSpec((1, 6, d), lambda i, j: (i, 0, 0)),
            _const_spec((6, d)),
            _const_spec((d, d)),
        ],
        out_specs=pl.BlockSpec((1, tm, d), lambda i, j: (i, j, 0)),
        out_shape=jax.ShapeDtypeStruct((b, t, d), F32),
        compiler_params=_cparams("parallel", "parallel"),
        name="gla_out",
    )(u, o, p, norm_g.reshape(1, GLA_HV), mx, mz, w_o)


def _rw_proj_kernel(u_ref, up_ref, un_ref, mx_ref, mz_ref, g_ref, mix_ref, wrkv_ref, w1_ref, w2_ref,
                    w0_ref, a1_ref, a2_ref, a0_ref, g1_ref, g2_ref, ka_ref, rk_ref,
                    r_ref, k_ref, v_ref, gate_ref, lw_ref, a_ref, bonus_ref, *, tm, n_ctx, t_total):
    row0 = pl.program_id(1) * tm
    g = g_ref[...]
    mx = mx_ref[0]
    mz = mz_ref[...]
    h = _norm_mod(u_ref[0], g, mx, mz, row0, n_ctx, 0, 1)
    h_prev = _norm_mod(up_ref[0], g, mx, mz, row0 - 8, n_ctx, 0, 1)[7:8]
    h_next = _norm_mod(un_ref[0], g, mx, mz, row0 + tm, n_ctx, 0, 1)[0:1]
    idx = lax.broadcasted_iota(jnp.int32, (tm, 1), 0)
    rows = row0 + idx
    h_dn = jnp.where(idx == 0, h_prev, pltpu.roll(h, 1, 0))
    h_dn = jnp.where((rows == 0) | (rows == n_ctx), 0.0, h_dn)
    h_up = jnp.where(idx == tm - 1, h_next, pltpu.roll(h, tm - 1, 0))
    h_up = jnp.where((rows == n_ctx - 1) | (rows == t_total - 1), 0.0, h_up)
    xx = 0.5 * (h_dn + h_up) - h
    xr, xw, xk, xv, xa, xg = ((h + xx * mix_ref[m:m + 1]).astype(BF16) for m in range(6))
    r = _dot(xr, wrkv_ref[0])
    k = _dot(xk, wrkv_ref[1])
    v = _dot(xv, wrkv_ref[2])
    r_ref[0] = r
    k_ref[0] = k
    v_ref[0] = v
    gate_ref[0] = _dot(_sigmoid(_dot(xg, g1_ref[...])).astype(BF16), g2_ref[...])
    tw = jnp.tanh(_dot(xw, w1_ref[...])).astype(BF16)
    lw_ref[0] = -math.exp(-0.5) * _sigmoid(w0_ref[...] + _dot(tw, w2_ref[...]))
    ta = _dot(xa, a1_ref[...]).astype(BF16)
    a = _sigmoid(a0_ref[...] + _dot(ta, a2_ref[...]))
    a_ref[0] = a
    lane_lo = lax.broadcasted_iota(jnp.int32, (1, LANES), 1) < RW_HEAD
    d = D_MODEL
    for p in range(d // LANES):
        ls = slice(p * LANES, (p + 1) * LANES)
        kd_sum = k[:, ls] * (2.0 + (a[:, ls] + a[:, d + p * LANES:d + (p + 1) * LANES] - 2.0) * ka_ref[:, ls])
        bonus_ref[0, :, ls] = _pair_sum(r[:, ls] * kd_sum * rk_ref[:, ls], lane_lo) * v[:, ls]


def _rw_proj(u, mx, mz, g, wts, n_ctx):
    b, t, d = u.shape
    tm = _pick_tile(t, NARROW_TILE)
    nb8 = t // 8
    tb = tm // 8
    full = lambda n: jax.ShapeDtypeStruct((b, t, n), F32)
    row_spec = lambda n: pl.BlockSpec((1, tm, n), lambda i, j: (i, j, 0))
    return pl.pallas_call(
        functools.partial(_rw_proj_kernel, tm=tm, n_ctx=n_ctx, t_total=t),
        grid=(b, t // tm),
        in_specs=[
            row_spec(d),
            pl.BlockSpec((1, 8, d), lambda i, j: (i, jnp.maximum(j * tb - 1, 0), 0)),
            pl.BlockSpec((1, 8, d), lambda i, j: (i, jnp.minimum((j + 1) * tb, nb8 - 1), 0)),
            pl.BlockSpec((1, 6, d), lambda i, j: (i, 0, 0)),
            _const_spec((6, d)),
            _const_spec((1, d)),
            _const_spec((6, d)),
            _const_spec((3, d, d)),
            _const_spec((d, 2 * RW_LORA)),
            _const_spec((2 * RW_LORA, 2 * d)),
            _const_spec((1, 2 * d)),
            _const_spec((d, 2 * RW_LORA)),
            _const_spec((2 * RW_LORA, 2 * d)),
            _const_spec((1, 2 * d)),
            _const_spec((d, 2 * LANES)),
            _const_spec((2 * LANES, d)),
            _const_spec((1, d)),
            _const_spec((1, d)),
        ],
        out_specs=[row_spec(d), row_spec(d), row_spec(d), row_spec(d), row_spec(2 * d), row_spec(2 * d),
                   row_spec(d)],
        out_shape=[full(d), full(d), full(d), full(d), full(2 * d), full(2 * d), full(d)],
        compiler_params=_cparams("parallel", "parallel"),
        name="rw_proj",
    )(u, u, u, mx, mz, g.reshape(1, d), wts["mix"], wts["w_rkv"], wts["w1"], wts["w2"], wts["w0"],
      wts["a1"], wts["a2"], wts["a0"], wts["g1"], wts["g2"], wts["k_a"], wts["r_k"])


def _rw_pair_terms(units, side_work=()):
    C = CHUNK
    lane_lo = lax.broadcasted_iota(jnp.int32, (1, LANES), 1) < RW_HEAD
    row2 = lax.broadcasted_iota(jnp.int32, (C, LANES), 0)
    col2 = lax.broadcasted_iota(jnp.int32, (C, LANES), 1) & (C - 1)
    strict2 = (row2 > col2, row2 < col2)
    incl2 = (row2 >= col2, row2 <= col2)
    eye2 = jnp.where(row2 == col2, 1.0, 0.0)
    nu = len(units)

    def bd(x):
        zero = jnp.zeros_like(x)
        return jnp.concatenate([jnp.where(lane_lo, x, zero), jnp.where(lane_lo, zero, x)], axis=0)

    def hilo(x):
        hi = x.astype(BF16)
        return hi, (x - hi.astype(F32)).astype(BF16)

    side = iter(side_work)

    def stage_done():
        thunk = next(side, None)
        if thunk is not None:
            thunk()

    pre = []
    for r2, k2, v2, lw2, a2, desc, kkw, kaw in units:
        d = int(desc)
        kk = k2 * kkw
        kk = kk * lax.rsqrt(jnp.maximum(_pair_sum(kk * kk, lane_lo), 1e-24))
        kd = k2 * (1.0 + (a2 - 1.0) * kaw)
        bvec = kk * a2
        cum = _cumsum_rows(lw2, desc)
        tot = jnp.sum(lw2, axis=0, keepdims=True)
        e_neg = jnp.exp(-cum)
        e_rem = jnp.exp(tot - cum)
        pre.append(dict(
            d=d, tot=tot, v_b=v2.astype(BF16),
            a_t=-kk * jnp.exp(cum - lw2), r_t=r2 * jnp.exp(cum),
            b_h=(bvec * e_neg).astype(BF16), k_h=(kd * e_neg).astype(BF16),
            b_rem_t=(bvec * e_rem).T.astype(BF16), k_rem_t=(kd * e_rem).T.astype(BF16)))

    ar = [jnp.concatenate([p["a_t"], p["r_t"]], axis=0).astype(BF16) for p in pre]
    p_bk = [_dot_nt(ar[u], jnp.concatenate([bd(pre[u]["b_h"]), bd(pre[u]["k_h"])], axis=0)) for u in range(nu)]
    l_ab = [jnp.where(strict2[pre[u]["d"]], p_bk[u][:C, :LANES], 0.0) for u in range(nu)]
    m_rb = [jnp.where(incl2[pre[u]["d"]], p_bk[u][C:, :LANES], 0.0).astype(BF16) for u in range(nu)]
    l_ak = [jnp.where(strict2[pre[u]["d"]], p_bk[u][:C, LANES:], 0.0).astype(BF16) for u in range(nu)]
    m_rk = [jnp.where(incl2[pre[u]["d"]], p_bk[u][C:, LANES:], 0.0).astype(BF16) for u in range(nu)]
    stage_done()

    blk = (row2 // 8) ^ (col2 // 8)
    ld = [jnp.where(blk == 0, l, 0.0) for l in l_ab]
    ld_b = [l.astype(BF16) for l in ld]
    p2 = [_dot(l, bd(l)) for l in ld_b]
    ra = [_dot(jnp.concatenate([p.astype(BF16), l], axis=0), bd(p.astype(BF16))) for p, l in zip(p2, ld_b)]
    xo = [l + p + a[C:] for l, p, a in zip(ld, p2, ra)]
    xi = [eye2 + x + a[:C] + _dot(x.astype(BF16), bd(a[:C].astype(BF16))) for x, a in zip(xo, ra)]
    stage_done()
    l_b16 = [l.astype(BF16) for l in l_ab]
    for shift in range(3):
        link_b = jnp.where((blk >> shift) == 1, 1.0, 0.0).astype(BF16)
        x_b = [x.astype(BF16) for x in xi]
        mx = [_dot(l * link_b, bd(x)).astype(BF16) for l, x in zip(l_b16, x_b)]
        xi = [x + _dot(xb, bd(y)) for x, xb, y in zip(xi, x_b, mx)]
        stage_done()
    xo = [x - eye2 for x in xi]
    l_hl = [hilo(l) for l in l_ab]
    x_hl = [hilo(x) for x in xo]
    lx = [_dot(jnp.concatenate([lh, ll], axis=0), bd(xh)) for (lh, ll), (xh, _) in zip(l_hl, x_hl)]
    lxb = [_dot(lh, bd(xlo)) for (lh, _), (_, xlo) in zip(l_hl, x_hl)]
    res = [(l - x) + (a[:C] + a[C:] + b) for l, x, a, b in zip(l_ab, xo, lx, lxb)]
    stage_done()
    xo = [x + r + _dot(xh, bd(r.astype(BF16))) for x, r, (xh, _) in zip(xo, res, x_hl)]
    t_inv = [(eye2 + x).astype(BF16) for x in xo]

    lvyk = [_dot(jnp.concatenate([l_ak[u], m_rk[u]], axis=0), bd(pre[u]["v_b"])) for u in range(nu)]
    kv = [_dot(pre[u]["k_rem_t"], pre[u]["v_b"]) for u in range(nu)]
    tw = [_dot(t_inv[u], jnp.concatenate([bd(ar[u][:C]), bd(lvyk[u][:C].astype(BF16))], axis=1))
          for u in range(nu)]
    qy = [_dot(m_rb[u], jnp.concatenate([bd(tw[u][:, :LANES].astype(BF16)),
                                         bd(tw[u][:, LANES:].astype(BF16))], axis=1)) for u in range(nu)]
    gh = [_dot(pre[u]["b_rem_t"], tw[u].astype(BF16)) for u in range(nu)]

    r = lax.broadcasted_iota(jnp.int32, (LANES, LANES), 0)
    c = lax.broadcasted_iota(jnp.int32, (LANES, LANES), 1)
    same_head = (r < RW_HEAD) == (c < RW_HEAD)
    out = []
    for u in range(nu):
        qp = pre[u]["r_t"] + qy[u][:, :LANES]
        y0 = qy[u][:, LANES:] + lvyk[u][C:]
        g_mat = jnp.where(same_head, gh[u][:, :LANES], 0.0) + jnp.where(r == c, jnp.exp(pre[u]["tot"]), 0.0)
        h_mat = jnp.where(same_head, gh[u][:, LANES:] + kv[u], 0.0)
        out.append((qp, y0, g_mat, h_mat))
    return out


def _rw_mix_kernel(r0_ref, k0_ref, v0_ref, lw0_ref, a0_ref, r1_ref, k1_ref, v1_ref, lw1_ref, a1_ref,
                   kk_ref, ka_ref, y0_ref, y1_ref, st_ref, qp_s, yc_s, g_s, h_s, *, nb, pp):
    def zero_scratch(_, carry):
        for ref in (st_ref, qp_s, yc_s, g_s, h_s):
            ref[...] = jnp.zeros_like(ref)
        return carry

    lax.fori_loop(0, jnp.where(pl.program_id(2) == 0, 1, 0), zero_scratch, 0)

    y_refs = (y0_ref, y1_ref)
    chains = [(d, p) for d in range(2) for p in range(pp)]
    states = {dp: st_ref[dp[0], dp[1]] for dp in chains}
    prev_terms = {(d, p, c): (qp_s[d, p, c], yc_s[d, p, c], g_s[d, p, c], h_s[d, p, c])
                  for d, p in chains for c in range(nb)}

    def link(ci):
        def emit():
            for d, p in chains:
                c = ci if d == 0 else nb - 1 - ci
                qp, yc, g_mat, h_mat = prev_terms[d, p, c]
                st_b = states[d, p].astype(BF16)
                y_refs[d][0, c * CHUNK:(c + 1) * CHUNK, p * LANES:(p + 1) * LANES] = _dot(qp, st_b) + yc
                states[d, p] = _dot(g_mat, st_b) + h_mat
        return emit

    ins = ((r0_ref, k0_ref, v0_ref, lw0_ref, a0_ref), (r1_ref, k1_ref, v1_ref, lw1_ref, a1_ref))
    units = []
    index = []
    for p in range(pp):
        ls = slice(p * LANES, (p + 1) * LANES)
        for c in range(nb):
            rows = slice(c * CHUNK, (c + 1) * CHUNK)
            for d in range(2):
                units.append(tuple(ref[0, rows, ls] for ref in ins[d]) + (d == 1, kk_ref[:, ls], ka_ref[:, ls]))
                index.append((d, p, c))
    links = [link(ci) for ci in range(nb)]
    assert nb <= 5, "one recurrence link per stage boundary of _rw_pair_terms"
    results = _rw_pair_terms(units, side_work=links)
    for d, p in chains:
        st_ref[d, p] = states[d, p]
    for (d, p, c), (qp, y0, g_mat, h_mat) in zip(index, results):
        qp_s[d, p, c] = qp.astype(BF16)
        yc_s[d, p, c] = y0
        g_s[d, p, c] = g_mat.astype(BF16)
        h_s[d, p, c] = h_mat


def _rw_mix(r, k, v, lw, a, k_k, k_a, n_ctx, nb):
    b, t, d = r.shape
    npairs = d // LANES
    rows = nb * CHUNK
    nblk = t // rows
    ncb = n_ctx // rows

    def rev(j):
        return jnp.where(j < ncb, ncb - 1 - j, nblk - 1 - (j - ncb))

    cur = lambda j: jnp.minimum(j, nblk - 1)
    prev = lambda j: jnp.maximum(j - 1, 0)
    pp = 2
    pw = pp * LANES
    ng = npairs // pp
    fwd = lambda off, blk: pl.BlockSpec((1, rows, pw), lambda i, p, j: (i, blk(j), off + p))
    bwd = lambda off, blk: pl.BlockSpec((1, rows, pw), lambda i, p, j: (i, rev(blk(j)), off + p))
    par = pl.BlockSpec((1, pw), lambda i, p, j: (0, p))
    return pl.pallas_call(
        functools.partial(_rw_mix_kernel, nb=nb, pp=pp),
        grid=(b, ng, nblk + 1),
        in_specs=[fwd(0, cur), fwd(0, cur), fwd(0, cur), fwd(0, cur), fwd(0, cur),
                  bwd(0, cur), bwd(0, cur), bwd(0, cur), bwd(ng, cur), bwd(ng, cur), par, par],
        out_specs=[fwd(0, prev), bwd(0, prev)],
        out_shape=[jax.ShapeDtypeStruct((b, t, d), F32), jax.ShapeDtypeStruct((b, t, d), F32)],
        scratch_shapes=[pltpu.VMEM((2, pp, LANES, LANES), F32),
                        pltpu.VMEM((2, pp, nb, CHUNK, LANES), BF16), pltpu.VMEM((2, pp, nb, CHUNK, LANES), F32),
                        pltpu.VMEM((2, pp, nb, LANES, LANES), BF16), pltpu.VMEM((2, pp, nb, LANES, LANES), F32)],
        compiler_params=_cparams("parallel", "parallel", "arbitrary"),
        name="rw_mix",
    )(r, k, v, lw, a, r, k, v, lw, a, k_k.reshape(1, d), k_a.reshape(1, d))


def _rw_gated_parts(y0_ref, y1_ref, bonus_ref, gate_ref, lng_ref, lnb_ref):
    lane_lo = lax.broadcasted_iota(jnp.int32, (1, LANES), 1) < RW_HEAD
    parts = []
    for p in range(D_MODEL // LANES):
        ls = slice(p * LANES, (p + 1) * LANES)
        y = y0_ref[0, :, ls] + y1_ref[0, :, ls]
        mu = _pair_sum(y, lane_lo) * (1.0 / RW_HEAD)
        yc = y - mu
        var = _pair_sum(yc * yc, lane_lo) * (1.0 / RW_HEAD)
        yn = yc * lax.rsqrt(var + RW_GN_EPS) * lng_ref[:, ls] + lnb_ref[:, ls]
        parts.append(((yn + bonus_ref[0, :, ls]) * gate_ref[0, :, ls]).astype(BF16))
    return parts


def _rw_out_kernel(u_ref, y0_ref, y1_ref, bonus_ref, gate_ref, lng_ref, lnb_ref, mx_ref, mz_ref, w_ref,
                   out_ref, *, tm, n_ctx):
    row0 = pl.program_id(1) * tm
    parts = _rw_gated_parts(y0_ref, y1_ref, bonus_ref, gate_ref, lng_ref, lnb_ref)
    out_ref[0] = _gated_residual(u_ref[0], parts, w_ref, mx_ref, mz_ref, row0, n_ctx)


def _rw_out(u, y0, y1, bonus, gate, ln_g, ln_b, mx, mz, w_o, n_ctx):
    b, t, d = u.shape
    tm = _pick_tile(t, ROW_TILE)
    tok = pl.BlockSpec((1, tm, d), lambda i, j: (i, j, 0))
    vec = _const_spec((1, d))
    return pl.pallas_call(
        functools.partial(_rw_out_kernel, tm=tm, n_ctx=n_ctx),
        grid=(b, t // tm),
        in_specs=[tok] * 5 + [vec, vec, pl.BlockSpec((1, 6, d), lambda i, j: (i, 0, 0)),
                              _const_spec((6, d)), _const_spec((d, d))],
        out_specs=tok,
        out_shape=jax.ShapeDtypeStruct((b, t, d), F32),
        compiler_params=_cparams("parallel", "parallel"),
        name="rw_out",
    )(u, y0, y1, bonus, gate, ln_g.reshape(1, d), ln_b.reshape(1, d), mx, mz, w_o)


def _block_rows(w_pair):
    _, k, n = w_pair.shape
    z = jnp.zeros((k, n), w_pair.dtype)
    return jnp.concatenate([jnp.concatenate([w_pair[0], z], axis=1),
                            jnp.concatenate([z, w_pair[1]], axis=1)], axis=0)


def kernel(x, c, ctx, c_ctx, ada_w, ada_b, norm1_g, norm2_g, mlp_w1, mlp_w2, final_g, na_w_qkv, na_w_o, na_rpb, gla_w_in, gla_w_dec2, gla_b_dec, gla_norm_g, gla_w_o, rw_mix, rw_w_rkv, rw_w0, rw_w1, rw_w2, rw_a0, rw_a1, rw_a2, rw_g1, rw_g2, rw_k_k, rw_k_a, rw_r_k, rw_ln_g, rw_ln_b, rw_w_o):
    b, seq, d = x.shape
    n_ctx = ctx.shape[1]
    depth = ada_w.shape[0]
    assert d == D_MODEL and n_ctx % CHUNK == 0 and seq % CHUNK == 0

    entry_fused = seq % n_ctx == 0 and n_ctx % 16 == 0
    u = None if entry_fused else jnp.concatenate([ctx.astype(x.dtype), x], axis=1)

    rows = -(-(b + 1) // 8) * 8
    c_all = jnp.zeros((rows, d), F32).at[:b].set(c).at[b].set(c_ctx)
    tables = _ada_tables(c_all, ada_w, ada_b)
    mods_x = tables[:, :b].reshape(depth, b, 6, d)
    mods_z = tables[:, b].reshape(depth, 6, d)

    nb = max(n for n in (4, 2, 1) if (n_ctx // CHUNK) % n == 0 and (seq // CHUNK) % n == 0)

    for i in range(depth):
        kind, j = i % 3, i // 3
        last = i == depth - 1
        mx, mz = mods_x[i], mods_z[i]
        w1, w2 = mlp_w1[i].astype(BF16), mlp_w2[i].astype(BF16)
        tok = lambda tm: pl.BlockSpec((1, tm, d), lambda bi, ti: (bi, ti, 0))
        mlp_done = False
        if kind == 0:
            if u is None:
                qkv, u = _entry_proj(x, ctx, mx, mz, norm1_g[i], na_w_qkv[j].astype(BF16), BF16)
            else:
                qkv = _norm_proj(u, mx, mz, norm1_g[i], na_w_qkv[j].astype(BF16), n_ctx, BF16)
            o = _na_attention(qkv, _na_bias_table(na_rpb[j]), n_ctx, with_ctx=not last)
            if last:
                u = _na_out_latent(u, o, mx, mz, na_w_o[j].astype(BF16), n_ctx)
            else:
                u = _mixer_mlp("na", u, [o], [tok], mx, mz, na_w_o[j].astype(BF16), norm2_g[i], w1, w2, n_ctx)
                mlp_done = True
        elif kind == 1:
            n_main = 2 * GLA_DK + 2 * GLA_DV
            w_in = gla_w_in[j]
            w_lr = jnp.zeros((d, LANES), F32).at[:, :2 * GLA_LOW_RANK].set(w_in[:, n_main:])
            w_dec = jnp.zeros((LANES, 2 * GLA_DK), F32).at[:2 * GLA_LOW_RANK].set(_block_rows(gla_w_dec2[j]))
            p, gates = _gla_proj(u, mx, mz, norm1_g[i], w_in[:, :n_main].astype(BF16), w_lr.astype(BF16),
                                 w_dec.astype(BF16), gla_b_dec[j].reshape(1, 2 * GLA_DK), n_ctx)
            o = _gla_scan(p, gates, n_ctx, nb)
            if last:
                u = _gla_out(u, o, p, gla_norm_g[j], mx, mz, gla_w_o[j].astype(BF16), n_ctx)
            else:
                gt_blk = (2 * GLA_DK + GLA_DV) // d
                gt_spec = lambda tm: pl.BlockSpec((1, tm, d), lambda bi, ti: (bi, ti, gt_blk))
                u = _mixer_mlp("gla", u, [o, p, gla_norm_g[j].reshape(1, GLA_HV)],
                               [tok, gt_spec, lambda tm: _const_spec((1, GLA_HV))],
                               mx, mz, gla_w_o[j].astype(BF16), norm2_g[i], w1, w2, n_ctx)
                mlp_done = True
        else:
            g1 = jnp.zeros((d, 2 * LANES), F32).at[:, :RW_GATE_LORA].set(rw_g1[j])
            g2 = jnp.zeros((2 * LANES, d), F32).at[:RW_GATE_LORA].set(rw_g2[j])
            wts = dict(
                mix=rw_mix[j], w_rkv=rw_w_rkv[j].astype(BF16),
                w1=jnp.concatenate([rw_w1[j, 0], rw_w1[j, 1]], axis=1).astype(BF16),
                w2=_block_rows(rw_w2[j]).astype(BF16), w0=rw_w0[j].reshape(1, 2 * d),
                a1=jnp.concatenate([rw_a1[j, 0], rw_a1[j, 1]], axis=1).astype(BF16),
                a2=_block_rows(rw_a2[j]).astype(BF16), a0=rw_a0[j].reshape(1, 2 * d),
                g1=g1.astype(BF16), g2=g2.astype(BF16),
                k_a=rw_k_a[j].reshape(1, d), r_k=rw_r_k[j].reshape(1, d))
            r, k, v, gate, lw, a, bonus = _rw_proj(u, mx, mz, norm1_g[i], wts, n_ctx)
            y0, y1 = _rw_mix(r, k, v, lw, a, rw_k_k[j], rw_k_a[j], n_ctx, nb)
            if last:
                u = _rw_out(u, y0, y1, bonus, gate, rw_ln_g[j], rw_ln_b[j], mx, mz, rw_w_o[j].astype(BF16), n_ctx)
            else:
                vec = lambda tm: _const_spec((1, d))
                u = _mixer_mlp("rw", u, [y0, y1, bonus, gate, rw_ln_g[j].reshape(1, d), rw_ln_b[j].reshape(1, d)],
                               [tok, tok, tok, tok, vec, vec], mx, mz, rw_w_o[j].astype(BF16), norm2_g[i],
                               w1, w2, n_ctx)
                mlp_done = True
        if last and u.shape[1] != seq:
            u = u[:, n_ctx:]
        if not mlp_done:
            u = _mlp(u, mx, mz, norm2_g[i], w1, w2, final_g, 0 if last else n_ctx, final=last)
    return u
```

```python
import functools
import math

import jax
import jax.numpy as jnp
from jax import lax
from jax.experimental import pallas as pl
from jax.experimental.pallas import tpu as pltpu

F32 = jnp.float32
BF16 = jnp.bfloat16

D_MODEL = 1024
D_FF = 4 * D_MODEL
NORM_EPS = 1e-6
GRID_W = 64

NA_HEAD_DIM = 64
NA_HEADS = D_MODEL // NA_HEAD_DIM
NA_WIN_R = 8
NA_WIN_C = 16
NA_MASK = -1e30

GLA_HEADS = 4
GLA_DK = D_MODEL // 2
GLA_DV = D_MODEL
GLA_HK = GLA_DK // GLA_HEADS
GLA_HV = GLA_DV // GLA_HEADS
GLA_LOW_RANK = 16
GLA_GATE_NORM = 16.0
CHUNK = 64

RW_HEAD = 64
RW_GATE_LORA = 160
RW_LORA = 64
RW_GN_EPS = 64e-5

LANES = 128
VMEM_LIMIT = 56 * 1024 * 1024
ROW_TILE = 544
NARROW_TILE = 256


def _cparams(*sem):
    return pltpu.CompilerParams(dimension_semantics=sem, vmem_limit_bytes=VMEM_LIMIT)


def _pick_tile(n, target, mult=16):
    best = None
    for t in range(mult, min(n, target) + 1, mult):
        if n % t == 0:
            best = t
    assert best is not None, (n, target)
    return best


def _const_spec(shape):
    nd = len(shape)
    return pl.BlockSpec(shape, lambda *_: (0,) * nd, pipeline_mode=pl.Buffered(1))


def _dot(a, b):
    return jnp.dot(a, b, preferred_element_type=F32)


def _dot_nt(a, b):
    return lax.dot_general(a, b, (((1,), (1,)), ((), ())), preferred_element_type=F32)


def _sigmoid(x):
    return 0.5 + 0.5 * jnp.tanh(0.5 * x)


def _softplus(x):
    return jnp.maximum(x, 0.0) + jnp.log(1.0 + jnp.exp(-jnp.abs(x)))


def _rms(u, g):
    return u * lax.rsqrt(jnp.mean(u * u, axis=-1, keepdims=True) + NORM_EPS) * g


def _mod_rows(mx, mz, idx, row0, rows, n_ctx):
    vx = mx[idx:idx + 1]
    if n_ctx == 0:
        return vx
    if n_ctx % rows == 0:
        return jnp.where(row0 < n_ctx, mz[idx:idx + 1], vx)
    r = row0 + lax.broadcasted_iota(jnp.int32, (rows, 1), 0)
    return jnp.where(r < n_ctx, mz[idx:idx + 1], vx)


def _norm_mod(u, g, mx, mz, row0, n_ctx, i_shift, i_scale):
    rows = u.shape[0]
    shift = _mod_rows(mx, mz, i_shift, row0, rows, n_ctx)
    scale = _mod_rows(mx, mz, i_scale, row0, rows, n_ctx)
    return _rms(u, g) * (1.0 + scale) + shift


def _pair_sum(x, lane_lo):
    s0 = jnp.sum(jnp.where(lane_lo, x, 0.0), axis=-1, keepdims=True)
    s1 = jnp.sum(jnp.where(lane_lo, 0.0, x), axis=-1, keepdims=True)
    return jnp.where(lane_lo, s0, s1)


def _ada_kernel(c_ref, w_ref, b_ref, o_ref):
    c = c_ref[...]
    sc = (c * _sigmoid(c)).astype(BF16)
    o_ref[0] = _dot(sc, w_ref[0].astype(BF16)) + b_ref[0]


def _ada_tables(c_all, ada_w, ada_b):
    depth, d, n = ada_w.shape
    rows = c_all.shape[0]
    tn = 1536
    return pl.pallas_call(
        _ada_kernel,
        grid=(depth, n // tn),
        in_specs=[
            pl.BlockSpec((rows, d), lambda l, j: (0, 0)),
            pl.BlockSpec((1, d, tn), lambda l, j: (l, 0, j)),
            pl.BlockSpec((1, 1, tn), lambda l, j: (l, 0, j)),
        ],
        out_specs=pl.BlockSpec((1, rows, tn), lambda l, j: (l, 0, j)),
        out_shape=jax.ShapeDtypeStruct((depth, rows, n), F32),
        compiler_params=_cparams("parallel", "parallel"),
        name="ada_tables",
    )(c_all, ada_w, ada_b.reshape(depth, 1, n))


def _proj_kernel(u_ref, mx_ref, mz_ref, g_ref, w_ref, o_ref, *, tm, n_ctx, nsplit):
    t = pl.program_id(1)
    h = _norm_mod(u_ref[0], g_ref[...], mx_ref[0], mz_ref[...], t * tm, n_ctx, 0, 1).astype(BF16)
    n = w_ref.shape[1] // nsplit
    for s in range(nsplit):
        o_ref[0, :, s * n:(s + 1) * n] = _dot(h, w_ref[:, s * n:(s + 1) * n]).astype(o_ref.dtype)


def _norm_proj(u, mx, mz, g, w, n_ctx, out_dtype):
    b, t, d = u.shape
    n = w.shape[1]
    tm = _pick_tile(t, ROW_TILE)
    return pl.pallas_call(
        functools.partial(_proj_kernel, tm=tm, n_ctx=n_ctx, nsplit=n // D_MODEL),
        grid=(b, t // tm),
        in_specs=[
            pl.BlockSpec((1, tm, d), lambda i, j: (i, j, 0)),
            pl.BlockSpec((1, 6, d), lambda i, j: (i, 0, 0)),
            _const_spec((6, d)),
            _const_spec((1, d)),
            _const_spec((d, n)),
        ],
        out_specs=pl.BlockSpec((1, tm, n), lambda i, j: (i, j, 0)),
        out_shape=jax.ShapeDtypeStruct((b, t, n), out_dtype),
        compiler_params=_cparams("parallel", "parallel"),
        name="norm_proj",
    )(u, mx, mz, g.reshape(1, d), w)


def _entry_proj_kernel(ctx_ref, x_ref, mx_ref, mz_ref, g_ref, w_ref, o_ref, u_ref, *, tm, nsplit):
    j = pl.program_id(1)
    u = jnp.where(j == 0, ctx_ref[0].astype(F32), x_ref[0])
    u_ref[0] = u
    h = _norm_mod(u, g_ref[...], mx_ref[0], mz_ref[...], j * tm, tm, 0, 1).astype(BF16)
    n = w_ref.shape[1] // nsplit
    for s in range(nsplit):
        o_ref[0, :, s * n:(s + 1) * n] = _dot(h, w_ref[:, s * n:(s + 1) * n]).astype(o_ref.dtype)


def _entry_proj(x, ctx, mx, mz, g, w, out_dtype):
    b, seq, d = x.shape
    tm = ctx.shape[1]
    assert seq % tm == 0 and tm % 16 == 0
    n = w.shape[1]
    t = tm + seq
    return pl.pallas_call(
        functools.partial(_entry_proj_kernel, tm=tm, nsplit=n // D_MODEL),
        grid=(b, t // tm),
        in_specs=[
            pl.BlockSpec((1, tm, d), lambda i, j: (i, 0, 0)),
            pl.BlockSpec((1, tm, d), lambda i, j: (i, jnp.maximum(j - 1, 0), 0)),
            pl.BlockSpec((1, 6, d), lambda i, j: (i, 0, 0)),
            _const_spec((6, d)),
            _const_spec((1, d)),
            _const_spec((d, n)),
        ],
        out_specs=[pl.BlockSpec((1, tm, n), lambda i, j: (i, j, 0)),
                   pl.BlockSpec((1, tm, d), lambda i, j: (i, j, 0))],
        out_shape=[jax.ShapeDtypeStruct((b, t, n), out_dtype), jax.ShapeDtypeStruct((b, t, d), F32)],
        compiler_params=_cparams("parallel", "parallel"),
        name="entry_proj",
    )(ctx, x, mx, mz, g.reshape(1, d), w)


def _na_bias_table(rpb):
    w = GRID_W
    cols = jnp.arange(w)
    c_start = jnp.clip(cols - NA_WIN_C // 2, 0, w - NA_WIN_C)
    col_ok = (cols[None, :] >= c_start[:, None]) & (cols[None, :] < c_start[:, None] + NA_WIN_C)
    rpb = rpb.astype(F32)
    edge = w - NA_WIN_C
    t = jnp.concatenate([jnp.repeat(rpb[..., :1], edge, axis=-1), rpb,
                         jnp.repeat(rpb[..., -1:], edge + 1, axis=-1)], axis=-1)
    m = jnp.tile(t, w)[..., :w * (2 * w - 1)].reshape(t.shape[:-1] + (w, 2 * w - 1))[..., w - 1:]
    m = jnp.where(col_ok[None, None], m, NA_MASK)
    return jnp.concatenate([m[:, :-1], m[:, 1:]], axis=-1)


def _na_kernel(q_ref, k_ref, v_ref, bias_ref, o_ref, *, n_ctx, n_rows, with_ctx, rq):
    nctxb = n_ctx // GRID_W
    out_off = 0 if with_ctx else n_ctx
    scale = NA_HEAD_DIM ** -0.5
    assert scale == 0.125, "the score scale is folded into bf16 q, exact only for a power of two"
    lane_lo = lax.broadcasted_iota(jnp.int32, (1, LANES), 1) < NA_HEAD_DIM
    head_masks = (jnp.where(lane_lo, scale, 0.0).astype(BF16), jnp.where(lane_lo, 0.0, scale).astype(BF16))
    n_pairs = q_ref.shape[2] // LANES
    strip = NA_WIN_R * GRID_W
    jobs_idx = [(qi, p) for qi in range(rq) for p in range(n_pairs)]

    def q_rows(tok0, qi):
        return pl.ds(pl.multiple_of(tok0 + qi * GRID_W, GRID_W), GRID_W)

    def stacked_q(tok0, qi, p):
        q2 = q_ref[0, q_rows(tok0, qi), p * LANES:(p + 1) * LANES]
        return jnp.concatenate([q2 * head_masks[0], q2 * head_masks[1]], axis=0)

    def attend(tok0, jobs):
        scores = [[_dot_nt(q, k) if b is None else _dot_nt(q, k) + b
                   for k, b in zip(ks, bs)] for q, ks, _, bs in jobs]
        def lane_tiles(arrays):
            return [a[:, t:t + LANES] for a in arrays for t in range(0, a.shape[1], LANES)]

        tops = [functools.reduce(jnp.maximum, lane_tiles(sc)).max(axis=-1, keepdims=True) for sc in scores]
        exps = [[jnp.exp(s - m) for s in sc] for sc, m in zip(scores, tops)]
        dens = [functools.reduce(jnp.add, lane_tiles(es)).sum(axis=-1, keepdims=True) for es in exps]
        for (qi, p), (_, _, vs, _), es, den in zip(jobs_idx, jobs, exps, dens):
            o = _dot(es[0].astype(BF16), vs[0])
            for e, v in zip(es[1:], vs[1:]):
                o = o + _dot(e.astype(BF16), v)
            o = o / den
            o2 = jnp.where(lane_lo, o[:GRID_W], o[GRID_W:])
            o_ref[0, q_rows(tok0 - out_off, qi), p * LANES:(p + 1) * LANES] = o2.astype(o_ref.dtype)

    def latent_rows(step, carry):
        tok0 = n_ctx + step * (rq * GRID_W)
        jobs = []
        for qi, p in jobs_idx:
            ls = slice(p * LANES, (p + 1) * LANES)
            r = step * rq + qi
            r0 = jnp.clip(r - NA_WIN_R // 2, 0, n_rows - NA_WIN_R)
            start = pl.multiple_of(n_ctx + r0 * GRID_W, GRID_W)
            ri0 = NA_WIN_R - 1 - (r - r0)
            bias = jnp.concatenate(
                [jnp.concatenate([bias_ref[2 * p + hh, ri0 + 2 * m] for m in range(NA_WIN_R // 2)], axis=1)
                 for hh in range(2)], axis=0)
            jobs.append((stacked_q(tok0, qi, p),
                         [k_ref[0, pl.ds(start, strip), ls], k_ref[0, 0:n_ctx, ls]],
                         [v_ref[0, pl.ds(start, strip), ls], v_ref[0, 0:n_ctx, ls]],
                         [bias, None]))
        attend(tok0, jobs)
        return carry

    def context_rows(step, carry):
        tok0 = step * (rq * GRID_W)
        jobs = []
        for qi, p in jobs_idx:
            ls = slice(p * LANES, (p + 1) * LANES)
            jobs.append((stacked_q(tok0, qi, p), [k_ref[0, 0:n_ctx, ls]], [v_ref[0, 0:n_ctx, ls]], [None]))
        attend(tok0, jobs)
        return carry

    if with_ctx:
        lax.fori_loop(0, nctxb // rq, context_rows, 0)
    lax.fori_loop(0, n_rows // rq, latent_rows, 0)


def _na_attention(qkv, bias, n_ctx, with_ctx):
    b, t, _ = qkv.shape
    seq = t - n_ctx
    n_rows = seq // GRID_W
    assert seq % GRID_W == 0 and n_ctx % GRID_W == 0 and n_rows >= NA_WIN_R
    hw = 2 * LANES
    ng = D_MODEL // hw
    nctxb = n_ctx // GRID_W
    rq = 4 if nctxb % 4 == 0 and n_rows % 4 == 0 else 2
    assert nctxb % rq == 0 and n_rows % rq == 0
    t_out = t if with_ctx else seq

    return pl.pallas_call(
        functools.partial(_na_kernel, n_ctx=n_ctx, n_rows=n_rows, with_ctx=with_ctx, rq=rq),
        grid=(b, ng),
        in_specs=[
            pl.BlockSpec((1, t, hw), lambda i, g: (i, 0, g)),
            pl.BlockSpec((1, t, hw), lambda i, g: (i, 0, ng + g)),
            pl.BlockSpec((1, t, hw), lambda i, g: (i, 0, 2 * ng + g)),
            pl.BlockSpec((hw // NA_HEAD_DIM, 2 * NA_WIN_R - 2, GRID_W, LANES), lambda i, g: (g, 0, 0, 0)),
        ],
        out_specs=pl.BlockSpec((1, t_out, hw), lambda i, g: (i, 0, g)),
        out_shape=jax.ShapeDtypeStruct((b, t_out, D_MODEL), BF16),
        compiler_params=_cparams("parallel", "parallel"),
        name="na_attention",
    )(qkv, qkv, qkv, bias)


def _gated_residual(u, o_bf16_parts, w_ref, mx_ref, mz_ref, row0, n_ctx):
    acc = None
    k0 = 0
    for part in o_bf16_parts:
        kw = part.shape[1]
        term = _dot(part, w_ref[k0:k0 + kw, :])
        acc = term if acc is None else acc + term
        k0 += kw
    gate = _mod_rows(mx_ref[0], mz_ref[...], 2, row0, u.shape[0], n_ctx)
    return u + gate * acc


def _na_out_kernel(u_ref, o_ref, mx_ref, mz_ref, w_ref, out_ref, *, tm, n_ctx, blk_off):
    row0 = (pl.program_id(1) + blk_off) * tm
    out_ref[0] = _gated_residual(u_ref[0], [o_ref[0]], w_ref, mx_ref, mz_ref, row0, n_ctx)


def _na_out_latent(u, o, mx, mz, w_o, n_ctx):
    b, t, d = u.shape
    tm = _pick_tile(n_ctx, NARROW_TILE)
    assert (t - n_ctx) % tm == 0
    blk_off = n_ctx // tm
    nt = (t - n_ctx) // tm
    return pl.pallas_call(
        functools.partial(_na_out_kernel, tm=tm, n_ctx=n_ctx, blk_off=blk_off),
        grid=(b, nt),
        in_specs=[
            pl.BlockSpec((1, tm, d), lambda i, j: (i, j + blk_off, 0)),
            pl.BlockSpec((1, tm, d), lambda i, j: (i, j, 0)),
            pl.BlockSpec((1, 6, d), lambda i, j: (i, 0, 0)),
            _const_spec((6, d)),
            _const_spec((d, d)),
        ],
        out_specs=pl.BlockSpec((1, tm, d), lambda i, j: (i, j, 0)),
        out_shape=jax.ShapeDtypeStruct((b, nt * tm, d), F32),
        compiler_params=_cparams("parallel", "parallel"),
        name="na_out",
    )(u, o, mx, mz, w_o)


MLP_FC = 1024


def _mlp_apply(u, mx, mz, g, w1_ref, w2_ref, row0, n_ctx):
    h = _norm_mod(u, g, mx, mz, row0, n_ctx, 3, 4).astype(BF16)
    acc = None
    for c in range(D_FF // MLP_FC):
        a = _dot(h, w1_ref[:, c * MLP_FC:(c + 1) * MLP_FC])
        a = jnp.square(jnp.maximum(a, 0.0)).astype(BF16)
        term = _dot(a, w2_ref[c * MLP_FC:(c + 1) * MLP_FC, :])
        acc = term if acc is None else acc + term
    return u + _mod_rows(mx, mz, 5, row0, u.shape[0], n_ctx) * acc


def _mlp_kernel(u_ref, mx_ref, mz_ref, g_ref, w1_ref, w2_ref, fg_ref, out_ref, *, tm, n_ctx, final):
    row0 = pl.program_id(1) * tm
    y = _mlp_apply(u_ref[0], mx_ref[0], mz_ref[...], g_ref[...], w1_ref, w2_ref, row0, n_ctx)
    if final:
        y = _rms(y, fg_ref[...])
    out_ref[0] = y


def _mixer_mlp_kernel(*refs, kind, tm, n_ctx):
    u_ref, mixer_refs = refs[0], refs[1:-7]
    mx_ref, mz_ref, wo_ref, g_ref, w1_ref, w2_ref, out_ref = refs[-7:]
    if kind == "na":
        parts = [mixer_refs[0][0]]
    elif kind == "gla":
        parts = _gla_gated_parts(*mixer_refs)
    else:
        parts = _rw_gated_parts(*mixer_refs)
    row0 = pl.program_id(1) * tm
    u = _gated_residual(u_ref[0], parts, wo_ref, mx_ref, mz_ref, row0, n_ctx)
    out_ref[0] = _mlp_apply(u, mx_ref[0], mz_ref[...], g_ref[...], w1_ref, w2_ref, row0, n_ctx)


def _mixer_mlp(kind, u, mixer_inputs, mixer_specs, mx, mz, w_o, g, w1, w2, n_ctx):
    b, t, d = u.shape
    tm = _pick_tile(t, ROW_TILE)
    tok = pl.BlockSpec((1, tm, d), lambda i, j: (i, j, 0))
    return pl.pallas_call(
        functools.partial(_mixer_mlp_kernel, kind=kind, tm=tm, n_ctx=n_ctx),
        grid=(b, t // tm),
        in_specs=[tok] + [spec(tm) for spec in mixer_specs] + [
            pl.BlockSpec((1, 6, d), lambda i, j: (i, 0, 0)),
            _const_spec((6, d)),
            _const_spec((d, d)),
            _const_spec((1, d)),
            _const_spec((d, D_FF)),
            _const_spec((D_FF, d)),
        ],
        out_specs=tok,
        out_shape=jax.ShapeDtypeStruct((b, t, d), F32),
        compiler_params=_cparams("parallel", "parallel"),
        name=kind + "_out_mlp",
    )(u, *mixer_inputs, mx, mz, w_o, g.reshape(1, d), w1, w2)


def _mlp(u, mx, mz, g, w1, w2, final_g, n_ctx, final):
    b, t, d = u.shape
    tm = _pick_tile(t, ROW_TILE)
    return pl.pallas_call(
        functools.partial(_mlp_kernel, tm=tm, n_ctx=n_ctx, final=final),
        grid=(b, t // tm),
        in_specs=[
            pl.BlockSpec((1, tm, d), lambda i, j: (i, j, 0)),
            pl.BlockSpec((1, 6, d), lambda i, j: (i, 0, 0)),
            _const_spec((6, d)),
            _const_spec((1, d)),
            _const_spec((d, D_FF)),
            _const_spec((D_FF, d)),
            _const_spec((1, d)),
        ],
        out_specs=pl.BlockSpec((1, tm, d), lambda i, j: (i, j, 0)),
        out_shape=jax.ShapeDtypeStruct((b, t, d), F32),
        compiler_params=_cparams("parallel", "parallel"),
        name="mlp",
    )(u, mx, mz, g.reshape(1, d), w1, w2, final_g.reshape(1, d))


def _gla_proj_kernel(u_ref, mx_ref, mz_ref, g_ref, w_ref, wlr_ref, wdec_ref, bdec_ref,
                     p_ref, gate_ref, *, tm, n_ctx):
    t = pl.program_id(1)
    h = _norm_mod(u_ref[0], g_ref[...], mx_ref[0], mz_ref[...], t * tm, n_ctx, 0, 1).astype(BF16)
    n = D_MODEL
    for s in range(w_ref.shape[1] // n):
        p_ref[0, :, s * n:(s + 1) * n] = _dot(h, w_ref[:, s * n:(s + 1) * n])
    lr = _dot(h, wlr_ref[...]).astype(BF16)
    z = _dot(lr, wdec_ref[...]) + bdec_ref[...]
    gate_ref[0] = -_softplus(-z) * (1.0 / GLA_GATE_NORM)


def _gla_proj(u, mx, mz, g, w_main, w_lr, w_dec, b_dec, n_ctx):
    b, t, d = u.shape
    n = w_main.shape[1]
    tm = _pick_tile(t, ROW_TILE)
    return pl.pallas_call(
        functools.partial(_gla_proj_kernel, tm=tm, n_ctx=n_ctx),
        grid=(b, t // tm),
        in_specs=[
            pl.BlockSpec((1, tm, d), lambda i, j: (i, j, 0)),
            pl.BlockSpec((1, 6, d), lambda i, j: (i, 0, 0)),
            _const_spec((6, d)),
            _const_spec((1, d)),
            _const_spec((d, n)),
            _const_spec((d, LANES)),
            _const_spec((LANES, 2 * GLA_DK)),
            _const_spec((1, 2 * GLA_DK)),
        ],
        out_specs=[
            pl.BlockSpec((1, tm, n), lambda i, j: (i, j, 0)),
            pl.BlockSpec((1, tm, 2 * GLA_DK), lambda i, j: (i, j, 0)),
        ],
        out_shape=[
            jax.ShapeDtypeStruct((b, t, n), F32),
            jax.ShapeDtypeStruct((b, t, 2 * GLA_DK), F32),
        ],
        compiler_params=_cparams("parallel", "parallel"),
        name="gla_proj",
    )(u, mx, mz, g.reshape(1, d), w_main, w_lr, w_dec, b_dec)


def _tri(n, upper, strict):
    r = lax.broadcasted_iota(jnp.int32, (n, n), 0)
    c = lax.broadcasted_iota(jnp.int32, (n, n), 1)
    if upper:
        return (r < c) if strict else (r <= c)
    return (r > c) if strict else (r >= c)


def _cumsum_rows(x, descending):
    n = x.shape[0]
    row = lax.broadcasted_iota(jnp.int32, (n, 1), 0)
    s = 1
    while s < n:
        if descending:
            x = x + jnp.where(row < n - s, pltpu.roll(x, n - s, 0), 0.0)
        else:
            x = x + jnp.where(row >= s, pltpu.roll(x, s, 0), 0.0)
        s *= 2
    return x


def _gla_scan_kernel(q_ref, k_ref, v_ref, g0_ref, g1_ref, o_ref, s_ref, *, n_ctx_chunks, n_chunks, nb):
    qscale = GLA_HK ** -0.5
    g_refs = (g0_ref, g1_ref)
    incl = (_tri(CHUNK, False, False), _tri(CHUNK, True, False))
    o_ref[...] = jnp.zeros_like(o_ref)

    def step(first):
        units = [(d, first[d] + (i if d == 0 else -i)) for i in range(nb) for d in range(2)]
        rows = [pl.ds(pl.multiple_of(c * CHUNK, CHUNK), CHUNK) for _, c in units]
        q = [q_ref[0, r, :] for r in rows]
        k = [k_ref[0, r, :] for r in rows]
        v = [v_ref[0, r, :].astype(BF16) for r in rows]
        g = [g_refs[d][0, r, :] for (d, _), r in zip(units, rows)]
        bcum = [_cumsum_rows(gi, d == 1) for (d, _), gi in zip(units, g)]
        b_last_row = [jnp.sum(gi, axis=0, keepdims=True) for gi in g]
        dec_col = [jnp.exp(jnp.sum(gi.T, axis=1, keepdims=True)) for gi in g]
        q_e = [(qi * jnp.exp(bi) * qscale).astype(BF16) for qi, bi in zip(q, bcum)]
        k_e = [(ki * jnp.exp(-bi)).astype(BF16) for ki, bi in zip(k, bcum)]
        k_dec_t = [(ki * jnp.exp(bl - bi)).T.astype(BF16) for ki, bl, bi in zip(k, b_last_row, bcum)]
        a = [jnp.where(incl[d], _dot_nt(qe, ke), 0.0).astype(BF16) for (d, _), qe, ke in zip(units, q_e, k_e)]
        o_intra = [_dot(ai, vi) for ai, vi in zip(a, v)]
        s_inc = [_dot(kt, vi) for kt, vi in zip(k_dec_t, v)]
        s = [s_ref[0], s_ref[1]]
        for i, (d, _) in enumerate(units):
            o_ref[0, rows[i], :] += o_intra[i] + _dot(q_e[i], s[d].astype(BF16))
            s[d] = dec_col[i] * s[d] + s_inc[i]
        s_ref[0] = s[0]
        s_ref[1] = s[1]

    s_ref[...] = jnp.zeros_like(s_ref)
    n_lat = n_chunks - n_ctx_chunks

    def ctx_body(i, carry):
        step((i * nb, n_ctx_chunks - 1 - i * nb))
        return carry

    def lat_body(i, carry):
        step((n_ctx_chunks + i * nb, n_chunks - 1 - i * nb))
        return carry

    lax.fori_loop(0, n_ctx_chunks // nb, ctx_body, 0)
    lax.fori_loop(0, n_lat // nb, lat_body, 0)


def _gla_scan(p, gates, n_ctx, nb):
    b, t, _ = p.shape
    nkb = GLA_DK // GLA_HK
    return pl.pallas_call(
        functools.partial(_gla_scan_kernel, n_ctx_chunks=n_ctx // CHUNK, n_chunks=t // CHUNK, nb=nb),
        grid=(b, GLA_HEADS),
        in_specs=[
            pl.BlockSpec((1, t, GLA_HK), lambda i, h: (i, 0, h)),
            pl.BlockSpec((1, t, GLA_HK), lambda i, h: (i, 0, nkb + h)),
            pl.BlockSpec((1, t, GLA_HV), lambda i, h: (i, 0, (2 * GLA_DK) // GLA_HV + h)),
            pl.BlockSpec((1, t, GLA_HK), lambda i, h: (i, 0, h)),
            pl.BlockSpec((1, t, GLA_HK), lambda i, h: (i, 0, nkb + h)),
        ],
        out_specs=pl.BlockSpec((1, t, GLA_HV), lambda i, h: (i, 0, h)),
        out_shape=jax.ShapeDtypeStruct((b, t, GLA_DV), F32),
        scratch_shapes=[pltpu.VMEM((2, GLA_HK, GLA_HV), F32)],
        compiler_params=_cparams("parallel", "parallel"),
        name="gla_scan",
    )(p, p, p, gates, gates)


def _gla_gated_parts(o_ref, gt_ref, ng_ref):
    parts = []
    for h in range(GLA_HEADS):
        ls = slice(h * GLA_HV, (h + 1) * GLA_HV)
        ov = o_ref[0, :, ls]
        ov = ov * lax.rsqrt(jnp.mean(ov * ov, axis=-1, keepdims=True) + NORM_EPS) * ng_ref[...]
        gt = gt_ref[0, :, ls]
        parts.append((ov * (gt * _sigmoid(gt))).astype(BF16))
    return parts


def _gla_out_kernel(u_ref, o_ref, gt_ref, ng_ref, mx_ref, mz_ref, w_ref, out_ref, *, tm, n_ctx):
    row0 = pl.program_id(1) * tm
    parts = _gla_gated_parts(o_ref, gt_ref, ng_ref)
    out_ref[0] = _gated_residual(u_ref[0], parts, w_ref, mx_ref, mz_ref, row0, n_ctx)


def _gla_out(u, o, p, norm_g, mx, mz, w_o, n_ctx):
    b, t, d = u.shape
    tm = _pick_tile(t, ROW_TILE)
    gt_blk = (2 * GLA_DK + GLA_DV) // d
    return pl.pallas_call(
        functools.partial(_gla_out_kernel, tm=tm, n_ctx=n_ctx),
        grid=(b, t // tm),
        in_specs=[
            pl.BlockSpec((1, tm, d), lambda i, j: (i, j, 0)),
            pl.BlockSpec((1, tm, d), lambda i, j: (i, j, 0)),
            pl.BlockSpec((1, tm, d), lambda i, j: (i, j, gt_blk)),
            _const_spec((1, GLA_HV)),
            pl.BlockSpec((1, 6, d), lambda i, j: (i, 0, 0)),
            _const_spec((6, d)),
            _const_spec((d, d)),
        ],
        out_specs=pl.BlockSpec((1, tm, d), lambda i, j: (i, j, 0)),
        out_shape=jax.ShapeDtypeStruct((b, t, d), F32),
        compiler_params=_cparams("parallel", "parallel"),
        name="gla_out",
    )(u, o, p, norm_g.reshape(1, GLA_HV), mx, mz, w_o)


def _rw_proj_kernel(u_ref, up_ref, un_ref, mx_ref, mz_ref, g_ref, mix_ref, wrkv_ref, w1_ref, w2_ref,
                    w0_ref, a1_ref, a2_ref, a0_ref, g1_ref, g2_ref, ka_ref, rk_ref,
                    r_ref, k_ref, v_ref, gate_ref, lw_ref, a_ref, bonus_ref, *, tm, n_ctx, t_total):
    row0 = pl.program_id(1) * tm
    g = g_ref[...]
    mx = mx_ref[0]
    mz = mz_ref[...]
    h = _norm_mod(u_ref[0], g, mx, mz, row0, n_ctx, 0, 1)
    h_prev = _norm_mod(up_ref[0], g, mx, mz, row0 - 8, n_ctx, 0, 1)[7:8]
    h_next = _norm_mod(un_ref[0], g, mx, mz, row0 + tm, n_ctx, 0, 1)[0:1]
    idx = lax.broadcasted_iota(jnp.int32, (tm, 1), 0)
    rows = row0 + idx
    h_dn = jnp.where(idx == 0, h_prev, pltpu.roll(h, 1, 0))
    h_dn = jnp.where((rows == 0) | (rows == n_ctx), 0.0, h_dn)
    h_up = jnp.where(idx == tm - 1, h_next, pltpu.roll(h, tm - 1, 0))
    h_up = jnp.where((rows == n_ctx - 1) | (rows == t_total - 1), 0.0, h_up)
    xx = 0.5 * (h_dn + h_up) - h
    xr, xw, xk, xv, xa, xg = ((h + xx * mix_ref[m:m + 1]).astype(BF16) for m in range(6))
    r = _dot(xr, wrkv_ref[0])
    k = _dot(xk, wrkv_ref[1])
    v = _dot(xv, wrkv_ref[2])
    r_ref[0] = r
    k_ref[0] = k
    v_ref[0] = v
    gate_ref[0] = _dot(_sigmoid(_dot(xg, g1_ref[...])).astype(BF16), g2_ref[...])
    tw = jnp.tanh(_dot(xw, w1_ref[...])).astype(BF16)
    lw_ref[0] = -math.exp(-0.5) * _sigmoid(w0_ref[...] + _dot(tw, w2_ref[...]))
    ta = _dot(xa, a1_ref[...]).astype(BF16)
    a = _sigmoid(a0_ref[...] + _dot(ta, a2_ref[...]))
    a_ref[0] = a
    lane_lo = lax.broadcasted_iota(jnp.int32, (1, LANES), 1) < RW_HEAD
    d = D_MODEL
    for p in range(d // LANES):
        ls = slice(p * LANES, (p + 1) * LANES)
        kd_sum = k[:, ls] * (2.0 + (a[:, ls] + a[:, d + p * LANES:d + (p + 1) * LANES] - 2.0) * ka_ref[:, ls])
        bonus_ref[0, :, ls] = _pair_sum(r[:, ls] * kd_sum * rk_ref[:, ls], lane_lo) * v[:, ls]


def _rw_proj(u, mx, mz, g, wts, n_ctx):
    b, t, d = u.shape
    tm = _pick_tile(t, NARROW_TILE)
    nb8 = t // 8
    tb = tm // 8
    full = lambda n: jax.ShapeDtypeStruct((b, t, n), F32)
    row_spec = lambda n: pl.BlockSpec((1, tm, n), lambda i, j: (i, j, 0))
    return pl.pallas_call(
        functools.partial(_rw_proj_kernel, tm=tm, n_ctx=n_ctx, t_total=t),
        grid=(b, t // tm),
        in_specs=[
            row_spec(d),
            pl.BlockSpec((1, 8, d), lambda i, j: (i, jnp.maximum(j * tb - 1, 0), 0)),
            pl.BlockSpec((1, 8, d), lambda i, j: (i, jnp.minimum((j + 1) * tb, nb8 - 1), 0)),
            pl.BlockSpec((1, 6, d), lambda i, j: (i, 0, 0)),
            _const_spec((6, d)),
            _const_spec((1, d)),
            _const_spec((6, d)),
            _const_spec((3, d, d)),
            _const_spec((d, 2 * RW_LORA)),
            _const_spec((2 * RW_LORA, 2 * d)),
            _const_spec((1, 2 * d)),
            _const_spec((d, 2 * RW_LORA)),
            _const_spec((2 * RW_LORA, 2 * d)),
            _const_spec((1, 2 * d)),
            _const_spec((d, 2 * LANES)),
            _const_spec((2 * LANES, d)),
            _const_spec((1, d)),
            _const_spec((1, d)),
        ],
        out_specs=[row_spec(d), row_spec(d), row_spec(d), row_spec(d), row_spec(2 * d), row_spec(2 * d),
                   row_spec(d)],
        out_shape=[full(d), full(d), full(d), full(d), full(2 * d), full(2 * d), full(d)],
        compiler_params=_cparams("parallel", "parallel"),
        name="rw_proj",
    )(u, u, u, mx, mz, g.reshape(1, d), wts["mix"], wts["w_rkv"], wts["w1"], wts["w2"], wts["w0"],
      wts["a1"], wts["a2"], wts["a0"], wts["g1"], wts["g2"], wts["k_a"], wts["r_k"])


def _rw_pair_terms(units, side_work=()):
    C = CHUNK
    lane_lo = lax.broadcasted_iota(jnp.int32, (1, LANES), 1) < RW_HEAD
    row2 = lax.broadcasted_iota(jnp.int32, (C, LANES), 0)
    col2 = lax.broadcasted_iota(jnp.int32, (C, LANES), 1) & (C - 1)
    strict2 = (row2 > col2, row2 < col2)
    incl2 = (row2 >= col2, row2 <= col2)
    eye2 = jnp.where(row2 == col2, 1.0, 0.0)
    nu = len(units)

    def bd(x):
        zero = jnp.zeros_like(x)
        return jnp.concatenate([jnp.where(lane_lo, x, zero), jnp.where(lane_lo, zero, x)], axis=0)

    def hilo(x):
        hi = x.astype(BF16)
        return hi, (x - hi.astype(F32)).astype(BF16)

    side = iter(side_work)

    def stage_done():
        thunk = next(side, None)
        if thunk is not None:
            thunk()

    pre = []
    for r2, k2, v2, lw2, a2, desc, kkw, kaw in units:
        d = int(desc)
        kk = k2 * kkw
        kk = kk * lax.rsqrt(jnp.maximum(_pair_sum(kk * kk, lane_lo), 1e-24))
        kd = k2 * (1.0 + (a2 - 1.0) * kaw)
        bvec = kk * a2
        cum = _cumsum_rows(lw2, desc)
        tot = jnp.sum(lw2, axis=0, keepdims=True)
        e_neg = jnp.exp(-cum)
        e_rem = jnp.exp(tot - cum)
        pre.append(dict(
            d=d, tot=tot, v_b=v2.astype(BF16),
            a_t=-kk * jnp.exp(cum - lw2), r_t=r2 * jnp.exp(cum),
            b_h=(bvec * e_neg).astype(BF16), k_h=(kd * e_neg).astype(BF16),
            b_rem_t=(bvec * e_rem).T.astype(BF16), k_rem_t=(kd * e_rem).T.astype(BF16)))

    ar = [jnp.concatenate([p["a_t"], p["r_t"]], axis=0).astype(BF16) for p in pre]
    p_bk = [_dot_nt(ar[u], jnp.concatenate([bd(pre[u]["b_h"]), bd(pre[u]["k_h"])], axis=0)) for u in range(nu)]
    l_ab = [jnp.where(strict2[pre[u]["d"]], p_bk[u][:C, :LANES], 0.0) for u in range(nu)]
    m_rb = [jnp.where(incl2[pre[u]["d"]], p_bk[u][C:, :LANES], 0.0).astype(BF16) for u in range(nu)]
    l_ak = [jnp.where(strict2[pre[u]["d"]], p_bk[u][:C, LANES:], 0.0).astype(BF16) for u in range(nu)]
    m_rk = [jnp.where(incl2[pre[u]["d"]], p_bk[u][C:, LANES:], 0.0).astype(BF16) for u in range(nu)]
    stage_done()

    blk = (row2 // 8) ^ (col2 // 8)
    ld = [jnp.where(blk == 0, l, 0.0) for l in l_ab]
    ld_b = [l.astype(BF16) for l in ld]
    p2 = [_dot(l, bd(l)) for l in ld_b]
    ra = [_dot(jnp.concatenate([p.astype(BF16), l], axis=0), bd(p.astype(BF16))) for p, l in zip(p2, ld_b)]
    xo = [l + p + a[C:] for l, p, a in zip(ld, p2, ra)]
    xi = [eye2 + x + a[:C] + _dot(x.astype(BF16), bd(a[:C].astype(BF16))) for x, a in zip(xo, ra)]
    stage_done()
    l_b16 = [l.astype(BF16) for l in l_ab]
    for shift in range(3):
        link_b = jnp.where((blk >> shift) == 1, 1.0, 0.0).astype(BF16)
        x_b = [x.astype(BF16) for x in xi]
        mx = [_dot(l * link_b, bd(x)).astype(BF16) for l, x in zip(l_b16, x_b)]
        xi = [x + _dot(xb, bd(y)) for x, xb, y in zip(xi, x_b, mx)]
        stage_done()
    xo = [x - eye2 for x in xi]
    l_hl = [hilo(l) for l in l_ab]
    x_hl = [hilo(x) for x in xo]
    lx = [_dot(jnp.concatenate([lh, ll], axis=0), bd(xh)) for (lh, ll), (xh, _) in zip(l_hl, x_hl)]
    lxb = [_dot(lh, bd(xlo)) for (lh, _), (_, xlo) in zip(l_hl, x_hl)]
    res = [(l - x) + (a[:C] + a[C:] + b) for l, x, a, b in zip(l_ab, xo, lx, lxb)]
    stage_done()
    xo = [x + r + _dot(xh, bd(r.astype(BF16))) for x, r, (xh, _) in zip(xo, res, x_hl)]
    t_inv = [(eye2 + x).astype(BF16) for x in xo]

    lvyk = [_dot(jnp.concatenate([l_ak[u], m_rk[u]], axis=0), bd(pre[u]["v_b"])) for u in range(nu)]
    kv = [_dot(pre[u]["k_rem_t"], pre[u]["v_b"]) for u in range(nu)]
    tw = [_dot(t_inv[u], jnp.concatenate([bd(ar[u][:C]), bd(lvyk[u][:C].astype(BF16))], axis=1))
          for u in range(nu)]
    qy = [_dot(m_rb[u], jnp.concatenate([bd(tw[u][:, :LANES].astype(BF16)),
                                         bd(tw[u][:, LANES:].astype(BF16))], axis=1)) for u in range(nu)]
    gh = [_dot(pre[u]["b_rem_t"], tw[u].astype(BF16)) for u in range(nu)]

    r = lax.broadcasted_iota(jnp.int32, (LANES, LANES), 0)
    c = lax.broadcasted_iota(jnp.int32, (LANES, LANES), 1)
    same_head = (r < RW_HEAD) == (c < RW_HEAD)
    out = []
    for u in range(nu):
        qp = pre[u]["r_t"] + qy[u][:, :LANES]
        y0 = qy[u][:, LANES:] + lvyk[u][C:]
        g_mat = jnp.where(same_head, gh[u][:, :LANES], 0.0) + jnp.where(r == c, jnp.exp(pre[u]["tot"]), 0.0)
        h_mat = jnp.where(same_head, gh[u][:, LANES:] + kv[u], 0.0)
        out.append((qp, y0, g_mat, h_mat))
    return out


def _rw_mix_kernel(r0_ref, k0_ref, v0_ref, lw0_ref, a0_ref, r1_ref, k1_ref, v1_ref, lw1_ref, a1_ref,
                   kk_ref, ka_ref, y0_ref, y1_ref, st_ref, qp_s, yc_s, g_s, h_s, *, nb, pp):
    def zero_scratch(_, carry):
        for ref in (st_ref, qp_s, yc_s, g_s, h_s):
            ref[...] = jnp.zeros_like(ref)
        return carry

    lax.fori_loop(0, jnp.where(pl.program_id(2) == 0, 1, 0), zero_scratch, 0)

    y_refs = (y0_ref, y1_ref)
    chains = [(d, p) for d in range(2) for p in range(pp)]
    states = {dp: st_ref[dp[0], dp[1]] for dp in chains}
    prev_terms = {(d, p, c): (qp_s[d, p, c], yc_s[d, p, c], g_s[d, p, c], h_s[d, p, c])
                  for d, p in chains for c in range(nb)}

    def link(ci):
        def emit():
            for d, p in chains:
                c = ci if d == 0 else nb - 1 - ci
                qp, yc, g_mat, h_mat = prev_terms[d, p, c]
                st_b = states[d, p].astype(BF16)
                y_refs[d][0, c * CHUNK:(c + 1) * CHUNK, p * LANES:(p + 1) * LANES] = _dot(qp, st_b) + yc
                states[d, p] = _dot(g_mat, st_b) + h_mat
        return emit

    ins = ((r0_ref, k0_ref, v0_ref, lw0_ref, a0_ref), (r1_ref, k1_ref, v1_ref, lw1_ref, a1_ref))
    units = []
    index = []
    for p in range(pp):
        ls = slice(p * LANES, (p + 1) * LANES)
        for c in range(nb):
            rows = slice(c * CHUNK, (c + 1) * CHUNK)
            for d in range(2):
                units.append(tuple(ref[0, rows, ls] for ref in ins[d]) + (d == 1, kk_ref[:, ls], ka_ref[:, ls]))
                index.append((d, p, c))
    links = [link(ci) for ci in range(nb)]
    assert nb <= 5, "one recurrence link per stage boundary of _rw_pair_terms"
    results = _rw_pair_terms(units, side_work=links)
    for d, p in chains:
        st_ref[d, p] = states[d, p]
    for (d, p, c), (qp, y0, g_mat, h_mat) in zip(index, results):
        qp_s[d, p, c] = qp.astype(BF16)
        yc_s[d, p, c] = y0
        g_s[d, p, c] = g_mat.astype(BF16)
        h_s[d, p, c] = h_mat


def _rw_mix(r, k, v, lw, a, k_k, k_a, n_ctx, nb):
    b, t, d = r.shape
    npairs = d // LANES
    rows = nb * CHUNK
    nblk = t // rows
    ncb = n_ctx // rows

    def rev(j):
        return jnp.where(j < ncb, ncb - 1 - j, nblk - 1 - (j - ncb))

    cur = lambda j: jnp.minimum(j, nblk - 1)
    prev = lambda j: jnp.maximum(j - 1, 0)
    pp = 4
    pw = pp * LANES
    ng = npairs // pp
    fwd = lambda off, blk: pl.BlockSpec((1, rows, pw), lambda i, p, j: (i, blk(j), off + p))
    bwd = lambda off, blk: pl.BlockSpec((1, rows, pw), lambda i, p, j: (i, rev(blk(j)), off + p))
    par = pl.BlockSpec((1, pw), lambda i, p, j: (0, p))
    return pl.pallas_call(
        functools.partial(_rw_mix_kernel, nb=nb, pp=pp),
        grid=(b, ng, nblk + 1),
        in_specs=[fwd(0, cur), fwd(0, cur), fwd(0, cur), fwd(0, cur), fwd(0, cur),
                  bwd(0, cur), bwd(0, cur), bwd(0, cur), bwd(ng, cur), bwd(ng, cur), par, par],
        out_specs=[fwd(0, prev), bwd(0, prev)],
        out_shape=[jax.ShapeDtypeStruct((b, t, d), F32), jax.ShapeDtypeStruct((b, t, d), F32)],
        scratch_shapes=[pltpu.VMEM((2, pp, LANES, LANES), F32),
                        pltpu.VMEM((2, pp, nb, CHUNK, LANES), BF16), pltpu.VMEM((2, pp, nb, CHUNK, LANES), F32),
                        pltpu.VMEM((2, pp, nb, LANES, LANES), BF16), pltpu.VMEM((2, pp, nb, LANES, LANES), F32)],
        compiler_params=_cparams("parallel", "parallel", "arbitrary"),
        name="rw_mix",
    )(r, k, v, lw, a, r, k, v, lw, a, k_k.reshape(1, d), k_a.reshape(1, d))


def _rw_gated_parts(y0_ref, y1_ref, bonus_ref, gate_ref, lng_ref, lnb_ref):
    lane_lo = lax.broadcasted_iota(jnp.int32, (1, LANES), 1) < RW_HEAD
    parts = []
    for p in range(D_MODEL // LANES):
        ls = slice(p * LANES, (p + 1) * LANES)
        y = y0_ref[0, :, ls] + y1_ref[0, :, ls]
        mu = _pair_sum(y, lane_lo) * (1.0 / RW_HEAD)
        yc = y - mu
        var = _pair_sum(yc * yc, lane_lo) * (1.0 / RW_HEAD)
        yn = yc * lax.rsqrt(var + RW_GN_EPS) * lng_ref[:, ls] + lnb_ref[:, ls]
        parts.append(((yn + bonus_ref[0, :, ls]) * gate_ref[0, :, ls]).astype(BF16))
    return parts


def _rw_out_kernel(u_ref, y0_ref, y1_ref, bonus_ref, gate_ref, lng_ref, lnb_ref, mx_ref, mz_ref, w_ref,
                   out_ref, *, tm, n_ctx):
    row0 = pl.program_id(1) * tm
    parts = _rw_gated_parts(y0_ref, y1_ref, bonus_ref, gate_ref, lng_ref, lnb_ref)
    out_ref[0] = _gated_residual(u_ref[0], parts, w_ref, mx_ref, mz_ref, row0, n_ctx)


def _rw_out(u, y0, y1, bonus, gate, ln_g, ln_b, mx, mz, w_o, n_ctx):
    b, t, d = u.shape
    tm = _pick_tile(t, ROW_TILE)
    tok = pl.BlockSpec((1, tm, d), lambda i, j: (i, j, 0))
    vec = _const_spec((1, d))
    return pl.pallas_call(
        functools.partial(_rw_out_kernel, tm=tm, n_ctx=n_ctx),
        grid=(b, t // tm),
        in_specs=[tok] * 5 + [vec, vec, pl.BlockSpec((1, 6, d), lambda i, j: (i, 0, 0)),
                              _const_spec((6, d)), _const_spec((d, d))],
        out_specs=tok,
        out_shape=jax.ShapeDtypeStruct((b, t, d), F32),
        compiler_params=_cparams("parallel", "parallel"),
        name="rw_out",
    )(u, y0, y1, bonus, gate, ln_g.reshape(1, d), ln_b.reshape(1, d), mx, mz, w_o)


def _block_rows(w_pair):
    _, k, n = w_pair.shape
    z = jnp.zeros((k, n), w_pair.dtype)
    return jnp.concatenate([jnp.concatenate([w_pair[0], z], axis=1),
                            jnp.concatenate([z, w_pair[1]], axis=1)], axis=0)


def kernel(x, c, ctx, c_ctx, ada_w, ada_b, norm1_g, norm2_g, mlp_w1, mlp_w2, final_g, na_w_qkv, na_w_o, na_rpb, gla_w_in, gla_w_dec2, gla_b_dec, gla_norm_g, gla_w_o, rw_mix, rw_w_rkv, rw_w0, rw_w1, rw_w2, rw_a0, rw_a1, rw_a2, rw_g1, rw_g2, rw_k_k, rw_k_a, rw_r_k, rw_ln_g, rw_ln_b, rw_w_o):
    b, seq, d = x.shape
    n_ctx = ctx.shape[1]
    depth = ada_w.shape[0]
    assert d == D_MODEL and n_ctx % CHUNK == 0 and seq % CHUNK == 0

    entry_fused = seq % n_ctx == 0 and n_ctx % 16 == 0
    u = None if entry_fused else jnp.concatenate([ctx.astype(x.dtype), x], axis=1)

    rows = -(-(b + 1) // 8) * 8
    c_all = jnp.zeros((rows, d), F32).at[:b].set(c).at[b].set(c_ctx)
    tables = _ada_tables(c_all, ada_w, ada_b)
    mods_x = tables[:, :b].reshape(depth, b, 6, d)
    mods_z = tables[:, b].reshape(depth, 6, d)

    nb = max(n for n in (4, 2, 1) if (n_ctx // CHUNK) % n == 0 and (seq // CHUNK) % n == 0)

    for i in range(depth):
        kind, j = i % 3, i // 3
        last = i == depth - 1
        mx, mz = mods_x[i], mods_z[i]
        w1, w2 = mlp_w1[i].astype(BF16), mlp_w2[i].astype(BF16)
        tok = lambda tm: pl.BlockSpec((1, tm, d), lambda bi, ti: (bi, ti, 0))
        mlp_done = False
        if kind == 0:
            if u is None:
                qkv, u = _entry_proj(x, ctx, mx, mz, norm1_g[i], na_w_qkv[j].astype(BF16), BF16)
            else:
                qkv = _norm_proj(u, mx, mz, norm1_g[i], na_w_qkv[j].astype(BF16), n_ctx, BF16)
            o = _na_attention(qkv, _na_bias_table(na_rpb[j]), n_ctx, with_ctx=not last)
            if last:
                u = _na_out_latent(u, o, mx, mz, na_w_o[j].astype(BF16), n_ctx)
            else:
                u = _mixer_mlp("na", u, [o], [tok], mx, mz, na_w_o[j].astype(BF16), norm2_g[i], w1, w2, n_ctx)
                mlp_done = True
        elif kind == 1:
            n_main = 2 * GLA_DK + 2 * GLA_DV
            w_in = gla_w_in[j]
            w_lr = jnp.zeros((d, LANES), F32).at[:, :2 * GLA_LOW_RANK].set(w_in[:, n_main:])
            w_dec = jnp.zeros((LANES, 2 * GLA_DK), F32).at[:2 * GLA_LOW_RANK].set(_block_rows(gla_w_dec2[j]))
            p, gates = _gla_proj(u, mx, mz, norm1_g[i], w_in[:, :n_main].astype(BF16), w_lr.astype(BF16),
                                 w_dec.astype(BF16), gla_b_dec[j].reshape(1, 2 * GLA_DK), n_ctx)
            o = _gla_scan(p, gates, n_ctx, nb)
            if last:
                u = _gla_out(u, o, p, gla_norm_g[j], mx, mz, gla_w_o[j].astype(BF16), n_ctx)
            else:
                gt_blk = (2 * GLA_DK + GLA_DV) // d
                gt_spec = lambda tm: pl.BlockSpec((1, tm, d), lambda bi, ti: (bi, ti, gt_blk))
                u = _mixer_mlp("gla", u, [o, p, gla_norm_g[j].reshape(1, GLA_HV)],
                               [tok, gt_spec, lambda tm: _const_spec((1, GLA_HV))],
                               mx, mz, gla_w_o[j].astype(BF16), norm2_g[i], w1, w2, n_ctx)
                mlp_done = True
        else:
            g1 = jnp.zeros((d, 2 * LANES), F32).at[:, :RW_GATE_LORA].set(rw_g1[j])
            g2 = jnp.zeros((2 * LANES, d), F32).at[:RW_GATE_LORA].set(rw_g2[j])
            wts = dict(
                mix=rw_mix[j], w_rkv=rw_w_rkv[j].astype(BF16),
                w1=jnp.concatenate([rw_w1[j, 0], rw_w1[j, 1]], axis=1).astype(BF16),
                w2=_block_rows(rw_w2[j]).astype(BF16), w0=rw_w0[j].reshape(1, 2 * d),
                a1=jnp.concatenate([rw_a1[j, 0], rw_a1[j, 1]], axis=1).astype(BF16),
                a2=_block_rows(rw_a2[j]).astype(BF16), a0=rw_a0[j].reshape(1, 2 * d),
                g1=g1.astype(BF16), g2=g2.astype(BF16),
                k_a=rw_k_a[j].reshape(1, d), r_k=rw_r_k[j].reshape(1, d))
            r, k, v, gate, lw, a, bonus = _rw_proj(u, mx, mz, norm1_g[i], wts, n_ctx)
            y0, y1 = _rw_mix(r, k, v, lw, a, rw_k_k[j], rw_k_a[j], n_ctx, nb)
            if last:
                u = _rw_out(u, y0, y1, bonus, gate, rw_ln_g[j], rw_ln_b[j], mx, mz, rw_w_o[j].astype(BF16), n_ctx)
            else:
                vec = lambda tm: _const_spec((1, d))
                u = _mixer_mlp("rw", u, [y0, y1, bonus, gate, rw_ln_g[j].reshape(1, d), rw_ln_b[j].reshape(1, d)],
                               [tok, tok, tok, tok, vec, vec], mx, mz, rw_w_o[j].astype(BF16), norm2_g[i],
                               w1, w2, n_ctx)
                mlp_done = True
        if last and u.shape[1] != seq:
            u = u[:, n_ctx:]
        if not mlp_done:
            u = _mlp(u, mx, mz, norm2_g[i], w1, w2, final_g, 0 if last else n_ctx, final=last)
    return u
```
